```python
import math
import jax, jax.numpy as jnp
from jax import lax
import numpy as np

D_MODEL = 1024
BATCH = 8
SEQ = 2048
DEPTH = 1

CHUNK = 64
LEFT_CHUNKS = 8
BAND = (LEFT_CHUNKS + 1) * CHUNK
Q_BLOCK = 128
EPS = 1e-6
NEG = -1e30

A_HEADS = 8
A_HEAD_DIM = 64
A_WIDTH = A_HEADS * A_HEAD_DIM
REL_CLIP = 128

B_HEADS = 4
B_HEAD_DIM = 64
B_V_DIM = 2 * B_HEAD_DIM
B_QK_WIDTH = B_HEADS * 2 * B_HEAD_DIM
B_WIDTH = B_HEADS * B_V_DIM
T5_BUCKETS = 32
T5_MAX_DIST = 128

N_BRANCH = 2
D_FF = 2816
MIX_IN = 3 * A_WIDTH + 2 * B_QK_WIDTH + B_WIDTH + N_BRANCH * D_MODEL
SPLITS = list(np.cumsum([A_WIDTH, A_WIDTH, A_WIDTH, B_QK_WIDTH, B_QK_WIDTH, B_WIDTH]).tolist())

kernel_name = "hybrid_chunk_band_diff_attn_gated_macaron"


def rms_norm(x, g):
    xf = x.astype(jnp.float32)
    y = xf * lax.rsqrt(jnp.mean(xf * xf, axis=-1, keepdims=True) + EPS)
    return (y * g.astype(jnp.float32)).astype(x.dtype)


def swiglu_ffn(x, w_in, w_out):
    gate, up = jnp.split(x @ w_in, 2, axis=-1)
    return (jax.nn.silu(gate) * up) @ w_out


def t5_bucket(rel):
    nb = T5_BUCKETS // 2
    ret = jnp.where(rel > 0, nb, 0)
    n = jnp.abs(rel)
    max_exact = nb // 2
    nf = jnp.maximum(n, 1).astype(jnp.float32)
    large = max_exact + (jnp.log(nf / max_exact) / math.log(T5_MAX_DIST / max_exact)
                         * (nb - max_exact)).astype(jnp.int32)
    large = jnp.minimum(large, nb - 1)
    return ret + jnp.where(n < max_exact, n, large)


def chunked_band_attention(q, k, v, rel_table):
    b, s, h, d = q.shape
    nc = s // CHUNK
    qc = q.reshape(b, nc, CHUNK, h, d)
    pad = ((0, 0), (LEFT_CHUNKS * CHUNK, 0), (0, 0), (0, 0))
    kp = jnp.pad(k, pad).reshape(b, nc + LEFT_CHUNKS, CHUNK, h, d)
    vp = jnp.pad(v, pad).reshape(b, nc + LEFT_CHUNKS, CHUNK, h, d)
    kb = jnp.concatenate([kp[:, i:i + nc] for i in range(LEFT_CHUNKS + 1)], axis=2)
    vb = jnp.concatenate([vp[:, i:i + nc] for i in range(LEFT_CHUNKS + 1)], axis=2)
    sc = jnp.einsum('bcqhd,bckhd->bhcqk', qc, kb).astype(jnp.float32) * (d ** -0.5)
    qi = jnp.arange(CHUNK)[:, None]
    kj = jnp.arange(BAND)[None, :]
    rel = kj - LEFT_CHUNKS * CHUNK - qi
    bias = rel_table[:, jnp.clip(rel, -REL_CLIP, REL_CLIP) + REL_CLIP]
    sc = sc + bias[:, None].astype(jnp.float32)
    key_pos = (jnp.arange(nc)[:, None] - LEFT_CHUNKS) * CHUNK + jnp.arange(BAND)[None, :]
    sc = jnp.where((key_pos >= 0)[None, None, :, None, :], sc, NEG)
    p = jax.nn.softmax(sc, axis=-1).astype(v.dtype)
    o = jnp.einsum('bhcqk,bckhd->bcqhd', p, vb)
    return o.reshape(b, s, h * d)


def diff_attention(q, k, v, t5_table, lam, lambda_init, subln_g):
    b, s, h, _, d = q.shape
    nqb = s // Q_BLOCK
    qblocks = q.reshape(b, nqb, Q_BLOCK, h, 2, d).transpose(1, 0, 2, 3, 4, 5)
    key_pos = jnp.arange(s)
    key_chunk = key_pos // CHUNK
    scale = d ** -0.5

    def block(args):
        qblk, bi = args
        q_pos = bi * Q_BLOCK + jnp.arange(Q_BLOCK)
        sc = jnp.einsum('bqhrd,bkhrd->bhrqk', qblk, k).astype(jnp.float32) * scale
        bias = t5_table[:, t5_bucket(key_pos[None, :] - q_pos[:, None])]
        sc = sc + bias[None, :, None].astype(jnp.float32)
        allowed = key_chunk[None, :] <= (q_pos // CHUNK)[:, None]
        sc = jnp.where(allowed, sc, NEG)
        p = jax.nn.softmax(sc, axis=-1)
        attn = p[:, :, 0] - lam * p[:, :, 1]
        return jnp.einsum('bhqk,bkhe->bqhe', attn.astype(v.dtype), v)

    o = lax.map(block, (qblocks, jnp.arange(nqb)))
    o = o.transpose(1, 0, 2, 3, 4).reshape(b, s, h, 2 * d)
    o = rms_norm(o, subln_g) * (1.0 - lambda_init)
    return o.reshape(b, s, h * 2 * d)


def setup_inputs(seed: int = 0) -> dict:
    key = jax.random.key(seed)
    ks = jax.random.split(key, 24)

    def dense(k, shape, fan_in):
        return jax.random.normal(k, shape, jnp.float32) * (fan_in ** -0.5)

    def gain(k, shape):
        return 1.0 + 0.05 * jax.random.normal(k, shape, jnp.float32)

    L = DEPTH
    return {
        "x": jax.random.normal(ks[0], (BATCH, SEQ, D_MODEL), jnp.float32),
        "ffn1_norm": gain(ks[1], (L, D_MODEL)),
        "ffn1_w_in": dense(ks[2], (L, D_MODEL, 2 * D_FF), D_MODEL),
        "ffn1_w_out": dense(ks[3], (L, D_FF, D_MODEL), D_FF),
        "mix_norm": gain(ks[4], (L, D_MODEL)),
        "w_mix_in": dense(ks[5], (L, D_MODEL, MIX_IN), D_MODEL),
        "b_gate": 0.02 * jax.random.normal(ks[6], (L, N_BRANCH * D_MODEL), jnp.float32),
        "rel_bias_a": 0.1 * jax.random.normal(ks[7], (L, A_HEADS, 2 * REL_CLIP + 1), jnp.float32),
        "lambda_q1": 0.1 * jax.random.normal(ks[8], (L, B_HEAD_DIM), jnp.float32),
        "lambda_k1": 0.1 * jax.random.normal(ks[9], (L, B_HEAD_DIM), jnp.float32),
        "lambda_q2": 0.1 * jax.random.normal(ks[10], (L, B_HEAD_DIM), jnp.float32),
        "lambda_k2": 0.1 * jax.random.normal(ks[11], (L, B_HEAD_DIM), jnp.float32),
        "subln_g": gain(ks[12], (L, B_V_DIM)),
        "t5_bias": 0.1 * jax.random.normal(ks[13], (B_HEADS, T5_BUCKETS), jnp.float32),
        "w_branch_a": dense(ks[14], (L, A_WIDTH, D_MODEL), A_WIDTH),
        "w_branch_b": dense(ks[15], (L, B_WIDTH, D_MODEL), B_WIDTH),
        "w_o": dense(ks[16], (L, D_MODEL, D_MODEL), D_MODEL),
        "ffn2_norm": gain(ks[17], (L, D_MODEL)),
        "ffn2_w_in": dense(ks[18], (L, D_MODEL, 2 * D_FF), D_MODEL),
        "ffn2_w_out": dense(ks[19], (L, D_FF, D_MODEL), D_FF),
        "final_norm": gain(ks[20], (D_MODEL,)),
    }


def reference(x, ffn1_norm, ffn1_w_in, ffn1_w_out, mix_norm, w_mix_in, b_gate, rel_bias_a,
              lambda_q1, lambda_k1, lambda_q2, lambda_k2, subln_g, t5_bias,
              w_branch_a, w_branch_b, w_o, ffn2_norm, ffn2_w_in, ffn2_w_out, final_norm):
    b, s, _ = x.shape
    for li in range(DEPTH):
        x = x + 0.5 * swiglu_ffn(rms_norm(x, ffn1_norm[li]), ffn1_w_in[li], ffn1_w_out[li])

        u = rms_norm(x, mix_norm[li])
        proj = u @ w_mix_in[li]
        q_a, k_a, v_a, q_b, k_b, v_b, gate_logits = jnp.split(proj, SPLITS, axis=-1)

        y_a = chunked_band_attention(q_a.reshape(b, s, A_HEADS, A_HEAD_DIM),
                                     k_a.reshape(b, s, A_HEADS, A_HEAD_DIM),
                                     v_a.reshape(b, s, A_HEADS, A_HEAD_DIM),
                                     rel_bias_a[li])

        lambda_init = 0.8 - 0.6 * math.exp(-0.3 * li)
        lam = (jnp.exp(jnp.sum(lambda_q1[li].astype(jnp.float32) * lambda_k1[li].astype(jnp.float32)))
               - jnp.exp(jnp.sum(lambda_q2[li].astype(jnp.float32) * lambda_k2[li].astype(jnp.float32)))
               + lambda_init)
        y_b = diff_attention(q_b.reshape(b, s, B_HEADS, 2, B_HEAD_DIM),
                             k_b.reshape(b, s, B_HEADS, 2, B_HEAD_DIM),
                             v_b.reshape(b, s, B_HEADS, B_V_DIM),
                             t5_bias, lam, lambda_init, subln_g[li])

        g_a, g_b = jnp.split(jax.nn.sigmoid(gate_logits + b_gate[li]), 2, axis=-1)
        merged = g_a * (y_a @ w_branch_a[li]) + g_b * (y_b @ w_branch_b[li])
        x = x + merged @ w_o[li]

        x = x + 0.5 * swiglu_ffn(rms_norm(x, ffn2_norm[li]), ffn2_w_in[li], ffn2_w_out[li])
    return rms_norm(x, final_norm)
```

```python
import functools
import math

import jax
import jax.numpy as jnp
from jax import lax
from jax.experimental import pallas as pl
from jax.experimental.pallas import tpu as pltpu

EPS = 1e-6
NEG = -1e30

CHUNK = 64
LEFT_CHUNKS = 8
REL_CLIP = 128
T5_BUCKETS = 32
T5_MAX_DIST = 128

HEAD_DIM = 64
PAIR = 2 * HEAD_DIM
BLK = 2 * CHUNK
A_LEFT_BLOCKS = LEFT_CHUNKS * CHUNK // BLK
A_TILES = A_LEFT_BLOCKS + 1
B_TILES = 3

V7X_VMEM_LIMIT = 60000 * 1024

ROW_TILE = 256

_NT = (((1,), (1,)), ((), ()))


def _rms(xf, g):
    return xf * lax.rsqrt(jnp.mean(xf * xf, axis=-1, keepdims=True) + EPS) * g


def _dot(a, b):
    return jnp.dot(a, b, preferred_element_type=jnp.float32)


def _swiglu(h, w_in_ref, w_out_ref):
    d_ff = w_out_ref.shape[0]
    gate = _dot(h, w_in_ref[:, :d_ff])
    up = _dot(h, w_in_ref[:, d_ff:])
    act = (gate * jax.nn.sigmoid(gate) * up).astype(jnp.bfloat16)
    return _dot(act, w_out_ref[...])


def _ffn_mix_kernel(x_ref, g1_ref, w_in_ref, w_out_ref, gm_ref, w_mix_ref, b_gate_ref,
                    x1_ref, qa_ref, ka_ref, va_ref, qb_ref, kb_ref, vb_ref, gates_ref):
    x = x_ref[...]
    h = _rms(x, g1_ref[...]).astype(jnp.bfloat16)
    x1 = x + 0.5 * _swiglu(h, w_in_ref, w_out_ref)
    x1_ref[...] = x1

    u = _rms(x1, gm_ref[...]).astype(jnp.bfloat16)
    width = qa_ref.shape[1]
    scale = HEAD_DIM ** -0.5
    outs = (qa_ref, ka_ref, va_ref, qb_ref, kb_ref, vb_ref)
    for n, o_ref in enumerate(outs):
        p = _dot(u, w_mix_ref[:, n * width:(n + 1) * width])
        if o_ref is qa_ref or o_ref is qb_ref:
            p = p * scale
        o_ref[...] = p.astype(o_ref.dtype)
    logits = _dot(u, w_mix_ref[:, len(outs) * width:]) + b_gate_ref[...]
    gates_ref[...] = jax.nn.sigmoid(logits)


def _out_ffn_kernel(x1_ref, ya_ref, yb_ref, gates_ref, wa_ref, wb_ref, wo_ref,
                    g2_ref, w_in_ref, w_out_ref, gf_ref, o_ref, *, final_norm):
    d = x1_ref.shape[1]
    merged = (gates_ref[:, :d] * _dot(ya_ref[...], wa_ref[...])
              + gates_ref[:, d:] * _dot(yb_ref[...], wb_ref[...]))
    x2 = x1_ref[...] + _dot(merged.astype(jnp.bfloat16), wo_ref[...])
    h = _rms(x2, g2_ref[...]).astype(jnp.bfloat16)
    x3 = x2 + 0.5 * _swiglu(h, w_in_ref, w_out_ref)
    o_ref[...] = _rms(x3, gf_ref[...]) if final_norm else x3


def _resident(shape):
    return pl.BlockSpec(shape, lambda i: (0,) * len(shape), pipeline_mode=pl.Buffered(1))


def _rows(tm, width):
    return pl.BlockSpec((tm, width), lambda i: (i, 0))


def _ffn_mix(x, g1, w_in, w_out, gm, w_mix, b_gate, width):
    m, d = x.shape
    tm = ROW_TILE
    bf = jnp.bfloat16
    out_shape = ([jax.ShapeDtypeStruct((m, d), jnp.float32)]
                 + [jax.ShapeDtypeStruct((m, width), bf)] * 6
                 + [jax.ShapeDtypeStruct((m, b_gate.shape[1]), jnp.float32)])
    return pl.pallas_call(
        _ffn_mix_kernel,
        grid=(m // tm,),
        in_specs=[_rows(tm, d), _resident(g1.shape), _resident(w_in.shape), _resident(w_out.shape),
                  _resident(gm.shape), _resident(w_mix.shape), _resident(b_gate.shape)],
        out_specs=[_rows(tm, d)] + [_rows(tm, width)] * 6 + [_rows(tm, b_gate.shape[1])],
        out_shape=out_shape,
        compiler_params=pltpu.CompilerParams(dimension_semantics=("arbitrary",),
                                             vmem_limit_bytes=V7X_VMEM_LIMIT),
        name="ffn_mix",
    )(x, g1, w_in, w_out, gm, w_mix, b_gate)


def _out_ffn(x1, ya, yb, gates, wa, wb, wo, g2, w_in, w_out, gf, final_norm):
    m, d = x1.shape
    tm = ROW_TILE
    return pl.pallas_call(
        functools.partial(_out_ffn_kernel, final_norm=final_norm),
        grid=(m // tm,),
        in_specs=[_rows(tm, d), _rows(tm, ya.shape[1]), _rows(tm, yb.shape[1]), _rows(tm, gates.shape[1]),
                  _resident(wa.shape), _resident(wb.shape), _resident(wo.shape), _resident(g2.shape),
                  _resident(w_in.shape), _resident(w_out.shape), _resident(gf.shape)],
        out_specs=_rows(tm, d),
        out_shape=jax.ShapeDtypeStruct((m, d), jnp.float32),
        compiler_params=pltpu.CompilerParams(dimension_semantics=("arbitrary",),
                                             vmem_limit_bytes=V7X_VMEM_LIMIT),
        name="out_ffn",
    )(x1, ya, yb, gates, wa, wb, wo, g2, w_in, w_out, gf)


def _t5_bucket(rel):
    nb = T5_BUCKETS // 2
    ret = jnp.where(rel > 0, nb, 0)
    n = jnp.abs(rel)
    max_exact = nb // 2
    nf = jnp.maximum(n, 1).astype(jnp.float32)
    large = max_exact + (jnp.log(nf / max_exact) / math.log(T5_MAX_DIST / max_exact)
                         * (nb - max_exact)).astype(jnp.int32)
    large = jnp.minimum(large, nb - 1)
    return ret + jnp.where(n < max_exact, n, large)


def _block_rel(n_tiles):
    t = jnp.arange(n_tiles, dtype=jnp.int32)[:, None]
    return jnp.arange(2 * BLK, dtype=jnp.int32)[None, :] - BLK - BLK * t


def _bias_tiles_kernel(rel_bias_ref, tiles_ref, *, left_chunks):
    n_pairs, n_tiles = tiles_ref.shape[:2]
    qi = lax.broadcasted_iota(jnp.int32, (BLK, BLK), 0)
    kj = lax.broadcasted_iota(jnp.int32, (BLK, BLK), 1)
    chunk_diff = kj // CHUNK - qi // CHUNK
    for t in range(n_tiles):
        diff = chunk_diff - t * (BLK // CHUNK)
        allowed = diff <= 0
        if left_chunks is not None:
            allowed = allowed & (diff >= -left_chunks)
        for p in range(n_pairs):
            for half in range(2):
                row = jnp.broadcast_to(rel_bias_ref[p, half, t], (BLK, 2 * BLK))
                skew = pltpu.roll(row, 0, 1, stride=1, stride_axis=0)
                tile = jnp.where(allowed, skew[:, BLK:], NEG)
                tiles_ref[p, t, half * BLK:(half + 1) * BLK, :] = tile


def _bias_tiles(rel_bias, left_chunks):
    n_pairs, _, n_tiles = rel_bias.shape[:3]
    return pl.pallas_call(
        functools.partial(_bias_tiles_kernel, left_chunks=left_chunks),
        out_shape=jax.ShapeDtypeStruct((n_pairs, n_tiles, 2 * BLK, BLK), jnp.float32),
        name="bias_tiles",
    )(rel_bias)


def _attn_kernel(*refs, left_blocks, lambda_init):
    if lambda_init is None:
        q_ref, k_ref, v_ref, tiles_ref, o_ref = refs
    else:
        q_ref, k_ref, v_ref, tiles_ref, lq1_ref, lk1_ref, lq2_ref, lk2_ref, subg_ref, o_ref = refs
        lam = (jnp.exp(jnp.sum(lq1_ref[...] * lk1_ref[...], axis=-1, keepdims=True))
               - jnp.exp(jnp.sum(lq2_ref[...] * lk2_ref[...], axis=-1, keepdims=True))
               + lambda_init)
    i = pl.program_id(1)
    n_tiles = tiles_ref.shape[1]
    lo = jnp.maximum(i - left_blocks, 0) if left_blocks is not None else 0
    lane = lax.broadcasted_iota(jnp.int32, (BLK, PAIR), 1)
    low_half = lane < HEAD_DIM

    for p in range(q_ref.shape[2] // PAIR):
        cols = slice(p * PAIR, (p + 1) * PAIR)
        q = q_ref[0, :, cols]
        zero = jnp.zeros_like(q)
        q2 = jnp.concatenate([jnp.where(low_half, q, zero), jnp.where(low_half, zero, q)], axis=0)

        def block(j, carry, cols=cols, p=p, q2=q2):
            m, l, acc = carry
            rows = pl.ds(pl.multiple_of(j * BLK, BLK), BLK)
            s = lax.dot_general(q2, k_ref[0, rows, cols], _NT, preferred_element_type=jnp.float32)
            s = s + tiles_ref[p, jnp.minimum(i - j, n_tiles - 1)]
            m_new = jnp.maximum(m, jnp.max(s, axis=-1, keepdims=True))
            alpha = jnp.exp(m - m_new)
            e = jnp.exp(s - m_new)
            l = alpha * l + jnp.sum(e, axis=-1, keepdims=True)
            acc = alpha * acc + _dot(e.astype(jnp.bfloat16), v_ref[0, rows, cols])
            return m_new, l, acc

        init = (jnp.full((2 * BLK, 1), NEG, jnp.float32), jnp.zeros((2 * BLK, 1), jnp.float32),
                jnp.zeros((2 * BLK, PAIR), jnp.float32))
        _, l, acc = lax.fori_loop(lo, i + 1, block, init)
        o = acc / l
        if lambda_init is None:
            y = jnp.where(low_half, o[:BLK], o[BLK:])
        else:
            y = _rms(o[:BLK] - lam * o[BLK:], subg_ref[...]) * (1.0 - lambda_init)
        o_ref[0, :, cols] = y.astype(o_ref.dtype)


def _attention(q, k, v, tiles, left_blocks, lambda_init=None, extra=()):
    b, s, w = q.shape
    blk_q = pl.BlockSpec((1, BLK, w), lambda bi, i: (bi, i, 0))
    whole = pl.BlockSpec((1, s, w), lambda bi, i: (bi, 0, 0))

    def const(shape):
        return pl.BlockSpec(shape, lambda bi, i: (0,) * len(shape))

    return pl.pallas_call(
        functools.partial(_attn_kernel, left_blocks=left_blocks, lambda_init=lambda_init),
        grid=(b, s // BLK),
        in_specs=[blk_q, whole, whole, const(tiles.shape)] + [const(e.shape) for e in extra],
        out_specs=blk_q,
        out_shape=jax.ShapeDtypeStruct((b, s, w), jnp.bfloat16),
        compiler_params=pltpu.CompilerParams(dimension_semantics=("arbitrary", "arbitrary"),
                                             vmem_limit_bytes=V7X_VMEM_LIMIT),
        name="attn_a" if lambda_init is None else "attn_b",
    )(q, k, v, tiles, *extra)


def kernel(x, ffn1_norm, ffn1_w_in, ffn1_w_out, mix_norm, w_mix_in, b_gate, rel_bias_a, lambda_q1, lambda_k1, lambda_q2, lambda_k2, subln_g, t5_bias, w_branch_a, w_branch_b, w_o, ffn2_norm, ffn2_w_in, ffn2_w_out, final_norm):
    b, s, d = x.shape
    depth = ffn1_norm.shape[0]
    width = w_branch_a.shape[1]
    n_pairs = width // PAIR
    bf = jnp.bfloat16
    f32 = jnp.float32

    t5_rel = t5_bias.astype(f32)[:, _t5_bucket(_block_rel(B_TILES))]
    t5_rel = jnp.broadcast_to(t5_rel[:, None, :, None, :], (n_pairs, 2, B_TILES, 1, 2 * BLK))
    tiles_b = _bias_tiles(t5_rel, None)

    xf = x.astype(f32).reshape(b * s, d)
    for li in range(depth):
        x1, qa, ka, va, qb, kb, vb, gates = _ffn_mix(
            xf, ffn1_norm[li][None].astype(f32), ffn1_w_in[li].astype(bf), ffn1_w_out[li].astype(bf),
            mix_norm[li][None].astype(f32), w_mix_in[li].astype(bf), b_gate[li][None].astype(f32), width)

        rel_a = jnp.clip(_block_rel(A_TILES), -REL_CLIP, REL_CLIP) + REL_CLIP
        a_rel = rel_bias_a[li].astype(f32)[:, rel_a]
        tiles_a = _bias_tiles(a_rel.reshape(n_pairs, 2, A_TILES, 1, 2 * BLK), LEFT_CHUNKS)

        def seq(t):
            return t.reshape(b, s, width)

        ya = _attention(seq(qa), seq(ka), seq(va), tiles_a, A_LEFT_BLOCKS)
        lambda_init = 0.8 - 0.6 * math.exp(-0.3 * li)
        lams = [t[li][None].astype(f32) for t in (lambda_q1, lambda_k1, lambda_q2, lambda_k2)]
        yb = _attention(seq(qb), seq(kb), seq(vb), tiles_b, None, lambda_init=lambda_init,
                        extra=(*lams, subln_g[li][None].astype(f32)))

        xf = _out_ffn(x1, ya.reshape(b * s, width), yb.reshape(b * s, width), gates,
                      w_branch_a[li].astype(bf), w_branch_b[li].astype(bf), w_o[li].astype(bf),
                      ffn2_norm[li][None].astype(f32), ffn2_w_in[li].astype(bf), ffn2_w_out[li].astype(bf),
                      final_norm[None].astype(f32), final_norm=(li == depth - 1))
    return xf.reshape(b, s, d).astype(x.dtype)
```

```python
import functools
import math

import jax
import jax.numpy as jnp
from jax import lax
from jax.experimental import pallas as pl
from jax.experimental.pallas import tpu as pltpu

EPS = 1e-6
NEG = -1e30

CHUNK = 64
LEFT_CHUNKS = 8
REL_CLIP = 128
T5_BUCKETS = 32
T5_MAX_DIST = 128

LANES = 128
HEAD_DIM = 64
PAIR = 2 * HEAD_DIM
SUB_CHUNKS = LANES // CHUNK

A_BLK = 128
A_TILES = LEFT_CHUNKS * CHUNK // A_BLK + 1
B_BLK = 256
B_TILES = 3

V7X_VMEM_LIMIT = 60000 * 1024

ROW_TILE = 256

_NT = (((1,), (1,)), ((), ()))


def _rms(xf, g):
    return xf * lax.rsqrt(jnp.mean(xf * xf, axis=-1, keepdims=True) + EPS) * g


def _dot(a, b):
    return jnp.dot(a, b, preferred_element_type=jnp.float32)


def _swiglu(h, w_in_ref, w_out_ref):
    d_ff = w_out_ref.shape[0]
    gate = _dot(h, w_in_ref[:, :d_ff])
    up = _dot(h, w_in_ref[:, d_ff:])
    act = (gate * jax.nn.sigmoid(gate) * up).astype(jnp.bfloat16)
    return _dot(act, w_out_ref[...])


def _ffn_mix_kernel(x_ref, g1_ref, w_in_ref, w_out_ref, gm_ref, w_mix_ref, b_gate_ref,
                    x1_ref, qa_ref, ka_ref, va_ref, qb_ref, kb_ref, vb_ref, gates_ref):
    x = x_ref[...]
    h = _rms(x, g1_ref[...]).astype(jnp.bfloat16)
    x1 = x + 0.5 * _swiglu(h, w_in_ref, w_out_ref)
    x1_ref[...] = x1

    u = _rms(x1, gm_ref[...]).astype(jnp.bfloat16)
    width = qa_ref.shape[1]
    scale = HEAD_DIM ** -0.5
    outs = (qa_ref, ka_ref, va_ref, qb_ref, kb_ref, vb_ref)
    for n, o_ref in enumerate(outs):
        p = _dot(u, w_mix_ref[:, n * width:(n + 1) * width])
        if o_ref is qa_ref or o_ref is qb_ref:
            p = p * scale
        o_ref[...] = p.astype(o_ref.dtype)
    logits = _dot(u, w_mix_ref[:, len(outs) * width:]) + b_gate_ref[...]
    gates_ref[...] = jax.nn.sigmoid(logits)


def _out_ffn_kernel(x1_ref, ya_ref, yb_ref, gates_ref, wa_ref, wb_ref, wo_ref,
                    g2_ref, w_in_ref, w_out_ref, gf_ref, o_ref, *, final_norm):
    d = x1_ref.shape[1]
    merged = (gates_ref[:, :d] * _dot(ya_ref[...], wa_ref[...])
              + gates_ref[:, d:] * _dot(yb_ref[...], wb_ref[...]))
    x2 = x1_ref[...] + _dot(merged.astype(jnp.bfloat16), wo_ref[...])
    h = _rms(x2, g2_ref[...]).astype(jnp.bfloat16)
    x3 = x2 + 0.5 * _swiglu(h, w_in_ref, w_out_ref)
    o_ref[...] = _rms(x3, gf_ref[...]) if final_norm else x3


def _resident(shape):
    return pl.BlockSpec(shape, lambda *_: (0,) * len(shape), pipeline_mode=pl.Buffered(1))


def _rows(tm, width):
    return pl.BlockSpec((tm, width), lambda i: (i, 0))


def _ffn_mix(x, g1, w_in, w_out, gm, w_mix, b_gate, width):
    m, d = x.shape
    tm = ROW_TILE
    bf = jnp.bfloat16
    out_shape = ([jax.ShapeDtypeStruct((m, d), jnp.float32)]
                 + [jax.ShapeDtypeStruct((m, width), bf)] * 6
                 + [jax.ShapeDtypeStruct((m, b_gate.shape[1]), jnp.float32)])
    return pl.pallas_call(
        _ffn_mix_kernel,
        grid=(m // tm,),
        in_specs=[_rows(tm, d), _resident(g1.shape), _resident(w_in.shape), _resident(w_out.shape),
                  _resident(gm.shape), _resident(w_mix.shape), _resident(b_gate.shape)],
        out_specs=[_rows(tm, d)] + [_rows(tm, width)] * 6 + [_rows(tm, b_gate.shape[1])],
        out_shape=out_shape,
        compiler_params=pltpu.CompilerParams(dimension_semantics=("arbitrary",),
                                             vmem_limit_bytes=V7X_VMEM_LIMIT),
        name="ffn_mix",
    )(x, g1, w_in, w_out, gm, w_mix, b_gate)


def _out_ffn(x1, ya, yb, gates, wa, wb, wo, g2, w_in, w_out, gf, final_norm):
    m, d = x1.shape
    tm = ROW_TILE
    return pl.pallas_call(
        functools.partial(_out_ffn_kernel, final_norm=final_norm),
        grid=(m // tm,),
        in_specs=[_rows(tm, d), _rows(tm, ya.shape[1]), _rows(tm, yb.shape[1]), _rows(tm, gates.shape[1]),
                  _resident(wa.shape), _resident(wb.shape), _resident(wo.shape), _resident(g2.shape),
                  _resident(w_in.shape), _resident(w_out.shape), _resident(gf.shape)],
        out_specs=_rows(tm, d),
        out_shape=jax.ShapeDtypeStruct((m, d), jnp.float32),
        compiler_params=pltpu.CompilerParams(dimension_semantics=("arbitrary",),
                                             vmem_limit_bytes=V7X_VMEM_LIMIT),
        name="out_ffn",
    )(x1, ya, yb, gates, wa, wb, wo, g2, w_in, w_out, gf)


def _t5_bucket(rel):
    nb = T5_BUCKETS // 2
    ret = jnp.where(rel > 0, nb, 0)
    n = jnp.abs(rel)
    max_exact = nb // 2
    nf = jnp.maximum(n, 1).astype(jnp.float32)
    large = max_exact + (jnp.log(nf / max_exact) / math.log(T5_MAX_DIST / max_exact)
                         * (nb - max_exact)).astype(jnp.int32)
    large = jnp.minimum(large, nb - 1)
    return ret + jnp.where(n < max_exact, n, large)


def _window_rel(n_tiles, blk):
    nb = blk // LANES
    u = jnp.arange(-(nb - 1), n_tiles * nb, dtype=jnp.int32)[:, None]
    return jnp.arange(2 * LANES, dtype=jnp.int32)[None, :] - LANES - LANES * u


def _bias_tiles_kernel(rel_bias_ref, tiles_ref, *, left_chunks):
    n_pairs, n_tiles, _, blk = tiles_ref.shape
    nb = blk // LANES
    qi = lax.broadcasted_iota(jnp.int32, (LANES, LANES), 0)
    kj = lax.broadcasted_iota(jnp.int32, (LANES, LANES), 1)
    chunk_diff = kj // CHUNK - qi // CHUNK
    for p in range(n_pairs):
        for half in range(2):
            for w in range(rel_bias_ref.shape[2]):
                u = w - (nb - 1)
                row = jnp.broadcast_to(rel_bias_ref[p, half, w], (LANES, 2 * LANES))
                sub = pltpu.roll(row, 0, 1, stride=1, stride_axis=0)[:, LANES:]
                for t in range(n_tiles):
                    for a in range(nb):
                        c = t * nb + a - u
                        if not 0 <= c < nb:
                            continue
                        diff = chunk_diff - u * SUB_CHUNKS
                        allowed = diff <= 0
                        if left_chunks is not None:
                            allowed = allowed & (diff >= -left_chunks)
                        tiles_ref[p, t, pl.ds(half * blk + a * LANES, LANES), pl.ds(c * LANES, LANES)] = (
                            jnp.where(allowed, sub, NEG))


def _bias_tiles(rel_bias, n_tiles, blk, left_chunks):
    n_pairs = rel_bias.shape[0]
    return pl.pallas_call(
        functools.partial(_bias_tiles_kernel, left_chunks=left_chunks),
        out_shape=jax.ShapeDtypeStruct((n_pairs, n_tiles, 2 * blk, blk), jnp.float32),
        compiler_params=pltpu.CompilerParams(vmem_limit_bytes=V7X_VMEM_LIMIT),
        name="bias_tiles",
    )(rel_bias)


def _attn_kernel(*refs, left_blocks, lambda_init):
    if lambda_init is None:
        q_ref, k_ref, v_ref, tiles_ref, o_ref, q2_ref, s_ref, m_ref, l_ref, acc_ref = refs
    else:
        (q_ref, k_ref, v_ref, tiles_ref, lq1_ref, lk1_ref, lq2_ref, lk2_ref, subg_ref, o_ref,
         q2_ref, s_ref, m_ref, l_ref, acc_ref) = refs
        lam = (jnp.exp(jnp.sum(lq1_ref[...] * lk1_ref[...], axis=-1, keepdims=True))
               - jnp.exp(jnp.sum(lq2_ref[...] * lk2_ref[...], axis=-1, keepdims=True))
               + lambda_init)
    blk = q_ref.shape[1]
    nb = blk // LANES
    n_pairs = q_ref.shape[2] // PAIR
    n_tiles = tiles_ref.shape[1]
    i = pl.program_id(1)
    lo = jnp.maximum(i - left_blocks, 0) if left_blocks is not None else 0
    low_half = lax.broadcasted_iota(jnp.int32, (blk, PAIR), 1) < HEAD_DIM

    def cols(p):
        return slice(p * PAIR, (p + 1) * PAIR)

    for p in range(n_pairs):
        q = q_ref[0, :, cols(p)]
        zero = jnp.zeros_like(q)
        q2_ref[p, :blk] = jnp.where(low_half, q, zero)
        q2_ref[p, blk:] = jnp.where(low_half, zero, q)
    m_ref[...] = jnp.full(m_ref.shape, NEG, jnp.float32)
    l_ref[...] = jnp.zeros(l_ref.shape, jnp.float32)
    acc_ref[...] = jnp.zeros(acc_ref.shape, jnp.float32)

    def scores(j, carry):
        rows = pl.ds(pl.multiple_of(j * blk, blk), blk)
        tile = jnp.minimum(i - j, n_tiles - 1)
        for p in range(n_pairs):
            s = lax.dot_general(q2_ref[p], k_ref[0, rows, cols(p)], _NT, preferred_element_type=jnp.float32)
            s = s + tiles_ref[p, tile]
            s_ref[j - lo, p] = s
            m = m_ref[p]
            for c in range(nb):
                m = jnp.maximum(m, s[:, c * LANES:(c + 1) * LANES])
            m_ref[p] = m
        return carry

    lax.fori_loop(lo, i + 1, scores, 0)

    for p in range(n_pairs):
        m_ref[p] = jnp.broadcast_to(jnp.max(m_ref[p], axis=-1, keepdims=True), m_ref.shape[1:])

    def weighted(j, carry):
        rows = pl.ds(pl.multiple_of(j * blk, blk), blk)
        for p in range(n_pairs):
            e = jnp.exp(s_ref[j - lo, p] - jnp.concatenate([m_ref[p]] * nb, axis=1))
            l = l_ref[p]
            for c in range(nb):
                l = l + e[:, c * LANES:(c + 1) * LANES]
            l_ref[p] = l
            acc_ref[p] += _dot(e.astype(jnp.bfloat16), v_ref[0, rows, cols(p)])
        return carry

    lax.fori_loop(lo, i + 1, weighted, 0)

    for p in range(n_pairs):
        o = acc_ref[p] / jnp.sum(l_ref[p], axis=-1, keepdims=True)
        if lambda_init is None:
            y = jnp.where(low_half, o[:blk], o[blk:])
        else:
            y = _rms(o[:blk] - lam * o[blk:], subg_ref[...]) * (1.0 - lambda_init)
        o_ref[0, :, cols(p)] = y.astype(o_ref.dtype)


def _attention(q, k, v, tiles, left_blocks, lambda_init=None, extra=()):
    b, s, w = q.shape
    n_pairs = w // PAIR
    blk = tiles.shape[3]
    n_live = s // blk if left_blocks is None else left_blocks + 1
    blk_q = pl.BlockSpec((1, blk, w), lambda bi, i: (bi, i, 0))
    whole = pl.BlockSpec((1, s, w), lambda bi, i: (bi, 0, 0))
    f32 = jnp.float32
    return pl.pallas_call(
        functools.partial(_attn_kernel, left_blocks=left_blocks, lambda_init=lambda_init),
        grid=(b, s // blk),
        in_specs=[blk_q, whole, whole, _resident(tiles.shape)] + [_resident(e.shape) for e in extra],
        out_specs=blk_q,
        out_shape=jax.ShapeDtypeStruct((b, s, w), jnp.bfloat16),
        scratch_shapes=[pltpu.VMEM((n_pairs, 2 * blk, PAIR), jnp.bfloat16),
                        pltpu.VMEM((n_live, n_pairs, 2 * blk, blk), f32),
                        pltpu.VMEM((n_pairs, 2 * blk, LANES), f32),
                        pltpu.VMEM((n_pairs, 2 * blk, LANES), f32),
                        pltpu.VMEM((n_pairs, 2 * blk, PAIR), f32)],
        compiler_params=pltpu.CompilerParams(dimension_semantics=("arbitrary", "arbitrary"),
                                             vmem_limit_bytes=V7X_VMEM_LIMIT),
        name="attn_a" if lambda_init is None else "attn_b",
    )(q, k, v, tiles, *extra)


def kernel(x, ffn1_norm, ffn1_w_in, ffn1_w_out, mix_norm, w_mix_in, b_gate, rel_bias_a, lambda_q1, lambda_k1, lambda_q2, lambda_k2, subln_g, t5_bias, w_branch_a, w_branch_b, w_o, ffn2_norm, ffn2_w_in, ffn2_w_out, final_norm):
    b, s, d = x.shape
    depth = ffn1_norm.shape[0]
    assert depth >= 1
    width = w_branch_a.shape[1]
    n_pairs = width // PAIR
    bf = jnp.bfloat16
    f32 = jnp.float32

    t5_rel = t5_bias.astype(f32)[:, _t5_bucket(_window_rel(B_TILES, B_BLK))]
    t5_rel = jnp.broadcast_to(t5_rel[:, None, :, None, :], (n_pairs, 2) + t5_rel.shape[1:2] + (1, 2 * LANES))
    tiles_b = _bias_tiles(t5_rel, B_TILES, B_BLK, None)

    xf = x.astype(f32).reshape(b * s, d)
    for li in range(depth):
        x1, qa, ka, va, qb, kb, vb, gates = _ffn_mix(
            xf, ffn1_norm[li][None].astype(f32), ffn1_w_in[li].astype(bf), ffn1_w_out[li].astype(bf),
            mix_norm[li][None].astype(f32), w_mix_in[li].astype(bf), b_gate[li][None].astype(f32), width)

        rel_a = jnp.clip(_window_rel(A_TILES, A_BLK), -REL_CLIP, REL_CLIP) + REL_CLIP
        a_rel = rel_bias_a[li].astype(f32)[:, rel_a]
        tiles_a = _bias_tiles(a_rel.reshape(n_pairs, 2, rel_a.shape[0], 1, 2 * LANES), A_TILES, A_BLK, LEFT_CHUNKS)

        def seq(t):
            return t.reshape(b, s, width)

        ya = _attention(seq(qa), seq(ka), seq(va), tiles_a, A_TILES - 1)
        lambda_init = 0.8 - 0.6 * math.exp(-0.3 * li)
        lams = [t[li][None].astype(f32) for t in (lambda_q1, lambda_k1, lambda_q2, lambda_k2)]
        yb = _attention(seq(qb), seq(kb), seq(vb), tiles_b, None, lambda_init=lambda_init,
                        extra=(*lams, subln_g[li][None].astype(f32)))

        xf = _out_ffn(x1, ya.reshape(b * s, width), yb.reshape(b * s, width), gates,
                      w_branch_a[li].astype(bf), w_branch_b[li].astype(bf), w_o[li].astype(bf),
                      ffn2_norm[li][None].astype(f32), ffn2_w_in[li].astype(bf), ffn2_w_out[li].astype(bf),
                      final_norm[None].astype(f32), final_norm=(li == depth - 1))
    return xf.reshape(b, s, d).astype(x.dtype)
```

```python
import functools
import math

import jax
import jax.numpy as jnp
from jax import lax
from jax.experimental import pallas as pl
from jax.experimental.pallas import tpu as pltpu

EPS = 1e-6
NEG = -1e30

CHUNK = 64
LEFT_CHUNKS = 8
REL_CLIP = 128
T5_BUCKETS = 32
T5_MAX_DIST = 128

LANES = 128
SUBLANES = 8
HEAD_DIM = 64
PAIR = 2 * HEAD_DIM
SUB_CHUNKS = LANES // CHUNK

A_BLK = 128
A_TILES = LEFT_CHUNKS * CHUNK // A_BLK + 1
B_BLK = 256
B_TILES = 3

V7X_VMEM_LIMIT = 60000 * 1024

ROW_TILE = 256

_NT = (((1,), (1,)), ((), ()))
_TN = (((0,), (0,)), ((), ()))


def _rms(xf, g):
    return xf * lax.rsqrt(jnp.mean(xf * xf, axis=-1, keepdims=True) + EPS) * g


def _dot(a, b):
    return jnp.dot(a, b, preferred_element_type=jnp.float32)


def _swiglu(h, w_in_ref, w_out_ref):
    d_ff = w_out_ref.shape[0]
    gate = _dot(h, w_in_ref[:, :d_ff])
    up = _dot(h, w_in_ref[:, d_ff:])
    act = (gate * jax.nn.sigmoid(gate) * up).astype(jnp.bfloat16)
    return _dot(act, w_out_ref[...])


def _ffn_mix_kernel(x_ref, g1_ref, w_in_ref, w_out_ref, gm_ref, w_mix_ref, b_gate_ref,
                    x1_ref, qa_ref, ka_ref, va_ref, qb_ref, kb_ref, vb_ref, gates_ref):
    x = x_ref[...]
    h = _rms(x, g1_ref[...]).astype(jnp.bfloat16)
    x1 = x + 0.5 * _swiglu(h, w_in_ref, w_out_ref)
    x1_ref[...] = x1

    u = _rms(x1, gm_ref[...]).astype(jnp.bfloat16)
    width = qa_ref.shape[1]
    scale = HEAD_DIM ** -0.5
    outs = (qa_ref, ka_ref, va_ref, qb_ref, kb_ref, vb_ref)
    for n, o_ref in enumerate(outs):
        p = _dot(u, w_mix_ref[:, n * width:(n + 1) * width])
        if o_ref is qa_ref or o_ref is qb_ref:
            p = p * scale
        o_ref[...] = p.astype(o_ref.dtype)
    logits = _dot(u, w_mix_ref[:, len(outs) * width:]) + b_gate_ref[...]
    gates_ref[...] = jax.nn.sigmoid(logits)


def _out_ffn_kernel(x1_ref, ya_ref, yb_ref, gates_ref, wa_ref, wb_ref, wo_ref,
                    g2_ref, w_in_ref, w_out_ref, gf_ref, o_ref, *, final_norm):
    d = x1_ref.shape[1]
    merged = (gates_ref[:, :d] * _dot(ya_ref[...], wa_ref[...])
              + gates_ref[:, d:] * _dot(yb_ref[...], wb_ref[...]))
    x2 = x1_ref[...] + _dot(merged.astype(jnp.bfloat16), wo_ref[...])
    h = _rms(x2, g2_ref[...]).astype(jnp.bfloat16)
    x3 = x2 + 0.5 * _swiglu(h, w_in_ref, w_out_ref)
    o_ref[...] = _rms(x3, gf_ref[...]) if final_norm else x3


def _resident(shape):
    return pl.BlockSpec(shape, lambda *_: (0,) * len(shape), pipeline_mode=pl.Buffered(1))


def _rows(tm, width):
    return pl.BlockSpec((tm, width), lambda i: (i, 0))


def _ffn_mix(x, g1, w_in, w_out, gm, w_mix, b_gate, width):
    m, d = x.shape
    tm = ROW_TILE
    bf = jnp.bfloat16
    out_shape = ([jax.ShapeDtypeStruct((m, d), jnp.float32)]
                 + [jax.ShapeDtypeStruct((m, width), bf)] * 6
                 + [jax.ShapeDtypeStruct((m, b_gate.shape[1]), jnp.float32)])
    return pl.pallas_call(
        _ffn_mix_kernel,
        grid=(m // tm,),
        in_specs=[_rows(tm, d), _resident(g1.shape), _resident(w_in.shape), _resident(w_out.shape),
                  _resident(gm.shape), _resident(w_mix.shape), _resident(b_gate.shape)],
        out_specs=[_rows(tm, d)] + [_rows(tm, width)] * 6 + [_rows(tm, b_gate.shape[1])],
        out_shape=out_shape,
        compiler_params=pltpu.CompilerParams(dimension_semantics=("arbitrary",),
                                             vmem_limit_bytes=V7X_VMEM_LIMIT),
        name="ffn_mix",
    )(x, g1, w_in, w_out, gm, w_mix, b_gate)


def _out_ffn(x1, ya, yb, gates, wa, wb, wo, g2, w_in, w_out, gf, final_norm):
    m, d = x1.shape
    tm = ROW_TILE
    return pl.pallas_call(
        functools.partial(_out_ffn_kernel, final_norm=final_norm),
        grid=(m // tm,),
        in_specs=[_rows(tm, d), _rows(tm, ya.shape[1]), _rows(tm, yb.shape[1]), _rows(tm, gates.shape[1]),
                  _resident(wa.shape), _resident(wb.shape), _resident(wo.shape), _resident(g2.shape),
                  _resident(w_in.shape), _resident(w_out.shape), _resident(gf.shape)],
        out_specs=_rows(tm, d),
        out_shape=jax.ShapeDtypeStruct((m, d), jnp.float32),
        compiler_params=pltpu.CompilerParams(dimension_semantics=("arbitrary",),
                                             vmem_limit_bytes=V7X_VMEM_LIMIT),
        name="out_ffn",
    )(x1, ya, yb, gates, wa, wb, wo, g2, w_in, w_out, gf)


def _t5_bucket(rel):
    nb = T5_BUCKETS // 2
    ret = jnp.where(rel > 0, nb, 0)
    n = jnp.abs(rel)
    max_exact = nb // 2
    nf = jnp.maximum(n, 1).astype(jnp.float32)
    large = max_exact + (jnp.log(nf / max_exact) / math.log(T5_MAX_DIST / max_exact)
                         * (nb - max_exact)).astype(jnp.int32)
    large = jnp.minimum(large, nb - 1)
    return ret + jnp.where(n < max_exact, n, large)


def _window_rel(n_tiles, blk):
    nb = blk // LANES
    u = jnp.arange(-(nb - 1), n_tiles * nb, dtype=jnp.int32)[:, None]
    return LANES - jnp.arange(2 * LANES, dtype=jnp.int32)[None, :] - LANES * u


def _bias_tiles_kernel(rel_bias_ref, tiles_ref, *, left_chunks):
    n_pairs, n_tiles, blk, _ = tiles_ref.shape
    nb = blk // LANES
    kj = lax.broadcasted_iota(jnp.int32, (LANES, LANES), 0)
    qi = lax.broadcasted_iota(jnp.int32, (LANES, LANES), 1)
    chunk_diff = kj // CHUNK - qi // CHUNK
    for p in range(n_pairs):
        for half in range(2):
            for w in range(rel_bias_ref.shape[2]):
                u = w - (nb - 1)
                row = jnp.broadcast_to(rel_bias_ref[p, half, w], (LANES, 2 * LANES))
                sub = pltpu.roll(row, 0, 1, stride=1, stride_axis=0)[:, LANES:]
                for t in range(n_tiles):
                    for a in range(nb):
                        c = t * nb + a - u
                        if not 0 <= c < nb:
                            continue
                        diff = chunk_diff - u * SUB_CHUNKS
                        allowed = diff <= 0
                        if left_chunks is not None:
                            allowed = allowed & (diff >= -left_chunks)
                        tiles_ref[p, t, pl.ds(c * LANES, LANES), pl.ds(half * blk + a * LANES, LANES)] = (
                            jnp.where(allowed, sub, NEG))


def _bias_tiles(rel_bias, n_tiles, blk, left_chunks):
    n_pairs = rel_bias.shape[0]
    return pl.pallas_call(
        functools.partial(_bias_tiles_kernel, left_chunks=left_chunks),
        out_shape=jax.ShapeDtypeStruct((n_pairs, n_tiles, blk, 2 * blk), jnp.float32),
        compiler_params=pltpu.CompilerParams(vmem_limit_bytes=V7X_VMEM_LIMIT),
        name="bias_tiles",
    )(rel_bias)


def _fold_rows(x):
    m = x[:SUBLANES]
    for r in range(1, x.shape[0] // SUBLANES):
        m = jnp.maximum(m, x[r * SUBLANES:(r + 1) * SUBLANES])
    return m


def _sum_rows(x):
    acc = x[:SUBLANES]
    for r in range(1, x.shape[0] // SUBLANES):
        acc = acc + x[r * SUBLANES:(r + 1) * SUBLANES]
    return acc


def _attn_kernel(*refs, left_blocks, lambda_init):
    if lambda_init is None:
        q_ref, k_ref, v_ref, tiles_ref, o_ref, q2_ref, s_ref, acc_ref = refs
    else:
        (q_ref, k_ref, v_ref, tiles_ref, lq1_ref, lk1_ref, lq2_ref, lk2_ref, subg_ref, o_ref,
         q2_ref, s_ref, acc_ref) = refs
        lam = (jnp.exp(jnp.sum(lq1_ref[...] * lk1_ref[...], axis=-1, keepdims=True))
               - jnp.exp(jnp.sum(lq2_ref[...] * lk2_ref[...], axis=-1, keepdims=True))
               + lambda_init)
    blk = q_ref.shape[1]
    n_pairs = q_ref.shape[2] // PAIR
    n_tiles = tiles_ref.shape[1]
    i = pl.program_id(1)
    lo = jnp.maximum(i - left_blocks, 0) if left_blocks is not None else 0
    low_half = lax.broadcasted_iota(jnp.int32, (blk, PAIR), 1) < HEAD_DIM

    def cols(p):
        return slice(p * PAIR, (p + 1) * PAIR)

    for p in range(n_pairs):
        q = q_ref[0, :, cols(p)]
        zero = jnp.zeros_like(q)
        q2_ref[p, :blk] = jnp.where(low_half, q, zero)
        q2_ref[p, blk:] = jnp.where(low_half, zero, q)
    acc_ref[...] = jnp.zeros(acc_ref.shape, jnp.float32)

    def scores(j, ms):
        rows = pl.ds(pl.multiple_of(j * blk, blk), blk)
        tile = jnp.minimum(i - j, n_tiles - 1)
        out = []
        for p in range(n_pairs):
            s = lax.dot_general(k_ref[0, rows, cols(p)], q2_ref[p], _NT, preferred_element_type=jnp.float32)
            s = s + tiles_ref[p, tile]
            s_ref[j - lo, p] = s
            out.append(jnp.maximum(ms[p], _fold_rows(s)))
        return tuple(out)

    ms = lax.fori_loop(lo, i + 1, scores,
                       tuple(jnp.full((SUBLANES, 2 * blk), NEG, jnp.float32) for _ in range(n_pairs)))
    ms = tuple(jnp.max(m, axis=0, keepdims=True) for m in ms)

    def weighted(j, ls):
        rows = pl.ds(pl.multiple_of(j * blk, blk), blk)
        out = []
        for p in range(n_pairs):
            e = jnp.exp(s_ref[j - lo, p] - ms[p])
            out.append(ls[p] + _sum_rows(e))
            acc_ref[p] += lax.dot_general(v_ref[0, rows, cols(p)], e.astype(jnp.bfloat16), _TN,
                                          preferred_element_type=jnp.float32)
        return tuple(out)

    ls = lax.fori_loop(lo, i + 1, weighted,
                       tuple(jnp.zeros((SUBLANES, 2 * blk), jnp.float32) for _ in range(n_pairs)))

    top_half = lax.broadcasted_iota(jnp.int32, (PAIR, blk), 0) < HEAD_DIM
    for p in range(n_pairs):
        o = acc_ref[p] / jnp.sum(ls[p], axis=0, keepdims=True)
        if lambda_init is None:
            y = jnp.where(top_half, o[:, :blk], o[:, blk:]).T
        else:
            d = o[:, :blk] - lam * o[:, blk:]
            d = d * lax.rsqrt(jnp.mean(d * d, axis=0, keepdims=True) + EPS)
            y = d.T * subg_ref[...] * (1.0 - lambda_init)
        o_ref[0, :, cols(p)] = y.astype(o_ref.dtype)


def _attention(q, k, v, tiles, left_blocks, lambda_init=None, extra=()):
    b, s, w = q.shape
    n_pairs = w // PAIR
    blk = tiles.shape[2]
    n_live = s // blk if left_blocks is None else left_blocks + 1
    blk_q = pl.BlockSpec((1, blk, w), lambda bi, i: (bi, i, 0))
    whole = pl.BlockSpec((1, s, w), lambda bi, i: (bi, 0, 0))
    f32 = jnp.float32
    return pl.pallas_call(
        functools.partial(_attn_kernel, left_blocks=left_blocks, lambda_init=lambda_init),
        grid=(b, s // blk),
        in_specs=[blk_q, whole, whole, _resident(tiles.shape)] + [_resident(e.shape) for e in extra],
        out_specs=blk_q,
        out_shape=jax.ShapeDtypeStruct((b, s, w), jnp.bfloat16),
        scratch_shapes=[pltpu.VMEM((n_pairs, 2 * blk, PAIR), jnp.bfloat16),
                        pltpu.VMEM((n_live, n_pairs, blk, 2 * blk), f32),
                        pltpu.VMEM((n_pairs, PAIR, 2 * blk), f32)],
        compiler_params=pltpu.CompilerParams(dimension_semantics=("arbitrary", "arbitrary"),
                                             vmem_limit_bytes=V7X_VMEM_LIMIT),
        name="attn_a" if lambda_init is None else "attn_b",
    )(q, k, v, tiles, *extra)


def kernel(x, ffn1_norm, ffn1_w_in, ffn1_w_out, mix_norm, w_mix_in, b_gate, rel_bias_a, lambda_q1, lambda_k1, lambda_q2, lambda_k2, subln_g, t5_bias, w_branch_a, w_branch_b, w_o, ffn2_norm, ffn2_w_in, ffn2_w_out, final_norm):
    b, s, d = x.shape
    depth = ffn1_norm.shape[0]
    assert depth >= 1
    width = w_branch_a.shape[1]
    n_pairs = width // PAIR
    bf = jnp.bfloat16
    f32 = jnp.float32

    t5_rel = t5_bias.astype(f32)[:, _t5_bucket(_window_rel(B_TILES, B_BLK))]
    t5_rel = jnp.broadcast_to(t5_rel[:, None, :, None, :], (n_pairs, 2) + t5_rel.shape[1:2] + (1, 2 * LANES))
    tiles_b = _bias_tiles(t5_rel, B_TILES, B_BLK, None)

    xf = x.astype(f32).reshape(b * s, d)
    for li in range(depth):
        x1, qa, ka, va, qb, kb, vb, gates = _ffn_mix(
            xf, ffn1_norm[li][None].astype(f32), ffn1_w_in[li].astype(bf), ffn1_w_out[li].astype(bf),
            mix_norm[li][None].astype(f32), w_mix_in[li].astype(bf), b_gate[li][None].astype(f32), width)

        rel_a = jnp.clip(_window_rel(A_TILES, A_BLK), -REL_CLIP, REL_CLIP) + REL_CLIP
        a_rel = rel_bias_a[li].astype(f32)[:, rel_a]
        tiles_a = _bias_tiles(a_rel.reshape(n_pairs, 2, rel_a.shape[0], 1, 2 * LANES), A_TILES, A_BLK, LEFT_CHUNKS)

        def seq(t):
            return t.reshape(b, s, width)

        ya = _attention(seq(qa), seq(ka), seq(va), tiles_a, A_TILES - 1)
        lambda_init = 0.8 - 0.6 * math.exp(-0.3 * li)
        lams = [t[li][None].astype(f32) for t in (lambda_q1, lambda_k1, lambda_q2, lambda_k2)]
        yb = _attention(seq(qb), seq(kb), seq(vb), tiles_b, None, lambda_init=lambda_init,
                        extra=(*lams, subln_g[li][None].astype(f32)))

        xf = _out_ffn(x1, ya.reshape(b * s, width), yb.reshape(b * s, width), gates,
                      w_branch_a[li].astype(bf), w_branch_b[li].astype(bf), w_o[li].astype(bf),
                      ffn2_norm[li][None].astype(f32), ffn2_w_in[li].astype(bf), ffn2_w_out[li].astype(bf),
                      final_norm[None].astype(f32), final_norm=(li == depth - 1))
    return xf.reshape(b, s, d).astype(x.dtype)
```

```python
import functools
import math

import jax
import jax.numpy as jnp
from jax import lax
from jax.experimental import pallas as pl
from jax.experimental.pallas import tpu as pltpu

EPS = 1e-6
NEG = -1e30

CHUNK = 64
LEFT_CHUNKS = 8
REL_CLIP = 128
T5_BUCKETS = 32
T5_MAX_DIST = 128

LANES = 128
SUBLANES = 8
HEAD_DIM = 64
PAIR = 2 * HEAD_DIM
SUB_CHUNKS = LANES // CHUNK

A_BLK = 128
A_TILES = LEFT_CHUNKS * CHUNK // A_BLK + 1
B_BLK = 256
B_TILES = 3

V7X_VMEM_LIMIT = 60000 * 1024

ROW_TILE = 256

_NT = (((1,), (1,)), ((), ()))
_TN = (((0,), (0,)), ((), ()))


def _rms(xf, g):
    return xf * lax.rsqrt(jnp.mean(xf * xf, axis=-1, keepdims=True) + EPS) * g


def _dot(a, b):
    return jnp.dot(a, b, preferred_element_type=jnp.float32)


def _swiglu(h, w_in_ref, w_out_ref):
    d_ff = w_out_ref.shape[0]
    gate = _dot(h, w_in_ref[:, :d_ff])
    up = _dot(h, w_in_ref[:, d_ff:])
    act = (gate * jax.nn.sigmoid(gate) * up).astype(jnp.bfloat16)
    return _dot(act, w_out_ref[...])


def _ffn_mix_kernel(x_ref, g1_ref, w_in_ref, w_out_ref, gm_ref, w_mix_ref, b_gate_ref,
                    x1_ref, qa_ref, ka_ref, va_ref, qb_ref, kb_ref, vb_ref, gates_ref):
    x = x_ref[...]
    h = _rms(x, g1_ref[...]).astype(jnp.bfloat16)
    x1 = x + 0.5 * _swiglu(h, w_in_ref, w_out_ref)
    x1_ref[...] = x1

    u = _rms(x1, gm_ref[...]).astype(jnp.bfloat16)
    width = qa_ref.shape[1]
    scale = HEAD_DIM ** -0.5
    outs = (qa_ref, ka_ref, va_ref, qb_ref, kb_ref, vb_ref)
    for n, o_ref in enumerate(outs):
        p = _dot(u, w_mix_ref[:, n * width:(n + 1) * width])
        if o_ref is qa_ref or o_ref is qb_ref:
            p = p * scale
        o_ref[...] = p.astype(o_ref.dtype)
    logits = _dot(u, w_mix_ref[:, len(outs) * width:]) + b_gate_ref[...]
    gates_ref[...] = jax.nn.sigmoid(logits)


def _out_ffn_kernel(x1_ref, ya_ref, yb_ref, gates_ref, wa_ref, wb_ref, wo_ref,
                    g2_ref, w_in_ref, w_out_ref, gf_ref, o_ref, *, final_norm):
    d = x1_ref.shape[1]
    merged = (gates_ref[:, :d] * _dot(ya_ref[...], wa_ref[...])
              + gates_ref[:, d:] * _dot(yb_ref[...], wb_ref[...]))
    x2 = x1_ref[...] + _dot(merged.astype(jnp.bfloat16), wo_ref[...])
    h = _rms(x2, g2_ref[...]).astype(jnp.bfloat16)
    x3 = x2 + 0.5 * _swiglu(h, w_in_ref, w_out_ref)
    o_ref[...] = _rms(x3, gf_ref[...]) if final_norm else x3


def _resident(shape):
    return pl.BlockSpec(shape, lambda *_: (0,) * len(shape), pipeline_mode=pl.Buffered(1))


def _rows(tm, width):
    return pl.BlockSpec((tm, width), lambda i: (i, 0))


def _ffn_mix(x, g1, w_in, w_out, gm, w_mix, b_gate, width):
    m, d = x.shape
    tm = ROW_TILE
    bf = jnp.bfloat16
    out_shape = ([jax.ShapeDtypeStruct((m, d), jnp.float32)]
                 + [jax.ShapeDtypeStruct((m, width), bf)] * 6
                 + [jax.ShapeDtypeStruct((m, b_gate.shape[1]), jnp.float32)])
    return pl.pallas_call(
        _ffn_mix_kernel,
        grid=(m // tm,),
        in_specs=[_rows(tm, d), _resident(g1.shape), _resident(w_in.shape), _resident(w_out.shape),
                  _resident(gm.shape), _resident(w_mix.shape), _resident(b_gate.shape)],
        out_specs=[_rows(tm, d)] + [_rows(tm, width)] * 6 + [_rows(tm, b_gate.shape[1])],
        out_shape=out_shape,
        compiler_params=pltpu.CompilerParams(dimension_semantics=("arbitrary",),
                                             vmem_limit_bytes=V7X_VMEM_LIMIT),
        name="ffn_mix",
    )(x, g1, w_in, w_out, gm, w_mix, b_gate)


def _out_ffn(x1, ya, yb, gates, wa, wb, wo, g2, w_in, w_out, gf, final_norm):
    m, d = x1.shape
    tm = ROW_TILE
    return pl.pallas_call(
        functools.partial(_out_ffn_kernel, final_norm=final_norm),
        grid=(m // tm,),
        in_specs=[_rows(tm, d), _rows(tm, ya.shape[1]), _rows(tm, yb.shape[1]), _rows(tm, gates.shape[1]),
                  _resident(wa.shape), _resident(wb.shape), _resident(wo.shape), _resident(g2.shape),
                  _resident(w_in.shape), _resident(w_out.shape), _resident(gf.shape)],
        out_specs=_rows(tm, d),
        out_shape=jax.ShapeDtypeStruct((m, d), jnp.float32),
        compiler_params=pltpu.CompilerParams(dimension_semantics=("arbitrary",),
                                             vmem_limit_bytes=V7X_VMEM_LIMIT),
        name="out_ffn",
    )(x1, ya, yb, gates, wa, wb, wo, g2, w_in, w_out, gf)


def _t5_bucket(rel):
    nb = T5_BUCKETS // 2
    ret = jnp.where(rel > 0, nb, 0)
    n = jnp.abs(rel)
    max_exact = nb // 2
    nf = jnp.maximum(n, 1).astype(jnp.float32)
    large = max_exact + (jnp.log(nf / max_exact) / math.log(T5_MAX_DIST / max_exact)
                         * (nb - max_exact)).astype(jnp.int32)
    large = jnp.minimum(large, nb - 1)
    return ret + jnp.where(n < max_exact, n, large)


def _window_rel(n_tiles, blk):
    nb = blk // LANES
    u = jnp.arange(-(nb - 1), n_tiles * nb, dtype=jnp.int32)[:, None]
    return LANES - jnp.arange(2 * LANES, dtype=jnp.int32)[None, :] - LANES * u


def _bias_tiles_kernel(rel_bias_ref, tiles_ref, *, left_chunks):
    n_pairs, n_tiles, blk, _ = tiles_ref.shape
    if left_chunks is not None:
        n_tiles -= 1
        tiles_ref[:, n_tiles] = jnp.full((n_pairs, blk, 2 * blk), NEG, jnp.float32)
    nb = blk // LANES
    kj = lax.broadcasted_iota(jnp.int32, (LANES, LANES), 0)
    qi = lax.broadcasted_iota(jnp.int32, (LANES, LANES), 1)
    chunk_diff = kj // CHUNK - qi // CHUNK
    for p in range(n_pairs):
        for half in range(2):
            for w in range(rel_bias_ref.shape[2]):
                u = w - (nb - 1)
                row = jnp.broadcast_to(rel_bias_ref[p, half, w], (LANES, 2 * LANES))
                sub = pltpu.roll(row, 0, 1, stride=1, stride_axis=0)[:, LANES:]
                for t in range(n_tiles):
                    for a in range(nb):
                        c = t * nb + a - u
                        if not 0 <= c < nb:
                            continue
                        diff = chunk_diff - u * SUB_CHUNKS
                        allowed = diff <= 0
                        if left_chunks is not None:
                            allowed = allowed & (diff >= -left_chunks)
                        tiles_ref[p, t, pl.ds(c * LANES, LANES), pl.ds(half * blk + a * LANES, LANES)] = (
                            jnp.where(allowed, sub, NEG))


def _bias_tiles(rel_bias, n_tiles, blk, left_chunks):
    n_pairs = rel_bias.shape[0]
    n_out = n_tiles if left_chunks is None else n_tiles + 1
    return pl.pallas_call(
        functools.partial(_bias_tiles_kernel, left_chunks=left_chunks),
        out_shape=jax.ShapeDtypeStruct((n_pairs, n_out, blk, 2 * blk), jnp.float32),
        compiler_params=pltpu.CompilerParams(vmem_limit_bytes=V7X_VMEM_LIMIT),
        name="bias_tiles",
    )(rel_bias)


def _fold_rows(x):
    m = x[:SUBLANES]
    for r in range(1, x.shape[0] // SUBLANES):
        m = jnp.maximum(m, x[r * SUBLANES:(r + 1) * SUBLANES])
    return m


def _sum_rows(x):
    acc = x[:SUBLANES]
    for r in range(1, x.shape[0] // SUBLANES):
        acc = acc + x[r * SUBLANES:(r + 1) * SUBLANES]
    return acc


def _attn_kernel(*refs, left_blocks, lambda_init):
    if lambda_init is None:
        q_ref, k_ref, v_ref, tiles_ref, o_ref, q2_ref, s_ref, acc_ref = refs
    else:
        (q_ref, k_ref, v_ref, tiles_ref, lq1_ref, lk1_ref, lq2_ref, lk2_ref, subg_ref, o_ref,
         q2_ref, s_ref, acc_ref) = refs
        lam = (jnp.exp(jnp.sum(lq1_ref[...] * lk1_ref[...], axis=-1, keepdims=True))
               - jnp.exp(jnp.sum(lq2_ref[...] * lk2_ref[...], axis=-1, keepdims=True))
               + lambda_init)
    blk = q_ref.shape[1]
    n_pairs = q_ref.shape[2] // PAIR
    n_tiles = tiles_ref.shape[1]
    i = pl.program_id(1)
    low_half = lax.broadcasted_iota(jnp.int32, (blk, PAIR), 1) < HEAD_DIM

    def cols(p):
        return slice(p * PAIR, (p + 1) * PAIR)

    for p in range(n_pairs):
        q = q_ref[0, :, cols(p)]
        zero = jnp.zeros_like(q)
        q2_ref[p, :blk] = jnp.where(low_half, q, zero)
        q2_ref[p, blk:] = jnp.where(low_half, zero, q)
    acc_ref[...] = jnp.zeros(acc_ref.shape, jnp.float32)

    def scores(j, slot, tile, ms):
        rows = pl.ds(pl.multiple_of(j * blk, blk), blk)
        out = []
        for p in range(n_pairs):
            s = lax.dot_general(k_ref[0, rows, cols(p)], q2_ref[p], _NT, preferred_element_type=jnp.float32)
            s = s + tiles_ref[p, tile]
            s_ref[slot, p] = s
            out.append(jnp.maximum(ms[p], _fold_rows(s)))
        return tuple(out)

    def weighted(j, slot, ms, ls):
        rows = pl.ds(pl.multiple_of(j * blk, blk), blk)
        out = []
        for p in range(n_pairs):
            e = jnp.exp(s_ref[slot, p] - ms[p])
            out.append(ls[p] + _sum_rows(e))
            acc_ref[p] += lax.dot_general(v_ref[0, rows, cols(p)], e.astype(jnp.bfloat16), _TN,
                                          preferred_element_type=jnp.float32)
        return tuple(out)

    ms = tuple(jnp.full((SUBLANES, 2 * blk), NEG, jnp.float32) for _ in range(n_pairs))
    ls = tuple(jnp.zeros((SUBLANES, 2 * blk), jnp.float32) for _ in range(n_pairs))
    if left_blocks is None:
        ms = lax.fori_loop(0, i + 1, lambda j, ms: scores(j, j, jnp.minimum(i - j, n_tiles - 1), ms), ms)
        ms = tuple(jnp.max(m, axis=0, keepdims=True) for m in ms)
        ls = lax.fori_loop(0, i + 1, lambda j, ls: weighted(j, j, ms, ls), ls)
    else:
        band = [(jnp.maximum(i - d, 0), left_blocks - d, jnp.where(i >= d, d, n_tiles - 1))
                for d in range(left_blocks, -1, -1)]
        for j, slot, tile in band:
            ms = scores(j, slot, tile, ms)
        ms = tuple(jnp.max(m, axis=0, keepdims=True) for m in ms)
        for j, slot, _ in band:
            ls = weighted(j, slot, ms, ls)

    top_half = lax.broadcasted_iota(jnp.int32, (PAIR, blk), 0) < HEAD_DIM
    for p in range(n_pairs):
        o = acc_ref[p] / jnp.sum(ls[p], axis=0, keepdims=True)
        if lambda_init is None:
            y = jnp.where(top_half, o[:, :blk], o[:, blk:]).T
        else:
            d = o[:, :blk] - lam * o[:, blk:]
            d = d * lax.rsqrt(jnp.mean(d * d, axis=0, keepdims=True) + EPS)
            y = d.T * subg_ref[...] * (1.0 - lambda_init)
        o_ref[0, :, cols(p)] = y.astype(o_ref.dtype)


def _attention(q, k, v, tiles, left_blocks, lambda_init=None, extra=()):
    b, s, w = q.shape
    n_pairs = w // PAIR
    blk = tiles.shape[2]
    n_live = s // blk if left_blocks is None else left_blocks + 1
    blk_q = pl.BlockSpec((1, blk, w), lambda bi, i: (bi, i, 0))
    whole = pl.BlockSpec((1, s, w), lambda bi, i: (bi, 0, 0))
    f32 = jnp.float32
    return pl.pallas_call(
        functools.partial(_attn_kernel, left_blocks=left_blocks, lambda_init=lambda_init),
        grid=(b, s // blk),
        in_specs=[blk_q, whole, whole, _resident(tiles.shape)] + [_resident(e.shape) for e in extra],
        out_specs=blk_q,
        out_shape=jax.ShapeDtypeStruct((b, s, w), jnp.bfloat16),
        scratch_shapes=[pltpu.VMEM((n_pairs, 2 * blk, PAIR), jnp.bfloat16),
                        pltpu.VMEM((n_live, n_pairs, blk, 2 * blk), f32),
                        pltpu.VMEM((n_pairs, PAIR, 2 * blk), f32)],
        compiler_params=pltpu.CompilerParams(dimension_semantics=("arbitrary", "arbitrary"),
                                             vmem_limit_bytes=V7X_VMEM_LIMIT),
        name="attn_a" if lambda_init is None else "attn_b",
    )(q, k, v, tiles, *extra)


def kernel(x, ffn1_norm, ffn1_w_in, ffn1_w_out, mix_norm, w_mix_in, b_gate, rel_bias_a, lambda_q1, lambda_k1, lambda_q2, lambda_k2, subln_g, t5_bias, w_branch_a, w_branch_b, w_o, ffn2_norm, ffn2_w_in, ffn2_w_out, final_norm):
    b, s, d = x.shape
    depth = ffn1_norm.shape[0]
    assert depth >= 1
    width = w_branch_a.shape[1]
    n_pairs = width // PAIR
    bf = jnp.bfloat16
    f32 = jnp.float32

    t5_rel = t5_bias.astype(f32)[:, _t5_bucket(_window_rel(B_TILES, B_BLK))]
    t5_rel = jnp.broadcast_to(t5_rel[:, None, :, None, :], (n_pairs, 2) + t5_rel.shape[1:2] + (1, 2 * LANES))
    tiles_b = _bias_tiles(t5_rel, B_TILES, B_BLK, None)

    xf = x.astype(f32).reshape(b * s, d)
    for li in range(depth):
        x1, qa, ka, va, qb, kb, vb, gates = _ffn_mix(
            xf, ffn1_norm[li][None].astype(f32), ffn1_w_in[li].astype(bf), ffn1_w_out[li].astype(bf),
            mix_norm[li][None].astype(f32), w_mix_in[li].astype(bf), b_gate[li][None].astype(f32), width)

        rel_a = jnp.clip(_window_rel(A_TILES, A_BLK), -REL_CLIP, REL_CLIP) + REL_CLIP
        a_rel = rel_bias_a[li].astype(f32)[:, rel_a]
        tiles_a = _bias_tiles(a_rel.reshape(n_pairs, 2, rel_a.shape[0], 1, 2 * LANES), A_TILES, A_BLK, LEFT_CHUNKS)

        def seq(t):
            return t.reshape(b, s, width)

        ya = _attention(seq(qa), seq(ka), seq(va), tiles_a, A_TILES - 1)
        lambda_init = 0.8 - 0.6 * math.exp(-0.3 * li)
        lams = [t[li][None].astype(f32) for t in (lambda_q1, lambda_k1, lambda_q2, lambda_k2)]
        yb = _attention(seq(qb), seq(kb), seq(vb), tiles_b, None, lambda_init=lambda_init,
                        extra=(*lams, subln_g[li][None].astype(f32)))

        xf = _out_ffn(x1, ya.reshape(b * s, width), yb.reshape(b * s, width), gates,
                      w_branch_a[li].astype(bf), w_branch_b[li].astype(bf), w_o[li].astype(bf),
                      ffn2_norm[li][None].astype(f32), ffn2_w_in[li].astype(bf), ffn2_w_out[li].astype(bf),
                      final_norm[None].astype(f32), final_norm=(li == depth - 1))
    return xf.reshape(b, s, d).astype(x.dtype)
```

```python
import functools
import math

import jax
import jax.numpy as jnp
from jax import lax
from jax.experimental import pallas as pl
from jax.experimental.pallas import tpu as pltpu

EPS = 1e-6
NEG = -1e30
LOG2E = math.log2(math.e)

CHUNK = 64
LEFT_CHUNKS = 8
REL_CLIP = 128
T5_BUCKETS = 32
T5_MAX_DIST = 128

LANES = 128
SUBLANES = 8
HEAD_DIM = 64
PAIR = 2 * HEAD_DIM
SUB_CHUNKS = LANES // CHUNK
ACC_ROWS = PAIR + SUBLANES

A_BLK = 128
A_TILES = LEFT_CHUNKS * CHUNK // A_BLK + 1
B_BLK = 256
B_TILES = 3

V7X_VMEM_LIMIT = 60000 * 1024

ROW_TILE = 256

_NT = (((1,), (1,)), ((), ()))


def _rms(xf, g):
    return xf * lax.rsqrt(jnp.mean(xf * xf, axis=-1, keepdims=True) + EPS) * g


def _dot(a, b):
    return jnp.dot(a, b, preferred_element_type=jnp.float32)


def _swiglu(h, w_in_ref, w_out_ref):
    d_ff = w_out_ref.shape[0]
    gate = _dot(h, w_in_ref[:, :d_ff])
    up = _dot(h, w_in_ref[:, d_ff:])
    act = (gate * jax.nn.sigmoid(gate) * up).astype(jnp.bfloat16)
    return _dot(act, w_out_ref[...])


def _ffn_mix_kernel(x_ref, g1_ref, w_in_ref, w_out_ref, gm_ref, w_mix_ref, b_gate_ref,
                    x1_ref, qa_ref, ka_ref, va_ref, qb_ref, kb_ref, vb_ref, gates_ref):
    x = x_ref[...]
    h = _rms(x, g1_ref[...]).astype(jnp.bfloat16)
    x1 = x + 0.5 * _swiglu(h, w_in_ref, w_out_ref)
    x1_ref[...] = x1

    u = _rms(x1, gm_ref[...]).astype(jnp.bfloat16)
    width = qa_ref.shape[1]
    scale = HEAD_DIM ** -0.5 * LOG2E
    outs = (qa_ref, ka_ref, va_ref, qb_ref, kb_ref, vb_ref)
    for n, o_ref in enumerate(outs):
        p = _dot(u, w_mix_ref[:, n * width:(n + 1) * width])
        if o_ref is va_ref or o_ref is vb_ref:
            _store_values_transposed(p, o_ref)
            continue
        if o_ref is qa_ref or o_ref is qb_ref:
            p = p * scale
        o_ref[...] = p.astype(o_ref.dtype)
    logits = _dot(u, w_mix_ref[:, len(outs) * width:]) + b_gate_ref[...]
    gates_ref[...] = jax.nn.sigmoid(logits)


def _store_values_transposed(p, o_ref):
    n_blk, n_pairs, _, blk = o_ref.shape[1:]
    pt = p.T
    for c in range(n_blk):
        for pr in range(n_pairs):
            o_ref[0, c, pr] = pt[pr * PAIR:(pr + 1) * PAIR, c * blk:(c + 1) * blk].astype(o_ref.dtype)


def _out_ffn_kernel(x1_ref, ya_ref, yb_ref, gates_ref, wa_ref, wb_ref, wo_ref,
                    g2_ref, w_in_ref, w_out_ref, gf_ref, o_ref, *, final_norm):
    d = x1_ref.shape[1]
    merged = (gates_ref[:, :d] * _dot(ya_ref[...], wa_ref[...])
              + gates_ref[:, d:] * _dot(yb_ref[...], wb_ref[...]))
    x2 = x1_ref[...] + _dot(merged.astype(jnp.bfloat16), wo_ref[...])
    h = _rms(x2, g2_ref[...]).astype(jnp.bfloat16)
    x3 = x2 + 0.5 * _swiglu(h, w_in_ref, w_out_ref)
    o_ref[...] = _rms(x3, gf_ref[...]) if final_norm else x3


def _resident(shape):
    return pl.BlockSpec(shape, lambda *_: (0,) * len(shape), pipeline_mode=pl.Buffered(1))


def _rows(tm, width):
    return pl.BlockSpec((tm, width), lambda i: (i, 0))


def _ffn_mix(x, g1, w_in, w_out, gm, w_mix, b_gate, width, seq):
    m, d = x.shape
    tm = ROW_TILE
    bf = jnp.bfloat16
    n_pairs = width // PAIR
    steps_per_seq = seq // tm

    def values_t(blk):
        shape = (m // seq, seq // blk, n_pairs, PAIR, blk)
        spec = pl.BlockSpec((1, tm // blk) + shape[2:], lambda i: (i // steps_per_seq, i % steps_per_seq, 0, 0, 0))
        return jax.ShapeDtypeStruct(shape, bf), spec

    qk = jax.ShapeDtypeStruct((m, width), bf), _rows(tm, width)
    outs = [(jax.ShapeDtypeStruct((m, d), jnp.float32), _rows(tm, d)),
            qk, qk, values_t(A_BLK), qk, qk, values_t(B_BLK),
            (jax.ShapeDtypeStruct((m, b_gate.shape[1]), jnp.float32), _rows(tm, b_gate.shape[1]))]
    return pl.pallas_call(
        _ffn_mix_kernel,
        grid=(m // tm,),
        in_specs=[_rows(tm, d), _resident(g1.shape), _resident(w_in.shape), _resident(w_out.shape),
                  _resident(gm.shape), _resident(w_mix.shape), _resident(b_gate.shape)],
        out_specs=[spec for _, spec in outs],
        out_shape=[shape for shape, _ in outs],
        compiler_params=pltpu.CompilerParams(dimension_semantics=("arbitrary",),
                                             vmem_limit_bytes=V7X_VMEM_LIMIT),
        name="ffn_mix",
    )(x, g1, w_in, w_out, gm, w_mix, b_gate)


def _out_ffn(x1, ya, yb, gates, wa, wb, wo, g2, w_in, w_out, gf, final_norm):
    m, d = x1.shape
    tm = ROW_TILE
    return pl.pallas_call(
        functools.partial(_out_ffn_kernel, final_norm=final_norm),
        grid=(m // tm,),
        in_specs=[_rows(tm, d), _rows(tm, ya.shape[1]), _rows(tm, yb.shape[1]), _rows(tm, gates.shape[1]),
                  _resident(wa.shape), _resident(wb.shape), _resident(wo.shape), _resident(g2.shape),
                  _resident(w_in.shape), _resident(w_out.shape), _resident(gf.shape)],
        out_specs=_rows(tm, d),
        out_shape=jax.ShapeDtypeStruct((m, d), jnp.float32),
        compiler_params=pltpu.CompilerParams(dimension_semantics=("arbitrary",),
                                             vmem_limit_bytes=V7X_VMEM_LIMIT),
        name="out_ffn",
    )(x1, ya, yb, gates, wa, wb, wo, g2, w_in, w_out, gf)


def _t5_bucket(rel):
    nb = T5_BUCKETS // 2
    ret = jnp.where(rel > 0, nb, 0)
    n = jnp.abs(rel)
    max_exact = nb // 2
    nf = jnp.maximum(n, 1).astype(jnp.float32)
    large = max_exact + (jnp.log(nf / max_exact) / math.log(T5_MAX_DIST / max_exact)
                         * (nb - max_exact)).astype(jnp.int32)
    large = jnp.minimum(large, nb - 1)
    return ret + jnp.where(n < max_exact, n, large)


def _window_rel(n_tiles, blk):
    nb = blk // LANES
    u = jnp.arange(-(nb - 1), n_tiles * nb, dtype=jnp.int32)[:, None]
    return LANES - jnp.arange(2 * LANES, dtype=jnp.int32)[None, :] - LANES * u


def _bias_tiles_kernel(rel_bias_ref, tiles_ref, *, left_chunks):
    n_pairs, n_tiles, blk, _ = tiles_ref.shape
    if left_chunks is not None:
        n_tiles -= 1
        tiles_ref[:, n_tiles] = jnp.full((n_pairs, blk, 2 * blk), NEG, jnp.float32)
    nb = blk // LANES
    kj = lax.broadcasted_iota(jnp.int32, (LANES, LANES), 0)
    qi = lax.broadcasted_iota(jnp.int32, (LANES, LANES), 1)
    chunk_diff = kj // CHUNK - qi // CHUNK
    for p in range(n_pairs):
        for half in range(2):
            for w in range(rel_bias_ref.shape[2]):
                u = w - (nb - 1)
                row = jnp.broadcast_to(rel_bias_ref[p, half, w], (LANES, 2 * LANES))
                sub = pltpu.roll(row, 0, 1, stride=1, stride_axis=0)[:, LANES:]
                for t in range(n_tiles):
                    for a in range(nb):
                        c = t * nb + a - u
                        if not 0 <= c < nb:
                            continue
                        diff = chunk_diff - u * SUB_CHUNKS
                        allowed = diff <= 0
                        if left_chunks is not None:
                            allowed = allowed & (diff >= -left_chunks)
                        tiles_ref[p, t, pl.ds(c * LANES, LANES), pl.ds(half * blk + a * LANES, LANES)] = (
                            jnp.where(allowed, sub * LOG2E, NEG))


def _bias_tiles(rel_bias, n_tiles, blk, left_chunks):
    n_pairs = rel_bias.shape[0]
    n_out = n_tiles if left_chunks is None else n_tiles + 1
    return pl.pallas_call(
        functools.partial(_bias_tiles_kernel, left_chunks=left_chunks),
        out_shape=jax.ShapeDtypeStruct((n_pairs, n_out, blk, 2 * blk), jnp.float32),
        compiler_params=pltpu.CompilerParams(vmem_limit_bytes=V7X_VMEM_LIMIT),
        name="bias_tiles",
    )(rel_bias)


def _sum_rows(x):
    acc = x[:SUBLANES]
    for r in range(1, x.shape[0] // SUBLANES):
        acc = acc + x[r * SUBLANES:(r + 1) * SUBLANES]
    return acc


def _fold_rows(x):
    m = x[:SUBLANES]
    for r in range(1, x.shape[0] // SUBLANES):
        m = jnp.maximum(m, x[r * SUBLANES:(r + 1) * SUBLANES])
    return m


def _attn_kernel(*refs, left_blocks, lambda_init):
    if lambda_init is None:
        q_ref, k_ref, v_ref, tiles_ref, o_ref, q2_ref, s_ref, acc_ref = refs
    else:
        (q_ref, k_ref, v_ref, tiles_ref, lq1_ref, lk1_ref, lq2_ref, lk2_ref, subg_ref, o_ref,
         q2_ref, s_even_ref, s_odd_ref, m_ref, acc_ref) = refs
        lam = (jnp.exp(jnp.sum(lq1_ref[...] * lk1_ref[...], axis=-1, keepdims=True))
               - jnp.exp(jnp.sum(lq2_ref[...] * lk2_ref[...], axis=-1, keepdims=True))
               + lambda_init)
    blk = q_ref.shape[1]
    n_pairs = q_ref.shape[2] // PAIR
    n_tiles = tiles_ref.shape[1]
    i = pl.program_id(1)
    low_half = lax.broadcasted_iota(jnp.int32, (blk, PAIR), 1) < HEAD_DIM

    def cols(p):
        return slice(p * PAIR, (p + 1) * PAIR)

    for p in range(n_pairs):
        q = q_ref[0, :, cols(p)]
        zero = jnp.zeros_like(q)
        q2_ref[p, :blk] = jnp.where(low_half, q, zero)
        q2_ref[p, blk:] = jnp.where(low_half, zero, q)
    acc_ref[...] = jnp.zeros(acc_ref.shape, jnp.float32)

    def key_rows(j):
        return pl.ds(pl.multiple_of(j * blk, blk), blk)

    def biased_scores(j, tile, p):
        s = lax.dot_general(k_ref[0, key_rows(j), cols(p)], q2_ref[p], _NT, preferred_element_type=jnp.float32)
        return s + tiles_ref[p, tile]

    def weigh(j, p, e):
        return jnp.concatenate([_dot(v_ref[0, j, p], e.astype(jnp.bfloat16)), _sum_rows(e)], axis=0)

    def scores(j, slot, tile, ms):
        out = []
        for p in range(n_pairs):
            s = biased_scores(j, tile, p)
            s_ref[slot, p] = s
            out.append(jnp.maximum(ms[p], _fold_rows(s)))
        return tuple(out)

    def weighted(j, slot, ms):
        for p in range(n_pairs):
            acc_ref[p] += weigh(j, p, jnp.exp2(s_ref[slot, p] - ms[p]))

    def prefetch_scores(j, buf_ref):
        tile = jnp.minimum(i - j, n_tiles - 1)
        for p in range(n_pairs):
            buf_ref[p] = biased_scores(j, tile, p)

    def online(j, buf_ref):
        for p in range(n_pairs):
            s = buf_ref[p]
            m_old = m_ref[p]
            m = jnp.maximum(m_old, jnp.max(_fold_rows(s), axis=0, keepdims=True))
            acc_ref[p] = jnp.exp2(m_old - m) * acc_ref[p] + weigh(j, p, jnp.exp2(s - m))
            m_ref[p] = m

    if left_blocks is None:
        n_blocks = i + 1
        m_ref[...] = jnp.full(m_ref.shape, NEG, jnp.float32)

        def two_blocks(t, carry):
            j = 2 * t
            prefetch_scores(j + 1, s_odd_ref)
            online(j, s_even_ref)
            prefetch_scores(jnp.minimum(j + 2, i), s_even_ref)
            online(j + 1, s_odd_ref)
            return carry

        prefetch_scores(0, s_even_ref)
        lax.fori_loop(0, n_blocks // 2, two_blocks, 0)

        @pl.when(n_blocks % 2 == 1)
        def _():
            online(i, s_even_ref)
    else:
        ms = tuple(jnp.full((SUBLANES, 2 * blk), NEG, jnp.float32) for _ in range(n_pairs))
        band = [(jnp.maximum(i - d, 0), left_blocks - d, jnp.where(i >= d, d, n_tiles - 1))
                for d in range(left_blocks, -1, -1)]
        for j, slot, tile in band:
            ms = scores(j, slot, tile, ms)
        ms = tuple(jnp.max(m, axis=0, keepdims=True) for m in ms)
        for j, slot, _ in band:
            weighted(j, slot, ms)

    top_half = lax.broadcasted_iota(jnp.int32, (PAIR, blk), 0) < HEAD_DIM
    for p in range(n_pairs):
        o = acc_ref[p, :PAIR] / jnp.sum(acc_ref[p, PAIR:], axis=0, keepdims=True)
        if lambda_init is None:
            y = jnp.where(top_half, o[:, :blk], o[:, blk:]).T
        else:
            d = o[:, :blk] - lam * o[:, blk:]
            d = d * lax.rsqrt(jnp.mean(d * d, axis=0, keepdims=True) + EPS)
            y = d.T * subg_ref[...] * (1.0 - lambda_init)
        o_ref[0, :, cols(p)] = y.astype(o_ref.dtype)


def _attention(q, k, v, tiles, left_blocks, lambda_init=None, extra=()):
    b, s, w = q.shape
    n_pairs = w // PAIR
    blk = tiles.shape[2]
    blk_q = pl.BlockSpec((1, blk, w), lambda bi, i: (bi, i, 0))
    whole = pl.BlockSpec((1, s, w), lambda bi, i: (bi, 0, 0))
    whole_v = pl.BlockSpec((1,) + v.shape[1:], lambda bi, i: (bi, 0, 0, 0, 0))
    f32 = jnp.float32
    score_block = (n_pairs, blk, 2 * blk)
    if left_blocks is None:
        score_scratch = [pltpu.VMEM(score_block, f32)] * 2 + [pltpu.VMEM((n_pairs, 1, 2 * blk), f32)]
    else:
        score_scratch = [pltpu.VMEM((left_blocks + 1,) + score_block, f32)]
    return pl.pallas_call(
        functools.partial(_attn_kernel, left_blocks=left_blocks, lambda_init=lambda_init),
        grid=(b, s // blk),
        in_specs=[blk_q, whole, whole_v, _resident(tiles.shape)] + [_resident(e.shape) for e in extra],
        out_specs=blk_q,
        out_shape=jax.ShapeDtypeStruct((b, s, w), jnp.bfloat16),
        scratch_shapes=[pltpu.VMEM((n_pairs, 2 * blk, PAIR), jnp.bfloat16), *score_scratch,
                        pltpu.VMEM((n_pairs, ACC_ROWS, 2 * blk), f32)],
        compiler_params=pltpu.CompilerParams(dimension_semantics=("arbitrary", "arbitrary"),
                                             vmem_limit_bytes=V7X_VMEM_LIMIT),
        name="attn_a" if lambda_init is None else "attn_b",
    )(q, k, v, tiles, *extra)


def kernel(x, ffn1_norm, ffn1_w_in, ffn1_w_out, mix_norm, w_mix_in, b_gate, rel_bias_a, lambda_q1, lambda_k1, lambda_q2, lambda_k2, subln_g, t5_bias, w_branch_a, w_branch_b, w_o, ffn2_norm, ffn2_w_in, ffn2_w_out, final_norm):
    b, s, d = x.shape
    depth = ffn1_norm.shape[0]
    assert depth >= 1
    width = w_branch_a.shape[1]
    n_pairs = width // PAIR
    bf = jnp.bfloat16
    f32 = jnp.float32

    t5_rel = t5_bias.astype(f32)[:, _t5_bucket(_window_rel(B_TILES, B_BLK))]
    t5_rel = jnp.broadcast_to(t5_rel[:, None, :, None, :], (n_pairs, 2) + t5_rel.shape[1:2] + (1, 2 * LANES))
    tiles_b = _bias_tiles(t5_rel, B_TILES, B_BLK, None)

    xf = x.astype(f32).reshape(b * s, d)
    for li in range(depth):
        x1, qa, ka, va, qb, kb, vb, gates = _ffn_mix(
            xf, ffn1_norm[li][None].astype(f32), ffn1_w_in[li].astype(bf), ffn1_w_out[li].astype(bf),
            mix_norm[li][None].astype(f32), w_mix_in[li].astype(bf), b_gate[li][None].astype(f32), width, s)

        rel_a = jnp.clip(_window_rel(A_TILES, A_BLK), -REL_CLIP, REL_CLIP) + REL_CLIP
        a_rel = rel_bias_a[li].astype(f32)[:, rel_a]
        tiles_a = _bias_tiles(a_rel.reshape(n_pairs, 2, rel_a.shape[0], 1, 2 * LANES), A_TILES, A_BLK, LEFT_CHUNKS)

        def seq(t):
            return t.reshape(b, s, width)

        ya = _attention(seq(qa), seq(ka), va, tiles_a, A_TILES - 1)
        lambda_init = 0.8 - 0.6 * math.exp(-0.3 * li)
        lams = [t[li][None].astype(f32) for t in (lambda_q1, lambda_k1, lambda_q2, lambda_k2)]
        yb = _attention(seq(qb), seq(kb), vb, tiles_b, None, lambda_init=lambda_init,
                        extra=(*lams, subln_g[li][None].astype(f32)))

        xf = _out_ffn(x1, ya.reshape(b * s, width), yb.reshape(b * s, width), gates,
                      w_branch_a[li].astype(bf), w_branch_b[li].astype(bf), w_o[li].astype(bf),
                      ffn2_norm[li][None].astype(f32), ffn2_w_in[li].astype(bf), ffn2_w_out[li].astype(bf),
                      final_norm[None].astype(f32), final_norm=(li == depth - 1))
    return xf.reshape(b, s, d).astype(x.dtype)
```

```python
import functools
import math

import jax
import jax.numpy as jnp
from jax import lax
from jax.experimental import pallas as pl
from jax.experimental.pallas import tpu as pltpu

EPS = 1e-6
NEG = -1e30
LOG2E = math.log2(math.e)

CHUNK = 64
LEFT_CHUNKS = 8
REL_CLIP = 128
T5_BUCKETS = 32
T5_MAX_DIST = 128

LANES = 128
SUBLANES = 8
HEAD_DIM = 64
PAIR = 2 * HEAD_DIM
SUB_CHUNKS = LANES // CHUNK
ACC_ROWS = PAIR + SUBLANES
E_CHUNK = 32

A_BLK = 128
A_TILES = LEFT_CHUNKS * CHUNK // A_BLK + 1
B_BLK = 256
B_TILES = 3

V7X_VMEM_LIMIT = 60000 * 1024

ROW_TILE = 512

_NT = (((1,), (1,)), ((), ()))


def _rms(xf, g):
    return xf * lax.rsqrt(jnp.mean(xf * xf, axis=-1, keepdims=True) + EPS) * g


def _dot(a, b):
    return jnp.dot(a, b, preferred_element_type=jnp.float32)


def _swiglu(h, w_in_ref, w_out_ref):
    d_ff = w_out_ref.shape[0]
    gate = _dot(h, w_in_ref[:, :d_ff])
    up = _dot(h, w_in_ref[:, d_ff:])
    act = (gate * jax.nn.sigmoid(gate) * up).astype(jnp.bfloat16)
    return _dot(act, w_out_ref[...])


def _ffn_mix_kernel(x_ref, g1_ref, w_in_ref, w_out_ref, gm_ref, w_mix_ref, b_gate_ref,
                    x1_ref, qa_ref, ka_ref, va_ref, qb_ref, kb_ref, vb_ref, gates_ref):
    x = x_ref[...]
    h = _rms(x, g1_ref[...]).astype(jnp.bfloat16)
    x1 = x + 0.5 * _swiglu(h, w_in_ref, w_out_ref)
    x1_ref[...] = x1

    u = _rms(x1, gm_ref[...]).astype(jnp.bfloat16)
    width = qa_ref.shape[1]
    scale = HEAD_DIM ** -0.5 * LOG2E
    outs = (qa_ref, ka_ref, va_ref, qb_ref, kb_ref, vb_ref)
    for n, o_ref in enumerate(outs):
        p = _dot(u, w_mix_ref[:, n * width:(n + 1) * width])
        if o_ref is va_ref or o_ref is vb_ref:
            _store_values_transposed(p, o_ref)
            continue
        if o_ref is qa_ref or o_ref is qb_ref:
            p = p * scale
        o_ref[...] = p.astype(o_ref.dtype)
    logits = _dot(u, w_mix_ref[:, len(outs) * width:]) + b_gate_ref[...]
    gates_ref[...] = jax.nn.sigmoid(logits)


def _store_values_transposed(p, o_ref):
    n_blk, n_pairs, _, blk = o_ref.shape[1:]
    pt = p.T
    for c in range(n_blk):
        for pr in range(n_pairs):
            o_ref[0, c, pr] = pt[pr * PAIR:(pr + 1) * PAIR, c * blk:(c + 1) * blk].astype(o_ref.dtype)


def _out_ffn_kernel(x1_ref, ya_ref, yb_ref, gates_ref, wa_ref, wb_ref, wo_ref,
                    g2_ref, w_in_ref, w_out_ref, gf_ref, o_ref, *, final_norm):
    d = x1_ref.shape[1]
    merged = (gates_ref[:, :d] * _dot(ya_ref[...], wa_ref[...])
              + gates_ref[:, d:] * _dot(yb_ref[...], wb_ref[...]))
    x2 = x1_ref[...] + _dot(merged.astype(jnp.bfloat16), wo_ref[...])
    h = _rms(x2, g2_ref[...]).astype(jnp.bfloat16)
    x3 = x2 + 0.5 * _swiglu(h, w_in_ref, w_out_ref)
    o_ref[...] = _rms(x3, gf_ref[...]) if final_norm else x3


def _resident(shape):
    return pl.BlockSpec(shape, lambda *_: (0,) * len(shape), pipeline_mode=pl.Buffered(1))


def _rows(tm, width):
    return pl.BlockSpec((tm, width), lambda i: (i, 0))


def _ffn_mix(x, g1, w_in, w_out, gm, w_mix, b_gate, width, seq):
    m, d = x.shape
    tm = ROW_TILE
    bf = jnp.bfloat16
    n_pairs = width // PAIR
    steps_per_seq = seq // tm

    def values_t(blk):
        shape = (m // seq, seq // blk, n_pairs, PAIR, blk)
        spec = pl.BlockSpec((1, tm // blk) + shape[2:], lambda i: (i // steps_per_seq, i % steps_per_seq, 0, 0, 0))
        return jax.ShapeDtypeStruct(shape, bf), spec

    qk = jax.ShapeDtypeStruct((m, width), bf), _rows(tm, width)
    outs = [(jax.ShapeDtypeStruct((m, d), jnp.float32), _rows(tm, d)),
            qk, qk, values_t(A_BLK), qk, qk, values_t(B_BLK),
            (jax.ShapeDtypeStruct((m, b_gate.shape[1]), jnp.float32), _rows(tm, b_gate.shape[1]))]
    return pl.pallas_call(
        _ffn_mix_kernel,
        grid=(m // tm,),
        in_specs=[_rows(tm, d), _resident(g1.shape), _resident(w_in.shape), _resident(w_out.shape),
                  _resident(gm.shape), _resident(w_mix.shape), _resident(b_gate.shape)],
        out_specs=[spec for _, spec in outs],
        out_shape=[shape for shape, _ in outs],
        compiler_params=pltpu.CompilerParams(dimension_semantics=("arbitrary",),
                                             vmem_limit_bytes=V7X_VMEM_LIMIT),
        name="ffn_mix",
    )(x, g1, w_in, w_out, gm, w_mix, b_gate)


def _out_ffn(x1, ya, yb, gates, wa, wb, wo, g2, w_in, w_out, gf, final_norm):
    m, d = x1.shape
    tm = ROW_TILE
    return pl.pallas_call(
        functools.partial(_out_ffn_kernel, final_norm=final_norm),
        grid=(m // tm,),
        in_specs=[_rows(tm, d), _rows(tm, ya.shape[1]), _rows(tm, yb.shape[1]), _rows(tm, gates.shape[1]),
                  _resident(wa.shape), _resident(wb.shape), _resident(wo.shape), _resident(g2.shape),
                  _resident(w_in.shape), _resident(w_out.shape), _resident(gf.shape)],
        out_specs=_rows(tm, d),
        out_shape=jax.ShapeDtypeStruct((m, d), jnp.float32),
        compiler_params=pltpu.CompilerParams(dimension_semantics=("arbitrary",),
                                             vmem_limit_bytes=V7X_VMEM_LIMIT),
        name="out_ffn",
    )(x1, ya, yb, gates, wa, wb, wo, g2, w_in, w_out, gf)


def _t5_bucket(rel):
    nb = T5_BUCKETS // 2
    ret = jnp.where(rel > 0, nb, 0)
    n = jnp.abs(rel)
    max_exact = nb // 2
    nf = jnp.maximum(n, 1).astype(jnp.float32)
    large = max_exact + (jnp.log(nf / max_exact) / math.log(T5_MAX_DIST / max_exact)
                         * (nb - max_exact)).astype(jnp.int32)
    large = jnp.minimum(large, nb - 1)
    return ret + jnp.where(n < max_exact, n, large)


def _window_rel(n_tiles, blk):
    nb = blk // LANES
    u = jnp.arange(-(nb - 1), n_tiles * nb, dtype=jnp.int32)[:, None]
    return LANES - jnp.arange(2 * LANES, dtype=jnp.int32)[None, :] - LANES * u


def _bias_tiles_kernel(rel_bias_ref, tiles_ref, *, left_chunks):
    n_pairs, n_tiles, blk, _ = tiles_ref.shape
    if left_chunks is not None:
        n_tiles -= 1
        tiles_ref[:, n_tiles] = jnp.full((n_pairs, blk, 2 * blk), NEG, jnp.float32)
    nb = blk // LANES
    kj = lax.broadcasted_iota(jnp.int32, (LANES, LANES), 0)
    qi = lax.broadcasted_iota(jnp.int32, (LANES, LANES), 1)
    chunk_diff = kj // CHUNK - qi // CHUNK
    for p in range(n_pairs):
        for half in range(2):
            for w in range(rel_bias_ref.shape[2]):
                u = w - (nb - 1)
                row = jnp.broadcast_to(rel_bias_ref[p, half, w], (LANES, 2 * LANES))
                sub = pltpu.roll(row, 0, 1, stride=1, stride_axis=0)[:, LANES:]
                for t in range(n_tiles):
                    for a in range(nb):
                        c = t * nb + a - u
                        if not 0 <= c < nb:
                            continue
                        diff = chunk_diff - u * SUB_CHUNKS
                        allowed = diff <= 0
                        if left_chunks is not None:
                            allowed = allowed & (diff >= -left_chunks)
                        tiles_ref[p, t, pl.ds(c * LANES, LANES), pl.ds(half * blk + a * LANES, LANES)] = (
                            jnp.where(allowed, sub * LOG2E, NEG))


def _bias_tiles(rel_bias, n_tiles, blk, left_chunks):
    n_pairs = rel_bias.shape[0]
    n_out = n_tiles if left_chunks is None else n_tiles + 1
    return pl.pallas_call(
        functools.partial(_bias_tiles_kernel, left_chunks=left_chunks),
        out_shape=jax.ShapeDtypeStruct((n_pairs, n_out, blk, 2 * blk), jnp.float32),
        compiler_params=pltpu.CompilerParams(vmem_limit_bytes=V7X_VMEM_LIMIT),
        name="bias_tiles",
    )(rel_bias)


def _sum_rows(x):
    acc = x[:SUBLANES]
    for r in range(1, x.shape[0] // SUBLANES):
        acc = acc + x[r * SUBLANES:(r + 1) * SUBLANES]
    return acc


def _fold_rows(x):
    m = x[:SUBLANES]
    for r in range(1, x.shape[0] // SUBLANES):
        m = jnp.maximum(m, x[r * SUBLANES:(r + 1) * SUBLANES])
    return m


def _attn_kernel(*refs, left_blocks, lambda_init):
    if lambda_init is None:
        q_ref, k_ref, v_ref, tiles_ref, o_ref, q2_ref, s_ref, acc_ref = refs
    else:
        (q_ref, k_ref, v_ref, tiles_ref, lq1_ref, lk1_ref, lq2_ref, lk2_ref, subg_ref, o_ref,
         q2_ref, s_even_ref, s_odd_ref, m_ref, e_ref, acc_ref) = refs
        lam = (jnp.exp(jnp.sum(lq1_ref[...] * lk1_ref[...], axis=-1, keepdims=True))
               - jnp.exp(jnp.sum(lq2_ref[...] * lk2_ref[...], axis=-1, keepdims=True))
               + lambda_init)
    blk = q_ref.shape[1]
    n_pairs = q_ref.shape[2] // PAIR
    n_tiles = tiles_ref.shape[1]
    i = pl.program_id(1)
    low_half = lax.broadcasted_iota(jnp.int32, (blk, PAIR), 1) < HEAD_DIM

    def cols(p):
        return slice(p * PAIR, (p + 1) * PAIR)

    for p in range(n_pairs):
        q = q_ref[0, :, cols(p)]
        zero = jnp.zeros_like(q)
        q2_ref[p, :blk] = jnp.where(low_half, q, zero)
        q2_ref[p, blk:] = jnp.where(low_half, zero, q)
    acc_ref[...] = jnp.zeros(acc_ref.shape, jnp.float32)

    def key_rows(j):
        return pl.ds(pl.multiple_of(j * blk, blk), blk)

    def biased_scores(j, tile, p):
        s = lax.dot_general(k_ref[0, key_rows(j), cols(p)], q2_ref[p], _NT, preferred_element_type=jnp.float32)
        return s + tiles_ref[p, tile]

    def weigh(j, p, e):
        return jnp.concatenate([_dot(v_ref[0, j, p], e.astype(jnp.bfloat16)), _sum_rows(e)], axis=0)

    def scores(j, slot, tile, ms):
        out = []
        for p in range(n_pairs):
            s = biased_scores(j, tile, p)
            s_ref[slot, p] = s
            out.append(jnp.maximum(ms[p], _fold_rows(s)))
        return tuple(out)

    def weighted(j, slot, ms):
        for p in range(n_pairs):
            acc_ref[p] += weigh(j, p, jnp.exp2(s_ref[slot, p] - ms[p]))

    def prefetch_scores(j, buf_ref):
        tile = jnp.minimum(i - j, n_tiles - 1)
        for p in range(n_pairs):
            buf_ref[p] = biased_scores(j, tile, p)

    def online(j, buf_ref):
        for p in range(n_pairs):
            m8 = buf_ref[p, :SUBLANES]
            for r in range(1, blk // SUBLANES):
                m8 = jnp.maximum(m8, buf_ref[p, r * SUBLANES:(r + 1) * SUBLANES])
            m_old = m_ref[p]
            m = jnp.maximum(m_old, jnp.max(m8, axis=0, keepdims=True))
            sums = jnp.zeros((SUBLANES, 2 * blk), jnp.float32)
            for c in range(blk // E_CHUNK):
                rows = slice(c * E_CHUNK, (c + 1) * E_CHUNK)
                e = jnp.exp2(buf_ref[p, rows] - m)
                sums = sums + _sum_rows(e)
                e_ref[p, rows] = e.astype(jnp.bfloat16)
            update = jnp.concatenate([_dot(v_ref[0, j, p], e_ref[p]), sums], axis=0)
            acc_ref[p] = jnp.exp2(m_old - m) * acc_ref[p] + update
            m_ref[p] = m

    if left_blocks is None:
        n_blocks = i + 1
        m_ref[...] = jnp.full(m_ref.shape, NEG, jnp.float32)

        def two_blocks(t, carry):
            j = 2 * t
            prefetch_scores(j + 1, s_odd_ref)
            online(j, s_even_ref)
            prefetch_scores(jnp.minimum(j + 2, i), s_even_ref)
            online(j + 1, s_odd_ref)
            return carry

        prefetch_scores(0, s_even_ref)
        lax.fori_loop(0, n_blocks // 2, two_blocks, 0)

        @pl.when(n_blocks % 2 == 1)
        def _():
            online(i, s_even_ref)
    else:
        ms = tuple(jnp.full((SUBLANES, 2 * blk), NEG, jnp.float32) for _ in range(n_pairs))
        band = [(jnp.maximum(i - d, 0), left_blocks - d, jnp.where(i >= d, d, n_tiles - 1))
                for d in range(left_blocks, -1, -1)]
        for j, slot, tile in band:
            ms = scores(j, slot, tile, ms)
        ms = tuple(jnp.max(m, axis=0, keepdims=True) for m in ms)
        for j, slot, _ in band:
            weighted(j, slot, ms)

    top_half = lax.broadcasted_iota(jnp.int32, (PAIR, blk), 0) < HEAD_DIM
    for p in range(n_pairs):
        o = acc_ref[p, :PAIR] / jnp.sum(acc_ref[p, PAIR:], axis=0, keepdims=True)
        if lambda_init is None:
            y = jnp.where(top_half, o[:, :blk], o[:, blk:]).T
        else:
            d = o[:, :blk] - lam * o[:, blk:]
            d = d * lax.rsqrt(jnp.mean(d * d, axis=0, keepdims=True) + EPS)
            y = d.T * subg_ref[...] * (1.0 - lambda_init)
        o_ref[0, :, cols(p)] = y.astype(o_ref.dtype)


def _attention(q, k, v, tiles, left_blocks, lambda_init=None, extra=()):
    b, s, w = q.shape
    n_pairs = w // PAIR
    blk = tiles.shape[2]
    blk_q = pl.BlockSpec((1, blk, w), lambda bi, i: (bi, i, 0))
    whole = pl.BlockSpec((1, s, w), lambda bi, i: (bi, 0, 0))
    whole_v = pl.BlockSpec((1,) + v.shape[1:], lambda bi, i: (bi, 0, 0, 0, 0))
    f32 = jnp.float32
    score_block = (n_pairs, blk, 2 * blk)
    if left_blocks is None:
        score_scratch = [pltpu.VMEM(score_block, f32)] * 2 + [pltpu.VMEM((n_pairs, 1, 2 * blk), f32),
                                                              pltpu.VMEM(score_block, jnp.bfloat16)]
    else:
        score_scratch = [pltpu.VMEM((left_blocks + 1,) + score_block, f32)]
    return pl.pallas_call(
        functools.partial(_attn_kernel, left_blocks=left_blocks, lambda_init=lambda_init),
        grid=(b, s // blk),
        in_specs=[blk_q, whole, whole_v, _resident(tiles.shape)] + [_resident(e.shape) for e in extra],
        out_specs=blk_q,
        out_shape=jax.ShapeDtypeStruct((b, s, w), jnp.bfloat16),
        scratch_shapes=[pltpu.VMEM((n_pairs, 2 * blk, PAIR), jnp.bfloat16), *score_scratch,
                        pltpu.VMEM((n_pairs, ACC_ROWS, 2 * blk), f32)],
        compiler_params=pltpu.CompilerParams(dimension_semantics=("arbitrary", "arbitrary"),
                                             vmem_limit_bytes=V7X_VMEM_LIMIT),
        name="attn_a" if lambda_init is None else "attn_b",
    )(q, k, v, tiles, *extra)


def kernel(x, ffn1_norm, ffn1_w_in, ffn1_w_out, mix_norm, w_mix_in, b_gate, rel_bias_a, lambda_q1, lambda_k1, lambda_q2, lambda_k2, subln_g, t5_bias, w_branch_a, w_branch_b, w_o, ffn2_norm, ffn2_w_in, ffn2_w_out, final_norm):
    b, s, d = x.shape
    depth = ffn1_norm.shape[0]
    assert depth >= 1
    width = w_branch_a.shape[1]
    n_pairs = width // PAIR
    bf = jnp.bfloat16
    f32 = jnp.float32

    t5_rel = t5_bias.astype(f32)[:, _t5_bucket(_window_rel(B_TILES, B_BLK))]
    t5_rel = jnp.broadcast_to(t5_rel[:, None, :, None, :], (n_pairs, 2) + t5_rel.shape[1:2] + (1, 2 * LANES))
    tiles_b = _bias_tiles(t5_rel, B_TILES, B_BLK, None)

    xf = x.astype(f32).reshape(b * s, d)
    for li in range(depth):
        x1, qa, ka, va, qb, kb, vb, gates = _ffn_mix(
            xf, ffn1_norm[li][None].astype(f32), ffn1_w_in[li].astype(bf), ffn1_w_out[li].astype(bf),
            mix_norm[li][None].astype(f32), w_mix_in[li].astype(bf), b_gate[li][None].astype(f32), width, s)

        rel_a = jnp.clip(_window_rel(A_TILES, A_BLK), -REL_CLIP, REL_CLIP) + REL_CLIP
        a_rel = rel_bias_a[li].astype(f32)[:, rel_a]
        tiles_a = _bias_tiles(a_rel.reshape(n_pairs, 2, rel_a.shape[0], 1, 2 * LANES), A_TILES, A_BLK, LEFT_CHUNKS)

        def seq(t):
            return t.reshape(b, s, width)

        ya = _attention(seq(qa), seq(ka), va, tiles_a, A_TILES - 1)
        lambda_init = 0.8 - 0.6 * math.exp(-0.3 * li)
        lams = [t[li][None].astype(f32) for t in (lambda_q1, lambda_k1, lambda_q2, lambda_k2)]
        yb = _attention(seq(qb), seq(kb), vb, tiles_b, None, lambda_init=lambda_init,
                        extra=(*lams, subln_g[li][None].astype(f32)))

        xf = _out_ffn(x1, ya.reshape(b * s, width), yb.reshape(b * s, width), gates,
                      w_branch_a[li].astype(bf), w_branch_b[li].astype(bf), w_o[li].astype(bf),
                      ffn2_norm[li][None].astype(f32), ffn2_w_in[li].astype(bf), ffn2_w_out[li].astype(bf),
                      final_norm[None].astype(f32), final_norm=(li == depth - 1))
    return xf.reshape(b, s, d).astype(x.dtype)
```

```python
import functools
import math

import jax
import jax.numpy as jnp
from jax import lax
from jax.experimental import pallas as pl
from jax.experimental.pallas import tpu as pltpu

EPS = 1e-6
NEG = -1e30
LOG2E = math.log2(math.e)

CHUNK = 64
LEFT_CHUNKS = 8
REL_CLIP = 128
T5_BUCKETS = 32
T5_MAX_DIST = 128

LANES = 128
SUBLANES = 8
HEAD_DIM = 64
PAIR = 2 * HEAD_DIM
SUB_CHUNKS = LANES // CHUNK
ACC_ROWS = PAIR + SUBLANES

A_BLK = 128
A_TILES = LEFT_CHUNKS * CHUNK // A_BLK + 1
B_BLK = 256
B_TILES = 3

V7X_VMEM_LIMIT = 60000 * 1024

ROW_TILE = 512

_NT = (((1,), (1,)), ((), ()))


def _rms(xf, g):
    return xf * lax.rsqrt(jnp.mean(xf * xf, axis=-1, keepdims=True) + EPS) * g


def _dot(a, b):
    return jnp.dot(a, b, preferred_element_type=jnp.float32)


def _swiglu(h, w_in_ref, w_out_ref):
    d_ff = w_out_ref.shape[0]
    gate = _dot(h, w_in_ref[:, :d_ff])
    up = _dot(h, w_in_ref[:, d_ff:])
    act = (gate * jax.nn.sigmoid(gate) * up).astype(jnp.bfloat16)
    return _dot(act, w_out_ref[...])


def _ffn_mix_kernel(x_ref, g1_ref, w_in_ref, w_out_ref, gm_ref, w_mix_ref, b_gate_ref,
                    x1_ref, qa_ref, ka_ref, va_ref, qb_ref, kb_ref, vb_ref, gates_ref):
    x = x_ref[...]
    h = _rms(x, g1_ref[...]).astype(jnp.bfloat16)
    x1 = x + 0.5 * _swiglu(h, w_in_ref, w_out_ref)
    x1_ref[...] = x1

    u = _rms(x1, gm_ref[...]).astype(jnp.bfloat16)
    width = qa_ref.shape[1]
    scale = HEAD_DIM ** -0.5 * LOG2E
    outs = (qa_ref, ka_ref, va_ref, qb_ref, kb_ref, vb_ref)
    for n, o_ref in enumerate(outs):
        p = _dot(u, w_mix_ref[:, n * width:(n + 1) * width])
        if o_ref is va_ref or o_ref is vb_ref:
            _store_values_transposed(p, o_ref)
            continue
        if o_ref is qa_ref or o_ref is qb_ref:
            p = p * scale
        o_ref[...] = p.astype(o_ref.dtype)
    logits = _dot(u, w_mix_ref[:, len(outs) * width:]) + b_gate_ref[...]
    gates_ref[...] = jax.nn.sigmoid(logits)


def _store_values_transposed(p, o_ref):
    n_blk, n_pairs, _, blk = o_ref.shape[1:]
    pt = p.T
    for c in range(n_blk):
        for pr in range(n_pairs):
            o_ref[0, c, pr] = pt[pr * PAIR:(pr + 1) * PAIR, c * blk:(c + 1) * blk].astype(o_ref.dtype)


def _out_ffn_kernel(x1_ref, ya_ref, yb_ref, gates_ref, wa_ref, wb_ref, wo_ref,
                    g2_ref, w_in_ref, w_out_ref, gf_ref, o_ref, *, final_norm):
    d = x1_ref.shape[1]
    merged = (gates_ref[:, :d] * _dot(ya_ref[...], wa_ref[...])
              + gates_ref[:, d:] * _dot(yb_ref[...], wb_ref[...]))
    x2 = x1_ref[...] + _dot(merged.astype(jnp.bfloat16), wo_ref[...])
    h = _rms(x2, g2_ref[...]).astype(jnp.bfloat16)
    x3 = x2 + 0.5 * _swiglu(h, w_in_ref, w_out_ref)
    o_ref[...] = _rms(x3, gf_ref[...]) if final_norm else x3


def _resident(shape):
    return pl.BlockSpec(shape, lambda *_: (0,) * len(shape), pipeline_mode=pl.Buffered(1))


def _rows(tm, width):
    return pl.BlockSpec((tm, width), lambda i: (i, 0))


def _ffn_mix(x, g1, w_in, w_out, gm, w_mix, b_gate, width, seq):
    m, d = x.shape
    tm = ROW_TILE
    bf = jnp.bfloat16
    n_pairs = width // PAIR
    steps_per_seq = seq // tm

    def values_t(blk):
        shape = (m // seq, seq // blk, n_pairs, PAIR, blk)
        spec = pl.BlockSpec((1, tm // blk) + shape[2:], lambda i: (i // steps_per_seq, i % steps_per_seq, 0, 0, 0))
        return jax.ShapeDtypeStruct(shape, bf), spec

    qk = jax.ShapeDtypeStruct((m, width), bf), _rows(tm, width)
    outs = [(jax.ShapeDtypeStruct((m, d), jnp.float32), _rows(tm, d)),
            qk, qk, values_t(A_BLK), qk, qk, values_t(B_BLK),
            (jax.ShapeDtypeStruct((m, b_gate.shape[1]), jnp.float32), _rows(tm, b_gate.shape[1]))]
    return pl.pallas_call(
        _ffn_mix_kernel,
        grid=(m // tm,),
        in_specs=[_rows(tm, d), _resident(g1.shape), _resident(w_in.shape), _resident(w_out.shape),
                  _resident(gm.shape), _resident(w_mix.shape), _resident(b_gate.shape)],
        out_specs=[spec for _, spec in outs],
        out_shape=[shape for shape, _ in outs],
        compiler_params=pltpu.CompilerParams(dimension_semantics=("arbitrary",),
                                             vmem_limit_bytes=V7X_VMEM_LIMIT),
        name="ffn_mix",
    )(x, g1, w_in, w_out, gm, w_mix, b_gate)


def _out_ffn(x1, ya, yb, gates, wa, wb, wo, g2, w_in, w_out, gf, final_norm):
    m, d = x1.shape
    tm = ROW_TILE
    return pl.pallas_call(
        functools.partial(_out_ffn_kernel, final_norm=final_norm),
        grid=(m // tm,),
        in_specs=[_rows(tm, d), _rows(tm, ya.shape[1]), _rows(tm, yb.shape[1]), _rows(tm, gates.shape[1]),
                  _resident(wa.shape), _resident(wb.shape), _resident(wo.shape), _resident(g2.shape),
                  _resident(w_in.shape), _resident(w_out.shape), _resident(gf.shape)],
        out_specs=_rows(tm, d),
        out_shape=jax.ShapeDtypeStruct((m, d), jnp.float32),
        compiler_params=pltpu.CompilerParams(dimension_semantics=("arbitrary",),
                                             vmem_limit_bytes=V7X_VMEM_LIMIT),
        name="out_ffn",
    )(x1, ya, yb, gates, wa, wb, wo, g2, w_in, w_out, gf)


def _t5_bucket(rel):
    nb = T5_BUCKETS // 2
    ret = jnp.where(rel > 0, nb, 0)
    n = jnp.abs(rel)
    max_exact = nb // 2
    nf = jnp.maximum(n, 1).astype(jnp.float32)
    large = max_exact + (jnp.log(nf / max_exact) / math.log(T5_MAX_DIST / max_exact)
                         * (nb - max_exact)).astype(jnp.int32)
    large = jnp.minimum(large, nb - 1)
    return ret + jnp.where(n < max_exact, n, large)


def _window_rel(n_tiles, blk):
    nb = blk // LANES
    u = jnp.arange(-(nb - 1), n_tiles * nb, dtype=jnp.int32)[:, None]
    return LANES - jnp.arange(2 * LANES, dtype=jnp.int32)[None, :] - LANES * u


def _bias_tiles_kernel(rel_bias_ref, tiles_ref, *, left_chunks):
    n_pairs, n_tiles, blk, _ = tiles_ref.shape
    nb = blk // LANES
    kj = lax.broadcasted_iota(jnp.int32, (LANES, LANES), 0)
    qi = lax.broadcasted_iota(jnp.int32, (LANES, LANES), 1)
    chunk_diff = kj // CHUNK - qi // CHUNK
    for p in range(n_pairs):
        for half in range(2):
            for w in range(rel_bias_ref.shape[2]):
                u = w - (nb - 1)
                row = jnp.broadcast_to(rel_bias_ref[p, half, w], (LANES, 2 * LANES))
                sub = pltpu.roll(row, 0, 1, stride=1, stride_axis=0)[:, LANES:]
                for t in range(n_tiles):
                    for a in range(nb):
                        c = t * nb + a - u
                        if not 0 <= c < nb:
                            continue
                        diff = chunk_diff - u * SUB_CHUNKS
                        allowed = diff <= 0
                        if left_chunks is not None:
                            allowed = allowed & (diff >= -left_chunks)
                        tiles_ref[p, t, pl.ds(c * LANES, LANES), pl.ds(half * blk + a * LANES, LANES)] = (
                            jnp.where(allowed, sub * LOG2E, NEG))


def _bias_tiles(rel_bias, n_tiles, blk, left_chunks):
    n_pairs = rel_bias.shape[0]
    return pl.pallas_call(
        functools.partial(_bias_tiles_kernel, left_chunks=left_chunks),
        out_shape=jax.ShapeDtypeStruct((n_pairs, n_tiles, blk, 2 * blk), jnp.float32),
        compiler_params=pltpu.CompilerParams(vmem_limit_bytes=V7X_VMEM_LIMIT),
        name="bias_tiles",
    )(rel_bias)


def _sum_rows(x):
    acc = x[:SUBLANES]
    for r in range(1, x.shape[0] // SUBLANES):
        acc = acc + x[r * SUBLANES:(r + 1) * SUBLANES]
    return acc


def _fold_rows(x):
    m = x[:SUBLANES]
    for r in range(1, x.shape[0] // SUBLANES):
        m = jnp.maximum(m, x[r * SUBLANES:(r + 1) * SUBLANES])
    return m


def _attn_kernel(*refs, left_blocks, lambda_init):
    if lambda_init is None:
        q_ref, k_ref, v_ref, tiles_ref, o_ref, q2_ref, s_ref, acc_ref = refs
    else:
        (q_ref, k_ref, v_ref, tiles_ref, lq1_ref, lk1_ref, lq2_ref, lk2_ref, subg_ref, o_ref,
         q2_ref, s_even_ref, s_odd_ref, m_ref, acc_ref) = refs
        lam = (jnp.exp(jnp.sum(lq1_ref[...] * lk1_ref[...], axis=-1, keepdims=True))
               - jnp.exp(jnp.sum(lq2_ref[...] * lk2_ref[...], axis=-1, keepdims=True))
               + lambda_init)
    blk = q_ref.shape[1]
    n_pairs = q_ref.shape[2] // PAIR
    n_tiles = tiles_ref.shape[1]
    i = pl.program_id(1)
    low_half = lax.broadcasted_iota(jnp.int32, (blk, PAIR), 1) < HEAD_DIM

    def cols(p):
        return slice(p * PAIR, (p + 1) * PAIR)

    for p in range(n_pairs):
        q = q_ref[0, :, cols(p)]
        zero = jnp.zeros_like(q)
        q2_ref[p, :blk] = jnp.where(low_half, q, zero)
        q2_ref[p, blk:] = jnp.where(low_half, zero, q)
    acc_ref[...] = jnp.zeros(acc_ref.shape, jnp.float32)

    def key_rows(j):
        return pl.ds(pl.multiple_of(j * blk, blk), blk)

    def biased_scores(j, tile, p):
        s = lax.dot_general(k_ref[0, key_rows(j), cols(p)], q2_ref[p], _NT, preferred_element_type=jnp.float32)
        return s + tiles_ref[p, tile]

    def weigh(j, p, e):
        return jnp.concatenate([_dot(v_ref[0, j, p], e.astype(jnp.bfloat16)), _sum_rows(e)], axis=0)

    def scores(j, slot, tile, ms):
        out = []
        for p in range(n_pairs):
            s = biased_scores(j, tile, p)
            s_ref[slot, p] = s
            out.append(jnp.maximum(ms[p], _fold_rows(s)))
        return tuple(out)

    def weighted(j, slot, ms):
        for p in range(n_pairs):
            acc_ref[p] += weigh(j, p, jnp.exp2(s_ref[slot, p] - ms[p]))

    def prefetch_scores(j, buf_ref):
        tile = jnp.minimum(i - j, n_tiles - 1)
        for p in range(n_pairs):
            buf_ref[p] = biased_scores(j, tile, p)

    def online(j, buf_ref):
        for p in range(n_pairs):
            s = buf_ref[p]
            m_old = m_ref[p]
            m = jnp.maximum(m_old, jnp.max(_fold_rows(s), axis=0, keepdims=True))
            acc_ref[p] = jnp.exp2(m_old - m) * acc_ref[p] + weigh(j, p, jnp.exp2(s - m))
            m_ref[p] = m

    if left_blocks is None:
        n_blocks = i + 1
        m_ref[...] = jnp.full(m_ref.shape, NEG, jnp.float32)

        def two_blocks(t, carry):
            j = 2 * t
            prefetch_scores(j + 1, s_odd_ref)
            online(j, s_even_ref)
            prefetch_scores(jnp.minimum(j + 2, i), s_even_ref)
            online(j + 1, s_odd_ref)
            return carry

        prefetch_scores(0, s_even_ref)
        lax.fori_loop(0, n_blocks // 2, two_blocks, 0)

        @pl.when(n_blocks % 2 == 1)
        def _():
            online(i, s_even_ref)
    else:
        def band(n_left):
            ms = tuple(jnp.full((SUBLANES, 2 * blk), NEG, jnp.float32) for _ in range(n_pairs))
            for d in range(n_left, -1, -1):
                ms = scores(i - d, d, d, ms)
            ms = tuple(jnp.max(m, axis=0, keepdims=True) for m in ms)
            for d in range(n_left, -1, -1):
                weighted(i - d, d, ms)

        for n_left in range(left_blocks):
            pl.when(i == n_left)(functools.partial(band, n_left))
        pl.when(i >= left_blocks)(functools.partial(band, left_blocks))

    top_half = lax.broadcasted_iota(jnp.int32, (PAIR, blk), 0) < HEAD_DIM
    for p in range(n_pairs):
        o = acc_ref[p, :PAIR] / jnp.sum(acc_ref[p, PAIR:], axis=0, keepdims=True)
        if lambda_init is None:
            y = jnp.where(top_half, o[:, :blk], o[:, blk:]).T
        else:
            d = o[:, :blk] - lam * o[:, blk:]
            d = d * lax.rsqrt(jnp.mean(d * d, axis=0, keepdims=True) + EPS)
            y = d.T * subg_ref[...] * (1.0 - lambda_init)
        o_ref[0, :, cols(p)] = y.astype(o_ref.dtype)


def _attention(q, k, v, tiles, left_blocks, lambda_init=None, extra=()):
    b, s, w = q.shape
    n_pairs = w // PAIR
    blk = tiles.shape[2]
    blk_q = pl.BlockSpec((1, blk, w), lambda bi, i: (bi, i, 0))
    whole = pl.BlockSpec((1, s, w), lambda bi, i: (bi, 0, 0))
    whole_v = pl.BlockSpec((1,) + v.shape[1:], lambda bi, i: (bi, 0, 0, 0, 0))
    f32 = jnp.float32
    score_block = (n_pairs, blk, 2 * blk)
    if left_blocks is None:
        score_scratch = [pltpu.VMEM(score_block, f32)] * 2 + [pltpu.VMEM((n_pairs, 1, 2 * blk), f32)]
    else:
        score_scratch = [pltpu.VMEM((left_blocks + 1,) + score_block, f32)]
    return pl.pallas_call(
        functools.partial(_attn_kernel, left_blocks=left_blocks, lambda_init=lambda_init),
        grid=(b, s // blk),
        in_specs=[blk_q, whole, whole_v, _resident(tiles.shape)] + [_resident(e.shape) for e in extra],
        out_specs=blk_q,
        out_shape=jax.ShapeDtypeStruct((b, s, w), jnp.bfloat16),
        scratch_shapes=[pltpu.VMEM((n_pairs, 2 * blk, PAIR), jnp.bfloat16), *score_scratch,
                        pltpu.VMEM((n_pairs, ACC_ROWS, 2 * blk), f32)],
        compiler_params=pltpu.CompilerParams(dimension_semantics=("arbitrary", "arbitrary"),
                                             vmem_limit_bytes=V7X_VMEM_LIMIT),
        name="attn_a" if lambda_init is None else "attn_b",
    )(q, k, v, tiles, *extra)


def kernel(x, ffn1_norm, ffn1_w_in, ffn1_w_out, mix_norm, w_mix_in, b_gate, rel_bias_a, lambda_q1, lambda_k1, lambda_q2, lambda_k2, subln_g, t5_bias, w_branch_a, w_branch_b, w_o, ffn2_norm, ffn2_w_in, ffn2_w_out, final_norm):
    b, s, d = x.shape
    depth = ffn1_norm.shape[0]
    assert depth >= 1
    width = w_branch_a.shape[1]
    n_pairs = width // PAIR
    bf = jnp.bfloat16
    f32 = jnp.float32

    t5_rel = t5_bias.astype(f32)[:, _t5_bucket(_window_rel(B_TILES, B_BLK))]
    t5_rel = jnp.broadcast_to(t5_rel[:, None, :, None, :], (n_pairs, 2) + t5_rel.shape[1:2] + (1, 2 * LANES))
    tiles_b = _bias_tiles(t5_rel, B_TILES, B_BLK, None)

    xf = x.astype(f32).reshape(b * s, d)
    for li in range(depth):
        x1, qa, ka, va, qb, kb, vb, gates = _ffn_mix(
            xf, ffn1_norm[li][None].astype(f32), ffn1_w_in[li].astype(bf), ffn1_w_out[li].astype(bf),
            mix_norm[li][None].astype(f32), w_mix_in[li].astype(bf), b_gate[li][None].astype(f32), width, s)

        rel_a = jnp.clip(_window_rel(A_TILES, A_BLK), -REL_CLIP, REL_CLIP) + REL_CLIP
        a_rel = rel_bias_a[li].astype(f32)[:, rel_a]
        tiles_a = _bias_tiles(a_rel.reshape(n_pairs, 2, rel_a.shape[0], 1, 2 * LANES), A_TILES, A_BLK, LEFT_CHUNKS)

        def seq(t):
            return t.reshape(b, s, width)

        ya = _attention(seq(qa), seq(ka), va, tiles_a, A_TILES - 1)
        lambda_init = 0.8 - 0.6 * math.exp(-0.3 * li)
        lams = [t[li][None].astype(f32) for t in (lambda_q1, lambda_k1, lambda_q2, lambda_k2)]
        yb = _attention(seq(qb), seq(kb), vb, tiles_b, None, lambda_init=lambda_init,
                        extra=(*lams, subln_g[li][None].astype(f32)))

        xf = _out_ffn(x1, ya.reshape(b * s, width), yb.reshape(b * s, width), gates,
                      w_branch_a[li].astype(bf), w_branch_b[li].astype(bf), w_o[li].astype(bf),
                      ffn2_norm[li][None].astype(f32), ffn2_w_in[li].astype(bf), ffn2_w_out[li].astype(bf),
                      final_norm[None].astype(f32), final_norm=(li == depth - 1))
    return xf.reshape(b, s, d).astype(x.dtype)
```

```python
import functools
import math

import jax
import jax.numpy as jnp
from jax import lax
from jax.experimental import pallas as pl
from jax.experimental.pallas import tpu as pltpu

EPS = 1e-6
NEG = -1e30
LOG2E = math.log2(math.e)

CHUNK = 64
LEFT_CHUNKS = 8
REL_CLIP = 128
T5_BUCKETS = 32
T5_MAX_DIST = 128

LANES = 128
SUBLANES = 8
HEAD_DIM = 64
PAIR = 2 * HEAD_DIM
SUB_CHUNKS = LANES // CHUNK
ACC_ROWS = PAIR + SUBLANES

A_BLK = 128
A_TILES = LEFT_CHUNKS * CHUNK // A_BLK + 1
B_BLK = 256
B_TILES = 3

V7X_VMEM_LIMIT = 60000 * 1024

ROW_TILE = 512


def _rms(xf, g):
    return xf * lax.rsqrt(jnp.mean(xf * xf, axis=-1, keepdims=True) + EPS) * g


def _dot(a, b):
    return jnp.dot(a, b, preferred_element_type=jnp.float32)


def _swiglu(h, w_in_ref, w_out_ref):
    d_ff = w_out_ref.shape[0]
    gate = _dot(h, w_in_ref[:, :d_ff])
    up = _dot(h, w_in_ref[:, d_ff:])
    act = (gate * jax.nn.sigmoid(gate) * up).astype(jnp.bfloat16)
    return _dot(act, w_out_ref[...])


def _ffn_mix_kernel(x_ref, g1_ref, w_in_ref, w_out_ref, gm_ref, w_mix_ref, b_gate_ref,
                    x1_ref, qa_ref, ka_ref, va_ref, qb_ref, kb_ref, vb_ref, gates_ref):
    x = x_ref[...]
    h = _rms(x, g1_ref[...]).astype(jnp.bfloat16)
    x1 = x + 0.5 * _swiglu(h, w_in_ref, w_out_ref)
    x1_ref[...] = x1

    u = _rms(x1, gm_ref[...]).astype(jnp.bfloat16)
    width = ka_ref.shape[1]
    scale = HEAD_DIM ** -0.5 * LOG2E
    outs = (qa_ref, ka_ref, va_ref, qb_ref, kb_ref, vb_ref)
    for n, o_ref in enumerate(outs):
        p = _dot(u, w_mix_ref[:, n * width:(n + 1) * width])
        if o_ref is ka_ref or o_ref is kb_ref:
            o_ref[...] = p.astype(o_ref.dtype)
        elif o_ref is qa_ref or o_ref is qb_ref:
            _store_transposed(p * scale, o_ref)
        else:
            _store_transposed(p, o_ref)
    logits = _dot(u, w_mix_ref[:, len(outs) * width:]) + b_gate_ref[...]
    gates_ref[...] = jax.nn.sigmoid(logits)


def _store_transposed(p, o_ref):
    n_blk, n_pairs, _, blk = o_ref.shape[1:]
    pt = p.T
    for c in range(n_blk):
        for pr in range(n_pairs):
            o_ref[0, c, pr] = pt[pr * PAIR:(pr + 1) * PAIR, c * blk:(c + 1) * blk].astype(o_ref.dtype)


def _out_ffn_kernel(x1_ref, ya_ref, yb_ref, gates_ref, wa_ref, wb_ref, wo_ref,
                    g2_ref, w_in_ref, w_out_ref, gf_ref, o_ref, *, final_norm):
    d = x1_ref.shape[1]
    merged = (gates_ref[:, :d] * _dot(ya_ref[...], wa_ref[...])
              + gates_ref[:, d:] * _dot(yb_ref[...], wb_ref[...]))
    x2 = x1_ref[...] + _dot(merged.astype(jnp.bfloat16), wo_ref[...])
    h = _rms(x2, g2_ref[...]).astype(jnp.bfloat16)
    x3 = x2 + 0.5 * _swiglu(h, w_in_ref, w_out_ref)
    o_ref[...] = _rms(x3, gf_ref[...]) if final_norm else x3


def _resident(shape):
    return pl.BlockSpec(shape, lambda *_: (0,) * len(shape), pipeline_mode=pl.Buffered(1))


def _rows(tm, width):
    return pl.BlockSpec((tm, width), lambda i: (i, 0))


def _ffn_mix(x, g1, w_in, w_out, gm, w_mix, b_gate, width, seq):
    m, d = x.shape
    tm = ROW_TILE
    bf = jnp.bfloat16
    n_pairs = width // PAIR
    steps_per_seq = seq // tm

    def transposed(blk):
        shape = (m // seq, seq // blk, n_pairs, PAIR, blk)
        spec = pl.BlockSpec((1, tm // blk) + shape[2:], lambda i: (i // steps_per_seq, i % steps_per_seq, 0, 0, 0))
        return jax.ShapeDtypeStruct(shape, bf), spec

    keys = jax.ShapeDtypeStruct((m, width), bf), _rows(tm, width)
    outs = [(jax.ShapeDtypeStruct((m, d), jnp.float32), _rows(tm, d)),
            transposed(A_BLK), keys, transposed(A_BLK), transposed(B_BLK), keys, transposed(B_BLK),
            (jax.ShapeDtypeStruct((m, b_gate.shape[1]), jnp.float32), _rows(tm, b_gate.shape[1]))]
    return pl.pallas_call(
        _ffn_mix_kernel,
        grid=(m // tm,),
        in_specs=[_rows(tm, d), _resident(g1.shape), _resident(w_in.shape), _resident(w_out.shape),
                  _resident(gm.shape), _resident(w_mix.shape), _resident(b_gate.shape)],
        out_specs=[spec for _, spec in outs],
        out_shape=[shape for shape, _ in outs],
        compiler_params=pltpu.CompilerParams(dimension_semantics=("arbitrary",),
                                             vmem_limit_bytes=V7X_VMEM_LIMIT),
        name="ffn_mix",
    )(x, g1, w_in, w_out, gm, w_mix, b_gate)


def _out_ffn(x1, ya, yb, gates, wa, wb, wo, g2, w_in, w_out, gf, final_norm):
    m, d = x1.shape
    tm = ROW_TILE
    return pl.pallas_call(
        functools.partial(_out_ffn_kernel, final_norm=final_norm),
        grid=(m // tm,),
        in_specs=[_rows(tm, d), _rows(tm, ya.shape[1]), _rows(tm, yb.shape[1]), _rows(tm, gates.shape[1]),
                  _resident(wa.shape), _resident(wb.shape), _resident(wo.shape), _resident(g2.shape),
                  _resident(w_in.shape), _resident(w_out.shape), _resident(gf.shape)],
        out_specs=_rows(tm, d),
        out_shape=jax.ShapeDtypeStruct((m, d), jnp.float32),
        compiler_params=pltpu.CompilerParams(dimension_semantics=("arbitrary",),
                                             vmem_limit_bytes=V7X_VMEM_LIMIT),
        name="out_ffn",
    )(x1, ya, yb, gates, wa, wb, wo, g2, w_in, w_out, gf)


def _t5_bucket(rel):
    nb = T5_BUCKETS // 2
    ret = jnp.where(rel > 0, nb, 0)
    n = jnp.abs(rel)
    max_exact = nb // 2
    nf = jnp.maximum(n, 1).astype(jnp.float32)
    large = max_exact + (jnp.log(nf / max_exact) / math.log(T5_MAX_DIST / max_exact)
                         * (nb - max_exact)).astype(jnp.int32)
    large = jnp.minimum(large, nb - 1)
    return ret + jnp.where(n < max_exact, n, large)


def _window_rel(n_tiles, blk):
    nb = blk // LANES
    u = jnp.arange(-(nb - 1), n_tiles * nb, dtype=jnp.int32)[:, None]
    return LANES - jnp.arange(2 * LANES, dtype=jnp.int32)[None, :] - LANES * u


def _bias_tiles_kernel(rel_bias_ref, tiles_ref, *, left_chunks):
    n_pairs, n_tiles, blk, _ = tiles_ref.shape
    nb = blk // LANES
    kj = lax.broadcasted_iota(jnp.int32, (LANES, LANES), 0)
    qi = lax.broadcasted_iota(jnp.int32, (LANES, LANES), 1)
    chunk_diff = kj // CHUNK - qi // CHUNK
    for p in range(n_pairs):
        for half in range(2):
            for w in range(rel_bias_ref.shape[2]):
                u = w - (nb - 1)
                row = jnp.broadcast_to(rel_bias_ref[p, half, w], (LANES, 2 * LANES))
                sub = pltpu.roll(row, 0, 1, stride=1, stride_axis=0)[:, LANES:]
                for t in range(n_tiles):
                    for a in range(nb):
                        c = t * nb + a - u
                        if not 0 <= c < nb:
                            continue
                        diff = chunk_diff - u * SUB_CHUNKS
                        allowed = diff <= 0
                        if left_chunks is not None:
                            allowed = allowed & (diff >= -left_chunks)
                        tiles_ref[p, t, pl.ds(c * LANES, LANES), pl.ds(half * blk + a * LANES, LANES)] = (
                            jnp.where(allowed, sub * LOG2E, NEG))


def _bias_tiles(rel_bias, n_tiles, blk, left_chunks):
    n_pairs = rel_bias.shape[0]
    return pl.pallas_call(
        functools.partial(_bias_tiles_kernel, left_chunks=left_chunks),
        out_shape=jax.ShapeDtypeStruct((n_pairs, n_tiles, blk, 2 * blk), jnp.float32),
        compiler_params=pltpu.CompilerParams(vmem_limit_bytes=V7X_VMEM_LIMIT),
        name="bias_tiles",
    )(rel_bias)


def _sum_rows(x):
    acc = x[:SUBLANES]
    for r in range(1, x.shape[0] // SUBLANES):
        acc = acc + x[r * SUBLANES:(r + 1) * SUBLANES]
    return acc


def _fold_rows(x):
    m = x[:SUBLANES]
    for r in range(1, x.shape[0] // SUBLANES):
        m = jnp.maximum(m, x[r * SUBLANES:(r + 1) * SUBLANES])
    return m


def _attn_kernel(*refs, left_blocks, lambda_init):
    if lambda_init is None:
        qt_ref, k_ref, vt_ref, tiles_ref, o_ref, q2t_ref, s_ref, acc_ref = refs
    else:
        (qt_ref, k_ref, vt_ref, tiles_ref, lq1_ref, lk1_ref, lq2_ref, lk2_ref, subg_ref, o_ref,
         q2t_ref, s_even_ref, s_odd_ref, m_ref, acc_ref) = refs
        lam = (jnp.exp(jnp.sum(lq1_ref[...] * lk1_ref[...], axis=-1, keepdims=True))
               - jnp.exp(jnp.sum(lq2_ref[...] * lk2_ref[...], axis=-1, keepdims=True))
               + lambda_init)
    n_pairs, _, blk = qt_ref.shape[2:]
    i = pl.program_id(1)
    top_half = lax.broadcasted_iota(jnp.int32, (PAIR, blk), 0) < HEAD_DIM

    def cols(p):
        return slice(p * PAIR, (p + 1) * PAIR)

    for p in range(n_pairs):
        qt = qt_ref[0, 0, p]
        zero = jnp.zeros_like(qt)
        q2t_ref[p, :, :blk] = jnp.where(top_half, qt, zero)
        q2t_ref[p, :, blk:] = jnp.where(top_half, zero, qt)

    def key_rows(j, n=1):
        return pl.ds(pl.multiple_of(j * blk, blk), n * blk)

    def weights_update(vt, e):
        return jnp.concatenate([_dot(vt, e.astype(jnp.bfloat16)), _sum_rows(e)], axis=0)

    if left_blocks is None:
        n_blocks = i + 1
        n_tiles = tiles_ref.shape[1]
        acc_ref[...] = jnp.zeros(acc_ref.shape, jnp.float32)
        m_ref[...] = jnp.full(m_ref.shape, NEG, jnp.float32)

        def prefetch_scores(j, buf_ref):
            tile = jnp.minimum(i - j, n_tiles - 1)
            for p in range(n_pairs):
                buf_ref[p] = _dot(k_ref[0, key_rows(j), cols(p)], q2t_ref[p]) + tiles_ref[p, tile]

        def online(j, buf_ref):
            for p in range(n_pairs):
                s = buf_ref[p]
                m_old = m_ref[p]
                m = jnp.maximum(m_old, jnp.max(_fold_rows(s), axis=0, keepdims=True))
                acc_ref[p] = jnp.exp2(m_old - m) * acc_ref[p] + weights_update(vt_ref[0, j, p], jnp.exp2(s - m))
                m_ref[p] = m

        def two_blocks(t, carry):
            j = 2 * t
            prefetch_scores(j + 1, s_odd_ref)
            online(j, s_even_ref)
            prefetch_scores(jnp.minimum(j + 2, i), s_even_ref)
            online(j + 1, s_odd_ref)
            return carry

        prefetch_scores(0, s_even_ref)
        lax.fori_loop(0, n_blocks // 2, two_blocks, 0)

        @pl.when(n_blocks % 2 == 1)
        def _():
            online(i, s_even_ref)
    else:
        def band(n_left):
            order = range(n_left, -1, -1)
            ms = [jnp.full((SUBLANES, 2 * blk), NEG, jnp.float32) for _ in range(n_pairs)]
            for d in order:
                for p in range(n_pairs):
                    s = _dot(k_ref[0, key_rows(i - d), cols(p)], q2t_ref[p]) + tiles_ref[p, d]
                    s_ref[d, p] = s
                    ms[p] = jnp.maximum(ms[p], _fold_rows(s))
            ms = [jnp.max(m, axis=0, keepdims=True) for m in ms]
            for d in order:
                for p in range(n_pairs):
                    update = weights_update(vt_ref[0, i - d, p], jnp.exp2(s_ref[d, p] - ms[p]))
                    acc_ref[p] = update if d == n_left else acc_ref[p] + update

        for n_left in range(left_blocks):
            pl.when(i == n_left)(functools.partial(band, n_left))
        pl.when(i >= left_blocks)(functools.partial(band, left_blocks))

    for p in range(n_pairs):
        o = acc_ref[p, :PAIR] / jnp.sum(acc_ref[p, PAIR:], axis=0, keepdims=True)
        if lambda_init is None:
            y = jnp.where(top_half, o[:, :blk], o[:, blk:]).T
        else:
            d = o[:, :blk] - lam * o[:, blk:]
            d = d * lax.rsqrt(jnp.mean(d * d, axis=0, keepdims=True) + EPS)
            y = d.T * subg_ref[...] * (1.0 - lambda_init)
        o_ref[0, :, cols(p)] = y.astype(o_ref.dtype)


def _attention(qt, k, vt, tiles, left_blocks, lambda_init=None, extra=()):
    b, s, w = k.shape
    n_pairs, _, blk = qt.shape[2:]
    blk_q = pl.BlockSpec((1, 1) + qt.shape[2:], lambda bi, i: (bi, i, 0, 0, 0))
    whole_k = pl.BlockSpec((1, s, w), lambda bi, i: (bi, 0, 0))
    whole_v = pl.BlockSpec((1,) + vt.shape[1:], lambda bi, i: (bi, 0, 0, 0, 0))
    f32 = jnp.float32
    score_block = (n_pairs, blk, 2 * blk)
    if left_blocks is None:
        score_scratch = [pltpu.VMEM(score_block, f32)] * 2 + [pltpu.VMEM((n_pairs, 1, 2 * blk), f32)]
    else:
        score_scratch = [pltpu.VMEM((left_blocks + 1,) + score_block, f32)]
    return pl.pallas_call(
        functools.partial(_attn_kernel, left_blocks=left_blocks, lambda_init=lambda_init),
        grid=(b, s // blk),
        in_specs=[blk_q, whole_k, whole_v, _resident(tiles.shape)] + [_resident(e.shape) for e in extra],
        out_specs=pl.BlockSpec((1, blk, w), lambda bi, i: (bi, i, 0)),
        out_shape=jax.ShapeDtypeStruct((b, s, w), jnp.bfloat16),
        scratch_shapes=[pltpu.VMEM((n_pairs, PAIR, 2 * blk), jnp.bfloat16), *score_scratch,
                        pltpu.VMEM((n_pairs, ACC_ROWS, 2 * blk), f32)],
        compiler_params=pltpu.CompilerParams(dimension_semantics=("arbitrary", "arbitrary"),
                                             vmem_limit_bytes=V7X_VMEM_LIMIT),
        name="attn_a" if lambda_init is None else "attn_b",
    )(qt, k, vt, tiles, *extra)


def kernel(x, ffn1_norm, ffn1_w_in, ffn1_w_out, mix_norm, w_mix_in, b_gate, rel_bias_a, lambda_q1, lambda_k1, lambda_q2, lambda_k2, subln_g, t5_bias, w_branch_a, w_branch_b, w_o, ffn2_norm, ffn2_w_in, ffn2_w_out, final_norm):
    b, s, d = x.shape
    depth = ffn1_norm.shape[0]
    assert depth >= 1
    width = w_branch_a.shape[1]
    n_pairs = width // PAIR
    bf = jnp.bfloat16
    f32 = jnp.float32

    t5_rel = t5_bias.astype(f32)[:, _t5_bucket(_window_rel(B_TILES, B_BLK))]
    t5_rel = jnp.broadcast_to(t5_rel[:, None, :, None, :], (n_pairs, 2) + t5_rel.shape[1:2] + (1, 2 * LANES))
    tiles_b = _bias_tiles(t5_rel, B_TILES, B_BLK, None)

    xf = x.astype(f32).reshape(b * s, d)
    for li in range(depth):
        x1, qa, ka, va, qb, kb, vb, gates = _ffn_mix(
            xf, ffn1_norm[li][None].astype(f32), ffn1_w_in[li].astype(bf), ffn1_w_out[li].astype(bf),
            mix_norm[li][None].astype(f32), w_mix_in[li].astype(bf), b_gate[li][None].astype(f32), width, s)

        rel_a = jnp.clip(_window_rel(A_TILES, A_BLK), -REL_CLIP, REL_CLIP) + REL_CLIP
        a_rel = rel_bias_a[li].astype(f32)[:, rel_a]
        tiles_a = _bias_tiles(a_rel.reshape(n_pairs, 2, rel_a.shape[0], 1, 2 * LANES), A_TILES, A_BLK, LEFT_CHUNKS)

        def seq(t):
            return t.reshape(b, s, width)

        ya = _attention(qa, seq(ka), va, tiles_a, A_TILES - 1)
        lambda_init = 0.8 - 0.6 * math.exp(-0.3 * li)
        lams = [t[li][None].astype(f32) for t in (lambda_q1, lambda_k1, lambda_q2, lambda_k2)]
        yb = _attention(qb, seq(kb), vb, tiles_b, None, lambda_init=lambda_init,
                        extra=(*lams, subln_g[li][None].astype(f32)))

        xf = _out_ffn(x1, ya.reshape(b * s, width), yb.reshape(b * s, width), gates,
                      w_branch_a[li].astype(bf), w_branch_b[li].astype(bf), w_o[li].astype(bf),
                      ffn2_norm[li][None].astype(f32), ffn2_w_in[li].astype(bf), ffn2_w_out[li].astype(bf),
                      final_norm[None].astype(f32), final_norm=(li == depth - 1))
    return xf.reshape(b, s, d).astype(x.dtype)
```

```python
import functools
import math

import jax
import jax.numpy as jnp
from jax import lax
from jax.experimental import pallas as pl
from jax.experimental.pallas import tpu as pltpu

EPS = 1e-6
NEG = -1e30
LOG2E = math.log2(math.e)

CHUNK = 64
LEFT_CHUNKS = 8
REL_CLIP = 128
T5_BUCKETS = 32
T5_MAX_DIST = 128

LANES = 128
SUBLANES = 8
HEAD_DIM = 64
PAIR = 2 * HEAD_DIM
SUB_CHUNKS = LANES // CHUNK
ACC_ROWS = PAIR + SUBLANES

A_BLK = 128
A_TILES = LEFT_CHUNKS * CHUNK // A_BLK + 1
B_BLK = 256
B_TILES = 3
B_SCORE_RING = 2

V7X_VMEM_LIMIT = 60000 * 1024

ROW_TILE = 512
BF16_ROWS = 16
CAST_CHUNK_BYTES = 3 << 19


def _rms(xf, g):
    return xf * lax.rsqrt(jnp.mean(xf * xf, axis=-1, keepdims=True) + EPS) * g


def _dot(a, b):
    return jnp.dot(a, b, preferred_element_type=jnp.float32)


def _swiglu(h, w_in_ref, w_out_ref):
    d_ff = w_out_ref.shape[0]
    gate = _dot(h, w_in_ref[:, :d_ff])
    up = _dot(h, w_in_ref[:, d_ff:])
    act = (gate * jax.nn.sigmoid(gate) * up).astype(jnp.bfloat16)
    return _dot(act, w_out_ref[...])


def _cast_chunk_rows(rows, width):
    fits = [r for r in range(BF16_ROWS, rows + 1, BF16_ROWS)
            if rows % r == 0 and r * width * 4 <= CAST_CHUNK_BYTES]
    return max(fits)


def _load_cast(src_hbm, dst_ref):
    rows, width = src_hbm.shape
    chunk = _cast_chunk_rows(rows, width)
    n_chunks = rows // chunk

    def body(stage_ref, sem):
        def copy(c, slot):
            return pltpu.make_async_copy(src_hbm.at[pl.ds(c * chunk, chunk)], stage_ref.at[slot], sem.at[slot])

        copy(0, 0).start()

        def one_chunk(c, carry):
            slot = c % 2

            @pl.when(c + 1 < n_chunks)
            def _():
                copy(c + 1, 1 - slot).start()

            copy(c, slot).wait()
            dst_ref[pl.ds(pl.multiple_of(c * chunk, chunk), chunk)] = stage_ref[slot].astype(dst_ref.dtype)
            return carry

        lax.fori_loop(0, n_chunks, one_chunk, 0)

    pl.run_scoped(body, pltpu.VMEM((2, chunk, width), src_hbm.dtype), pltpu.SemaphoreType.DMA((2,)))


def _load_weights_once(pairs):
    @pl.when(pl.program_id(0) == 0)
    def _():
        for src_hbm, dst_ref in pairs:
            _load_cast(src_hbm, dst_ref)


def _ffn_mix_kernel(x_ref, g1_ref, w_in_hbm, w_out_hbm, gm_ref, w_mix_hbm, b_gate_ref,
                    x1_ref, qa_ref, ka_ref, va_ref, qb_ref, kb_ref, vb_ref, gates_ref,
                    w_in_ref, w_out_ref, w_mix_ref):
    _load_weights_once([(w_in_hbm, w_in_ref), (w_out_hbm, w_out_ref), (w_mix_hbm, w_mix_ref)])
    x = x_ref[...]
    h = _rms(x, g1_ref[...]).astype(jnp.bfloat16)
    x1 = x + 0.5 * _swiglu(h, w_in_ref, w_out_ref)
    x1_ref[...] = x1

    u = _rms(x1, gm_ref[...]).astype(jnp.bfloat16)
    width = ka_ref.shape[1]
    scale = HEAD_DIM ** -0.5 * LOG2E
    outs = (qa_ref, ka_ref, va_ref, qb_ref, kb_ref, vb_ref)
    for n, o_ref in enumerate(outs):
        p = _dot(u, w_mix_ref[:, n * width:(n + 1) * width])
        if o_ref is ka_ref or o_ref is kb_ref:
            o_ref[...] = p.astype(o_ref.dtype)
        elif o_ref is qa_ref or o_ref is qb_ref:
            _store_transposed(p * scale, o_ref)
        else:
            _store_transposed(p, o_ref)
    logits = _dot(u, w_mix_ref[:, len(outs) * width:]) + b_gate_ref[...]
    gates_ref[...] = jax.nn.sigmoid(logits)


def _store_transposed(p, o_ref):
    n_blk, n_pairs, _, blk = o_ref.shape[1:]
    pt = p.T
    for c in range(n_blk):
        for pr in range(n_pairs):
            o_ref[0, c, pr] = pt[pr * PAIR:(pr + 1) * PAIR, c * blk:(c + 1) * blk].astype(o_ref.dtype)


def _out_ffn_kernel(x1_ref, ya_ref, yb_ref, gates_ref, wa_hbm, wb_hbm, wo_hbm,
                    g2_ref, w_in_hbm, w_out_hbm, gf_ref, o_ref,
                    wa_ref, wb_ref, wo_ref, w_in_ref, w_out_ref, *, final_norm):
    _load_weights_once([(wa_hbm, wa_ref), (wb_hbm, wb_ref), (wo_hbm, wo_ref),
                        (w_in_hbm, w_in_ref), (w_out_hbm, w_out_ref)])
    d = x1_ref.shape[1]
    merged = (gates_ref[:, :d] * _dot(ya_ref[...], wa_ref[...])
              + gates_ref[:, d:] * _dot(yb_ref[...], wb_ref[...]))
    x2 = x1_ref[...] + _dot(merged.astype(jnp.bfloat16), wo_ref[...])
    h = _rms(x2, g2_ref[...]).astype(jnp.bfloat16)
    x3 = x2 + 0.5 * _swiglu(h, w_in_ref, w_out_ref)
    o_ref[...] = _rms(x3, gf_ref[...]) if final_norm else x3


def _resident(shape):
    return pl.BlockSpec(shape, lambda *_: (0,) * len(shape), pipeline_mode=pl.Buffered(1))


_IN_HBM = pl.BlockSpec(memory_space=pl.ANY)


def _bf16_scratch(*weights):
    return [pltpu.VMEM(w.shape, jnp.bfloat16) for w in weights]


def _rows(tm, width):
    return pl.BlockSpec((tm, width), lambda i: (i, 0))


def _ffn_mix(x, g1, w_in, w_out, gm, w_mix, b_gate, width, seq):
    m, d = x.shape
    tm = ROW_TILE
    bf = jnp.bfloat16
    n_pairs = width // PAIR
    steps_per_seq = seq // tm

    def transposed(blk):
        shape = (m // seq, seq // blk, n_pairs, PAIR, blk)
        spec = pl.BlockSpec((1, tm // blk) + shape[2:], lambda i: (i // steps_per_seq, i % steps_per_seq, 0, 0, 0))
        return jax.ShapeDtypeStruct(shape, bf), spec

    keys = jax.ShapeDtypeStruct((m, width), bf), _rows(tm, width)
    outs = [(jax.ShapeDtypeStruct((m, d), jnp.float32), _rows(tm, d)),
            transposed(A_BLK), keys, transposed(A_BLK), transposed(B_BLK), keys, transposed(B_BLK),
            (jax.ShapeDtypeStruct((m, b_gate.shape[1]), jnp.float32), _rows(tm, b_gate.shape[1]))]
    return pl.pallas_call(
        _ffn_mix_kernel,
        grid=(m // tm,),
        in_specs=[_rows(tm, d), _resident(g1.shape), _IN_HBM, _IN_HBM,
                  _resident(gm.shape), _IN_HBM, _resident(b_gate.shape)],
        out_specs=[spec for _, spec in outs],
        out_shape=[shape for shape, _ in outs],
        scratch_shapes=_bf16_scratch(w_in, w_out, w_mix),
        compiler_params=pltpu.CompilerParams(dimension_semantics=("arbitrary",),
                                             vmem_limit_bytes=V7X_VMEM_LIMIT),
        name="ffn_mix",
    )(x, g1, w_in, w_out, gm, w_mix, b_gate)


def _out_ffn(x1, ya, yb, gates, wa, wb, wo, g2, w_in, w_out, gf, final_norm):
    m, d = x1.shape
    tm = ROW_TILE
    return pl.pallas_call(
        functools.partial(_out_ffn_kernel, final_norm=final_norm),
        grid=(m // tm,),
        in_specs=[_rows(tm, d), _rows(tm, ya.shape[1]), _rows(tm, yb.shape[1]), _rows(tm, gates.shape[1]),
                  _IN_HBM, _IN_HBM, _IN_HBM, _resident(g2.shape), _IN_HBM, _IN_HBM, _resident(gf.shape)],
        out_specs=_rows(tm, d),
        out_shape=jax.ShapeDtypeStruct((m, d), jnp.float32),
        scratch_shapes=_bf16_scratch(wa, wb, wo, w_in, w_out),
        compiler_params=pltpu.CompilerParams(dimension_semantics=("arbitrary",),
                                             vmem_limit_bytes=V7X_VMEM_LIMIT),
        name="out_ffn",
    )(x1, ya, yb, gates, wa, wb, wo, g2, w_in, w_out, gf)


def _t5_bucket(rel):
    nb = T5_BUCKETS // 2
    ret = jnp.where(rel > 0, nb, 0)
    n = jnp.abs(rel)
    max_exact = nb // 2
    nf = jnp.maximum(n, 1).astype(jnp.float32)
    large = max_exact + (jnp.log(nf / max_exact) / math.log(T5_MAX_DIST / max_exact)
                         * (nb - max_exact)).astype(jnp.int32)
    large = jnp.minimum(large, nb - 1)
    return ret + jnp.where(n < max_exact, n, large)


def _window_rel(n_tiles, blk):
    nb = blk // LANES
    u = jnp.arange(-(nb - 1), n_tiles * nb, dtype=jnp.int32)[:, None]
    return LANES - jnp.arange(2 * LANES, dtype=jnp.int32)[None, :] - LANES * u


def _bias_tiles_kernel(rel_bias_ref, tiles_ref, *, left_chunks):
    n_pairs, n_tiles, blk, _ = tiles_ref.shape
    nb = blk // LANES
    kj = lax.broadcasted_iota(jnp.int32, (LANES, LANES), 0)
    qi = lax.broadcasted_iota(jnp.int32, (LANES, LANES), 1)
    chunk_diff = kj // CHUNK - qi // CHUNK
    for p in range(n_pairs):
        for half in range(2):
            for w in range(rel_bias_ref.shape[2]):
                u = w - (nb - 1)
                row = jnp.broadcast_to(rel_bias_ref[p, half, w], (LANES, 2 * LANES))
                sub = pltpu.roll(row, 0, 1, stride=1, stride_axis=0)[:, LANES:]
                for t in range(n_tiles):
                    for a in range(nb):
                        c = t * nb + a - u
                        if not 0 <= c < nb:
                            continue
                        diff = chunk_diff - u * SUB_CHUNKS
                        allowed = diff <= 0
                        if left_chunks is not None:
                            allowed = allowed & (diff >= -left_chunks)
                        tiles_ref[p, t, pl.ds(c * LANES, LANES), pl.ds(half * blk + a * LANES, LANES)] = (
                            jnp.where(allowed, sub * LOG2E, NEG))


def _bias_tiles(rel_bias, n_tiles, blk, left_chunks):
    n_pairs = rel_bias.shape[0]
    return pl.pallas_call(
        functools.partial(_bias_tiles_kernel, left_chunks=left_chunks),
        out_shape=jax.ShapeDtypeStruct((n_pairs, n_tiles, blk, 2 * blk), jnp.float32),
        compiler_params=pltpu.CompilerParams(vmem_limit_bytes=V7X_VMEM_LIMIT),
        name="bias_tiles",
    )(rel_bias)


def _sum_rows(x):
    acc = x[:SUBLANES]
    for r in range(1, x.shape[0] // SUBLANES):
        acc = acc + x[r * SUBLANES:(r + 1) * SUBLANES]
    return acc


def _fold_rows(x):
    m = x[:SUBLANES]
    for r in range(1, x.shape[0] // SUBLANES):
        m = jnp.maximum(m, x[r * SUBLANES:(r + 1) * SUBLANES])
    return m


def _attn_kernel(*refs, left_blocks, lambda_init):
    if lambda_init is None:
        qt_ref, k_ref, vt_ref, tiles_ref, o_ref, q2t_ref, s_ref, acc_ref = refs
    else:
        (qt_ref, k_ref, vt_ref, tiles_ref, lq1_ref, lk1_ref, lq2_ref, lk2_ref, subg_ref, o_ref,
         q2t_ref, *s_refs, m_ref, acc_ref) = refs
        lam = (jnp.exp(jnp.sum(lq1_ref[...] * lk1_ref[...], axis=-1, keepdims=True))
               - jnp.exp(jnp.sum(lq2_ref[...] * lk2_ref[...], axis=-1, keepdims=True))
               + lambda_init)
    n_pairs, _, blk = qt_ref.shape[2:]
    i = pl.program_id(1)
    top_half = lax.broadcasted_iota(jnp.int32, (PAIR, blk), 0) < HEAD_DIM

    def cols(p):
        return slice(p * PAIR, (p + 1) * PAIR)

    for p in range(n_pairs):
        qt = qt_ref[0, 0, p]
        zero = jnp.zeros_like(qt)
        q2t_ref[p, :, :blk] = jnp.where(top_half, qt, zero)
        q2t_ref[p, :, blk:] = jnp.where(top_half, zero, qt)

    def key_rows(j, n=1):
        return pl.ds(pl.multiple_of(j * blk, blk), n * blk)

    def weights_update(vt, e):
        return jnp.concatenate([_dot(vt, e.astype(jnp.bfloat16)), _sum_rows(e)], axis=0)

    if left_blocks is None:
        n_blocks = i + 1
        n_tiles = tiles_ref.shape[1]
        acc_ref[...] = jnp.zeros(acc_ref.shape, jnp.float32)
        m_ref[...] = jnp.full(m_ref.shape, NEG, jnp.float32)

        def prefetch_scores(j, buf_ref, p):
            tile = jnp.minimum(i - j, n_tiles - 1)
            buf_ref[p] = _dot(k_ref[0, key_rows(j), cols(p)], q2t_ref[p]) + tiles_ref[p, tile]

        def online(j, buf_ref, p):
            s = buf_ref[p]
            m_old = m_ref[p]
            m = jnp.maximum(m_old, jnp.max(_fold_rows(s), axis=0, keepdims=True))
            acc_ref[p] = jnp.exp2(m_old - m) * acc_ref[p] + weights_update(vt_ref[0, j, p], jnp.exp2(s - m))
            m_ref[p] = m

        def step(j, cur_ref, nxt_ref=None):
            for p in range(n_pairs):
                if nxt_ref is not None:
                    prefetch_scores(jnp.minimum(j + 1, i), nxt_ref, p)
                online(j, cur_ref, p)

        ring = len(s_refs)

        def ring_trip(t, carry):
            for r in range(ring):
                step(ring * t + r, s_refs[r], s_refs[(r + 1) % ring])
            return carry

        for p in range(n_pairs):
            prefetch_scores(0, s_refs[0], p)
        lax.fori_loop(0, n_blocks // ring, ring_trip, 0)
        first_left = n_blocks - n_blocks % ring
        for left in range(1, ring):
            @pl.when(n_blocks % ring == left)
            def _(left=left):
                for r in range(left):
                    step(first_left + r, s_refs[r], s_refs[r + 1] if r + 1 < left else None)
    else:
        def band(n_left):
            order = range(n_left, -1, -1)
            ms = [jnp.full((SUBLANES, 2 * blk), NEG, jnp.float32) for _ in range(n_pairs)]
            for d in order:
                for p in range(n_pairs):
                    s = _dot(k_ref[0, key_rows(i - d), cols(p)], q2t_ref[p]) + tiles_ref[p, d]
                    s_ref[d, p] = s
                    ms[p] = jnp.maximum(ms[p], _fold_rows(s))
            ms = [jnp.max(m, axis=0, keepdims=True) for m in ms]
            for d in order:
                for p in range(n_pairs):
                    update = weights_update(vt_ref[0, i - d, p], jnp.exp2(s_ref[d, p] - ms[p]))
                    acc_ref[p] = update if d == n_left else acc_ref[p] + update

        for n_left in range(left_blocks):
            pl.when(i == n_left)(functools.partial(band, n_left))
        pl.when(i >= left_blocks)(functools.partial(band, left_blocks))

    for p in range(n_pairs):
        o = acc_ref[p, :PAIR] / jnp.sum(acc_ref[p, PAIR:], axis=0, keepdims=True)
        if lambda_init is None:
            y = jnp.where(top_half, o[:, :blk], o[:, blk:]).T
        else:
            d = o[:, :blk] - lam * o[:, blk:]
            d = d * lax.rsqrt(jnp.mean(d * d, axis=0, keepdims=True) + EPS)
            y = d.T * subg_ref[...] * (1.0 - lambda_init)
        o_ref[0, :, cols(p)] = y.astype(o_ref.dtype)


def _attention(qt, k, vt, tiles, left_blocks, lambda_init=None, extra=()):
    b, s, w = k.shape
    n_pairs, _, blk = qt.shape[2:]
    blk_q = pl.BlockSpec((1, 1) + qt.shape[2:], lambda bi, i: (bi, i, 0, 0, 0))
    whole_k = pl.BlockSpec((1, s, w), lambda bi, i: (bi, 0, 0))
    whole_v = pl.BlockSpec((1,) + vt.shape[1:], lambda bi, i: (bi, 0, 0, 0, 0))
    f32 = jnp.float32
    score_block = (n_pairs, blk, 2 * blk)
    if left_blocks is None:
        score_scratch = [pltpu.VMEM(score_block, f32)] * B_SCORE_RING + [pltpu.VMEM((n_pairs, 1, 2 * blk), f32)]
    else:
        score_scratch = [pltpu.VMEM((left_blocks + 1,) + score_block, f32)]
    return pl.pallas_call(
        functools.partial(_attn_kernel, left_blocks=left_blocks, lambda_init=lambda_init),
        grid=(b, s // blk),
        in_specs=[blk_q, whole_k, whole_v, _resident(tiles.shape)] + [_resident(e.shape) for e in extra],
        out_specs=pl.BlockSpec((1, blk, w), lambda bi, i: (bi, i, 0)),
        out_shape=jax.ShapeDtypeStruct((b, s, w), jnp.bfloat16),
        scratch_shapes=[pltpu.VMEM((n_pairs, PAIR, 2 * blk), jnp.bfloat16), *score_scratch,
                        pltpu.VMEM((n_pairs, ACC_ROWS, 2 * blk), f32)],
        compiler_params=pltpu.CompilerParams(dimension_semantics=("arbitrary", "arbitrary"),
                                             vmem_limit_bytes=V7X_VMEM_LIMIT),
        name="attn_a" if lambda_init is None else "attn_b",
    )(qt, k, vt, tiles, *extra)


def kernel(x, ffn1_norm, ffn1_w_in, ffn1_w_out, mix_norm, w_mix_in, b_gate, rel_bias_a, lambda_q1, lambda_k1, lambda_q2, lambda_k2, subln_g, t5_bias, w_branch_a, w_branch_b, w_o, ffn2_norm, ffn2_w_in, ffn2_w_out, final_norm):
    b, s, d = x.shape
    depth = ffn1_norm.shape[0]
    assert depth >= 1
    width = w_branch_a.shape[1]
    n_pairs = width // PAIR
    f32 = jnp.float32

    t5_rel = t5_bias.astype(f32)[:, _t5_bucket(_window_rel(B_TILES, B_BLK))]
    t5_rel = jnp.broadcast_to(t5_rel[:, None, :, None, :], (n_pairs, 2) + t5_rel.shape[1:2] + (1, 2 * LANES))
    tiles_b = _bias_tiles(t5_rel, B_TILES, B_BLK, None)

    xf = x.astype(f32).reshape(b * s, d)
    for li in range(depth):
        x1, qa, ka, va, qb, kb, vb, gates = _ffn_mix(
            xf, ffn1_norm[li][None].astype(f32), ffn1_w_in[li].astype(f32), ffn1_w_out[li].astype(f32),
            mix_norm[li][None].astype(f32), w_mix_in[li].astype(f32), b_gate[li][None].astype(f32), width, s)

        rel_a = jnp.clip(_window_rel(A_TILES, A_BLK), -REL_CLIP, REL_CLIP) + REL_CLIP
        a_rel = rel_bias_a[li].astype(f32)[:, rel_a]
        tiles_a = _bias_tiles(a_rel.reshape(n_pairs, 2, rel_a.shape[0], 1, 2 * LANES), A_TILES, A_BLK, LEFT_CHUNKS)

        def seq(t):
            return t.reshape(b, s, width)

        ya = _attention(qa, seq(ka), va, tiles_a, A_TILES - 1)
        lambda_init = 0.8 - 0.6 * math.exp(-0.3 * li)
        lams = [t[li][None].astype(f32) for t in (lambda_q1, lambda_k1, lambda_q2, lambda_k2)]
        yb = _attention(qb, seq(kb), vb, tiles_b, None, lambda_init=lambda_init,
                        extra=(*lams, subln_g[li][None].astype(f32)))

        xf = _out_ffn(x1, ya.reshape(b * s, width), yb.reshape(b * s, width), gates,
                      w_branch_a[li].astype(f32), w_branch_b[li].astype(f32), w_o[li].astype(f32),
                      ffn2_norm[li][None].astype(f32), ffn2_w_in[li].astype(f32), ffn2_w_out[li].astype(f32),
                      final_norm[None].astype(f32), final_norm=(li == depth - 1))
    return xf.reshape(b, s, d).astype(x.dtype)
```

```python
import functools
import math

import jax
import jax.numpy as jnp
from jax import lax
from jax.experimental import pallas as pl
from jax.experimental.pallas import tpu as pltpu

EPS = 1e-6
NEG = -1e30
LOG2E = math.log2(math.e)

CHUNK = 64
LEFT_CHUNKS = 8
REL_CLIP = 128
T5_BUCKETS = 32
T5_MAX_DIST = 128

LANES = 128
SUBLANES = 8
HEAD_DIM = 64
PAIR = 2 * HEAD_DIM
SUB_CHUNKS = LANES // CHUNK
ACC_ROWS = PAIR + SUBLANES

A_BLK = 128
A_TILES = LEFT_CHUNKS * CHUNK // A_BLK + 1
B_BLK = 256
B_TILES = 3
B_SCORE_RING = 2

V7X_VMEM_LIMIT = 60000 * 1024

ROW_TILE = 512
BF16_ROWS = 16
CAST_CHUNK_BYTES = 3 << 18
CAST_SLOTS = 4


def _rms(xf, g):
    return xf * lax.rsqrt(jnp.mean(xf * xf, axis=-1, keepdims=True) + EPS) * g


def _dot(a, b):
    return jnp.dot(a, b, preferred_element_type=jnp.float32)


def _swiglu(h, w_in_ref, w_out_ref):
    d_ff = w_out_ref.shape[0]
    gate = _dot(h, w_in_ref[:, :d_ff])
    up = _dot(h, w_in_ref[:, d_ff:])
    act = (gate * jax.nn.sigmoid(gate) * up).astype(jnp.bfloat16)
    return _dot(act, w_out_ref[...])


def _cast_chunk_rows(rows, width):
    fits = [r for r in range(BF16_ROWS, rows + 1, BF16_ROWS)
            if rows % r == 0 and r * width * 4 <= CAST_CHUNK_BYTES]
    return max(fits)


def _load_cast(src_hbm, dst_ref):
    rows, width = src_hbm.shape
    chunk = _cast_chunk_rows(rows, width)
    n_chunks = rows // chunk

    ahead = min(CAST_SLOTS, n_chunks) - 1

    def body(stage_ref, sem):
        def copy(c):
            slot = c % CAST_SLOTS
            return pltpu.make_async_copy(src_hbm.at[pl.ds(c * chunk, chunk)], stage_ref.at[slot], sem.at[slot])

        for c in range(ahead):
            copy(c).start()

        def one_chunk(c, carry):
            @pl.when(c + ahead < n_chunks)
            def _():
                copy(c + ahead).start()

            copy(c).wait()
            dst_ref[pl.ds(pl.multiple_of(c * chunk, chunk), chunk)] = (
                stage_ref[c % CAST_SLOTS].astype(dst_ref.dtype))
            return carry

        lax.fori_loop(0, n_chunks, one_chunk, 0)

    pl.run_scoped(body, pltpu.VMEM((CAST_SLOTS, chunk, width), src_hbm.dtype),
                  pltpu.SemaphoreType.DMA((CAST_SLOTS,)))


def _load_weights_once(pairs):
    @pl.when(pl.program_id(0) == 0)
    def _():
        for src_hbm, dst_ref in pairs:
            _load_cast(src_hbm, dst_ref)


def _ffn_mix_kernel(x_ref, g1_ref, w_in_hbm, w_out_hbm, gm_ref, w_mix_hbm, b_gate_ref,
                    x1_ref, qa_ref, ka_ref, va_ref, qb_ref, kb_ref, vb_ref, gates_ref,
                    w_in_ref, w_out_ref, w_mix_ref):
    _load_weights_once([(w_in_hbm, w_in_ref), (w_out_hbm, w_out_ref), (w_mix_hbm, w_mix_ref)])
    x = x_ref[...]
    h = _rms(x, g1_ref[...]).astype(jnp.bfloat16)
    x1 = x + 0.5 * _swiglu(h, w_in_ref, w_out_ref)
    x1_ref[...] = x1

    u = _rms(x1, gm_ref[...]).astype(jnp.bfloat16)
    width = ka_ref.shape[1]
    scale = HEAD_DIM ** -0.5 * LOG2E
    outs = (qa_ref, ka_ref, va_ref, qb_ref, kb_ref, vb_ref)
    for n, o_ref in enumerate(outs):
        p = _dot(u, w_mix_ref[:, n * width:(n + 1) * width])
        if o_ref is ka_ref or o_ref is kb_ref:
            o_ref[...] = p.astype(o_ref.dtype)
        elif o_ref is qa_ref or o_ref is qb_ref:
            _store_transposed(p * scale, o_ref)
        else:
            _store_transposed(p, o_ref)
    logits = _dot(u, w_mix_ref[:, len(outs) * width:]) + b_gate_ref[...]
    gates_ref[...] = jax.nn.sigmoid(logits)


def _store_transposed(p, o_ref):
    n_blk, n_pairs, _, blk = o_ref.shape[1:]
    pt = p.T
    for c in range(n_blk):
        for pr in range(n_pairs):
            o_ref[0, c, pr] = pt[pr * PAIR:(pr + 1) * PAIR, c * blk:(c + 1) * blk].astype(o_ref.dtype)


def _out_ffn_kernel(x1_ref, ya_ref, yb_ref, gates_ref, wa_hbm, wb_hbm, wo_hbm,
                    g2_ref, w_in_hbm, w_out_hbm, gf_ref, o_ref,
                    wa_ref, wb_ref, wo_ref, w_in_ref, w_out_ref, *, final_norm):
    _load_weights_once([(wa_hbm, wa_ref), (wb_hbm, wb_ref), (wo_hbm, wo_ref),
                        (w_in_hbm, w_in_ref), (w_out_hbm, w_out_ref)])
    d = x1_ref.shape[1]
    merged = (gates_ref[:, :d] * _dot(ya_ref[...], wa_ref[...])
              + gates_ref[:, d:] * _dot(yb_ref[...], wb_ref[...]))
    x2 = x1_ref[...] + _dot(merged.astype(jnp.bfloat16), wo_ref[...])
    h = _rms(x2, g2_ref[...]).astype(jnp.bfloat16)
    x3 = x2 + 0.5 * _swiglu(h, w_in_ref, w_out_ref)
    o_ref[...] = _rms(x3, gf_ref[...]) if final_norm else x3


def _resident(shape):
    return pl.BlockSpec(shape, lambda *_: (0,) * len(shape), pipeline_mode=pl.Buffered(1))


_IN_HBM = pl.BlockSpec(memory_space=pl.ANY)


def _bf16_scratch(*weights):
    return [pltpu.VMEM(w.shape, jnp.bfloat16) for w in weights]


def _rows(tm, width):
    return pl.BlockSpec((tm, width), lambda i: (i, 0))


def _ffn_mix(x, g1, w_in, w_out, gm, w_mix, b_gate, width, seq):
    m, d = x.shape
    tm = ROW_TILE
    bf = jnp.bfloat16
    n_pairs = width // PAIR
    steps_per_seq = seq // tm

    def transposed(blk):
        shape = (m // seq, seq // blk, n_pairs, PAIR, blk)
        spec = pl.BlockSpec((1, tm // blk) + shape[2:], lambda i: (i // steps_per_seq, i % steps_per_seq, 0, 0, 0))
        return jax.ShapeDtypeStruct(shape, bf), spec

    keys = jax.ShapeDtypeStruct((m, width), bf), _rows(tm, width)
    outs = [(jax.ShapeDtypeStruct((m, d), jnp.float32), _rows(tm, d)),
            transposed(A_BLK), keys, transposed(A_BLK), transposed(B_BLK), keys, transposed(B_BLK),
            (jax.ShapeDtypeStruct((m, b_gate.shape[1]), jnp.float32), _rows(tm, b_gate.shape[1]))]
    return pl.pallas_call(
        _ffn_mix_kernel,
        grid=(m // tm,),
        in_specs=[_rows(tm, d), _resident(g1.shape), _IN_HBM, _IN_HBM,
                  _resident(gm.shape), _IN_HBM, _resident(b_gate.shape)],
        out_specs=[spec for _, spec in outs],
        out_shape=[shape for shape, _ in outs],
        scratch_shapes=_bf16_scratch(w_in, w_out, w_mix),
        compiler_params=pltpu.CompilerParams(dimension_semantics=("arbitrary",),
                                             vmem_limit_bytes=V7X_VMEM_LIMIT),
        name="ffn_mix",
    )(x, g1, w_in, w_out, gm, w_mix, b_gate)


def _out_ffn(x1, ya, yb, gates, wa, wb, wo, g2, w_in, w_out, gf, final_norm):
    m, d = x1.shape
    tm = ROW_TILE
    return pl.pallas_call(
        functools.partial(_out_ffn_kernel, final_norm=final_norm),
        grid=(m // tm,),
        in_specs=[_rows(tm, d), _rows(tm, ya.shape[1]), _rows(tm, yb.shape[1]), _rows(tm, gates.shape[1]),
                  _IN_HBM, _IN_HBM, _IN_HBM, _resident(g2.shape), _IN_HBM, _IN_HBM, _resident(gf.shape)],
        out_specs=_rows(tm, d),
        out_shape=jax.ShapeDtypeStruct((m, d), jnp.float32),
        scratch_shapes=_bf16_scratch(wa, wb, wo, w_in, w_out),
        compiler_params=pltpu.CompilerParams(dimension_semantics=("arbitrary",),
                                             vmem_limit_bytes=V7X_VMEM_LIMIT),
        name="out_ffn",
    )(x1, ya, yb, gates, wa, wb, wo, g2, w_in, w_out, gf)


def _t5_bucket(rel):
    nb = T5_BUCKETS // 2
    ret = jnp.where(rel > 0, nb, 0)
    n = jnp.abs(rel)
    max_exact = nb // 2
    nf = jnp.maximum(n, 1).astype(jnp.float32)
    large = max_exact + (jnp.log(nf / max_exact) / math.log(T5_MAX_DIST / max_exact)
                         * (nb - max_exact)).astype(jnp.int32)
    large = jnp.minimum(large, nb - 1)
    return ret + jnp.where(n < max_exact, n, large)


def _window_rel(n_tiles, blk):
    nb = blk // LANES
    u = jnp.arange(-(nb - 1), n_tiles * nb, dtype=jnp.int32)[:, None]
    return LANES - jnp.arange(2 * LANES, dtype=jnp.int32)[None, :] - LANES * u


def _bias_tiles_kernel(rel_bias_ref, tiles_ref, *, left_chunks):
    n_pairs, n_tiles, blk, _ = tiles_ref.shape
    nb = blk // LANES
    kj = lax.broadcasted_iota(jnp.int32, (LANES, LANES), 0)
    qi = lax.broadcasted_iota(jnp.int32, (LANES, LANES), 1)
    chunk_diff = kj // CHUNK - qi // CHUNK
    for p in range(n_pairs):
        for half in range(2):
            for w in range(rel_bias_ref.shape[2]):
                u = w - (nb - 1)
                row = jnp.broadcast_to(rel_bias_ref[p, half, w], (LANES, 2 * LANES))
                sub = pltpu.roll(row, 0, 1, stride=1, stride_axis=0)[:, LANES:]
                for t in range(n_tiles):
                    for a in range(nb):
                        c = t * nb + a - u
                        if not 0 <= c < nb:
                            continue
                        diff = chunk_diff - u * SUB_CHUNKS
                        allowed = diff <= 0
                        if left_chunks is not None:
                            allowed = allowed & (diff >= -left_chunks)
                        tiles_ref[p, t, pl.ds(c * LANES, LANES), pl.ds(half * blk + a * LANES, LANES)] = (
                            jnp.where(allowed, sub * LOG2E, NEG))


def _bias_tiles(rel_bias, n_tiles, blk, left_chunks):
    n_pairs = rel_bias.shape[0]
    return pl.pallas_call(
        functools.partial(_bias_tiles_kernel, left_chunks=left_chunks),
        out_shape=jax.ShapeDtypeStruct((n_pairs, n_tiles, blk, 2 * blk), jnp.float32),
        compiler_params=pltpu.CompilerParams(vmem_limit_bytes=V7X_VMEM_LIMIT),
        name="bias_tiles",
    )(rel_bias)


def _sum_rows(x):
    acc = x[:SUBLANES]
    for r in range(1, x.shape[0] // SUBLANES):
        acc = acc + x[r * SUBLANES:(r + 1) * SUBLANES]
    return acc


def _fold_rows(x):
    m = x[:SUBLANES]
    for r in range(1, x.shape[0] // SUBLANES):
        m = jnp.maximum(m, x[r * SUBLANES:(r + 1) * SUBLANES])
    return m


def _attn_kernel(*refs, left_blocks, lambda_init):
    if lambda_init is None:
        qt_ref, k_ref, vt_ref, tiles_ref, o_ref, q2t_ref, s_ref, acc_ref = refs
    else:
        (qt_ref, k_ref, vt_ref, tiles_ref, lq1_ref, lk1_ref, lq2_ref, lk2_ref, subg_ref, o_ref,
         q2t_ref, *s_refs, m_ref, acc_ref) = refs
        lam = (jnp.exp(jnp.sum(lq1_ref[...] * lk1_ref[...], axis=-1, keepdims=True))
               - jnp.exp(jnp.sum(lq2_ref[...] * lk2_ref[...], axis=-1, keepdims=True))
               + lambda_init)
    n_pairs, _, blk = qt_ref.shape[2:]
    i = pl.program_id(1)
    top_half = lax.broadcasted_iota(jnp.int32, (PAIR, blk), 0) < HEAD_DIM

    def cols(p):
        return slice(p * PAIR, (p + 1) * PAIR)

    for p in range(n_pairs):
        qt = qt_ref[0, 0, p]
        zero = jnp.zeros_like(qt)
        q2t_ref[p, :, :blk] = jnp.where(top_half, qt, zero)
        q2t_ref[p, :, blk:] = jnp.where(top_half, zero, qt)

    def key_rows(j, n=1):
        return pl.ds(pl.multiple_of(j * blk, blk), n * blk)

    def weights_update(vt, e):
        return jnp.concatenate([_dot(vt, e.astype(jnp.bfloat16)), _sum_rows(e)], axis=0)

    if left_blocks is None:
        n_blocks = i + 1
        n_tiles = tiles_ref.shape[1]
        acc_ref[...] = jnp.zeros(acc_ref.shape, jnp.float32)
        m_ref[...] = jnp.full(m_ref.shape, NEG, jnp.float32)

        def prefetch_scores(j, buf_ref, p):
            tile = jnp.minimum(i - j, n_tiles - 1)
            buf_ref[p] = _dot(k_ref[0, key_rows(j), cols(p)], q2t_ref[p]) + tiles_ref[p, tile]

        def online(j, buf_ref, p):
            s = buf_ref[p]
            m_old = m_ref[p]
            m = jnp.maximum(m_old, jnp.max(_fold_rows(s), axis=0, keepdims=True))
            acc_ref[p] = jnp.exp2(m_old - m) * acc_ref[p] + weights_update(vt_ref[0, j, p], jnp.exp2(s - m))
            m_ref[p] = m

        def step(j, cur_ref, nxt_ref=None):
            if nxt_ref is not None:
                for p in range(n_pairs):
                    prefetch_scores(jnp.minimum(j + 1, i), nxt_ref, p)
            for p in range(n_pairs):
                online(j, cur_ref, p)

        ring = len(s_refs)

        def ring_trip(t, carry):
            for r in range(ring):
                step(ring * t + r, s_refs[r], s_refs[(r + 1) % ring])
            return carry

        for p in range(n_pairs):
            prefetch_scores(0, s_refs[0], p)
        lax.fori_loop(0, n_blocks // ring, ring_trip, 0)
        first_left = n_blocks - n_blocks % ring
        for left in range(1, ring):
            @pl.when(n_blocks % ring == left)
            def _(left=left):
                for r in range(left):
                    step(first_left + r, s_refs[r], s_refs[r + 1] if r + 1 < left else None)
    else:
        def band(n_left):
            order = range(n_left, -1, -1)
            ms = [jnp.full((SUBLANES, 2 * blk), NEG, jnp.float32) for _ in range(n_pairs)]
            for d in order:
                for p in range(n_pairs):
                    s = _dot(k_ref[0, key_rows(i - d), cols(p)], q2t_ref[p]) + tiles_ref[p, d]
                    s_ref[d, p] = s
                    ms[p] = jnp.maximum(ms[p], _fold_rows(s))
            ms = [jnp.max(m, axis=0, keepdims=True) for m in ms]
            for d in order:
                for p in range(n_pairs):
                    update = weights_update(vt_ref[0, i - d, p], jnp.exp2(s_ref[d, p] - ms[p]))
                    acc_ref[p] = update if d == n_left else acc_ref[p] + update

        for n_left in range(left_blocks):
            pl.when(i == n_left)(functools.partial(band, n_left))
        pl.when(i >= left_blocks)(functools.partial(band, left_blocks))

    for p in range(n_pairs):
        o = acc_ref[p, :PAIR] / jnp.sum(acc_ref[p, PAIR:], axis=0, keepdims=True)
        if lambda_init is None:
            y = jnp.where(top_half, o[:, :blk], o[:, blk:]).T
        else:
            d = o[:, :blk] - lam * o[:, blk:]
            d = d * lax.rsqrt(jnp.mean(d * d, axis=0, keepdims=True) + EPS)
            y = d.T * subg_ref[...] * (1.0 - lambda_init)
        o_ref[0, :, cols(p)] = y.astype(o_ref.dtype)


def _attention(qt, k, vt, tiles, left_blocks, lambda_init=None, extra=()):
    b, s, w = k.shape
    n_pairs, _, blk = qt.shape[2:]
    blk_q = pl.BlockSpec((1, 1) + qt.shape[2:], lambda bi, i: (bi, i, 0, 0, 0))
    whole_k = pl.BlockSpec((1, s, w), lambda bi, i: (bi, 0, 0))
    whole_v = pl.BlockSpec((1,) + vt.shape[1:], lambda bi, i: (bi, 0, 0, 0, 0))
    f32 = jnp.float32
    score_block = (n_pairs, blk, 2 * blk)
    if left_blocks is None:
        score_scratch = [pltpu.VMEM(score_block, f32)] * B_SCORE_RING + [pltpu.VMEM((n_pairs, 1, 2 * blk), f32)]
    else:
        score_scratch = [pltpu.VMEM((left_blocks + 1,) + score_block, f32)]
    return pl.pallas_call(
        functools.partial(_attn_kernel, left_blocks=left_blocks, lambda_init=lambda_init),
        grid=(b, s // blk),
        in_specs=[blk_q, whole_k, whole_v, _resident(tiles.shape)] + [_resident(e.shape) for e in extra],
        out_specs=pl.BlockSpec((1, blk, w), lambda bi, i: (bi, i, 0)),
        out_shape=jax.ShapeDtypeStruct((b, s, w), jnp.bfloat16),
        scratch_shapes=[pltpu.VMEM((n_pairs, PAIR, 2 * blk), jnp.bfloat16), *score_scratch,
                        pltpu.VMEM((n_pairs, ACC_ROWS, 2 * blk), f32)],
        compiler_params=pltpu.CompilerParams(dimension_semantics=("arbitrary", "arbitrary"),
                                             vmem_limit_bytes=V7X_VMEM_LIMIT),
        name="attn_a" if lambda_init is None else "attn_b",
    )(qt, k, vt, tiles, *extra)


def kernel(x, ffn1_norm, ffn1_w_in, ffn1_w_out, mix_norm, w_mix_in, b_gate, rel_bias_a, lambda_q1, lambda_k1, lambda_q2, lambda_k2, subln_g, t5_bias, w_branch_a, w_branch_b, w_o, ffn2_norm, ffn2_w_in, ffn2_w_out, final_norm):
    b, s, d = x.shape
    depth = ffn1_norm.shape[0]
    assert depth >= 1
    width = w_branch_a.shape[1]
    n_pairs = width // PAIR
    f32 = jnp.float32

    t5_rel = t5_bias.astype(f32)[:, _t5_bucket(_window_rel(B_TILES, B_BLK))]
    t5_rel = jnp.broadcast_to(t5_rel[:, None, :, None, :], (n_pairs, 2) + t5_rel.shape[1:2] + (1, 2 * LANES))
    tiles_b = _bias_tiles(t5_rel, B_TILES, B_BLK, None)

    xf = x.astype(f32).reshape(b * s, d)
    for li in range(depth):
        x1, qa, ka, va, qb, kb, vb, gates = _ffn_mix(
            xf, ffn1_norm[li][None].astype(f32), ffn1_w_in[li].astype(f32), ffn1_w_out[li].astype(f32),
            mix_norm[li][None].astype(f32), w_mix_in[li].astype(f32), b_gate[li][None].astype(f32), width, s)

        rel_a = jnp.clip(_window_rel(A_TILES, A_BLK), -REL_CLIP, REL_CLIP) + REL_CLIP
        a_rel = rel_bias_a[li].astype(f32)[:, rel_a]
        tiles_a = _bias_tiles(a_rel.reshape(n_pairs, 2, rel_a.shape[0], 1, 2 * LANES), A_TILES, A_BLK, LEFT_CHUNKS)

        def seq(t):
            return t.reshape(b, s, width)

        ya = _attention(qa, seq(ka), va, tiles_a, A_TILES - 1)
        lambda_init = 0.8 - 0.6 * math.exp(-0.3 * li)
        lams = [t[li][None].astype(f32) for t in (lambda_q1, lambda_k1, lambda_q2, lambda_k2)]
        yb = _attention(qb, seq(kb), vb, tiles_b, None, lambda_init=lambda_init,
                        extra=(*lams, subln_g[li][None].astype(f32)))

        xf = _out_ffn(x1, ya.reshape(b * s, width), yb.reshape(b * s, width), gates,
                      w_branch_a[li].astype(f32), w_branch_b[li].astype(f32), w_o[li].astype(f32),
                      ffn2_norm[li][None].astype(f32), ffn2_w_in[li].astype(f32), ffn2_w_out[li].astype(f32),
                      final_norm[None].astype(f32), final_norm=(li == depth - 1))
    return xf.reshape(b, s, d).astype(x.dtype)
```

```python
import functools
import math

import jax
import jax.numpy as jnp
from jax import lax
from jax.experimental import pallas as pl
from jax.experimental.pallas import tpu as pltpu

EPS = 1e-6
NEG = -1e30
LOG2E = math.log2(math.e)

CHUNK = 64
LEFT_CHUNKS = 8
REL_CLIP = 128
T5_BUCKETS = 32
T5_MAX_DIST = 128

LANES = 128
SUBLANES = 8
HEAD_DIM = 64
PAIR = 2 * HEAD_DIM
SUB_CHUNKS = LANES // CHUNK
ACC_ROWS = PAIR + SUBLANES

A_BLK = 128
A_TILES = LEFT_CHUNKS * CHUNK // A_BLK + 1
A_STEP_BLOCKS = 2
B_BLK = 256
B_TILES = 3
B_SCORE_RING = 2

V7X_VMEM_LIMIT = 60000 * 1024

ROW_TILE = 512
BF16_ROWS = 16
CAST_CHUNK_BYTES = 3 << 18
CAST_SLOTS = 4


def _rms(xf, g):
    return xf * lax.rsqrt(jnp.mean(xf * xf, axis=-1, keepdims=True) + EPS) * g


def _dot(a, b):
    return jnp.dot(a, b, preferred_element_type=jnp.float32)


def _swiglu(h, w_in_ref, w_out_ref):
    d_ff = w_out_ref.shape[0]
    gate = _dot(h, w_in_ref[:, :d_ff])
    up = _dot(h, w_in_ref[:, d_ff:])
    act = (gate * jax.nn.sigmoid(gate) * up).astype(jnp.bfloat16)
    return _dot(act, w_out_ref[...])


def _cast_chunk_rows(rows, width):
    fits = [r for r in range(BF16_ROWS, rows + 1, BF16_ROWS)
            if rows % r == 0 and r * width * 4 <= CAST_CHUNK_BYTES]
    return max(fits)


def _load_cast(src_hbm, dst_ref):
    rows, width = src_hbm.shape
    chunk = _cast_chunk_rows(rows, width)
    n_chunks = rows // chunk

    ahead = min(CAST_SLOTS, n_chunks) - 1

    def body(stage_ref, sem):
        def copy(c):
            slot = c % CAST_SLOTS
            return pltpu.make_async_copy(src_hbm.at[pl.ds(c * chunk, chunk)], stage_ref.at[slot], sem.at[slot])

        for c in range(ahead):
            copy(c).start()

        def one_chunk(c, carry):
            @pl.when(c + ahead < n_chunks)
            def _():
                copy(c + ahead).start()

            copy(c).wait()
            dst_ref[pl.ds(pl.multiple_of(c * chunk, chunk), chunk)] = (
                stage_ref[c % CAST_SLOTS].astype(dst_ref.dtype))
            return carry

        lax.fori_loop(0, n_chunks, one_chunk, 0)

    pl.run_scoped(body, pltpu.VMEM((CAST_SLOTS, chunk, width), src_hbm.dtype),
                  pltpu.SemaphoreType.DMA((CAST_SLOTS,)))


def _load_weights_once(pairs):
    @pl.when(pl.program_id(0) == 0)
    def _():
        for src_hbm, dst_ref in pairs:
            _load_cast(src_hbm, dst_ref)


def _ffn_mix_kernel(x_ref, g1_ref, w_in_hbm, w_out_hbm, gm_ref, w_mix_hbm, b_gate_ref,
                    x1_ref, qa_ref, ka_ref, va_ref, qb_ref, kb_ref, vb_ref, gates_ref,
                    w_in_ref, w_out_ref, w_mix_ref):
    _load_weights_once([(w_in_hbm, w_in_ref), (w_out_hbm, w_out_ref), (w_mix_hbm, w_mix_ref)])
    x = x_ref[...]
    h = _rms(x, g1_ref[...]).astype(jnp.bfloat16)
    x1 = x + 0.5 * _swiglu(h, w_in_ref, w_out_ref)
    x1_ref[...] = x1

    u = _rms(x1, gm_ref[...]).astype(jnp.bfloat16)
    width = ka_ref.shape[1]
    scale = HEAD_DIM ** -0.5 * LOG2E
    outs = (qa_ref, ka_ref, va_ref, qb_ref, kb_ref, vb_ref)
    for n, o_ref in enumerate(outs):
        p = _dot(u, w_mix_ref[:, n * width:(n + 1) * width])
        if o_ref is ka_ref or o_ref is kb_ref:
            o_ref[...] = p.astype(o_ref.dtype)
        elif o_ref is qa_ref or o_ref is qb_ref:
            _store_transposed(p * scale, o_ref)
        else:
            _store_transposed(p, o_ref)
    logits = _dot(u, w_mix_ref[:, len(outs) * width:]) + b_gate_ref[...]
    gates_ref[...] = jax.nn.sigmoid(logits)


def _store_transposed(p, o_ref):
    n_blk, n_pairs, _, blk = o_ref.shape[1:]
    pt = p.T
    for c in range(n_blk):
        for pr in range(n_pairs):
            o_ref[0, c, pr] = pt[pr * PAIR:(pr + 1) * PAIR, c * blk:(c + 1) * blk].astype(o_ref.dtype)


def _out_ffn_kernel(x1_ref, ya_ref, yb_ref, gates_ref, wa_hbm, wb_hbm, wo_hbm,
                    g2_ref, w_in_hbm, w_out_hbm, gf_ref, o_ref,
                    wa_ref, wb_ref, wo_ref, w_in_ref, w_out_ref, *, final_norm):
    _load_weights_once([(wa_hbm, wa_ref), (wb_hbm, wb_ref), (wo_hbm, wo_ref),
                        (w_in_hbm, w_in_ref), (w_out_hbm, w_out_ref)])
    d = x1_ref.shape[1]
    merged = (gates_ref[:, :d] * _dot(ya_ref[...], wa_ref[...])
              + gates_ref[:, d:] * _dot(yb_ref[...], wb_ref[...]))
    x2 = x1_ref[...] + _dot(merged.astype(jnp.bfloat16), wo_ref[...])
    h = _rms(x2, g2_ref[...]).astype(jnp.bfloat16)
    x3 = x2 + 0.5 * _swiglu(h, w_in_ref, w_out_ref)
    o_ref[...] = _rms(x3, gf_ref[...]) if final_norm else x3


def _resident(shape):
    return pl.BlockSpec(shape, lambda *_: (0,) * len(shape), pipeline_mode=pl.Buffered(1))


_IN_HBM = pl.BlockSpec(memory_space=pl.ANY)


def _bf16_scratch(*weights):
    return [pltpu.VMEM(w.shape, jnp.bfloat16) for w in weights]


def _rows(tm, width):
    return pl.BlockSpec((tm, width), lambda i: (i, 0))


def _ffn_mix(x, g1, w_in, w_out, gm, w_mix, b_gate, width, seq):
    m, d = x.shape
    tm = ROW_TILE
    bf = jnp.bfloat16
    n_pairs = width // PAIR
    steps_per_seq = seq // tm

    def transposed(blk):
        shape = (m // seq, seq // blk, n_pairs, PAIR, blk)
        spec = pl.BlockSpec((1, tm // blk) + shape[2:], lambda i: (i // steps_per_seq, i % steps_per_seq, 0, 0, 0))
        return jax.ShapeDtypeStruct(shape, bf), spec

    keys = jax.ShapeDtypeStruct((m, width), bf), _rows(tm, width)
    outs = [(jax.ShapeDtypeStruct((m, d), jnp.float32), _rows(tm, d)),
            transposed(A_BLK), keys, transposed(A_BLK), transposed(B_BLK), keys, transposed(B_BLK),
            (jax.ShapeDtypeStruct((m, b_gate.shape[1]), jnp.float32), _rows(tm, b_gate.shape[1]))]
    return pl.pallas_call(
        _ffn_mix_kernel,
        grid=(m // tm,),
        in_specs=[_rows(tm, d), _resident(g1.shape), _IN_HBM, _IN_HBM,
                  _resident(gm.shape), _IN_HBM, _resident(b_gate.shape)],
        out_specs=[spec for _, spec in outs],
        out_shape=[shape for shape, _ in outs],
        scratch_shapes=_bf16_scratch(w_in, w_out, w_mix),
        compiler_params=pltpu.CompilerParams(dimension_semantics=("arbitrary",),
                                             vmem_limit_bytes=V7X_VMEM_LIMIT),
        name="ffn_mix",
    )(x, g1, w_in, w_out, gm, w_mix, b_gate)


def _out_ffn(x1, ya, yb, gates, wa, wb, wo, g2, w_in, w_out, gf, final_norm):
    m, d = x1.shape
    tm = ROW_TILE
    return pl.pallas_call(
        functools.partial(_out_ffn_kernel, final_norm=final_norm),
        grid=(m // tm,),
        in_specs=[_rows(tm, d), _rows(tm, ya.shape[1]), _rows(tm, yb.shape[1]), _rows(tm, gates.shape[1]),
                  _IN_HBM, _IN_HBM, _IN_HBM, _resident(g2.shape), _IN_HBM, _IN_HBM, _resident(gf.shape)],
        out_specs=_rows(tm, d),
        out_shape=jax.ShapeDtypeStruct((m, d), jnp.float32),
        scratch_shapes=_bf16_scratch(wa, wb, wo, w_in, w_out),
        compiler_params=pltpu.CompilerParams(dimension_semantics=("arbitrary",),
                                             vmem_limit_bytes=V7X_VMEM_LIMIT),
        name="out_ffn",
    )(x1, ya, yb, gates, wa, wb, wo, g2, w_in, w_out, gf)


def _t5_bucket(rel):
    nb = T5_BUCKETS // 2
    ret = jnp.where(rel > 0, nb, 0)
    n = jnp.abs(rel)
    max_exact = nb // 2
    nf = jnp.maximum(n, 1).astype(jnp.float32)
    large = max_exact + (jnp.log(nf / max_exact) / math.log(T5_MAX_DIST / max_exact)
                         * (nb - max_exact)).astype(jnp.int32)
    large = jnp.minimum(large, nb - 1)
    return ret + jnp.where(n < max_exact, n, large)


def _window_rel(n_tiles, blk):
    nb = blk // LANES
    u = jnp.arange(-(nb - 1), n_tiles * nb, dtype=jnp.int32)[:, None]
    return LANES - jnp.arange(2 * LANES, dtype=jnp.int32)[None, :] - LANES * u


def _bias_tiles_kernel(rel_bias_ref, tiles_ref, *, left_chunks):
    n_pairs, n_tiles, blk, _ = tiles_ref.shape
    nb = blk // LANES
    kj = lax.broadcasted_iota(jnp.int32, (LANES, LANES), 0)
    qi = lax.broadcasted_iota(jnp.int32, (LANES, LANES), 1)
    chunk_diff = kj // CHUNK - qi // CHUNK
    for p in range(n_pairs):
        for half in range(2):
            for w in range(rel_bias_ref.shape[2]):
                u = w - (nb - 1)
                row = jnp.broadcast_to(rel_bias_ref[p, half, w], (LANES, 2 * LANES))
                sub = pltpu.roll(row, 0, 1, stride=1, stride_axis=0)[:, LANES:]
                for t in range(n_tiles):
                    for a in range(nb):
                        c = t * nb + a - u
                        if not 0 <= c < nb:
                            continue
                        diff = chunk_diff - u * SUB_CHUNKS
                        allowed = diff <= 0
                        if left_chunks is not None:
                            allowed = allowed & (diff >= -left_chunks)
                        tiles_ref[p, t, pl.ds(c * LANES, LANES), pl.ds(half * blk + a * LANES, LANES)] = (
                            jnp.where(allowed, sub * LOG2E, NEG))


def _bias_tiles(rel_bias, n_tiles, blk, left_chunks):
    n_pairs = rel_bias.shape[0]
    return pl.pallas_call(
        functools.partial(_bias_tiles_kernel, left_chunks=left_chunks),
        out_shape=jax.ShapeDtypeStruct((n_pairs, n_tiles, blk, 2 * blk), jnp.float32),
        compiler_params=pltpu.CompilerParams(vmem_limit_bytes=V7X_VMEM_LIMIT),
        name="bias_tiles",
    )(rel_bias)


def _sum_rows(x):
    acc = x[:SUBLANES]
    for r in range(1, x.shape[0] // SUBLANES):
        acc = acc + x[r * SUBLANES:(r + 1) * SUBLANES]
    return acc


def _fold_rows(x):
    m = x[:SUBLANES]
    for r in range(1, x.shape[0] // SUBLANES):
        m = jnp.maximum(m, x[r * SUBLANES:(r + 1) * SUBLANES])
    return m


def _attn_kernel(*refs, left_blocks, lambda_init):
    if lambda_init is None:
        qt_ref, k_ref, vt_ref, tiles_ref, o_ref, q2t_ref, s_ref, acc_ref = refs
    else:
        (qt_ref, k_ref, vt_ref, tiles_ref, lq1_ref, lk1_ref, lq2_ref, lk2_ref, subg_ref, o_ref,
         q2t_ref, *s_refs, m_ref, acc_ref) = refs
        lam = (jnp.exp(jnp.sum(lq1_ref[...] * lk1_ref[...], axis=-1, keepdims=True))
               - jnp.exp(jnp.sum(lq2_ref[...] * lk2_ref[...], axis=-1, keepdims=True))
               + lambda_init)
    n_q, n_pairs, _, blk = qt_ref.shape[1:]
    top_half = lax.broadcasted_iota(jnp.int32, (PAIR, blk), 0) < HEAD_DIM

    def cols(p):
        return slice(p * PAIR, (p + 1) * PAIR)

    def key_rows(j, n=1):
        return pl.ds(pl.multiple_of(j * blk, blk), n * blk)

    def weights_update(vt, e):
        return jnp.concatenate([_dot(vt, e.astype(jnp.bfloat16)), _sum_rows(e)], axis=0)

    def stack_queries(c):
        for p in range(n_pairs):
            qt = qt_ref[0, c, p]
            zero = jnp.zeros_like(qt)
            q2t_ref[p, :, :blk] = jnp.where(top_half, qt, zero)
            q2t_ref[p, :, blk:] = jnp.where(top_half, zero, qt)

    def finish(c):
        for p in range(n_pairs):
            o = acc_ref[p, :PAIR] / jnp.sum(acc_ref[p, PAIR:], axis=0, keepdims=True)
            if lambda_init is None:
                y = jnp.where(top_half, o[:, :blk], o[:, blk:]).T
            else:
                d = o[:, :blk] - lam * o[:, blk:]
                d = d * lax.rsqrt(jnp.mean(d * d, axis=0, keepdims=True) + EPS)
                y = d.T * subg_ref[...] * (1.0 - lambda_init)
            o_ref[0, c * blk:(c + 1) * blk, cols(p)] = y.astype(o_ref.dtype)

    if left_blocks is None:
        assert n_q == 1
        i = pl.program_id(1)
        stack_queries(0)
        n_blocks = i + 1
        n_tiles = tiles_ref.shape[1]
        acc_ref[...] = jnp.zeros(acc_ref.shape, jnp.float32)
        m_ref[...] = jnp.full(m_ref.shape, NEG, jnp.float32)

        def prefetch_scores(j, buf_ref, p):
            tile = jnp.minimum(i - j, n_tiles - 1)
            buf_ref[p] = _dot(k_ref[0, key_rows(j), cols(p)], q2t_ref[p]) + tiles_ref[p, tile]

        def online(j, buf_ref, p):
            s = buf_ref[p]
            m_old = m_ref[p]
            m = jnp.maximum(m_old, jnp.max(_fold_rows(s), axis=0, keepdims=True))
            acc_ref[p] = jnp.exp2(m_old - m) * acc_ref[p] + weights_update(vt_ref[0, j, p], jnp.exp2(s - m))
            m_ref[p] = m

        def step(j, cur_ref, nxt_ref=None):
            if nxt_ref is not None:
                for p in range(n_pairs):
                    prefetch_scores(jnp.minimum(j + 1, i), nxt_ref, p)
            for p in range(n_pairs):
                online(j, cur_ref, p)

        ring = len(s_refs)

        def ring_trip(t, carry):
            for r in range(ring):
                step(ring * t + r, s_refs[r], s_refs[(r + 1) % ring])
            return carry

        for p in range(n_pairs):
            prefetch_scores(0, s_refs[0], p)
        lax.fori_loop(0, n_blocks // ring, ring_trip, 0)
        first_left = n_blocks - n_blocks % ring
        for left in range(1, ring):
            @pl.when(n_blocks % ring == left)
            def _(left=left):
                for r in range(left):
                    step(first_left + r, s_refs[r], s_refs[r + 1] if r + 1 < left else None)
        finish(0)
    else:
        def band(i, n_left):
            order = range(n_left, -1, -1)
            ms = [jnp.full((SUBLANES, 2 * blk), NEG, jnp.float32) for _ in range(n_pairs)]
            for d in order:
                for p in range(n_pairs):
                    s = _dot(k_ref[0, key_rows(i - d), cols(p)], q2t_ref[p]) + tiles_ref[p, d]
                    s_ref[d, p] = s
                    ms[p] = jnp.maximum(ms[p], _fold_rows(s))
            ms = [jnp.max(m, axis=0, keepdims=True) for m in ms]
            for d in order:
                for p in range(n_pairs):
                    update = weights_update(vt_ref[0, i - d, p], jnp.exp2(s_ref[d, p] - ms[p]))
                    acc_ref[p] = update if d == n_left else acc_ref[p] + update

        for c in range(n_q):
            i = pl.program_id(1) * n_q + c
            stack_queries(c)
            for n_left in range(c, left_blocks, n_q):
                pl.when(i == n_left)(functools.partial(band, i, n_left))
            pl.when(i >= left_blocks)(functools.partial(band, i, left_blocks))
            finish(c)


def _attention(qt, k, vt, tiles, left_blocks, lambda_init=None, extra=()):
    b, s, w = k.shape
    n_pairs, _, blk = qt.shape[2:]
    n_q = 1 if left_blocks is None else A_STEP_BLOCKS
    blk_q = pl.BlockSpec((1, n_q) + qt.shape[2:], lambda bi, i: (bi, i, 0, 0, 0))
    whole_k = pl.BlockSpec((1, s, w), lambda bi, i: (bi, 0, 0))
    whole_v = pl.BlockSpec((1,) + vt.shape[1:], lambda bi, i: (bi, 0, 0, 0, 0))
    f32 = jnp.float32
    score_block = (n_pairs, blk, 2 * blk)
    if left_blocks is None:
        score_scratch = [pltpu.VMEM(score_block, f32)] * B_SCORE_RING + [pltpu.VMEM((n_pairs, 1, 2 * blk), f32)]
    else:
        score_scratch = [pltpu.VMEM((left_blocks + 1,) + score_block, f32)]
    return pl.pallas_call(
        functools.partial(_attn_kernel, left_blocks=left_blocks, lambda_init=lambda_init),
        grid=(b, s // (n_q * blk)),
        in_specs=[blk_q, whole_k, whole_v, _resident(tiles.shape)] + [_resident(e.shape) for e in extra],
        out_specs=pl.BlockSpec((1, n_q * blk, w), lambda bi, i: (bi, i, 0)),
        out_shape=jax.ShapeDtypeStruct((b, s, w), jnp.bfloat16),
        scratch_shapes=[pltpu.VMEM((n_pairs, PAIR, 2 * blk), jnp.bfloat16), *score_scratch,
                        pltpu.VMEM((n_pairs, ACC_ROWS, 2 * blk), f32)],
        compiler_params=pltpu.CompilerParams(dimension_semantics=("arbitrary", "arbitrary"),
                                             vmem_limit_bytes=V7X_VMEM_LIMIT),
        name="attn_a" if lambda_init is None else "attn_b",
    )(qt, k, vt, tiles, *extra)


def kernel(x, ffn1_norm, ffn1_w_in, ffn1_w_out, mix_norm, w_mix_in, b_gate, rel_bias_a, lambda_q1, lambda_k1, lambda_q2, lambda_k2, subln_g, t5_bias, w_branch_a, w_branch_b, w_o, ffn2_norm, ffn2_w_in, ffn2_w_out, final_norm):
    b, s, d = x.shape
    depth = ffn1_norm.shape[0]
    assert depth >= 1
    width = w_branch_a.shape[1]
    n_pairs = width // PAIR
    f32 = jnp.float32

    t5_rel = t5_bias.astype(f32)[:, _t5_bucket(_window_rel(B_TILES, B_BLK))]
    t5_rel = jnp.broadcast_to(t5_rel[:, None, :, None, :], (n_pairs, 2) + t5_rel.shape[1:2] + (1, 2 * LANES))
    tiles_b = _bias_tiles(t5_rel, B_TILES, B_BLK, None)

    xf = x.astype(f32).reshape(b * s, d)
    for li in range(depth):
        x1, qa, ka, va, qb, kb, vb, gates = _ffn_mix(
            xf, ffn1_norm[li][None].astype(f32), ffn1_w_in[li].astype(f32), ffn1_w_out[li].astype(f32),
            mix_norm[li][None].astype(f32), w_mix_in[li].astype(f32), b_gate[li][None].astype(f32), width, s)

        rel_a = jnp.clip(_window_rel(A_TILES, A_BLK), -REL_CLIP, REL_CLIP) + REL_CLIP
        a_rel = rel_bias_a[li].astype(f32)[:, rel_a]
        tiles_a = _bias_tiles(a_rel.reshape(n_pairs, 2, rel_a.shape[0], 1, 2 * LANES), A_TILES, A_BLK, LEFT_CHUNKS)

        def seq(t):
            return t.reshape(b, s, width)

        ya = _attention(qa, seq(ka), va, tiles_a, A_TILES - 1)
        lambda_init = 0.8 - 0.6 * math.exp(-0.3 * li)
        lams = [t[li][None].astype(f32) for t in (lambda_q1, lambda_k1, lambda_q2, lambda_k2)]
        yb = _attention(qb, seq(kb), vb, tiles_b, None, lambda_init=lambda_init,
                        extra=(*lams, subln_g[li][None].astype(f32)))

        xf = _out_ffn(x1, ya.reshape(b * s, width), yb.reshape(b * s, width), gates,
                      w_branch_a[li].astype(f32), w_branch_b[li].astype(f32), w_o[li].astype(f32),
                      ffn2_norm[li][None].astype(f32), ffn2_w_in[li].astype(f32), ffn2_w_out[li].astype(f32),
                      final_norm[None].astype(f32), final_norm=(li == depth - 1))
    return xf.reshape(b, s, d).astype(x.dtype)
```

```python
import functools
import math

import jax
import jax.numpy as jnp
from jax import lax
from jax.experimental import pallas as pl
from jax.experimental.pallas import tpu as pltpu

EPS = 1e-6
NEG = -1e30
LOG2E = math.log2(math.e)

CHUNK = 64
LEFT_CHUNKS = 8
REL_CLIP = 128
T5_BUCKETS = 32
T5_MAX_DIST = 128

LANES = 128
SUBLANES = 8
HEAD_DIM = 64
PAIR = 2 * HEAD_DIM
SUB_CHUNKS = LANES // CHUNK
ACC_ROWS = PAIR + SUBLANES

A_BLK = 128
A_TILES = LEFT_CHUNKS * CHUNK // A_BLK + 1
A_STEP_BLOCKS = 4
B_BLK = 256
B_TILES = 3
B_SCORE_RING = 2

V7X_VMEM_LIMIT = 60000 * 1024

ROW_TILE = 512
BF16_ROWS = 16
CAST_CHUNK_BYTES = 3 << 18
CAST_SLOTS = 4


def _rms(xf, g):
    return xf * lax.rsqrt(jnp.mean(xf * xf, axis=-1, keepdims=True) + EPS) * g


def _dot(a, b):
    return jnp.dot(a, b, preferred_element_type=jnp.float32)


def _swiglu(h, w_in_ref, w_out_ref):
    d_ff = w_out_ref.shape[0]
    gate = _dot(h, w_in_ref[:, :d_ff])
    up = _dot(h, w_in_ref[:, d_ff:])
    act = (gate * jax.nn.sigmoid(gate) * up).astype(jnp.bfloat16)
    return _dot(act, w_out_ref[...])


def _cast_chunk_rows(rows, width):
    fits = [r for r in range(BF16_ROWS, rows + 1, BF16_ROWS)
            if rows % r == 0 and r * width * 4 <= CAST_CHUNK_BYTES]
    return max(fits)


def _load_cast(src_hbm, dst_ref):
    rows, width = src_hbm.shape
    chunk = _cast_chunk_rows(rows, width)
    n_chunks = rows // chunk

    ahead = min(CAST_SLOTS, n_chunks) - 1

    def body(stage_ref, sem):
        def copy(c):
            slot = c % CAST_SLOTS
            return pltpu.make_async_copy(src_hbm.at[pl.ds(c * chunk, chunk)], stage_ref.at[slot], sem.at[slot])

        for c in range(ahead):
            copy(c).start()

        def one_chunk(c, carry):
            @pl.when(c + ahead < n_chunks)
            def _():
                copy(c + ahead).start()

            copy(c).wait()
            dst_ref[pl.ds(pl.multiple_of(c * chunk, chunk), chunk)] = (
                stage_ref[c % CAST_SLOTS].astype(dst_ref.dtype))
            return carry

        lax.fori_loop(0, n_chunks, one_chunk, 0)

    pl.run_scoped(body, pltpu.VMEM((CAST_SLOTS, chunk, width), src_hbm.dtype),
                  pltpu.SemaphoreType.DMA((CAST_SLOTS,)))


def _load_weights_once(pairs):
    @pl.when(pl.program_id(0) == 0)
    def _():
        for src_hbm, dst_ref in pairs:
            _load_cast(src_hbm, dst_ref)


def _ffn_mix_kernel(x_ref, g1_ref, w_in_hbm, w_out_hbm, gm_ref, w_mix_hbm, b_gate_ref,
                    x1_ref, qa_ref, ka_ref, va_ref, qb_ref, kb_ref, vb_ref, gates_ref,
                    w_in_ref, w_out_ref, w_mix_ref):
    _load_weights_once([(w_in_hbm, w_in_ref), (w_out_hbm, w_out_ref), (w_mix_hbm, w_mix_ref)])
    x = x_ref[...]
    h = _rms(x, g1_ref[...]).astype(jnp.bfloat16)
    x1 = x + 0.5 * _swiglu(h, w_in_ref, w_out_ref)
    x1_ref[...] = x1

    u = _rms(x1, gm_ref[...]).astype(jnp.bfloat16)
    width = ka_ref.shape[1]
    scale = HEAD_DIM ** -0.5 * LOG2E
    outs = (qa_ref, ka_ref, va_ref, qb_ref, kb_ref, vb_ref)
    for n, o_ref in enumerate(outs):
        p = _dot(u, w_mix_ref[:, n * width:(n + 1) * width])
        if o_ref is ka_ref or o_ref is kb_ref:
            o_ref[...] = p.astype(o_ref.dtype)
        elif o_ref is qa_ref or o_ref is qb_ref:
            _store_transposed(p * scale, o_ref)
        else:
            _store_transposed(p, o_ref)
    logits = _dot(u, w_mix_ref[:, len(outs) * width:]) + b_gate_ref[...]
    gates_ref[...] = jax.nn.sigmoid(logits)


def _store_transposed(p, o_ref):
    n_blk, n_pairs, _, blk = o_ref.shape[1:]
    pt = p.T
    for c in range(n_blk):
        for pr in range(n_pairs):
            o_ref[0, c, pr] = pt[pr * PAIR:(pr + 1) * PAIR, c * blk:(c + 1) * blk].astype(o_ref.dtype)


def _out_ffn_kernel(x1_ref, ya_ref, yb_ref, gates_ref, wa_hbm, wb_hbm, wo_hbm,
                    g2_ref, w_in_hbm, w_out_hbm, gf_ref, o_ref,
                    wa_ref, wb_ref, wo_ref, w_in_ref, w_out_ref, *, final_norm):
    _load_weights_once([(wa_hbm, wa_ref), (wb_hbm, wb_ref), (wo_hbm, wo_ref),
                        (w_in_hbm, w_in_ref), (w_out_hbm, w_out_ref)])
    d = x1_ref.shape[1]
    merged = (gates_ref[:, :d] * _dot(ya_ref[...], wa_ref[...])
              + gates_ref[:, d:] * _dot(yb_ref[...], wb_ref[...]))
    x2 = x1_ref[...] + _dot(merged.astype(jnp.bfloat16), wo_ref[...])
    h = _rms(x2, g2_ref[...]).astype(jnp.bfloat16)
    x3 = x2 + 0.5 * _swiglu(h, w_in_ref, w_out_ref)
    o_ref[...] = _rms(x3, gf_ref[...]) if final_norm else x3


def _resident(shape):
    return pl.BlockSpec(shape, lambda *_: (0,) * len(shape), pipeline_mode=pl.Buffered(1))


_IN_HBM = pl.BlockSpec(memory_space=pl.ANY)


def _bf16_scratch(*weights):
    return [pltpu.VMEM(w.shape, jnp.bfloat16) for w in weights]


def _rows(tm, width):
    return pl.BlockSpec((tm, width), lambda i: (i, 0))


def _ffn_mix(x, g1, w_in, w_out, gm, w_mix, b_gate, width, seq):
    m, d = x.shape
    tm = ROW_TILE
    bf = jnp.bfloat16
    n_pairs = width // PAIR
    steps_per_seq = seq // tm

    def transposed(blk):
        shape = (m // seq, seq // blk, n_pairs, PAIR, blk)
        spec = pl.BlockSpec((1, tm // blk) + shape[2:], lambda i: (i // steps_per_seq, i % steps_per_seq, 0, 0, 0))
        return jax.ShapeDtypeStruct(shape, bf), spec

    keys = jax.ShapeDtypeStruct((m, width), bf), _rows(tm, width)
    outs = [(jax.ShapeDtypeStruct((m, d), jnp.float32), _rows(tm, d)),
            transposed(A_BLK), keys, transposed(A_BLK), transposed(B_BLK), keys, transposed(B_BLK),
            (jax.ShapeDtypeStruct((m, b_gate.shape[1]), jnp.float32), _rows(tm, b_gate.shape[1]))]
    return pl.pallas_call(
        _ffn_mix_kernel,
        grid=(m // tm,),
        in_specs=[_rows(tm, d), _resident(g1.shape), _IN_HBM, _IN_HBM,
                  _resident(gm.shape), _IN_HBM, _resident(b_gate.shape)],
        out_specs=[spec for _, spec in outs],
        out_shape=[shape for shape, _ in outs],
        scratch_shapes=_bf16_scratch(w_in, w_out, w_mix),
        compiler_params=pltpu.CompilerParams(dimension_semantics=("arbitrary",),
                                             vmem_limit_bytes=V7X_VMEM_LIMIT),
        name="ffn_mix",
    )(x, g1, w_in, w_out, gm, w_mix, b_gate)


def _out_ffn(x1, ya, yb, gates, wa, wb, wo, g2, w_in, w_out, gf, final_norm):
    m, d = x1.shape
    tm = ROW_TILE
    return pl.pallas_call(
        functools.partial(_out_ffn_kernel, final_norm=final_norm),
        grid=(m // tm,),
        in_specs=[_rows(tm, d), _rows(tm, ya.shape[1]), _rows(tm, yb.shape[1]), _rows(tm, gates.shape[1]),
                  _IN_HBM, _IN_HBM, _IN_HBM, _resident(g2.shape), _IN_HBM, _IN_HBM, _resident(gf.shape)],
        out_specs=_rows(tm, d),
        out_shape=jax.ShapeDtypeStruct((m, d), jnp.float32),
        scratch_shapes=_bf16_scratch(wa, wb, wo, w_in, w_out),
        compiler_params=pltpu.CompilerParams(dimension_semantics=("arbitrary",),
                                             vmem_limit_bytes=V7X_VMEM_LIMIT),
        name="out_ffn",
    )(x1, ya, yb, gates, wa, wb, wo, g2, w_in, w_out, gf)


def _t5_bucket(rel):
    nb = T5_BUCKETS // 2
    ret = jnp.where(rel > 0, nb, 0)
    n = jnp.abs(rel)
    max_exact = nb // 2
    nf = jnp.maximum(n, 1).astype(jnp.float32)
    large = max_exact + (jnp.log(nf / max_exact) / math.log(T5_MAX_DIST / max_exact)
                         * (nb - max_exact)).astype(jnp.int32)
    large = jnp.minimum(large, nb - 1)
    return ret + jnp.where(n < max_exact, n, large)


def _window_rel(n_tiles, blk):
    nb = blk // LANES
    u = jnp.arange(-(nb - 1), n_tiles * nb, dtype=jnp.int32)[:, None]
    return LANES - jnp.arange(2 * LANES, dtype=jnp.int32)[None, :] - LANES * u


def _bias_tiles_kernel(rel_bias_ref, tiles_ref, *, left_chunks):
    n_pairs, n_tiles, blk, _ = tiles_ref.shape
    nb = blk // LANES
    kj = lax.broadcasted_iota(jnp.int32, (LANES, LANES), 0)
    qi = lax.broadcasted_iota(jnp.int32, (LANES, LANES), 1)
    chunk_diff = kj // CHUNK - qi // CHUNK
    for p in range(n_pairs):
        for half in range(2):
            for w in range(rel_bias_ref.shape[2]):
                u = w - (nb - 1)
                row = jnp.broadcast_to(rel_bias_ref[p, half, w], (LANES, 2 * LANES))
                sub = pltpu.roll(row, 0, 1, stride=1, stride_axis=0)[:, LANES:]
                for t in range(n_tiles):
                    for a in range(nb):
                        c = t * nb + a - u
                        if not 0 <= c < nb:
                            continue
                        diff = chunk_diff - u * SUB_CHUNKS
                        allowed = diff <= 0
                        if left_chunks is not None:
                            allowed = allowed & (diff >= -left_chunks)
                        tiles_ref[p, t, pl.ds(c * LANES, LANES), pl.ds(half * blk + a * LANES, LANES)] = (
                            jnp.where(allowed, sub * LOG2E, NEG))


def _bias_tiles(rel_bias, n_tiles, blk, left_chunks):
    n_pairs = rel_bias.shape[0]
    return pl.pallas_call(
        functools.partial(_bias_tiles_kernel, left_chunks=left_chunks),
        out_shape=jax.ShapeDtypeStruct((n_pairs, n_tiles, blk, 2 * blk), jnp.float32),
        compiler_params=pltpu.CompilerParams(vmem_limit_bytes=V7X_VMEM_LIMIT),
        name="bias_tiles",
    )(rel_bias)


def _sum_rows(x):
    acc = x[:SUBLANES]
    for r in range(1, x.shape[0] // SUBLANES):
        acc = acc + x[r * SUBLANES:(r + 1) * SUBLANES]
    return acc


def _fold_rows(x):
    m = x[:SUBLANES]
    for r in range(1, x.shape[0] // SUBLANES):
        m = jnp.maximum(m, x[r * SUBLANES:(r + 1) * SUBLANES])
    return m


def _attn_kernel(*refs, left_blocks, lambda_init):
    if lambda_init is None:
        qt_ref, k_ref, vt_ref, tiles_ref, o_ref, q2t_ref, s_ref, acc_ref = refs
    else:
        (qt_ref, k_ref, vt_ref, tiles_ref, lq1_ref, lk1_ref, lq2_ref, lk2_ref, subg_ref, o_ref,
         q2t_ref, *s_refs, m_ref, acc_ref) = refs
        lam = (jnp.exp(jnp.sum(lq1_ref[...] * lk1_ref[...], axis=-1, keepdims=True))
               - jnp.exp(jnp.sum(lq2_ref[...] * lk2_ref[...], axis=-1, keepdims=True))
               + lambda_init)
    n_q, n_pairs, _, blk = qt_ref.shape[1:]
    top_half = lax.broadcasted_iota(jnp.int32, (PAIR, blk), 0) < HEAD_DIM

    def cols(p):
        return slice(p * PAIR, (p + 1) * PAIR)

    def key_rows(j, n=1):
        return pl.ds(pl.multiple_of(j * blk, blk), n * blk)

    def weights_update(vt, e):
        return jnp.concatenate([_dot(vt, e.astype(jnp.bfloat16)), _sum_rows(e)], axis=0)

    def stack_queries(c):
        for p in range(n_pairs):
            qt = qt_ref[0, c, p]
            zero = jnp.zeros_like(qt)
            q2t_ref[p, :, :blk] = jnp.where(top_half, qt, zero)
            q2t_ref[p, :, blk:] = jnp.where(top_half, zero, qt)

    def finish(c):
        for p in range(n_pairs):
            o = acc_ref[p, :PAIR] / jnp.sum(acc_ref[p, PAIR:], axis=0, keepdims=True)
            if lambda_init is None:
                y = jnp.where(top_half, o[:, :blk], o[:, blk:]).T
            else:
                d = o[:, :blk] - lam * o[:, blk:]
                d = d * lax.rsqrt(jnp.mean(d * d, axis=0, keepdims=True) + EPS)
                y = d.T * subg_ref[...] * (1.0 - lambda_init)
            o_ref[0, c * blk:(c + 1) * blk, cols(p)] = y.astype(o_ref.dtype)

    if left_blocks is None:
        assert n_q == 1
        i = pl.program_id(1)
        stack_queries(0)
        n_blocks = i + 1
        n_tiles = tiles_ref.shape[1]
        acc_ref[...] = jnp.zeros(acc_ref.shape, jnp.float32)
        m_ref[...] = jnp.full(m_ref.shape, NEG, jnp.float32)

        def prefetch_scores(j, buf_ref, p):
            tile = jnp.minimum(i - j, n_tiles - 1)
            buf_ref[p] = _dot(k_ref[0, key_rows(j), cols(p)], q2t_ref[p]) + tiles_ref[p, tile]

        def online(j, buf_ref, p):
            s = buf_ref[p]
            m_old = m_ref[p]
            m = jnp.maximum(m_old, jnp.max(_fold_rows(s), axis=0, keepdims=True))
            acc_ref[p] = jnp.exp2(m_old - m) * acc_ref[p] + weights_update(vt_ref[0, j, p], jnp.exp2(s - m))
            m_ref[p] = m

        def step(j, cur_ref, nxt_ref=None):
            if nxt_ref is not None:
                for p in range(n_pairs):
                    prefetch_scores(jnp.minimum(j + 1, i), nxt_ref, p)
            for p in range(n_pairs):
                online(j, cur_ref, p)

        ring = len(s_refs)

        def ring_trip(t, carry):
            for r in range(ring):
                step(ring * t + r, s_refs[r], s_refs[(r + 1) % ring])
            return carry

        for p in range(n_pairs):
            prefetch_scores(0, s_refs[0], p)
        lax.fori_loop(0, n_blocks // ring, ring_trip, 0)
        first_left = n_blocks - n_blocks % ring
        for left in range(1, ring):
            @pl.when(n_blocks % ring == left)
            def _(left=left):
                for r in range(left):
                    step(first_left + r, s_refs[r], s_refs[r + 1] if r + 1 < left else None)
        finish(0)
    else:
        def band(i, n_left):
            order = range(n_left, -1, -1)
            ms = [jnp.full((SUBLANES, 2 * blk), NEG, jnp.float32) for _ in range(n_pairs)]
            for d in order:
                for p in range(n_pairs):
                    s = _dot(k_ref[0, key_rows(i - d), cols(p)], q2t_ref[p]) + tiles_ref[p, d]
                    s_ref[d, p] = s
                    ms[p] = jnp.maximum(ms[p], _fold_rows(s))
            ms = [jnp.max(m, axis=0, keepdims=True) for m in ms]
            for d in order:
                for p in range(n_pairs):
                    update = weights_update(vt_ref[0, i - d, p], jnp.exp2(s_ref[d, p] - ms[p]))
                    acc_ref[p] = update if d == n_left else acc_ref[p] + update

        for c in range(n_q):
            i = pl.program_id(1) * n_q + c
            stack_queries(c)
            for n_left in range(c, left_blocks, n_q):
                pl.when(i == n_left)(functools.partial(band, i, n_left))
            pl.when(i >= left_blocks)(functools.partial(band, i, left_blocks))
            finish(c)


def _attention(qt, k, vt, tiles, left_blocks, lambda_init=None, extra=()):
    b, s, w = k.shape
    n_pairs, _, blk = qt.shape[2:]
    n_q = 1 if left_blocks is None else A_STEP_BLOCKS
    blk_q = pl.BlockSpec((1, n_q) + qt.shape[2:], lambda bi, i: (bi, i, 0, 0, 0))
    whole_k = pl.BlockSpec((1, s, w), lambda bi, i: (bi, 0, 0))
    whole_v = pl.BlockSpec((1,) + vt.shape[1:], lambda bi, i: (bi, 0, 0, 0, 0))
    f32 = jnp.float32
    score_block = (n_pairs, blk, 2 * blk)
    if left_blocks is None:
        score_scratch = [pltpu.VMEM(score_block, f32)] * B_SCORE_RING + [pltpu.VMEM((n_pairs, 1, 2 * blk), f32)]
    else:
        score_scratch = [pltpu.VMEM((left_blocks + 1,) + score_block, f32)]
    return pl.pallas_call(
        functools.partial(_attn_kernel, left_blocks=left_blocks, lambda_init=lambda_init),
        grid=(b, s // (n_q * blk)),
        in_specs=[blk_q, whole_k, whole_v, _resident(tiles.shape)] + [_resident(e.shape) for e in extra],
        out_specs=pl.BlockSpec((1, n_q * blk, w), lambda bi, i: (bi, i, 0)),
        out_shape=jax.ShapeDtypeStruct((b, s, w), jnp.bfloat16),
        scratch_shapes=[pltpu.VMEM((n_pairs, PAIR, 2 * blk), jnp.bfloat16), *score_scratch,
                        pltpu.VMEM((n_pairs, ACC_ROWS, 2 * blk), f32)],
        compiler_params=pltpu.CompilerParams(dimension_semantics=("arbitrary", "arbitrary"),
                                             vmem_limit_bytes=V7X_VMEM_LIMIT),
        name="attn_a" if lambda_init is None else "attn_b",
    )(qt, k, vt, tiles, *extra)


def kernel(x, ffn1_norm, ffn1_w_in, ffn1_w_out, mix_norm, w_mix_in, b_gate, rel_bias_a, lambda_q1, lambda_k1, lambda_q2, lambda_k2, subln_g, t5_bias, w_branch_a, w_branch_b, w_o, ffn2_norm, ffn2_w_in, ffn2_w_out, final_norm):
    b, s, d = x.shape
    depth = ffn1_norm.shape[0]
    assert depth >= 1
    width = w_branch_a.shape[1]
    n_pairs = width // PAIR
    f32 = jnp.float32

    t5_rel = t5_bias.astype(f32)[:, _t5_bucket(_window_rel(B_TILES, B_BLK))]
    t5_rel = jnp.broadcast_to(t5_rel[:, None, :, None, :], (n_pairs, 2) + t5_rel.shape[1:2] + (1, 2 * LANES))
    tiles_b = _bias_tiles(t5_rel, B_TILES, B_BLK, None)

    xf = x.astype(f32).reshape(b * s, d)
    for li in range(depth):
        x1, qa, ka, va, qb, kb, vb, gates = _ffn_mix(
            xf, ffn1_norm[li][None].astype(f32), ffn1_w_in[li].astype(f32), ffn1_w_out[li].astype(f32),
            mix_norm[li][None].astype(f32), w_mix_in[li].astype(f32), b_gate[li][None].astype(f32), width, s)

        rel_a = jnp.clip(_window_rel(A_TILES, A_BLK), -REL_CLIP, REL_CLIP) + REL_CLIP
        a_rel = rel_bias_a[li].astype(f32)[:, rel_a]
        tiles_a = _bias_tiles(a_rel.reshape(n_pairs, 2, rel_a.shape[0], 1, 2 * LANES), A_TILES, A_BLK, LEFT_CHUNKS)

        def seq(t):
            return t.reshape(b, s, width)

        ya = _attention(qa, seq(ka), va, tiles_a, A_TILES - 1)
        lambda_init = 0.8 - 0.6 * math.exp(-0.3 * li)
        lams = [t[li][None].astype(f32) for t in (lambda_q1, lambda_k1, lambda_q2, lambda_k2)]
        yb = _attention(qb, seq(kb), vb, tiles_b, None, lambda_init=lambda_init,
                        extra=(*lams, subln_g[li][None].astype(f32)))

        xf = _out_ffn(x1, ya.reshape(b * s, width), yb.reshape(b * s, width), gates,
                      w_branch_a[li].astype(f32), w_branch_b[li].astype(f32), w_o[li].astype(f32),
                      ffn2_norm[li][None].astype(f32), ffn2_w_in[li].astype(f32), ffn2_w_out[li].astype(f32),
                      final_norm[None].astype(f32), final_norm=(li == depth - 1))
    return xf.reshape(b, s, d).astype(x.dtype)
```

```python
import functools
import math

import jax
import jax.numpy as jnp
from jax import lax
from jax.experimental import pallas as pl
from jax.experimental.pallas import tpu as pltpu

EPS = 1e-6
NEG = -1e30
LOG2E = math.log2(math.e)

CHUNK = 64
LEFT_CHUNKS = 8
REL_CLIP = 128
T5_BUCKETS = 32
T5_MAX_DIST = 128

LANES = 128
SUBLANES = 8
HEAD_DIM = 64
PAIR = 2 * HEAD_DIM
SUB_CHUNKS = LANES // CHUNK
ACC_ROWS = PAIR + SUBLANES

A_BLK = 128
A_TILES = LEFT_CHUNKS * CHUNK // A_BLK + 1
A_STEP_BLOCKS = 4
B_BLK = 256
B_TILES = 3
B_SCORE_RING = 2
B_STEP_BLOCKS = 2

V7X_VMEM_LIMIT = 60000 * 1024

ROW_TILE = 512
BF16_ROWS = 16
CAST_CHUNK_BYTES = 3 << 18
CAST_SLOTS = 4


def _rms(xf, g):
    return xf * lax.rsqrt(jnp.mean(xf * xf, axis=-1, keepdims=True) + EPS) * g


def _dot(a, b):
    return jnp.dot(a, b, preferred_element_type=jnp.float32)


def _swiglu(h, w_in_ref, w_out_ref):
    d_ff = w_out_ref.shape[0]
    gate = _dot(h, w_in_ref[:, :d_ff])
    up = _dot(h, w_in_ref[:, d_ff:])
    act = (gate * jax.nn.sigmoid(gate) * up).astype(jnp.bfloat16)
    return _dot(act, w_out_ref[...])


def _cast_chunk_rows(rows, width):
    fits = [r for r in range(BF16_ROWS, rows + 1, BF16_ROWS)
            if rows % r == 0 and r * width * 4 <= CAST_CHUNK_BYTES]
    return max(fits)


def _load_cast(src_hbm, dst_ref):
    rows, width = src_hbm.shape
    chunk = _cast_chunk_rows(rows, width)
    n_chunks = rows // chunk

    ahead = min(CAST_SLOTS, n_chunks) - 1

    def body(stage_ref, sem):
        def copy(c):
            slot = c % CAST_SLOTS
            return pltpu.make_async_copy(src_hbm.at[pl.ds(c * chunk, chunk)], stage_ref.at[slot], sem.at[slot])

        for c in range(ahead):
            copy(c).start()

        def one_chunk(c, carry):
            @pl.when(c + ahead < n_chunks)
            def _():
                copy(c + ahead).start()

            copy(c).wait()
            dst_ref[pl.ds(pl.multiple_of(c * chunk, chunk), chunk)] = (
                stage_ref[c % CAST_SLOTS].astype(dst_ref.dtype))
            return carry

        lax.fori_loop(0, n_chunks, one_chunk, 0)

    pl.run_scoped(body, pltpu.VMEM((CAST_SLOTS, chunk, width), src_hbm.dtype),
                  pltpu.SemaphoreType.DMA((CAST_SLOTS,)))


def _load_weights_once(pairs):
    @pl.when(pl.program_id(0) == 0)
    def _():
        for src_hbm, dst_ref in pairs:
            _load_cast(src_hbm, dst_ref)


def _ffn_mix_kernel(x_ref, g1_ref, w_in_hbm, w_out_hbm, gm_ref, w_mix_hbm, b_gate_ref,
                    x1_ref, qa_ref, ka_ref, va_ref, qb_ref, kb_ref, vb_ref, gates_ref,
                    w_in_ref, w_out_ref, w_mix_ref):
    _load_weights_once([(w_in_hbm, w_in_ref), (w_out_hbm, w_out_ref), (w_mix_hbm, w_mix_ref)])
    x = x_ref[...]
    h = _rms(x, g1_ref[...]).astype(jnp.bfloat16)
    x1 = x + 0.5 * _swiglu(h, w_in_ref, w_out_ref)
    x1_ref[...] = x1

    u = _rms(x1, gm_ref[...]).astype(jnp.bfloat16)
    width = ka_ref.shape[1]
    scale = HEAD_DIM ** -0.5 * LOG2E
    outs = (qa_ref, ka_ref, va_ref, qb_ref, kb_ref, vb_ref)
    for n, o_ref in enumerate(outs):
        p = _dot(u, w_mix_ref[:, n * width:(n + 1) * width])
        if o_ref is ka_ref or o_ref is kb_ref:
            o_ref[...] = p.astype(o_ref.dtype)
        elif o_ref is qa_ref or o_ref is qb_ref:
            _store_transposed(p * scale, o_ref)
        else:
            _store_transposed(p, o_ref)
    logits = _dot(u, w_mix_ref[:, len(outs) * width:]) + b_gate_ref[...]
    gates_ref[...] = jax.nn.sigmoid(logits)


def _store_transposed(p, o_ref):
    n_blk, n_pairs, _, blk = o_ref.shape[1:]
    pt = p.T
    for c in range(n_blk):
        for pr in range(n_pairs):
            o_ref[0, c, pr] = pt[pr * PAIR:(pr + 1) * PAIR, c * blk:(c + 1) * blk].astype(o_ref.dtype)


def _out_ffn_kernel(x1_ref, ya_ref, yb_ref, gates_ref, wa_hbm, wb_hbm, wo_hbm,
                    g2_ref, w_in_hbm, w_out_hbm, gf_ref, o_ref,
                    wa_ref, wb_ref, wo_ref, w_in_ref, w_out_ref, *, final_norm):
    _load_weights_once([(wa_hbm, wa_ref), (wb_hbm, wb_ref), (wo_hbm, wo_ref),
                        (w_in_hbm, w_in_ref), (w_out_hbm, w_out_ref)])
    d = x1_ref.shape[1]
    merged = (gates_ref[:, :d] * _dot(ya_ref[...], wa_ref[...])
              + gates_ref[:, d:] * _dot(yb_ref[...], wb_ref[...]))
    x2 = x1_ref[...] + _dot(merged.astype(jnp.bfloat16), wo_ref[...])
    h = _rms(x2, g2_ref[...]).astype(jnp.bfloat16)
    x3 = x2 + 0.5 * _swiglu(h, w_in_ref, w_out_ref)
    o_ref[...] = _rms(x3, gf_ref[...]) if final_norm else x3


def _resident(shape):
    return pl.BlockSpec(shape, lambda *_: (0,) * len(shape), pipeline_mode=pl.Buffered(1))


_IN_HBM = pl.BlockSpec(memory_space=pl.ANY)


def _bf16_scratch(*weights):
    return [pltpu.VMEM(w.shape, jnp.bfloat16) for w in weights]


def _rows(tm, width):
    return pl.BlockSpec((tm, width), lambda i: (i, 0))


def _ffn_mix(x, g1, w_in, w_out, gm, w_mix, b_gate, width, seq):
    m, d = x.shape
    tm = ROW_TILE
    bf = jnp.bfloat16
    n_pairs = width // PAIR
    steps_per_seq = seq // tm

    def transposed(blk):
        shape = (m // seq, seq // blk, n_pairs, PAIR, blk)
        spec = pl.BlockSpec((1, tm // blk) + shape[2:], lambda i: (i // steps_per_seq, i % steps_per_seq, 0, 0, 0))
        return jax.ShapeDtypeStruct(shape, bf), spec

    keys = jax.ShapeDtypeStruct((m, width), bf), _rows(tm, width)
    outs = [(jax.ShapeDtypeStruct((m, d), jnp.float32), _rows(tm, d)),
            transposed(A_BLK), keys, transposed(A_BLK), transposed(B_BLK), keys, transposed(B_BLK),
            (jax.ShapeDtypeStruct((m, b_gate.shape[1]), jnp.float32), _rows(tm, b_gate.shape[1]))]
    return pl.pallas_call(
        _ffn_mix_kernel,
        grid=(m // tm,),
        in_specs=[_rows(tm, d), _resident(g1.shape), _IN_HBM, _IN_HBM,
                  _resident(gm.shape), _IN_HBM, _resident(b_gate.shape)],
        out_specs=[spec for _, spec in outs],
        out_shape=[shape for shape, _ in outs],
        scratch_shapes=_bf16_scratch(w_in, w_out, w_mix),
        compiler_params=pltpu.CompilerParams(dimension_semantics=("arbitrary",),
                                             vmem_limit_bytes=V7X_VMEM_LIMIT),
        name="ffn_mix",
    )(x, g1, w_in, w_out, gm, w_mix, b_gate)


def _out_ffn(x1, ya, yb, gates, wa, wb, wo, g2, w_in, w_out, gf, final_norm):
    m, d = x1.shape
    tm = ROW_TILE
    return pl.pallas_call(
        functools.partial(_out_ffn_kernel, final_norm=final_norm),
        grid=(m // tm,),
        in_specs=[_rows(tm, d), _rows(tm, ya.shape[1]), _rows(tm, yb.shape[1]), _rows(tm, gates.shape[1]),
                  _IN_HBM, _IN_HBM, _IN_HBM, _resident(g2.shape), _IN_HBM, _IN_HBM, _resident(gf.shape)],
        out_specs=_rows(tm, d),
        out_shape=jax.ShapeDtypeStruct((m, d), jnp.float32),
        scratch_shapes=_bf16_scratch(wa, wb, wo, w_in, w_out),
        compiler_params=pltpu.CompilerParams(dimension_semantics=("arbitrary",),
                                             vmem_limit_bytes=V7X_VMEM_LIMIT),
        name="out_ffn",
    )(x1, ya, yb, gates, wa, wb, wo, g2, w_in, w_out, gf)


def _t5_bucket(rel):
    nb = T5_BUCKETS // 2
    ret = jnp.where(rel > 0, nb, 0)
    n = jnp.abs(rel)
    max_exact = nb // 2
    nf = jnp.maximum(n, 1).astype(jnp.float32)
    large = max_exact + (jnp.log(nf / max_exact) / math.log(T5_MAX_DIST / max_exact)
                         * (nb - max_exact)).astype(jnp.int32)
    large = jnp.minimum(large, nb - 1)
    return ret + jnp.where(n < max_exact, n, large)


def _window_rel(n_tiles, blk):
    nb = blk // LANES
    u = jnp.arange(-(nb - 1), n_tiles * nb, dtype=jnp.int32)[:, None]
    return LANES - jnp.arange(2 * LANES, dtype=jnp.int32)[None, :] - LANES * u


def _bias_tiles_kernel(rel_bias_ref, tiles_ref, *, left_chunks):
    n_pairs, n_tiles, blk, _ = tiles_ref.shape
    nb = blk // LANES
    kj = lax.broadcasted_iota(jnp.int32, (LANES, LANES), 0)
    qi = lax.broadcasted_iota(jnp.int32, (LANES, LANES), 1)
    chunk_diff = kj // CHUNK - qi // CHUNK
    for p in range(n_pairs):
        for half in range(2):
            for w in range(rel_bias_ref.shape[2]):
                u = w - (nb - 1)
                row = jnp.broadcast_to(rel_bias_ref[p, half, w], (LANES, 2 * LANES))
                sub = pltpu.roll(row, 0, 1, stride=1, stride_axis=0)[:, LANES:]
                for t in range(n_tiles):
                    for a in range(nb):
                        c = t * nb + a - u
                        if not 0 <= c < nb:
                            continue
                        diff = chunk_diff - u * SUB_CHUNKS
                        allowed = diff <= 0
                        if left_chunks is not None:
                            allowed = allowed & (diff >= -left_chunks)
                        tiles_ref[p, t, pl.ds(c * LANES, LANES), pl.ds(half * blk + a * LANES, LANES)] = (
                            jnp.where(allowed, sub * LOG2E, NEG))


def _bias_tiles(rel_bias, n_tiles, blk, left_chunks):
    n_pairs = rel_bias.shape[0]
    return pl.pallas_call(
        functools.partial(_bias_tiles_kernel, left_chunks=left_chunks),
        out_shape=jax.ShapeDtypeStruct((n_pairs, n_tiles, blk, 2 * blk), jnp.float32),
        compiler_params=pltpu.CompilerParams(vmem_limit_bytes=V7X_VMEM_LIMIT),
        name="bias_tiles",
    )(rel_bias)


def _sum_rows(x):
    acc = x[:SUBLANES]
    for r in range(1, x.shape[0] // SUBLANES):
        acc = acc + x[r * SUBLANES:(r + 1) * SUBLANES]
    return acc


def _fold_rows(x):
    m = x[:SUBLANES]
    for r in range(1, x.shape[0] // SUBLANES):
        m = jnp.maximum(m, x[r * SUBLANES:(r + 1) * SUBLANES])
    return m


def _attn_kernel(*refs, left_blocks, lambda_init):
    if lambda_init is None:
        qt_ref, k_ref, vt_ref, tiles_ref, o_ref, q2t_ref, s_ref, acc_ref = refs
    else:
        (qt_ref, k_ref, vt_ref, tiles_ref, lq1_ref, lk1_ref, lq2_ref, lk2_ref, subg_ref, o_ref,
         q2t_ref, *s_refs, m_ref, acc_ref) = refs
        lam = (jnp.exp(jnp.sum(lq1_ref[...] * lk1_ref[...], axis=-1, keepdims=True))
               - jnp.exp(jnp.sum(lq2_ref[...] * lk2_ref[...], axis=-1, keepdims=True))
               + lambda_init)
    n_q, n_pairs, _, blk = qt_ref.shape[1:]
    top_half = lax.broadcasted_iota(jnp.int32, (PAIR, blk), 0) < HEAD_DIM

    def cols(p):
        return slice(p * PAIR, (p + 1) * PAIR)

    def key_rows(j, n=1):
        return pl.ds(pl.multiple_of(j * blk, blk), n * blk)

    def weights_update(vt, e):
        return jnp.concatenate([_dot(vt, e.astype(jnp.bfloat16)), _sum_rows(e)], axis=0)

    def stack_queries(c):
        for p in range(n_pairs):
            qt = qt_ref[0, c, p]
            zero = jnp.zeros_like(qt)
            q2t_ref[p, :, :blk] = jnp.where(top_half, qt, zero)
            q2t_ref[p, :, blk:] = jnp.where(top_half, zero, qt)

    def finish(c):
        for p in range(n_pairs):
            o = acc_ref[p, :PAIR] / jnp.sum(acc_ref[p, PAIR:], axis=0, keepdims=True)
            if lambda_init is None:
                y = jnp.where(top_half, o[:, :blk], o[:, blk:]).T
            else:
                d = o[:, :blk] - lam * o[:, blk:]
                d = d * lax.rsqrt(jnp.mean(d * d, axis=0, keepdims=True) + EPS)
                y = d.T * subg_ref[...] * (1.0 - lambda_init)
            o_ref[0, c * blk:(c + 1) * blk, cols(p)] = y.astype(o_ref.dtype)

    if left_blocks is None:
        n_tiles = tiles_ref.shape[1]
        ring = len(s_refs)

        def causal_prefix(c):
            i = pl.program_id(1) * n_q + c
            stack_queries(c)
            n_blocks = i + 1
            acc_ref[...] = jnp.zeros(acc_ref.shape, jnp.float32)
            m_ref[...] = jnp.full(m_ref.shape, NEG, jnp.float32)

            def prefetch_scores(j, buf_ref, p):
                tile = jnp.minimum(i - j, n_tiles - 1)
                buf_ref[p] = _dot(k_ref[0, key_rows(j), cols(p)], q2t_ref[p]) + tiles_ref[p, tile]

            def online(j, buf_ref, p):
                s = buf_ref[p]
                m_old = m_ref[p]
                m = jnp.maximum(m_old, jnp.max(_fold_rows(s), axis=0, keepdims=True))
                acc_ref[p] = jnp.exp2(m_old - m) * acc_ref[p] + weights_update(vt_ref[0, j, p], jnp.exp2(s - m))
                m_ref[p] = m

            def step(j, cur_ref, nxt_ref=None):
                if nxt_ref is not None:
                    for p in range(n_pairs):
                        prefetch_scores(jnp.minimum(j + 1, i), nxt_ref, p)
                for p in range(n_pairs):
                    online(j, cur_ref, p)

            def ring_trip(t, carry):
                for r in range(ring):
                    step(ring * t + r, s_refs[r], s_refs[(r + 1) % ring])
                return carry

            for p in range(n_pairs):
                prefetch_scores(0, s_refs[0], p)
            lax.fori_loop(0, n_blocks // ring, ring_trip, 0)
            first_left = n_blocks - n_blocks % ring
            for left in range(1, ring):
                @pl.when(n_blocks % ring == left)
                def _(left=left):
                    for r in range(left):
                        step(first_left + r, s_refs[r], s_refs[r + 1] if r + 1 < left else None)
            finish(c)

        for c in range(n_q):
            causal_prefix(c)
    else:
        def band(i, n_left):
            order = range(n_left, -1, -1)
            ms = [jnp.full((SUBLANES, 2 * blk), NEG, jnp.float32) for _ in range(n_pairs)]
            for d in order:
                for p in range(n_pairs):
                    s = _dot(k_ref[0, key_rows(i - d), cols(p)], q2t_ref[p]) + tiles_ref[p, d]
                    s_ref[d, p] = s
                    ms[p] = jnp.maximum(ms[p], _fold_rows(s))
            ms = [jnp.max(m, axis=0, keepdims=True) for m in ms]
            for d in order:
                for p in range(n_pairs):
                    update = weights_update(vt_ref[0, i - d, p], jnp.exp2(s_ref[d, p] - ms[p]))
                    acc_ref[p] = update if d == n_left else acc_ref[p] + update

        for c in range(n_q):
            i = pl.program_id(1) * n_q + c
            stack_queries(c)
            for n_left in range(c, left_blocks, n_q):
                pl.when(i == n_left)(functools.partial(band, i, n_left))
            pl.when(i >= left_blocks)(functools.partial(band, i, left_blocks))
            finish(c)


def _attention(qt, k, vt, tiles, left_blocks, lambda_init=None, extra=()):
    b, s, w = k.shape
    n_pairs, _, blk = qt.shape[2:]
    n_q = B_STEP_BLOCKS if left_blocks is None else A_STEP_BLOCKS
    blk_q = pl.BlockSpec((1, n_q) + qt.shape[2:], lambda bi, i: (bi, i, 0, 0, 0))
    whole_k = pl.BlockSpec((1, s, w), lambda bi, i: (bi, 0, 0))
    whole_v = pl.BlockSpec((1,) + vt.shape[1:], lambda bi, i: (bi, 0, 0, 0, 0))
    f32 = jnp.float32
    score_block = (n_pairs, blk, 2 * blk)
    if left_blocks is None:
        score_scratch = [pltpu.VMEM(score_block, f32)] * B_SCORE_RING + [pltpu.VMEM((n_pairs, 1, 2 * blk), f32)]
    else:
        score_scratch = [pltpu.VMEM((left_blocks + 1,) + score_block, f32)]
    return pl.pallas_call(
        functools.partial(_attn_kernel, left_blocks=left_blocks, lambda_init=lambda_init),
        grid=(b, s // (n_q * blk)),
        in_specs=[blk_q, whole_k, whole_v, _resident(tiles.shape)] + [_resident(e.shape) for e in extra],
        out_specs=pl.BlockSpec((1, n_q * blk, w), lambda bi, i: (bi, i, 0)),
        out_shape=jax.ShapeDtypeStruct((b, s, w), jnp.bfloat16),
        scratch_shapes=[pltpu.VMEM((n_pairs, PAIR, 2 * blk), jnp.bfloat16), *score_scratch,
                        pltpu.VMEM((n_pairs, ACC_ROWS, 2 * blk), f32)],
        compiler_params=pltpu.CompilerParams(dimension_semantics=("arbitrary", "arbitrary"),
                                             vmem_limit_bytes=V7X_VMEM_LIMIT),
        name="attn_a" if lambda_init is None else "attn_b",
    )(qt, k, vt, tiles, *extra)


def kernel(x, ffn1_norm, ffn1_w_in, ffn1_w_out, mix_norm, w_mix_in, b_gate, rel_bias_a, lambda_q1, lambda_k1, lambda_q2, lambda_k2, subln_g, t5_bias, w_branch_a, w_branch_b, w_o, ffn2_norm, ffn2_w_in, ffn2_w_out, final_norm):
    b, s, d = x.shape
    depth = ffn1_norm.shape[0]
    assert depth >= 1
    width = w_branch_a.shape[1]
    n_pairs = width // PAIR
    f32 = jnp.float32

    t5_rel = t5_bias.astype(f32)[:, _t5_bucket(_window_rel(B_TILES, B_BLK))]
    t5_rel = jnp.broadcast_to(t5_rel[:, None, :, None, :], (n_pairs, 2) + t5_rel.shape[1:2] + (1, 2 * LANES))
    tiles_b = _bias_tiles(t5_rel, B_TILES, B_BLK, None)

    xf = x.astype(f32).reshape(b * s, d)
    for li in range(depth):
        x1, qa, ka, va, qb, kb, vb, gates = _ffn_mix(
            xf, ffn1_norm[li][None].astype(f32), ffn1_w_in[li].astype(f32), ffn1_w_out[li].astype(f32),
            mix_norm[li][None].astype(f32), w_mix_in[li].astype(f32), b_gate[li][None].astype(f32), width, s)

        rel_a = jnp.clip(_window_rel(A_TILES, A_BLK), -REL_CLIP, REL_CLIP) + REL_CLIP
        a_rel = rel_bias_a[li].astype(f32)[:, rel_a]
        tiles_a = _bias_tiles(a_rel.reshape(n_pairs, 2, rel_a.shape[0], 1, 2 * LANES), A_TILES, A_BLK, LEFT_CHUNKS)

        def seq(t):
            return t.reshape(b, s, width)

        ya = _attention(qa, seq(ka), va, tiles_a, A_TILES - 1)
        lambda_init = 0.8 - 0.6 * math.exp(-0.3 * li)
        lams = [t[li][None].astype(f32) for t in (lambda_q1, lambda_k1, lambda_q2, lambda_k2)]
        yb = _attention(qb, seq(kb), vb, tiles_b, None, lambda_init=lambda_init,
                        extra=(*lams, subln_g[li][None].astype(f32)))

        xf = _out_ffn(x1, ya.reshape(b * s, width), yb.reshape(b * s, width), gates,
                      w_branch_a[li].astype(f32), w_branch_b[li].astype(f32), w_o[li].astype(f32),
                      ffn2_norm[li][None].astype(f32), ffn2_w_in[li].astype(f32), ffn2_w_out[li].astype(f32),
                      final_norm[None].astype(f32), final_norm=(li == depth - 1))
    return xf.reshape(b, s, d).astype(x.dtype)
```

```python
import functools
import math

import jax
import jax.numpy as jnp
from jax import lax
from jax.experimental import pallas as pl
from jax.experimental.pallas import tpu as pltpu

EPS = 1e-6
NEG = -1e30
LOG2E = math.log2(math.e)

CHUNK = 64
LEFT_CHUNKS = 8
REL_CLIP = 128
T5_BUCKETS = 32
T5_MAX_DIST = 128

LANES = 128
SUBLANES = 8
HEAD_DIM = 64
PAIR = 2 * HEAD_DIM
SUB_CHUNKS = LANES // CHUNK
ACC_ROWS = PAIR + SUBLANES

A_BLK = 128
A_TILES = LEFT_CHUNKS * CHUNK // A_BLK + 1
B_BLK = 256
B_TILES = 3
B_SCORE_RING = 2

V7X_VMEM_LIMIT = 60000 * 1024

ROW_TILE = 512
BF16_ROWS = 16
CAST_CHUNK_BYTES = 3 << 18
CAST_SLOTS = 4


def _rms(xf, g):
    return xf * lax.rsqrt(jnp.mean(xf * xf, axis=-1, keepdims=True) + EPS) * g


def _dot(a, b):
    return jnp.dot(a, b, preferred_element_type=jnp.float32)


def _swiglu(h, w_in_ref, w_out_ref):
    d_ff = w_out_ref.shape[0]
    gate = _dot(h, w_in_ref[:, :d_ff])
    up = _dot(h, w_in_ref[:, d_ff:])
    act = (gate * jax.nn.sigmoid(gate) * up).astype(jnp.bfloat16)
    return _dot(act, w_out_ref[...])


def _cast_chunk_rows(rows, width):
    fits = [r for r in range(BF16_ROWS, rows + 1, BF16_ROWS)
            if rows % r == 0 and r * width * 4 <= CAST_CHUNK_BYTES]
    return max(fits)


def _load_cast(src_hbm, dst_ref):
    rows, width = src_hbm.shape
    chunk = _cast_chunk_rows(rows, width)
    n_chunks = rows // chunk

    ahead = min(CAST_SLOTS, n_chunks) - 1

    def body(stage_ref, sem):
        def copy(c):
            slot = c % CAST_SLOTS
            return pltpu.make_async_copy(src_hbm.at[pl.ds(c * chunk, chunk)], stage_ref.at[slot], sem.at[slot])

        for c in range(ahead):
            copy(c).start()

        def one_chunk(c, carry):
            @pl.when(c + ahead < n_chunks)
            def _():
                copy(c + ahead).start()

            copy(c).wait()
            dst_ref[pl.ds(pl.multiple_of(c * chunk, chunk), chunk)] = (
                stage_ref[c % CAST_SLOTS].astype(dst_ref.dtype))
            return carry

        lax.fori_loop(0, n_chunks, one_chunk, 0)

    pl.run_scoped(body, pltpu.VMEM((CAST_SLOTS, chunk, width), src_hbm.dtype),
                  pltpu.SemaphoreType.DMA((CAST_SLOTS,)))


def _load_weights_once(pairs):
    @pl.when(pl.program_id(0) == 0)
    def _():
        for src_hbm, dst_ref in pairs:
            _load_cast(src_hbm, dst_ref)


def _ffn_mix_kernel(x_ref, g1_ref, w_in_hbm, w_out_hbm, gm_ref, w_mix_hbm, b_gate_ref,
                    x1_ref, qa_ref, ka_ref, va_ref, qb_ref, kb_ref, vb_ref, gates_ref,
                    w_in_ref, w_out_ref, w_mix_ref):
    _load_weights_once([(w_in_hbm, w_in_ref), (w_out_hbm, w_out_ref), (w_mix_hbm, w_mix_ref)])
    x = x_ref[...]
    h = _rms(x, g1_ref[...]).astype(jnp.bfloat16)
    x1 = x + 0.5 * _swiglu(h, w_in_ref, w_out_ref)
    x1_ref[...] = x1

    u = _rms(x1, gm_ref[...]).astype(jnp.bfloat16)
    width = ka_ref.shape[1]
    scale = HEAD_DIM ** -0.5 * LOG2E
    outs = (qa_ref, ka_ref, va_ref, qb_ref, kb_ref, vb_ref)
    for n, o_ref in enumerate(outs):
        p = _dot(u, w_mix_ref[:, n * width:(n + 1) * width])
        if o_ref is ka_ref or o_ref is kb_ref:
            o_ref[...] = p.astype(o_ref.dtype)
        elif o_ref is qa_ref or o_ref is qb_ref:
            _store_transposed(p * scale, o_ref)
        else:
            _store_transposed(p, o_ref)
    logits = _dot(u, w_mix_ref[:, len(outs) * width:]) + b_gate_ref[...]
    gates_ref[...] = jax.nn.sigmoid(logits)


def _store_transposed(p, o_ref):
    n_blk, n_pairs, _, blk = o_ref.shape[1:]
    pt = p.T
    for c in range(n_blk):
        for pr in range(n_pairs):
            o_ref[0, c, pr] = pt[pr * PAIR:(pr + 1) * PAIR, c * blk:(c + 1) * blk].astype(o_ref.dtype)


def _out_ffn_kernel(x1_ref, ya_ref, yb_ref, gates_ref, wa_hbm, wb_hbm, wo_hbm,
                    g2_ref, w_in_hbm, w_out_hbm, gf_ref, o_ref,
                    wa_ref, wb_ref, wo_ref, w_in_ref, w_out_ref, *, final_norm):
    _load_weights_once([(wa_hbm, wa_ref), (wb_hbm, wb_ref), (wo_hbm, wo_ref),
                        (w_in_hbm, w_in_ref), (w_out_hbm, w_out_ref)])
    d = x1_ref.shape[1]
    merged = (gates_ref[:, :d] * _dot(ya_ref[...], wa_ref[...])
              + gates_ref[:, d:] * _dot(yb_ref[...], wb_ref[...]))
    x2 = x1_ref[...] + _dot(merged.astype(jnp.bfloat16), wo_ref[...])
    h = _rms(x2, g2_ref[...]).astype(jnp.bfloat16)
    x3 = x2 + 0.5 * _swiglu(h, w_in_ref, w_out_ref)
    o_ref[...] = _rms(x3, gf_ref[...]) if final_norm else x3


def _resident(shape):
    return pl.BlockSpec(shape, lambda *_: (0,) * len(shape), pipeline_mode=pl.Buffered(1))


_IN_HBM = pl.BlockSpec(memory_space=pl.ANY)


def _bf16_scratch(*weights):
    return [pltpu.VMEM(w.shape, jnp.bfloat16) for w in weights]


def _rows(tm, width):
    return pl.BlockSpec((tm, width), lambda i: (i, 0))


def _ffn_mix(x, g1, w_in, w_out, gm, w_mix, b_gate, width, seq):
    m, d = x.shape
    tm = ROW_TILE
    bf = jnp.bfloat16
    n_pairs = width // PAIR
    steps_per_seq = seq // tm

    def transposed(blk):
        shape = (m // seq, seq // blk, n_pairs, PAIR, blk)
        spec = pl.BlockSpec((1, tm // blk) + shape[2:], lambda i: (i // steps_per_seq, i % steps_per_seq, 0, 0, 0))
        return jax.ShapeDtypeStruct(shape, bf), spec

    keys = jax.ShapeDtypeStruct((m, width), bf), _rows(tm, width)
    outs = [(jax.ShapeDtypeStruct((m, d), jnp.float32), _rows(tm, d)),
            transposed(A_BLK), keys, transposed(A_BLK), transposed(B_BLK), keys, transposed(B_BLK),
            (jax.ShapeDtypeStruct((m, b_gate.shape[1]), jnp.float32), _rows(tm, b_gate.shape[1]))]
    return pl.pallas_call(
        _ffn_mix_kernel,
        grid=(m // tm,),
        in_specs=[_rows(tm, d), _resident(g1.shape), _IN_HBM, _IN_HBM,
                  _resident(gm.shape), _IN_HBM, _resident(b_gate.shape)],
        out_specs=[spec for _, spec in outs],
        out_shape=[shape for shape, _ in outs],
        scratch_shapes=_bf16_scratch(w_in, w_out, w_mix),
        compiler_params=pltpu.CompilerParams(dimension_semantics=("arbitrary",),
                                             vmem_limit_bytes=V7X_VMEM_LIMIT),
        name="ffn_mix",
    )(x, g1, w_in, w_out, gm, w_mix, b_gate)


def _out_ffn(x1, ya, yb, gates, wa, wb, wo, g2, w_in, w_out, gf, final_norm):
    m, d = x1.shape
    tm = ROW_TILE
    return pl.pallas_call(
        functools.partial(_out_ffn_kernel, final_norm=final_norm),
        grid=(m // tm,),
        in_specs=[_rows(tm, d), _rows(tm, ya.shape[1]), _rows(tm, yb.shape[1]), _rows(tm, gates.shape[1]),
                  _IN_HBM, _IN_HBM, _IN_HBM, _resident(g2.shape), _IN_HBM, _IN_HBM, _resident(gf.shape)],
        out_specs=_rows(tm, d),
        out_shape=jax.ShapeDtypeStruct((m, d), jnp.float32),
        scratch_shapes=_bf16_scratch(wa, wb, wo, w_in, w_out),
        compiler_params=pltpu.CompilerParams(dimension_semantics=("arbitrary",),
                                             vmem_limit_bytes=V7X_VMEM_LIMIT),
        name="out_ffn",
    )(x1, ya, yb, gates, wa, wb, wo, g2, w_in, w_out, gf)


def _t5_bucket(rel):
    nb = T5_BUCKETS // 2
    ret = jnp.where(rel > 0, nb, 0)
    n = jnp.abs(rel)
    max_exact = nb // 2
    nf = jnp.maximum(n, 1).astype(jnp.float32)
    large = max_exact + (jnp.log(nf / max_exact) / math.log(T5_MAX_DIST / max_exact)
                         * (nb - max_exact)).astype(jnp.int32)
    large = jnp.minimum(large, nb - 1)
    return ret + jnp.where(n < max_exact, n, large)


def _window_rel(n_tiles, blk):
    nb = blk // LANES
    u = jnp.arange(-(nb - 1), n_tiles * nb, dtype=jnp.int32)[:, None]
    return LANES - jnp.arange(2 * LANES, dtype=jnp.int32)[None, :] - LANES * u


def _bias_tiles_kernel(rel_bias_ref, tiles_ref, *, left_chunks):
    n_pairs, n_tiles, blk, _ = tiles_ref.shape
    nb = blk // LANES
    kj = lax.broadcasted_iota(jnp.int32, (LANES, LANES), 0)
    qi = lax.broadcasted_iota(jnp.int32, (LANES, LANES), 1)
    chunk_diff = kj // CHUNK - qi // CHUNK
    for p in range(n_pairs):
        for half in range(2):
            for w in range(rel_bias_ref.shape[2]):
                u = w - (nb - 1)
                row = jnp.broadcast_to(rel_bias_ref[p, half, w], (LANES, 2 * LANES))
                sub = pltpu.roll(row, 0, 1, stride=1, stride_axis=0)[:, LANES:]
                for t in range(n_tiles):
                    for a in range(nb):
                        c = t * nb + a - u
                        if not 0 <= c < nb:
                            continue
                        diff = chunk_diff - u * SUB_CHUNKS
                        allowed = diff <= 0
                        if left_chunks is not None:
                            allowed = allowed & (diff >= -left_chunks)
                        tiles_ref[p, t, pl.ds(c * LANES, LANES), pl.ds(half * blk + a * LANES, LANES)] = (
                            jnp.where(allowed, sub * LOG2E, NEG))


def _bias_tiles(rel_bias, n_tiles, blk, left_chunks):
    n_pairs = rel_bias.shape[0]
    return pl.pallas_call(
        functools.partial(_bias_tiles_kernel, left_chunks=left_chunks),
        out_shape=jax.ShapeDtypeStruct((n_pairs, n_tiles, blk, 2 * blk), jnp.float32),
        compiler_params=pltpu.CompilerParams(vmem_limit_bytes=V7X_VMEM_LIMIT),
        name="bias_tiles",
    )(rel_bias)


def _sum_rows(x):
    acc = x[:SUBLANES]
    for r in range(1, x.shape[0] // SUBLANES):
        acc = acc + x[r * SUBLANES:(r + 1) * SUBLANES]
    return acc


def _fold_rows(x):
    m = x[:SUBLANES]
    for r in range(1, x.shape[0] // SUBLANES):
        m = jnp.maximum(m, x[r * SUBLANES:(r + 1) * SUBLANES])
    return m


def _attn_kernel(*refs, left_blocks, lambda_init):
    if lambda_init is None:
        qt_ref, k_ref, vt_ref, tiles_ref, o_ref, q2t_ref, s_ref, acc_ref = refs
    else:
        (qt_ref, k_ref, vt_ref, tiles_ref, lq1_ref, lk1_ref, lq2_ref, lk2_ref, subg_ref, o_ref,
         q2t_ref, *s_refs, m_ref, acc_ref) = refs
        lam = (jnp.exp(jnp.sum(lq1_ref[...] * lk1_ref[...], axis=-1, keepdims=True))
               - jnp.exp(jnp.sum(lq2_ref[...] * lk2_ref[...], axis=-1, keepdims=True))
               + lambda_init)
    n_q, n_pairs, _, blk = qt_ref.shape[1:]
    top_half = lax.broadcasted_iota(jnp.int32, (PAIR, blk), 0) < HEAD_DIM

    def cols(p):
        return slice(p * PAIR, (p + 1) * PAIR)

    def key_rows(j, n=1):
        return pl.ds(pl.multiple_of(j * blk, blk), n * blk)

    def weights_update(vt, e):
        return jnp.concatenate([_dot(vt, e.astype(jnp.bfloat16)), _sum_rows(e)], axis=0)

    def stack_queries(c):
        for p in range(n_pairs):
            qt = qt_ref[0, c, p]
            zero = jnp.zeros_like(qt)
            q2t_ref[p, :, :blk] = jnp.where(top_half, qt, zero)
            q2t_ref[p, :, blk:] = jnp.where(top_half, zero, qt)

    def finish(c):
        for p in range(n_pairs):
            o = acc_ref[p, :PAIR] / jnp.sum(acc_ref[p, PAIR:], axis=0, keepdims=True)
            if lambda_init is None:
                y = jnp.where(top_half, o[:, :blk], o[:, blk:]).T
            else:
                d = o[:, :blk] - lam * o[:, blk:]
                d = d * lax.rsqrt(jnp.mean(d * d, axis=0, keepdims=True) + EPS)
                y = d.T * subg_ref[...] * (1.0 - lambda_init)
            o_ref[0, key_rows(c), cols(p)] = y.astype(o_ref.dtype)

    if left_blocks is None:
        n_tiles = tiles_ref.shape[1]
        ring = len(s_refs)

        def causal_prefix(i):
            stack_queries(i)
            n_blocks = i + 1
            acc_ref[...] = jnp.zeros(acc_ref.shape, jnp.float32)
            m_ref[...] = jnp.full(m_ref.shape, NEG, jnp.float32)

            def prefetch_scores(j, buf_ref, p):
                tile = jnp.minimum(i - j, n_tiles - 1)
                buf_ref[p] = _dot(k_ref[0, key_rows(j), cols(p)], q2t_ref[p]) + tiles_ref[p, tile]

            def online(j, buf_ref, p):
                s = buf_ref[p]
                m_old = m_ref[p]
                m = jnp.maximum(m_old, jnp.max(_fold_rows(s), axis=0, keepdims=True))
                acc_ref[p] = jnp.exp2(m_old - m) * acc_ref[p] + weights_update(vt_ref[0, j, p], jnp.exp2(s - m))
                m_ref[p] = m

            def step(j, cur_ref, nxt_ref=None):
                if nxt_ref is not None:
                    for p in range(n_pairs):
                        prefetch_scores(jnp.minimum(j + 1, i), nxt_ref, p)
                for p in range(n_pairs):
                    online(j, cur_ref, p)

            def ring_trip(t, carry):
                for r in range(ring):
                    step(ring * t + r, s_refs[r], s_refs[(r + 1) % ring])
                return carry

            for p in range(n_pairs):
                prefetch_scores(0, s_refs[0], p)
            lax.fori_loop(0, n_blocks // ring, ring_trip, 0)
            first_left = n_blocks - n_blocks % ring
            for left in range(1, ring):
                @pl.when(n_blocks % ring == left)
                def _(left=left):
                    for r in range(left):
                        step(first_left + r, s_refs[r], s_refs[r + 1] if r + 1 < left else None)
            finish(i)

        pl.loop(0, n_q)(causal_prefix)
    else:
        def band(i, n_left):
            order = range(n_left, -1, -1)
            ms = [jnp.full((SUBLANES, 2 * blk), NEG, jnp.float32) for _ in range(n_pairs)]
            for d in order:
                for p in range(n_pairs):
                    s = _dot(k_ref[0, key_rows(i - d), cols(p)], q2t_ref[p]) + tiles_ref[p, d]
                    s_ref[d, p] = s
                    ms[p] = jnp.maximum(ms[p], _fold_rows(s))
            ms = [jnp.max(m, axis=0, keepdims=True) for m in ms]
            for d in order:
                for p in range(n_pairs):
                    update = weights_update(vt_ref[0, i - d, p], jnp.exp2(s_ref[d, p] - ms[p]))
                    acc_ref[p] = update if d == n_left else acc_ref[p] + update

        @pl.loop(0, n_q)
        def _(i):
            stack_queries(i)
            for n_left in range(left_blocks):
                pl.when(i == n_left)(functools.partial(band, i, n_left))
            pl.when(i >= left_blocks)(functools.partial(band, i, left_blocks))
            finish(i)


def _attention(qt, k, vt, tiles, left_blocks, lambda_init=None, extra=()):
    b, s, w = k.shape
    n_pairs, _, blk = qt.shape[2:]
    whole_k = pl.BlockSpec((1, s, w), lambda bi: (bi, 0, 0))
    whole_t = pl.BlockSpec((1,) + vt.shape[1:], lambda bi: (bi, 0, 0, 0, 0))
    f32 = jnp.float32
    score_block = (n_pairs, blk, 2 * blk)
    if left_blocks is None:
        score_scratch = [pltpu.VMEM(score_block, f32)] * B_SCORE_RING + [pltpu.VMEM((n_pairs, 1, 2 * blk), f32)]
    else:
        score_scratch = [pltpu.VMEM((left_blocks + 1,) + score_block, f32)]
    return pl.pallas_call(
        functools.partial(_attn_kernel, left_blocks=left_blocks, lambda_init=lambda_init),
        grid=(b,),
        in_specs=[whole_t, whole_k, whole_t, _resident(tiles.shape)] + [_resident(e.shape) for e in extra],
        out_specs=whole_k,
        out_shape=jax.ShapeDtypeStruct((b, s, w), jnp.bfloat16),
        scratch_shapes=[pltpu.VMEM((n_pairs, PAIR, 2 * blk), jnp.bfloat16), *score_scratch,
                        pltpu.VMEM((n_pairs, ACC_ROWS, 2 * blk), f32)],
        compiler_params=pltpu.CompilerParams(dimension_semantics=("arbitrary",),
                                             vmem_limit_bytes=V7X_VMEM_LIMIT),
        name="attn_a" if lambda_init is None else "attn_b",
    )(qt, k, vt, tiles, *extra)


def kernel(x, ffn1_norm, ffn1_w_in, ffn1_w_out, mix_norm, w_mix_in, b_gate, rel_bias_a, lambda_q1, lambda_k1, lambda_q2, lambda_k2, subln_g, t5_bias, w_branch_a, w_branch_b, w_o, ffn2_norm, ffn2_w_in, ffn2_w_out, final_norm):
    b, s, d = x.shape
    depth = ffn1_norm.shape[0]
    assert depth >= 1
    width = w_branch_a.shape[1]
    n_pairs = width // PAIR
    f32 = jnp.float32

    t5_rel = t5_bias.astype(f32)[:, _t5_bucket(_window_rel(B_TILES, B_BLK))]
    t5_rel = jnp.broadcast_to(t5_rel[:, None, :, None, :], (n_pairs, 2) + t5_rel.shape[1:2] + (1, 2 * LANES))
    tiles_b = _bias_tiles(t5_rel, B_TILES, B_BLK, None)

    xf = x.astype(f32).reshape(b * s, d)
    for li in range(depth):
        x1, qa, ka, va, qb, kb, vb, gates = _ffn_mix(
            xf, ffn1_norm[li][None].astype(f32), ffn1_w_in[li].astype(f32), ffn1_w_out[li].astype(f32),
            mix_norm[li][None].astype(f32), w_mix_in[li].astype(f32), b_gate[li][None].astype(f32), width, s)

        rel_a = jnp.clip(_window_rel(A_TILES, A_BLK), -REL_CLIP, REL_CLIP) + REL_CLIP
        a_rel = rel_bias_a[li].astype(f32)[:, rel_a]
        tiles_a = _bias_tiles(a_rel.reshape(n_pairs, 2, rel_a.shape[0], 1, 2 * LANES), A_TILES, A_BLK, LEFT_CHUNKS)

        def seq(t):
            return t.reshape(b, s, width)

        ya = _attention(qa, seq(ka), va, tiles_a, A_TILES - 1)
        lambda_init = 0.8 - 0.6 * math.exp(-0.3 * li)
        lams = [t[li][None].astype(f32) for t in (lambda_q1, lambda_k1, lambda_q2, lambda_k2)]
        yb = _attention(qb, seq(kb), vb, tiles_b, None, lambda_init=lambda_init,
                        extra=(*lams, subln_g[li][None].astype(f32)))

        xf = _out_ffn(x1, ya.reshape(b * s, width), yb.reshape(b * s, width), gates,
                      w_branch_a[li].astype(f32), w_branch_b[li].astype(f32), w_o[li].astype(f32),
                      ffn2_norm[li][None].astype(f32), ffn2_w_in[li].astype(f32), ffn2_w_out[li].astype(f32),
                      final_norm[None].astype(f32), final_norm=(li == depth - 1))
    return xf.reshape(b, s, d).astype(x.dtype)
```

```python
import functools
import math

import jax
import jax.numpy as jnp
from jax import lax
from jax.experimental import pallas as pl
from jax.experimental.pallas import tpu as pltpu

EPS = 1e-6
NEG = -1e30
LOG2E = math.log2(math.e)

CHUNK = 64
LEFT_CHUNKS = 8
REL_CLIP = 128
T5_BUCKETS = 32
T5_MAX_DIST = 128

LANES = 128
SUBLANES = 8
HEAD_DIM = 64
PAIR = 2 * HEAD_DIM
SUB_CHUNKS = LANES // CHUNK
ACC_ROWS = PAIR + SUBLANES

A_BLK = 128
A_TILES = LEFT_CHUNKS * CHUNK // A_BLK + 1
B_BLK = 256
B_TILES = 3
B_SCORE_RING = 2
A_STEP_BLOCKS = 4
B_STEP_BLOCKS = 4

V7X_VMEM_LIMIT = 60000 * 1024

ROW_TILE = 512
BF16_ROWS = 16
CAST_CHUNK_BYTES = 3 << 18
CAST_SLOTS = 4


def _rms(xf, g):
    return xf * lax.rsqrt(jnp.mean(xf * xf, axis=-1, keepdims=True) + EPS) * g


def _dot(a, b):
    return jnp.dot(a, b, preferred_element_type=jnp.float32)


def _swiglu(h, w_in_ref, w_out_ref):
    d_ff = w_out_ref.shape[0]
    gate = _dot(h, w_in_ref[:, :d_ff])
    up = _dot(h, w_in_ref[:, d_ff:])
    act = (gate * jax.nn.sigmoid(gate) * up).astype(jnp.bfloat16)
    return _dot(act, w_out_ref[...])


def _cast_chunk_rows(rows, width):
    fits = [r for r in range(BF16_ROWS, rows + 1, BF16_ROWS)
            if rows % r == 0 and r * width * 4 <= CAST_CHUNK_BYTES]
    return max(fits)


def _load_cast(src_hbm, dst_ref):
    rows, width = src_hbm.shape
    chunk = _cast_chunk_rows(rows, width)
    n_chunks = rows // chunk

    ahead = min(CAST_SLOTS, n_chunks) - 1

    def body(stage_ref, sem):
        def copy(c):
            slot = c % CAST_SLOTS
            return pltpu.make_async_copy(src_hbm.at[pl.ds(c * chunk, chunk)], stage_ref.at[slot], sem.at[slot])

        for c in range(ahead):
            copy(c).start()

        def one_chunk(c, carry):
            @pl.when(c + ahead < n_chunks)
            def _():
                copy(c + ahead).start()

            copy(c).wait()
            dst_ref[pl.ds(pl.multiple_of(c * chunk, chunk), chunk)] = (
                stage_ref[c % CAST_SLOTS].astype(dst_ref.dtype))
            return carry

        lax.fori_loop(0, n_chunks, one_chunk, 0)

    pl.run_scoped(body, pltpu.VMEM((CAST_SLOTS, chunk, width), src_hbm.dtype),
                  pltpu.SemaphoreType.DMA((CAST_SLOTS,)))


def _load_weights_once(pairs):
    @pl.when(pl.program_id(0) == 0)
    def _():
        for src_hbm, dst_ref in pairs:
            _load_cast(src_hbm, dst_ref)


def _ffn_mix_kernel(x_ref, g1_ref, w_in_hbm, w_out_hbm, gm_ref, w_mix_hbm, b_gate_ref,
                    x1_ref, qa_ref, ka_ref, va_ref, qb_ref, kb_ref, vb_ref, gates_ref,
                    w_in_ref, w_out_ref, w_mix_ref):
    _load_weights_once([(w_in_hbm, w_in_ref), (w_out_hbm, w_out_ref), (w_mix_hbm, w_mix_ref)])
    x = x_ref[...]
    h = _rms(x, g1_ref[...]).astype(jnp.bfloat16)
    x1 = x + 0.5 * _swiglu(h, w_in_ref, w_out_ref)
    x1_ref[...] = x1

    u = _rms(x1, gm_ref[...]).astype(jnp.bfloat16)
    width = ka_ref.shape[1]
    scale = HEAD_DIM ** -0.5 * LOG2E
    outs = (qa_ref, ka_ref, va_ref, qb_ref, kb_ref, vb_ref)
    for n, o_ref in enumerate(outs):
        p = _dot(u, w_mix_ref[:, n * width:(n + 1) * width])
        if o_ref is ka_ref or o_ref is kb_ref:
            o_ref[...] = p.astype(o_ref.dtype)
        elif o_ref is qa_ref or o_ref is qb_ref:
            _store_transposed(p * scale, o_ref)
        else:
            _store_transposed(p, o_ref)
    logits = _dot(u, w_mix_ref[:, len(outs) * width:]) + b_gate_ref[...]
    gates_ref[...] = jax.nn.sigmoid(logits)


def _store_transposed(p, o_ref):
    n_blk, n_pairs, _, blk = o_ref.shape[1:]
    pt = p.T
    for c in range(n_blk):
        for pr in range(n_pairs):
            o_ref[0, c, pr] = pt[pr * PAIR:(pr + 1) * PAIR, c * blk:(c + 1) * blk].astype(o_ref.dtype)


def _out_ffn_kernel(x1_ref, ya_ref, yb_ref, gates_ref, wa_hbm, wb_hbm, wo_hbm,
                    g2_ref, w_in_hbm, w_out_hbm, gf_ref, o_ref,
                    wa_ref, wb_ref, wo_ref, w_in_ref, w_out_ref, *, final_norm):
    _load_weights_once([(wa_hbm, wa_ref), (wb_hbm, wb_ref), (wo_hbm, wo_ref),
                        (w_in_hbm, w_in_ref), (w_out_hbm, w_out_ref)])
    d = x1_ref.shape[1]
    merged = (gates_ref[:, :d] * _dot(ya_ref[...], wa_ref[...])
              + gates_ref[:, d:] * _dot(yb_ref[...], wb_ref[...]))
    x2 = x1_ref[...] + _dot(merged.astype(jnp.bfloat16), wo_ref[...])
    h = _rms(x2, g2_ref[...]).astype(jnp.bfloat16)
    x3 = x2 + 0.5 * _swiglu(h, w_in_ref, w_out_ref)
    o_ref[...] = _rms(x3, gf_ref[...]) if final_norm else x3


def _resident(shape):
    return pl.BlockSpec(shape, lambda *_: (0,) * len(shape), pipeline_mode=pl.Buffered(1))


_IN_HBM = pl.BlockSpec(memory_space=pl.ANY)


def _bf16_scratch(*weights):
    return [pltpu.VMEM(w.shape, jnp.bfloat16) for w in weights]


def _rows(tm, width):
    return pl.BlockSpec((tm, width), lambda i: (i, 0))


def _ffn_mix(x, g1, w_in, w_out, gm, w_mix, b_gate, width, seq):
    m, d = x.shape
    tm = ROW_TILE
    bf = jnp.bfloat16
    n_pairs = width // PAIR
    steps_per_seq = seq // tm

    def transposed(blk):
        shape = (m // seq, seq // blk, n_pairs, PAIR, blk)
        spec = pl.BlockSpec((1, tm // blk) + shape[2:], lambda i: (i // steps_per_seq, i % steps_per_seq, 0, 0, 0))
        return jax.ShapeDtypeStruct(shape, bf), spec

    keys = jax.ShapeDtypeStruct((m, width), bf), _rows(tm, width)
    outs = [(jax.ShapeDtypeStruct((m, d), jnp.float32), _rows(tm, d)),
            transposed(A_BLK), keys, transposed(A_BLK), transposed(B_BLK), keys, transposed(B_BLK),
            (jax.ShapeDtypeStruct((m, b_gate.shape[1]), jnp.float32), _rows(tm, b_gate.shape[1]))]
    return pl.pallas_call(
        _ffn_mix_kernel,
        grid=(m // tm,),
        in_specs=[_rows(tm, d), _resident(g1.shape), _IN_HBM, _IN_HBM,
                  _resident(gm.shape), _IN_HBM, _resident(b_gate.shape)],
        out_specs=[spec for _, spec in outs],
        out_shape=[shape for shape, _ in outs],
        scratch_shapes=_bf16_scratch(w_in, w_out, w_mix),
        compiler_params=pltpu.CompilerParams(dimension_semantics=("arbitrary",),
                                             vmem_limit_bytes=V7X_VMEM_LIMIT),
        name="ffn_mix",
    )(x, g1, w_in, w_out, gm, w_mix, b_gate)


def _out_ffn(x1, ya, yb, gates, wa, wb, wo, g2, w_in, w_out, gf, final_norm):
    m, d = x1.shape
    tm = ROW_TILE
    return pl.pallas_call(
        functools.partial(_out_ffn_kernel, final_norm=final_norm),
        grid=(m // tm,),
        in_specs=[_rows(tm, d), _rows(tm, ya.shape[1]), _rows(tm, yb.shape[1]), _rows(tm, gates.shape[1]),
                  _IN_HBM, _IN_HBM, _IN_HBM, _resident(g2.shape), _IN_HBM, _IN_HBM, _resident(gf.shape)],
        out_specs=_rows(tm, d),
        out_shape=jax.ShapeDtypeStruct((m, d), jnp.float32),
        scratch_shapes=_bf16_scratch(wa, wb, wo, w_in, w_out),
        compiler_params=pltpu.CompilerParams(dimension_semantics=("arbitrary",),
                                             vmem_limit_bytes=V7X_VMEM_LIMIT),
        name="out_ffn",
    )(x1, ya, yb, gates, wa, wb, wo, g2, w_in, w_out, gf)


def _t5_bucket(rel):
    nb = T5_BUCKETS // 2
    ret = jnp.where(rel > 0, nb, 0)
    n = jnp.abs(rel)
    max_exact = nb // 2
    nf = jnp.maximum(n, 1).astype(jnp.float32)
    large = max_exact + (jnp.log(nf / max_exact) / math.log(T5_MAX_DIST / max_exact)
                         * (nb - max_exact)).astype(jnp.int32)
    large = jnp.minimum(large, nb - 1)
    return ret + jnp.where(n < max_exact, n, large)


def _window_rel(n_tiles, blk):
    nb = blk // LANES
    u = jnp.arange(-(nb - 1), n_tiles * nb, dtype=jnp.int32)[:, None]
    return LANES - jnp.arange(2 * LANES, dtype=jnp.int32)[None, :] - LANES * u


def _bias_tiles_kernel(rel_bias_ref, tiles_ref, *, left_chunks):
    n_pairs, n_tiles, blk, _ = tiles_ref.shape
    nb = blk // LANES
    kj = lax.broadcasted_iota(jnp.int32, (LANES, LANES), 0)
    qi = lax.broadcasted_iota(jnp.int32, (LANES, LANES), 1)
    chunk_diff = kj // CHUNK - qi // CHUNK
    for p in range(n_pairs):
        for half in range(2):
            for w in range(rel_bias_ref.shape[2]):
                u = w - (nb - 1)
                row = jnp.broadcast_to(rel_bias_ref[p, half, w], (LANES, 2 * LANES))
                sub = pltpu.roll(row, 0, 1, stride=1, stride_axis=0)[:, LANES:]
                for t in range(n_tiles):
                    for a in range(nb):
                        c = t * nb + a - u
                        if not 0 <= c < nb:
                            continue
                        diff = chunk_diff - u * SUB_CHUNKS
                        allowed = diff <= 0
                        if left_chunks is not None:
                            allowed = allowed & (diff >= -left_chunks)
                        tiles_ref[p, t, pl.ds(c * LANES, LANES), pl.ds(half * blk + a * LANES, LANES)] = (
                            jnp.where(allowed, sub * LOG2E, NEG))


def _bias_tiles(rel_bias, n_tiles, blk, left_chunks):
    n_pairs = rel_bias.shape[0]
    return pl.pallas_call(
        functools.partial(_bias_tiles_kernel, left_chunks=left_chunks),
        out_shape=jax.ShapeDtypeStruct((n_pairs, n_tiles, blk, 2 * blk), jnp.float32),
        compiler_params=pltpu.CompilerParams(vmem_limit_bytes=V7X_VMEM_LIMIT),
        name="bias_tiles",
    )(rel_bias)


def _sum_rows(x):
    acc = x[:SUBLANES]
    for r in range(1, x.shape[0] // SUBLANES):
        acc = acc + x[r * SUBLANES:(r + 1) * SUBLANES]
    return acc


def _fold_rows(x):
    m = x[:SUBLANES]
    for r in range(1, x.shape[0] // SUBLANES):
        m = jnp.maximum(m, x[r * SUBLANES:(r + 1) * SUBLANES])
    return m


def _attn_kernel(*refs, left_blocks, lambda_init):
    if lambda_init is None:
        qt_ref, k_ref, vt_ref, tiles_ref, o_ref, q2t_ref, s_ref, acc_ref = refs
    else:
        (qt_ref, k_ref, vt_ref, tiles_ref, lq1_ref, lk1_ref, lq2_ref, lk2_ref, subg_ref, o_ref,
         q2t_ref, *s_refs, m_ref, acc_ref) = refs
        lam = (jnp.exp(jnp.sum(lq1_ref[...] * lk1_ref[...], axis=-1, keepdims=True))
               - jnp.exp(jnp.sum(lq2_ref[...] * lk2_ref[...], axis=-1, keepdims=True))
               + lambda_init)
    n_q, n_pairs, _, blk = qt_ref.shape[1:]
    top_half = lax.broadcasted_iota(jnp.int32, (PAIR, blk), 0) < HEAD_DIM

    def cols(p):
        return slice(p * PAIR, (p + 1) * PAIR)

    def key_rows(j, n=1):
        return pl.ds(pl.multiple_of(j * blk, blk), n * blk)

    def weights_update(vt, e):
        return jnp.concatenate([_dot(vt, e.astype(jnp.bfloat16)), _sum_rows(e)], axis=0)

    def stack_queries(c):
        for p in range(n_pairs):
            qt = qt_ref[0, c, p]
            zero = jnp.zeros_like(qt)
            q2t_ref[p, :, :blk] = jnp.where(top_half, qt, zero)
            q2t_ref[p, :, blk:] = jnp.where(top_half, zero, qt)

    def finish(c):
        for p in range(n_pairs):
            o = acc_ref[p, :PAIR] / jnp.sum(acc_ref[p, PAIR:], axis=0, keepdims=True)
            if lambda_init is None:
                y = jnp.where(top_half, o[:, :blk], o[:, blk:]).T
            else:
                d = o[:, :blk] - lam * o[:, blk:]
                d = d * lax.rsqrt(jnp.mean(d * d, axis=0, keepdims=True) + EPS)
                y = d.T * subg_ref[...] * (1.0 - lambda_init)
            o_ref[0, c * blk:(c + 1) * blk, cols(p)] = y.astype(o_ref.dtype)

    if left_blocks is None:
        n_tiles = tiles_ref.shape[1]
        ring = len(s_refs)

        def causal_prefix(c):
            i = pl.program_id(1) * n_q + c
            stack_queries(c)
            n_blocks = i + 1
            acc_ref[...] = jnp.zeros(acc_ref.shape, jnp.float32)
            m_ref[...] = jnp.full(m_ref.shape, NEG, jnp.float32)

            def prefetch_scores(j, buf_ref, p):
                tile = jnp.minimum(i - j, n_tiles - 1)
                buf_ref[p] = _dot(k_ref[0, key_rows(j), cols(p)], q2t_ref[p]) + tiles_ref[p, tile]

            def online(j, buf_ref, p):
                s = buf_ref[p]
                m_old = m_ref[p]
                m = jnp.maximum(m_old, jnp.max(_fold_rows(s), axis=0, keepdims=True))
                acc_ref[p] = jnp.exp2(m_old - m) * acc_ref[p] + weights_update(vt_ref[0, j, p], jnp.exp2(s - m))
                m_ref[p] = m

            def step(j, cur_ref, nxt_ref=None):
                if nxt_ref is not None:
                    for p in range(n_pairs):
                        prefetch_scores(jnp.minimum(j + 1, i), nxt_ref, p)
                for p in range(n_pairs):
                    online(j, cur_ref, p)

            def ring_trip(t, carry):
                for r in range(ring):
                    step(ring * t + r, s_refs[r], s_refs[(r + 1) % ring])
                return carry

            for p in range(n_pairs):
                prefetch_scores(0, s_refs[0], p)
            lax.fori_loop(0, n_blocks // ring, ring_trip, 0)
            first_left = n_blocks - n_blocks % ring
            for left in range(1, ring):
                @pl.when(n_blocks % ring == left)
                def _(left=left):
                    for r in range(left):
                        step(first_left + r, s_refs[r], s_refs[r + 1] if r + 1 < left else None)
            finish(c)

        for c in range(n_q):
            causal_prefix(c)
    else:
        def band(i, n_left):
            order = range(n_left, -1, -1)
            ms = [jnp.full((SUBLANES, 2 * blk), NEG, jnp.float32) for _ in range(n_pairs)]
            for d in order:
                for p in range(n_pairs):
                    s = _dot(k_ref[0, key_rows(i - d), cols(p)], q2t_ref[p]) + tiles_ref[p, d]
                    s_ref[d, p] = s
                    ms[p] = jnp.maximum(ms[p], _fold_rows(s))
            ms = [jnp.max(m, axis=0, keepdims=True) for m in ms]
            for d in order:
                for p in range(n_pairs):
                    update = weights_update(vt_ref[0, i - d, p], jnp.exp2(s_ref[d, p] - ms[p]))
                    acc_ref[p] = update if d == n_left else acc_ref[p] + update

        for c in range(n_q):
            i = pl.program_id(1) * n_q + c
            stack_queries(c)
            for n_left in range(c, left_blocks, n_q):
                pl.when(i == n_left)(functools.partial(band, i, n_left))
            pl.when(i >= left_blocks)(functools.partial(band, i, left_blocks))
            finish(c)


def _attention(qt, k, vt, tiles, left_blocks, lambda_init=None, extra=()):
    b, s, w = k.shape
    n_pairs, _, blk = qt.shape[2:]
    n_q = B_STEP_BLOCKS if left_blocks is None else A_STEP_BLOCKS
    blk_q = pl.BlockSpec((1, n_q) + qt.shape[2:], lambda bi, i: (bi, i, 0, 0, 0))
    whole_k = pl.BlockSpec((1, s, w), lambda bi, i: (bi, 0, 0))
    whole_v = pl.BlockSpec((1,) + vt.shape[1:], lambda bi, i: (bi, 0, 0, 0, 0))
    f32 = jnp.float32
    score_block = (n_pairs, blk, 2 * blk)
    if left_blocks is None:
        score_scratch = [pltpu.VMEM(score_block, f32)] * B_SCORE_RING + [pltpu.VMEM((n_pairs, 1, 2 * blk), f32)]
    else:
        score_scratch = [pltpu.VMEM((left_blocks + 1,) + score_block, f32)]
    return pl.pallas_call(
        functools.partial(_attn_kernel, left_blocks=left_blocks, lambda_init=lambda_init),
        grid=(b, s // (n_q * blk)),
        in_specs=[blk_q, whole_k, whole_v, _resident(tiles.shape)] + [_resident(e.shape) for e in extra],
        out_specs=pl.BlockSpec((1, n_q * blk, w), lambda bi, i: (bi, i, 0)),
        out_shape=jax.ShapeDtypeStruct((b, s, w), jnp.bfloat16),
        scratch_shapes=[pltpu.VMEM((n_pairs, PAIR, 2 * blk), jnp.bfloat16), *score_scratch,
                        pltpu.VMEM((n_pairs, ACC_ROWS, 2 * blk), f32)],
        compiler_params=pltpu.CompilerParams(dimension_semantics=("arbitrary", "arbitrary"),
                                             vmem_limit_bytes=V7X_VMEM_LIMIT),
        name="attn_a" if lambda_init is None else "attn_b",
    )(qt, k, vt, tiles, *extra)


def kernel(x, ffn1_norm, ffn1_w_in, ffn1_w_out, mix_norm, w_mix_in, b_gate, rel_bias_a, lambda_q1, lambda_k1, lambda_q2, lambda_k2, subln_g, t5_bias, w_branch_a, w_branch_b, w_o, ffn2_norm, ffn2_w_in, ffn2_w_out, final_norm):
    b, s, d = x.shape
    depth = ffn1_norm.shape[0]
    assert depth >= 1
    width = w_branch_a.shape[1]
    n_pairs = width // PAIR
    f32 = jnp.float32

    t5_rel = t5_bias.astype(f32)[:, _t5_bucket(_window_rel(B_TILES, B_BLK))]
    t5_rel = jnp.broadcast_to(t5_rel[:, None, :, None, :], (n_pairs, 2) + t5_rel.shape[1:2] + (1, 2 * LANES))
    tiles_b = _bias_tiles(t5_rel, B_TILES, B_BLK, None)

    xf = x.astype(f32).reshape(b * s, d)
    for li in range(depth):
        x1, qa, ka, va, qb, kb, vb, gates = _ffn_mix(
            xf, ffn1_norm[li][None].astype(f32), ffn1_w_in[li].astype(f32), ffn1_w_out[li].astype(f32),
            mix_norm[li][None].astype(f32), w_mix_in[li].astype(f32), b_gate[li][None].astype(f32), width, s)

        rel_a = jnp.clip(_window_rel(A_TILES, A_BLK), -REL_CLIP, REL_CLIP) + REL_CLIP
        a_rel = rel_bias_a[li].astype(f32)[:, rel_a]
        tiles_a = _bias_tiles(a_rel.reshape(n_pairs, 2, rel_a.shape[0], 1, 2 * LANES), A_TILES, A_BLK, LEFT_CHUNKS)

        def seq(t):
            return t.reshape(b, s, width)

        ya = _attention(qa, seq(ka), va, tiles_a, A_TILES - 1)
        lambda_init = 0.8 - 0.6 * math.exp(-0.3 * li)
        lams = [t[li][None].astype(f32) for t in (lambda_q1, lambda_k1, lambda_q2, lambda_k2)]
        yb = _attention(qb, seq(kb), vb, tiles_b, None, lambda_init=lambda_init,
                        extra=(*lams, subln_g[li][None].astype(f32)))

        xf = _out_ffn(x1, ya.reshape(b * s, width), yb.reshape(b * s, width), gates,
                      w_branch_a[li].astype(f32), w_branch_b[li].astype(f32), w_o[li].astype(f32),
                      ffn2_norm[li][None].astype(f32), ffn2_w_in[li].astype(f32), ffn2_w_out[li].astype(f32),
                      final_norm[None].astype(f32), final_norm=(li == depth - 1))
    return xf.reshape(b, s, d).astype(x.dtype)
```

```python
import functools
import math

import jax
import jax.numpy as jnp
from jax import lax
from jax.experimental import pallas as pl
from jax.experimental.pallas import tpu as pltpu

EPS = 1e-6
NEG = -1e30
LOG2E = math.log2(math.e)

CHUNK = 64
LEFT_CHUNKS = 8
REL_CLIP = 128
T5_BUCKETS = 32
T5_MAX_DIST = 128

LANES = 128
SUBLANES = 8
HEAD_DIM = 64
PAIR = 2 * HEAD_DIM
SUB_CHUNKS = LANES // CHUNK
ACC_ROWS = PAIR + SUBLANES

A_BLK = 128
A_TILES = LEFT_CHUNKS * CHUNK // A_BLK + 1
B_BLK = 256
B_TILES = 3
B_SCORE_RING = 4
A_STEP_BLOCKS = 4
B_STEP_BLOCKS = 4

V7X_VMEM_LIMIT = 60000 * 1024

ROW_TILE = 512
BF16_ROWS = 16
CAST_CHUNK_BYTES = 3 << 18
CAST_SLOTS = 4


def _rms(xf, g):
    return xf * lax.rsqrt(jnp.mean(xf * xf, axis=-1, keepdims=True) + EPS) * g


def _dot(a, b):
    return jnp.dot(a, b, preferred_element_type=jnp.float32)


def _swiglu(h, w_in_ref, w_out_ref):
    d_ff = w_out_ref.shape[0]
    gate = _dot(h, w_in_ref[:, :d_ff])
    up = _dot(h, w_in_ref[:, d_ff:])
    act = (gate * jax.nn.sigmoid(gate) * up).astype(jnp.bfloat16)
    return _dot(act, w_out_ref[...])


def _cast_chunk_rows(rows, width):
    fits = [r for r in range(BF16_ROWS, rows + 1, BF16_ROWS)
            if rows % r == 0 and r * width * 4 <= CAST_CHUNK_BYTES]
    return max(fits)


def _load_cast(src_hbm, dst_ref):
    rows, width = src_hbm.shape
    chunk = _cast_chunk_rows(rows, width)
    n_chunks = rows // chunk

    ahead = min(CAST_SLOTS, n_chunks) - 1

    def body(stage_ref, sem):
        def copy(c):
            slot = c % CAST_SLOTS
            return pltpu.make_async_copy(src_hbm.at[pl.ds(c * chunk, chunk)], stage_ref.at[slot], sem.at[slot])

        for c in range(ahead):
            copy(c).start()

        def one_chunk(c, carry):
            @pl.when(c + ahead < n_chunks)
            def _():
                copy(c + ahead).start()

            copy(c).wait()
            dst_ref[pl.ds(pl.multiple_of(c * chunk, chunk), chunk)] = (
                stage_ref[c % CAST_SLOTS].astype(dst_ref.dtype))
            return carry

        lax.fori_loop(0, n_chunks, one_chunk, 0)

    pl.run_scoped(body, pltpu.VMEM((CAST_SLOTS, chunk, width), src_hbm.dtype),
                  pltpu.SemaphoreType.DMA((CAST_SLOTS,)))


def _load_weights_once(pairs):
    @pl.when(pl.program_id(0) == 0)
    def _():
        for src_hbm, dst_ref in pairs:
            _load_cast(src_hbm, dst_ref)


def _ffn_mix_kernel(x_ref, g1_ref, w_in_hbm, w_out_hbm, gm_ref, w_mix_hbm, b_gate_ref,
                    x1_ref, qa_ref, ka_ref, va_ref, qb_ref, kb_ref, vb_ref, gates_ref,
                    w_in_ref, w_out_ref, w_mix_ref):
    _load_weights_once([(w_in_hbm, w_in_ref), (w_out_hbm, w_out_ref), (w_mix_hbm, w_mix_ref)])
    x = x_ref[...]
    h = _rms(x, g1_ref[...]).astype(jnp.bfloat16)
    x1 = x + 0.5 * _swiglu(h, w_in_ref, w_out_ref)
    x1_ref[...] = x1

    u = _rms(x1, gm_ref[...]).astype(jnp.bfloat16)
    width = ka_ref.shape[1]
    scale = HEAD_DIM ** -0.5 * LOG2E
    outs = (qa_ref, ka_ref, va_ref, qb_ref, kb_ref, vb_ref)
    for n, o_ref in enumerate(outs):
        p = _dot(u, w_mix_ref[:, n * width:(n + 1) * width])
        if o_ref is ka_ref or o_ref is kb_ref:
            o_ref[...] = p.astype(o_ref.dtype)
        elif o_ref is qa_ref or o_ref is qb_ref:
            _store_transposed(p * scale, o_ref)
        else:
            _store_transposed(p, o_ref)
    logits = _dot(u, w_mix_ref[:, len(outs) * width:]) + b_gate_ref[...]
    gates_ref[...] = jax.nn.sigmoid(logits)


def _store_transposed(p, o_ref):
    n_blk, n_pairs, _, blk = o_ref.shape[1:]
    pt = p.T
    for c in range(n_blk):
        for pr in range(n_pairs):
            o_ref[0, c, pr] = pt[pr * PAIR:(pr + 1) * PAIR, c * blk:(c + 1) * blk].astype(o_ref.dtype)


def _out_ffn_kernel(x1_ref, ya_ref, yb_ref, gates_ref, wa_hbm, wb_hbm, wo_hbm,
                    g2_ref, w_in_hbm, w_out_hbm, gf_ref, o_ref,
                    wa_ref, wb_ref, wo_ref, w_in_ref, w_out_ref, *, final_norm):
    _load_weights_once([(wa_hbm, wa_ref), (wb_hbm, wb_ref), (wo_hbm, wo_ref),
                        (w_in_hbm, w_in_ref), (w_out_hbm, w_out_ref)])
    d = x1_ref.shape[1]
    merged = (gates_ref[:, :d] * _dot(ya_ref[...], wa_ref[...])
              + gates_ref[:, d:] * _dot(yb_ref[...], wb_ref[...]))
    x2 = x1_ref[...] + _dot(merged.astype(jnp.bfloat16), wo_ref[...])
    h = _rms(x2, g2_ref[...]).astype(jnp.bfloat16)
    x3 = x2 + 0.5 * _swiglu(h, w_in_ref, w_out_ref)
    o_ref[...] = _rms(x3, gf_ref[...]) if final_norm else x3


def _resident(shape):
    return pl.BlockSpec(shape, lambda *_: (0,) * len(shape), pipeline_mode=pl.Buffered(1))


_IN_HBM = pl.BlockSpec(memory_space=pl.ANY)


def _bf16_scratch(*weights):
    return [pltpu.VMEM(w.shape, jnp.bfloat16) for w in weights]


def _rows(tm, width):
    return pl.BlockSpec((tm, width), lambda i: (i, 0))


def _ffn_mix(x, g1, w_in, w_out, gm, w_mix, b_gate, width, seq):
    m, d = x.shape
    tm = ROW_TILE
    bf = jnp.bfloat16
    n_pairs = width // PAIR
    steps_per_seq = seq // tm

    def transposed(blk):
        shape = (m // seq, seq // blk, n_pairs, PAIR, blk)
        spec = pl.BlockSpec((1, tm // blk) + shape[2:], lambda i: (i // steps_per_seq, i % steps_per_seq, 0, 0, 0))
        return jax.ShapeDtypeStruct(shape, bf), spec

    keys = jax.ShapeDtypeStruct((m, width), bf), _rows(tm, width)
    outs = [(jax.ShapeDtypeStruct((m, d), jnp.float32), _rows(tm, d)),
            transposed(A_BLK), keys, transposed(A_BLK), transposed(B_BLK), keys, transposed(B_BLK),
            (jax.ShapeDtypeStruct((m, b_gate.shape[1]), jnp.float32), _rows(tm, b_gate.shape[1]))]
    return pl.pallas_call(
        _ffn_mix_kernel,
        grid=(m // tm,),
        in_specs=[_rows(tm, d), _resident(g1.shape), _IN_HBM, _IN_HBM,
                  _resident(gm.shape), _IN_HBM, _resident(b_gate.shape)],
        out_specs=[spec for _, spec in outs],
        out_shape=[shape for shape, _ in outs],
        scratch_shapes=_bf16_scratch(w_in, w_out, w_mix),
        compiler_params=pltpu.CompilerParams(dimension_semantics=("arbitrary",),
                                             vmem_limit_bytes=V7X_VMEM_LIMIT),
        name="ffn_mix",
    )(x, g1, w_in, w_out, gm, w_mix, b_gate)


def _out_ffn(x1, ya, yb, gates, wa, wb, wo, g2, w_in, w_out, gf, final_norm):
    m, d = x1.shape
    tm = ROW_TILE
    return pl.pallas_call(
        functools.partial(_out_ffn_kernel, final_norm=final_norm),
        grid=(m // tm,),
        in_specs=[_rows(tm, d), _rows(tm, ya.shape[1]), _rows(tm, yb.shape[1]), _rows(tm, gates.shape[1]),
                  _IN_HBM, _IN_HBM, _IN_HBM, _resident(g2.shape), _IN_HBM, _IN_HBM, _resident(gf.shape)],
        out_specs=_rows(tm, d),
        out_shape=jax.ShapeDtypeStruct((m, d), jnp.float32),
        scratch_shapes=_bf16_scratch(wa, wb, wo, w_in, w_out),
        compiler_params=pltpu.CompilerParams(dimension_semantics=("arbitrary",),
                                             vmem_limit_bytes=V7X_VMEM_LIMIT),
        name="out_ffn",
    )(x1, ya, yb, gates, wa, wb, wo, g2, w_in, w_out, gf)


def _t5_bucket(rel):
    nb = T5_BUCKETS // 2
    ret = jnp.where(rel > 0, nb, 0)
    n = jnp.abs(rel)
    max_exact = nb // 2
    nf = jnp.maximum(n, 1).astype(jnp.float32)
    large = max_exact + (jnp.log(nf / max_exact) / math.log(T5_MAX_DIST / max_exact)
                         * (nb - max_exact)).astype(jnp.int32)
    large = jnp.minimum(large, nb - 1)
    return ret + jnp.where(n < max_exact, n, large)


def _window_rel(n_tiles, blk):
    nb = blk // LANES
    u = jnp.arange(-(nb - 1), n_tiles * nb, dtype=jnp.int32)[:, None]
    return LANES - jnp.arange(2 * LANES, dtype=jnp.int32)[None, :] - LANES * u


def _bias_tiles_kernel(rel_bias_ref, tiles_ref, *, left_chunks):
    n_pairs, n_tiles, blk, _ = tiles_ref.shape
    nb = blk // LANES
    kj = lax.broadcasted_iota(jnp.int32, (LANES, LANES), 0)
    qi = lax.broadcasted_iota(jnp.int32, (LANES, LANES), 1)
    chunk_diff = kj // CHUNK - qi // CHUNK
    for p in range(n_pairs):
        for half in range(2):
            for w in range(rel_bias_ref.shape[2]):
                u = w - (nb - 1)
                row = jnp.broadcast_to(rel_bias_ref[p, half, w], (LANES, 2 * LANES))
                sub = pltpu.roll(row, 0, 1, stride=1, stride_axis=0)[:, LANES:]
                for t in range(n_tiles):
                    for a in range(nb):
                        c = t * nb + a - u
                        if not 0 <= c < nb:
                            continue
                        diff = chunk_diff - u * SUB_CHUNKS
                        allowed = diff <= 0
                        if left_chunks is not None:
                            allowed = allowed & (diff >= -left_chunks)
                        tiles_ref[p, t, pl.ds(c * LANES, LANES), pl.ds(half * blk + a * LANES, LANES)] = (
                            jnp.where(allowed, sub * LOG2E, NEG))


def _bias_tiles(rel_bias, n_tiles, blk, left_chunks):
    n_pairs = rel_bias.shape[0]
    return pl.pallas_call(
        functools.partial(_bias_tiles_kernel, left_chunks=left_chunks),
        out_shape=jax.ShapeDtypeStruct((n_pairs, n_tiles, blk, 2 * blk), jnp.float32),
        compiler_params=pltpu.CompilerParams(vmem_limit_bytes=V7X_VMEM_LIMIT),
        name="bias_tiles",
    )(rel_bias)


def _sum_rows(x):
    acc = x[:SUBLANES]
    for r in range(1, x.shape[0] // SUBLANES):
        acc = acc + x[r * SUBLANES:(r + 1) * SUBLANES]
    return acc


def _fold_rows(x):
    m = x[:SUBLANES]
    for r in range(1, x.shape[0] // SUBLANES):
        m = jnp.maximum(m, x[r * SUBLANES:(r + 1) * SUBLANES])
    return m


def _attn_kernel(*refs, left_blocks, lambda_init):
    if lambda_init is None:
        qt_ref, k_ref, vt_ref, tiles_ref, o_ref, q2t_ref, s_ref, acc_ref = refs
    else:
        (qt_ref, k_ref, vt_ref, tiles_ref, lq1_ref, lk1_ref, lq2_ref, lk2_ref, subg_ref, o_ref,
         q2t_ref, *s_refs, m_ref, acc_ref) = refs
        lam = (jnp.exp(jnp.sum(lq1_ref[...] * lk1_ref[...], axis=-1, keepdims=True))
               - jnp.exp(jnp.sum(lq2_ref[...] * lk2_ref[...], axis=-1, keepdims=True))
               + lambda_init)
    n_q, n_pairs, _, blk = qt_ref.shape[1:]
    top_half = lax.broadcasted_iota(jnp.int32, (PAIR, blk), 0) < HEAD_DIM

    def cols(p):
        return slice(p * PAIR, (p + 1) * PAIR)

    def key_rows(j, n=1):
        return pl.ds(pl.multiple_of(j * blk, blk), n * blk)

    def weights_update(vt, e):
        return jnp.concatenate([_dot(vt, e.astype(jnp.bfloat16)), _sum_rows(e)], axis=0)

    def stack_queries(c):
        for p in range(n_pairs):
            qt = qt_ref[0, c, p]
            zero = jnp.zeros_like(qt)
            q2t_ref[p, :, :blk] = jnp.where(top_half, qt, zero)
            q2t_ref[p, :, blk:] = jnp.where(top_half, zero, qt)

    def finish(c):
        for p in range(n_pairs):
            o = acc_ref[p, :PAIR] / jnp.sum(acc_ref[p, PAIR:], axis=0, keepdims=True)
            if lambda_init is None:
                y = jnp.where(top_half, o[:, :blk], o[:, blk:]).T
            else:
                d = o[:, :blk] - lam * o[:, blk:]
                d = d * lax.rsqrt(jnp.mean(d * d, axis=0, keepdims=True) + EPS)
                y = d.T * subg_ref[...] * (1.0 - lambda_init)
            o_ref[0, c * blk:(c + 1) * blk, cols(p)] = y.astype(o_ref.dtype)

    if left_blocks is None:
        n_tiles = tiles_ref.shape[1]
        ring = len(s_refs)

        def causal_prefix(c):
            i = pl.program_id(1) * n_q + c
            stack_queries(c)
            n_blocks = i + 1
            acc_ref[...] = jnp.zeros(acc_ref.shape, jnp.float32)
            m_ref[...] = jnp.full(m_ref.shape, NEG, jnp.float32)

            def prefetch_scores(j, buf_ref, p):
                tile = jnp.minimum(i - j, n_tiles - 1)
                buf_ref[p] = _dot(k_ref[0, key_rows(j), cols(p)], q2t_ref[p]) + tiles_ref[p, tile]

            def online(j, buf_ref, p):
                s = buf_ref[p]
                m_old = m_ref[p]
                m = jnp.maximum(m_old, jnp.max(_fold_rows(s), axis=0, keepdims=True))
                acc_ref[p] = jnp.exp2(m_old - m) * acc_ref[p] + weights_update(vt_ref[0, j, p], jnp.exp2(s - m))
                m_ref[p] = m

            def step(j, cur_ref, nxt_ref=None):
                if nxt_ref is not None:
                    for p in range(n_pairs):
                        prefetch_scores(jnp.minimum(j + 1, i), nxt_ref, p)
                for p in range(n_pairs):
                    online(j, cur_ref, p)

            def ring_trip(t, carry):
                for r in range(ring):
                    step(ring * t + r, s_refs[r], s_refs[(r + 1) % ring])
                return carry

            for p in range(n_pairs):
                prefetch_scores(0, s_refs[0], p)
            lax.fori_loop(0, n_blocks // ring, ring_trip, 0)
            first_left = n_blocks - n_blocks % ring
            for left in range(1, ring):
                @pl.when(n_blocks % ring == left)
                def _(left=left):
                    for r in range(left):
                        step(first_left + r, s_refs[r], s_refs[r + 1] if r + 1 < left else None)
            finish(c)

        for c in range(n_q):
            causal_prefix(c)
    else:
        def band(i, n_left):
            order = range(n_left, -1, -1)
            ms = [jnp.full((SUBLANES, 2 * blk), NEG, jnp.float32) for _ in range(n_pairs)]
            for d in order:
                for p in range(n_pairs):
                    s = _dot(k_ref[0, key_rows(i - d), cols(p)], q2t_ref[p]) + tiles_ref[p, d]
                    s_ref[d, p] = s
                    ms[p] = jnp.maximum(ms[p], _fold_rows(s))
            ms = [jnp.max(m, axis=0, keepdims=True) for m in ms]
            for d in order:
                for p in range(n_pairs):
                    update = weights_update(vt_ref[0, i - d, p], jnp.exp2(s_ref[d, p] - ms[p]))
                    acc_ref[p] = update if d == n_left else acc_ref[p] + update

        for c in range(n_q):
            i = pl.program_id(1) * n_q + c
            stack_queries(c)
            for n_left in range(c, left_blocks, n_q):
                pl.when(i == n_left)(functools.partial(band, i, n_left))
            pl.when(i >= left_blocks)(functools.partial(band, i, left_blocks))
            finish(c)


def _attention(qt, k, vt, tiles, left_blocks, lambda_init=None, extra=()):
    b, s, w = k.shape
    n_pairs, _, blk = qt.shape[2:]
    n_q = B_STEP_BLOCKS if left_blocks is None else A_STEP_BLOCKS
    blk_q = pl.BlockSpec((1, n_q) + qt.shape[2:], lambda bi, i: (bi, i, 0, 0, 0))
    whole_k = pl.BlockSpec((1, s, w), lambda bi, i: (bi, 0, 0))
    whole_v = pl.BlockSpec((1,) + vt.shape[1:], lambda bi, i: (bi, 0, 0, 0, 0))
    f32 = jnp.float32
    score_block = (n_pairs, blk, 2 * blk)
    if left_blocks is None:
        score_scratch = [pltpu.VMEM(score_block, f32)] * B_SCORE_RING + [pltpu.VMEM((n_pairs, 1, 2 * blk), f32)]
    else:
        score_scratch = [pltpu.VMEM((left_blocks + 1,) + score_block, f32)]
    return pl.pallas_call(
        functools.partial(_attn_kernel, left_blocks=left_blocks, lambda_init=lambda_init),
        grid=(b, s // (n_q * blk)),
        in_specs=[blk_q, whole_k, whole_v, _resident(tiles.shape)] + [_resident(e.shape) for e in extra],
        out_specs=pl.BlockSpec((1, n_q * blk, w), lambda bi, i: (bi, i, 0)),
        out_shape=jax.ShapeDtypeStruct((b, s, w), jnp.bfloat16),
        scratch_shapes=[pltpu.VMEM((n_pairs, PAIR, 2 * blk), jnp.bfloat16), *score_scratch,
                        pltpu.VMEM((n_pairs, ACC_ROWS, 2 * blk), f32)],
        compiler_params=pltpu.CompilerParams(dimension_semantics=("arbitrary", "arbitrary"),
                                             vmem_limit_bytes=V7X_VMEM_LIMIT),
        name="attn_a" if lambda_init is None else "attn_b",
    )(qt, k, vt, tiles, *extra)


def kernel(x, ffn1_norm, ffn1_w_in, ffn1_w_out, mix_norm, w_mix_in, b_gate, rel_bias_a, lambda_q1, lambda_k1, lambda_q2, lambda_k2, subln_g, t5_bias, w_branch_a, w_branch_b, w_o, ffn2_norm, ffn2_w_in, ffn2_w_out, final_norm):
    b, s, d = x.shape
    depth = ffn1_norm.shape[0]
    assert depth >= 1
    width = w_branch_a.shape[1]
    n_pairs = width // PAIR
    f32 = jnp.float32

    t5_rel = t5_bias.astype(f32)[:, _t5_bucket(_window_rel(B_TILES, B_BLK))]
    t5_rel = jnp.broadcast_to(t5_rel[:, None, :, None, :], (n_pairs, 2) + t5_rel.shape[1:2] + (1, 2 * LANES))
    tiles_b = _bias_tiles(t5_rel, B_TILES, B_BLK, None)

    xf = x.astype(f32).reshape(b * s, d)
    for li in range(depth):
        x1, qa, ka, va, qb, kb, vb, gates = _ffn_mix(
            xf, ffn1_norm[li][None].astype(f32), ffn1_w_in[li].astype(f32), ffn1_w_out[li].astype(f32),
            mix_norm[li][None].astype(f32), w_mix_in[li].astype(f32), b_gate[li][None].astype(f32), width, s)

        rel_a = jnp.clip(_window_rel(A_TILES, A_BLK), -REL_CLIP, REL_CLIP) + REL_CLIP
        a_rel = rel_bias_a[li].astype(f32)[:, rel_a]
        tiles_a = _bias_tiles(a_rel.reshape(n_pairs, 2, rel_a.shape[0], 1, 2 * LANES), A_TILES, A_BLK, LEFT_CHUNKS)

        def seq(t):
            return t.reshape(b, s, width)

        ya = _attention(qa, seq(ka), va, tiles_a, A_TILES - 1)
        lambda_init = 0.8 - 0.6 * math.exp(-0.3 * li)
        lams = [t[li][None].astype(f32) for t in (lambda_q1, lambda_k1, lambda_q2, lambda_k2)]
        yb = _attention(qb, seq(kb), vb, tiles_b, None, lambda_init=lambda_init,
                        extra=(*lams, subln_g[li][None].astype(f32)))

        xf = _out_ffn(x1, ya.reshape(b * s, width), yb.reshape(b * s, width), gates,
                      w_branch_a[li].astype(f32), w_branch_b[li].astype(f32), w_o[li].astype(f32),
                      ffn2_norm[li][None].astype(f32), ffn2_w_in[li].astype(f32), ffn2_w_out[li].astype(f32),
                      final_norm[None].astype(f32), final_norm=(li == depth - 1))
    return xf.reshape(b, s, d).astype(x.dtype)
```

```python
import functools
import math

import jax
import jax.numpy as jnp
from jax import lax
from jax.experimental import pallas as pl
from jax.experimental.pallas import tpu as pltpu

EPS = 1e-6
NEG = -1e30
LOG2E = math.log2(math.e)

CHUNK = 64
LEFT_CHUNKS = 8
REL_CLIP = 128
T5_BUCKETS = 32
T5_MAX_DIST = 128

LANES = 128
SUBLANES = 8
HEAD_DIM = 64
PAIR = 2 * HEAD_DIM
SUB_CHUNKS = LANES // CHUNK
ACC_ROWS = PAIR + SUBLANES

A_BLK = 128
A_TILES = LEFT_CHUNKS * CHUNK // A_BLK + 1
B_BLK = 256
B_TILES = 3
B_SCORE_RING = 4
A_STEP_BLOCKS = 4
B_STEP_BLOCKS = 2

V7X_VMEM_LIMIT = 60000 * 1024

ROW_TILE = 512
BF16_ROWS = 16
CAST_CHUNK_BYTES = 3 << 18
CAST_SLOTS = 4


def _rms(xf, g):
    return xf * lax.rsqrt(jnp.mean(xf * xf, axis=-1, keepdims=True) + EPS) * g


def _dot(a, b):
    return jnp.dot(a, b, preferred_element_type=jnp.float32)


def _norm_operand(xf, g):
    r = lax.rsqrt(jnp.mean(xf * xf, axis=-1, keepdims=True) + EPS)
    return (xf * g).astype(jnp.bfloat16), r


def _swiglu(x, g, w_in_ref, w_out_ref):
    d_ff = w_out_ref.shape[0]
    h, r = _norm_operand(x, g)
    gate = r * _dot(h, w_in_ref[:, :d_ff])
    up = r * _dot(h, w_in_ref[:, d_ff:])
    act = (gate * jax.nn.sigmoid(gate) * up).astype(jnp.bfloat16)
    return _dot(act, w_out_ref[...])


def _cast_chunk_rows(rows, width):
    fits = [r for r in range(BF16_ROWS, rows + 1, BF16_ROWS)
            if rows % r == 0 and r * width * 4 <= CAST_CHUNK_BYTES]
    return max(fits)


def _load_cast(src_hbm, dst_ref):
    rows, width = src_hbm.shape
    chunk = _cast_chunk_rows(rows, width)
    n_chunks = rows // chunk

    ahead = min(CAST_SLOTS, n_chunks) - 1

    def body(stage_ref, sem):
        def copy(c):
            slot = c % CAST_SLOTS
            return pltpu.make_async_copy(src_hbm.at[pl.ds(c * chunk, chunk)], stage_ref.at[slot], sem.at[slot])

        for c in range(ahead):
            copy(c).start()

        def one_chunk(c, carry):
            @pl.when(c + ahead < n_chunks)
            def _():
                copy(c + ahead).start()

            copy(c).wait()
            dst_ref[pl.ds(pl.multiple_of(c * chunk, chunk), chunk)] = (
                stage_ref[c % CAST_SLOTS].astype(dst_ref.dtype))
            return carry

        lax.fori_loop(0, n_chunks, one_chunk, 0)

    pl.run_scoped(body, pltpu.VMEM((CAST_SLOTS, chunk, width), src_hbm.dtype),
                  pltpu.SemaphoreType.DMA((CAST_SLOTS,)))


def _load_weights_once(pairs):
    @pl.when(pl.program_id(0) == 0)
    def _():
        for src_hbm, dst_ref in pairs:
            _load_cast(src_hbm, dst_ref)


def _ffn_mix_kernel(x_ref, g1_ref, w_in_hbm, w_out_hbm, gm_ref, w_mix_hbm, b_gate_ref,
                    x1_ref, qa_ref, ka_ref, va_ref, qb_ref, kb_ref, vb_ref, gates_ref,
                    w_in_ref, w_out_ref, w_mix_ref):
    _load_weights_once([(w_in_hbm, w_in_ref), (w_out_hbm, w_out_ref), (w_mix_hbm, w_mix_ref)])
    x = x_ref[...]
    x1 = x + 0.5 * _swiglu(x, g1_ref[...], w_in_ref, w_out_ref)
    x1_ref[...] = x1

    u, r = _norm_operand(x1, gm_ref[...])
    width = ka_ref.shape[1]
    r_scaled = r * (HEAD_DIM ** -0.5 * LOG2E)
    outs = (qa_ref, ka_ref, va_ref, qb_ref, kb_ref, vb_ref)
    for n, o_ref in enumerate(outs):
        p = _dot(u, w_mix_ref[:, n * width:(n + 1) * width])
        if o_ref is ka_ref or o_ref is kb_ref:
            o_ref[...] = (r * p).astype(o_ref.dtype)
        elif o_ref is qa_ref or o_ref is qb_ref:
            _store_transposed(r_scaled * p, o_ref)
        else:
            _store_transposed(r * p, o_ref)
    logits = r * _dot(u, w_mix_ref[:, len(outs) * width:]) + b_gate_ref[...]
    gates_ref[...] = jax.nn.sigmoid(logits)


def _store_transposed(p, o_ref):
    n_blk, n_pairs, _, blk = o_ref.shape[1:]
    pt = p.T
    for c in range(n_blk):
        for pr in range(n_pairs):
            o_ref[0, c, pr] = pt[pr * PAIR:(pr + 1) * PAIR, c * blk:(c + 1) * blk].astype(o_ref.dtype)


def _out_ffn_kernel(x1_ref, ya_ref, yb_ref, gates_ref, wa_hbm, wb_hbm, wo_hbm,
                    g2_ref, w_in_hbm, w_out_hbm, gf_ref, o_ref,
                    wa_ref, wb_ref, wo_ref, w_in_ref, w_out_ref, *, final_norm):
    _load_weights_once([(wa_hbm, wa_ref), (wb_hbm, wb_ref), (wo_hbm, wo_ref),
                        (w_in_hbm, w_in_ref), (w_out_hbm, w_out_ref)])
    d = x1_ref.shape[1]
    merged = (gates_ref[:, :d] * _dot(ya_ref[...], wa_ref[...])
              + gates_ref[:, d:] * _dot(yb_ref[...], wb_ref[...]))
    x2 = x1_ref[...] + _dot(merged.astype(jnp.bfloat16), wo_ref[...])
    x3 = x2 + 0.5 * _swiglu(x2, g2_ref[...], w_in_ref, w_out_ref)
    o_ref[...] = _rms(x3, gf_ref[...]) if final_norm else x3


def _resident(shape):
    return pl.BlockSpec(shape, lambda *_: (0,) * len(shape), pipeline_mode=pl.Buffered(1))


_IN_HBM = pl.BlockSpec(memory_space=pl.ANY)


def _bf16_scratch(*weights):
    return [pltpu.VMEM(w.shape, jnp.bfloat16) for w in weights]


def _rows(tm, width):
    return pl.BlockSpec((tm, width), lambda i: (i, 0))


def _ffn_mix(x, g1, w_in, w_out, gm, w_mix, b_gate, width, seq):
    m, d = x.shape
    tm = ROW_TILE
    bf = jnp.bfloat16
    n_pairs = width // PAIR
    steps_per_seq = seq // tm

    def transposed(blk):
        shape = (m // seq, seq // blk, n_pairs, PAIR, blk)
        spec = pl.BlockSpec((1, tm // blk) + shape[2:], lambda i: (i // steps_per_seq, i % steps_per_seq, 0, 0, 0))
        return jax.ShapeDtypeStruct(shape, bf), spec

    keys = jax.ShapeDtypeStruct((m, width), bf), _rows(tm, width)
    outs = [(jax.ShapeDtypeStruct((m, d), jnp.float32), _rows(tm, d)),
            transposed(A_BLK), keys, transposed(A_BLK), transposed(B_BLK), keys, transposed(B_BLK),
            (jax.ShapeDtypeStruct((m, b_gate.shape[1]), jnp.float32), _rows(tm, b_gate.shape[1]))]
    return pl.pallas_call(
        _ffn_mix_kernel,
        grid=(m // tm,),
        in_specs=[_rows(tm, d), _resident(g1.shape), _IN_HBM, _IN_HBM,
                  _resident(gm.shape), _IN_HBM, _resident(b_gate.shape)],
        out_specs=[spec for _, spec in outs],
        out_shape=[shape for shape, _ in outs],
        scratch_shapes=_bf16_scratch(w_in, w_out, w_mix),
        compiler_params=pltpu.CompilerParams(dimension_semantics=("arbitrary",),
                                             vmem_limit_bytes=V7X_VMEM_LIMIT),
        name="ffn_mix",
    )(x, g1, w_in, w_out, gm, w_mix, b_gate)


def _out_ffn(x1, ya, yb, gates, wa, wb, wo, g2, w_in, w_out, gf, final_norm):
    m, d = x1.shape
    tm = ROW_TILE
    return pl.pallas_call(
        functools.partial(_out_ffn_kernel, final_norm=final_norm),
        grid=(m // tm,),
        in_specs=[_rows(tm, d), _rows(tm, ya.shape[1]), _rows(tm, yb.shape[1]), _rows(tm, gates.shape[1]),
                  _IN_HBM, _IN_HBM, _IN_HBM, _resident(g2.shape), _IN_HBM, _IN_HBM, _resident(gf.shape)],
        out_specs=_rows(tm, d),
        out_shape=jax.ShapeDtypeStruct((m, d), jnp.float32),
        scratch_shapes=_bf16_scratch(wa, wb, wo, w_in, w_out),
        compiler_params=pltpu.CompilerParams(dimension_semantics=("arbitrary",),
                                             vmem_limit_bytes=V7X_VMEM_LIMIT),
        name="out_ffn",
    )(x1, ya, yb, gates, wa, wb, wo, g2, w_in, w_out, gf)


def _t5_bucket(rel):
    nb = T5_BUCKETS // 2
    ret = jnp.where(rel > 0, nb, 0)
    n = jnp.abs(rel)
    max_exact = nb // 2
    nf = jnp.maximum(n, 1).astype(jnp.float32)
    large = max_exact + (jnp.log(nf / max_exact) / math.log(T5_MAX_DIST / max_exact)
                         * (nb - max_exact)).astype(jnp.int32)
    large = jnp.minimum(large, nb - 1)
    return ret + jnp.where(n < max_exact, n, large)


def _window_rel(n_tiles, blk):
    nb = blk // LANES
    u = jnp.arange(-(nb - 1), n_tiles * nb, dtype=jnp.int32)[:, None]
    return LANES - jnp.arange(2 * LANES, dtype=jnp.int32)[None, :] - LANES * u


def _bias_tiles_kernel(rel_bias_ref, tiles_ref, *, left_chunks):
    n_pairs, n_tiles, blk, _ = tiles_ref.shape
    nb = blk // LANES
    kj = lax.broadcasted_iota(jnp.int32, (LANES, LANES), 0)
    qi = lax.broadcasted_iota(jnp.int32, (LANES, LANES), 1)
    chunk_diff = kj // CHUNK - qi // CHUNK
    for p in range(n_pairs):
        for half in range(2):
            for w in range(rel_bias_ref.shape[2]):
                u = w - (nb - 1)
                row = jnp.broadcast_to(rel_bias_ref[p, half, w], (LANES, 2 * LANES))
                sub = pltpu.roll(row, 0, 1, stride=1, stride_axis=0)[:, LANES:]
                for t in range(n_tiles):
                    for a in range(nb):
                        c = t * nb + a - u
                        if not 0 <= c < nb:
                            continue
                        diff = chunk_diff - u * SUB_CHUNKS
                        allowed = diff <= 0
                        if left_chunks is not None:
                            allowed = allowed & (diff >= -left_chunks)
                        tiles_ref[p, t, pl.ds(c * LANES, LANES), pl.ds(half * blk + a * LANES, LANES)] = (
                            jnp.where(allowed, sub * LOG2E, NEG))


def _bias_tiles(rel_bias, n_tiles, blk, left_chunks):
    n_pairs = rel_bias.shape[0]
    return pl.pallas_call(
        functools.partial(_bias_tiles_kernel, left_chunks=left_chunks),
        out_shape=jax.ShapeDtypeStruct((n_pairs, n_tiles, blk, 2 * blk), jnp.float32),
        compiler_params=pltpu.CompilerParams(vmem_limit_bytes=V7X_VMEM_LIMIT),
        name="bias_tiles",
    )(rel_bias)


def _sum_rows(x):
    acc = x[:SUBLANES]
    for r in range(1, x.shape[0] // SUBLANES):
        acc = acc + x[r * SUBLANES:(r + 1) * SUBLANES]
    return acc


def _fold_rows(x):
    m = x[:SUBLANES]
    for r in range(1, x.shape[0] // SUBLANES):
        m = jnp.maximum(m, x[r * SUBLANES:(r + 1) * SUBLANES])
    return m


def _attn_kernel(*refs, left_blocks, lambda_init):
    if lambda_init is None:
        qt_ref, k_ref, vt_ref, tiles_ref, o_ref, q2t_ref, s_ref, acc_ref = refs
    else:
        (qt_ref, k_ref, vt_ref, tiles_ref, lq1_ref, lk1_ref, lq2_ref, lk2_ref, subg_ref, o_ref,
         q2t_ref, *s_refs, m_ref, acc_ref) = refs
        lam = (jnp.exp(jnp.sum(lq1_ref[...] * lk1_ref[...], axis=-1, keepdims=True))
               - jnp.exp(jnp.sum(lq2_ref[...] * lk2_ref[...], axis=-1, keepdims=True))
               + lambda_init)
    n_q, n_pairs, _, blk = qt_ref.shape[1:]
    top_half = lax.broadcasted_iota(jnp.int32, (PAIR, blk), 0) < HEAD_DIM

    def cols(p):
        return slice(p * PAIR, (p + 1) * PAIR)

    def key_rows(j, n=1):
        return pl.ds(pl.multiple_of(j * blk, blk), n * blk)

    def weights_update(vt, e):
        return jnp.concatenate([_dot(vt, e.astype(jnp.bfloat16)), _sum_rows(e)], axis=0)

    def stack_queries(c):
        for p in range(n_pairs):
            qt = qt_ref[0, c, p]
            zero = jnp.zeros_like(qt)
            q2t_ref[p, :, :blk] = jnp.where(top_half, qt, zero)
            q2t_ref[p, :, blk:] = jnp.where(top_half, zero, qt)

    def finish(c):
        for p in range(n_pairs):
            o = acc_ref[p, :PAIR] / jnp.sum(acc_ref[p, PAIR:], axis=0, keepdims=True)
            if lambda_init is None:
                y = jnp.where(top_half, o[:, :blk], o[:, blk:]).T
            else:
                d = o[:, :blk] - lam * o[:, blk:]
                d = d * lax.rsqrt(jnp.mean(d * d, axis=0, keepdims=True) + EPS)
                y = d.T * subg_ref[...] * (1.0 - lambda_init)
            o_ref[0, c * blk:(c + 1) * blk, cols(p)] = y.astype(o_ref.dtype)

    if left_blocks is None:
        n_tiles = tiles_ref.shape[1]
        ring = len(s_refs)

        def causal_prefix(c):
            i = pl.program_id(1) * n_q + c
            stack_queries(c)
            n_blocks = i + 1
            acc_ref[...] = jnp.zeros(acc_ref.shape, jnp.float32)
            m_ref[...] = jnp.full(m_ref.shape, NEG, jnp.float32)

            def prefetch_scores(j, buf_ref, p):
                tile = jnp.minimum(i - j, n_tiles - 1)
                buf_ref[p] = _dot(k_ref[0, key_rows(j), cols(p)], q2t_ref[p]) + tiles_ref[p, tile]

            def online(j, buf_ref, p):
                s = buf_ref[p]
                m_old = m_ref[p]
                m = jnp.maximum(m_old, jnp.max(_fold_rows(s), axis=0, keepdims=True))
                acc_ref[p] = jnp.exp2(m_old - m) * acc_ref[p] + weights_update(vt_ref[0, j, p], jnp.exp2(s - m))
                m_ref[p] = m

            def step(j, cur_ref, nxt_ref=None):
                if nxt_ref is not None:
                    for p in range(n_pairs):
                        prefetch_scores(jnp.minimum(j + 1, i), nxt_ref, p)
                for p in range(n_pairs):
                    online(j, cur_ref, p)

            def ring_trip(t, carry):
                for r in range(ring):
                    step(ring * t + r, s_refs[r], s_refs[(r + 1) % ring])
                return carry

            for p in range(n_pairs):
                prefetch_scores(0, s_refs[0], p)
            lax.fori_loop(0, n_blocks // ring, ring_trip, 0)
            first_left = n_blocks - n_blocks % ring
            for left in range(1, ring):
                @pl.when(n_blocks % ring == left)
                def _(left=left):
                    for r in range(left):
                        step(first_left + r, s_refs[r], s_refs[r + 1] if r + 1 < left else None)
            finish(c)

        for c in range(n_q):
            causal_prefix(c)
    else:
        def band(i, n_left):
            order = range(n_left, -1, -1)
            ms = [jnp.full((SUBLANES, 2 * blk), NEG, jnp.float32) for _ in range(n_pairs)]
            for d in order:
                for p in range(n_pairs):
                    s = _dot(k_ref[0, key_rows(i - d), cols(p)], q2t_ref[p]) + tiles_ref[p, d]
                    s_ref[d, p] = s
                    ms[p] = jnp.maximum(ms[p], _fold_rows(s))
            ms = [jnp.max(m, axis=0, keepdims=True) for m in ms]
            for d in order:
                for p in range(n_pairs):
                    update = weights_update(vt_ref[0, i - d, p], jnp.exp2(s_ref[d, p] - ms[p]))
                    acc_ref[p] = update if d == n_left else acc_ref[p] + update

        for c in range(n_q):
            i = pl.program_id(1) * n_q + c
            stack_queries(c)
            for n_left in range(c, left_blocks, n_q):
                pl.when(i == n_left)(functools.partial(band, i, n_left))
            pl.when(i >= left_blocks)(functools.partial(band, i, left_blocks))
            finish(c)


def _attention(qt, k, vt, tiles, left_blocks, lambda_init=None, extra=()):
    b, s, w = k.shape
    n_pairs, _, blk = qt.shape[2:]
    n_q = B_STEP_BLOCKS if left_blocks is None else A_STEP_BLOCKS
    blk_q = pl.BlockSpec((1, n_q) + qt.shape[2:], lambda bi, i: (bi, i, 0, 0, 0))
    whole_k = pl.BlockSpec((1, s, w), lambda bi, i: (bi, 0, 0))
    whole_v = pl.BlockSpec((1,) + vt.shape[1:], lambda bi, i: (bi, 0, 0, 0, 0))
    f32 = jnp.float32
    score_block = (n_pairs, blk, 2 * blk)
    if left_blocks is None:
        score_scratch = [pltpu.VMEM(score_block, f32)] * B_SCORE_RING + [pltpu.VMEM((n_pairs, 1, 2 * blk), f32)]
    else:
        score_scratch = [pltpu.VMEM((left_blocks + 1,) + score_block, f32)]
    return pl.pallas_call(
        functools.partial(_attn_kernel, left_blocks=left_blocks, lambda_init=lambda_init),
        grid=(b, s // (n_q * blk)),
        in_specs=[blk_q, whole_k, whole_v, _resident(tiles.shape)] + [_resident(e.shape) for e in extra],
        out_specs=pl.BlockSpec((1, n_q * blk, w), lambda bi, i: (bi, i, 0)),
        out_shape=jax.ShapeDtypeStruct((b, s, w), jnp.bfloat16),
        scratch_shapes=[pltpu.VMEM((n_pairs, PAIR, 2 * blk), jnp.bfloat16), *score_scratch,
                        pltpu.VMEM((n_pairs, ACC_ROWS, 2 * blk), f32)],
        compiler_params=pltpu.CompilerParams(dimension_semantics=("arbitrary", "arbitrary"),
                                             vmem_limit_bytes=V7X_VMEM_LIMIT),
        name="attn_a" if lambda_init is None else "attn_b",
    )(qt, k, vt, tiles, *extra)


def kernel(x, ffn1_norm, ffn1_w_in, ffn1_w_out, mix_norm, w_mix_in, b_gate, rel_bias_a, lambda_q1, lambda_k1, lambda_q2, lambda_k2, subln_g, t5_bias, w_branch_a, w_branch_b, w_o, ffn2_norm, ffn2_w_in, ffn2_w_out, final_norm):
    b, s, d = x.shape
    depth = ffn1_norm.shape[0]
    assert depth >= 1
    width = w_branch_a.shape[1]
    n_pairs = width // PAIR
    f32 = jnp.float32

    t5_rel = t5_bias.astype(f32)[:, _t5_bucket(_window_rel(B_TILES, B_BLK))]
    t5_rel = jnp.broadcast_to(t5_rel[:, None, :, None, :], (n_pairs, 2) + t5_rel.shape[1:2] + (1, 2 * LANES))
    tiles_b = _bias_tiles(t5_rel, B_TILES, B_BLK, None)

    xf = x.astype(f32).reshape(b * s, d)
    for li in range(depth):
        x1, qa, ka, va, qb, kb, vb, gates = _ffn_mix(
            xf, ffn1_norm[li][None].astype(f32), ffn1_w_in[li].astype(f32), ffn1_w_out[li].astype(f32),
            mix_norm[li][None].astype(f32), w_mix_in[li].astype(f32), b_gate[li][None].astype(f32), width, s)

        rel_a = jnp.clip(_window_rel(A_TILES, A_BLK), -REL_CLIP, REL_CLIP) + REL_CLIP
        a_rel = rel_bias_a[li].astype(f32)[:, rel_a]
        tiles_a = _bias_tiles(a_rel.reshape(n_pairs, 2, rel_a.shape[0], 1, 2 * LANES), A_TILES, A_BLK, LEFT_CHUNKS)

        def seq(t):
            return t.reshape(b, s, width)

        ya = _attention(qa, seq(ka), va, tiles_a, A_TILES - 1)
        lambda_init = 0.8 - 0.6 * math.exp(-0.3 * li)
        lams = [t[li][None].astype(f32) for t in (lambda_q1, lambda_k1, lambda_q2, lambda_k2)]
        yb = _attention(qb, seq(kb), vb, tiles_b, None, lambda_init=lambda_init,
                        extra=(*lams, subln_g[li][None].astype(f32)))

        xf = _out_ffn(x1, ya.reshape(b * s, width), yb.reshape(b * s, width), gates,
                      w_branch_a[li].astype(f32), w_branch_b[li].astype(f32), w_o[li].astype(f32),
                      ffn2_norm[li][None].astype(f32), ffn2_w_in[li].astype(f32), ffn2_w_out[li].astype(f32),
                      final_norm[None].astype(f32), final_norm=(li == depth - 1))
    return xf.reshape(b, s, d).astype(x.dtype)
```

```python
import functools
import math

import jax
import jax.numpy as jnp
from jax import lax
from jax.experimental import pallas as pl
from jax.experimental.pallas import tpu as pltpu

EPS = 1e-6
NEG = -1e30
LOG2E = math.log2(math.e)

CHUNK = 64
LEFT_CHUNKS = 8
REL_CLIP = 128
T5_BUCKETS = 32
T5_MAX_DIST = 128

LANES = 128
SUBLANES = 8
HEAD_DIM = 64
PAIR = 2 * HEAD_DIM
SUB_CHUNKS = LANES // CHUNK
ACC_ROWS = PAIR + SUBLANES

A_BLK = 128
A_TILES = LEFT_CHUNKS * CHUNK // A_BLK + 1
B_BLK = 256
B_TILES = 3
B_SCORE_RING = 4
A_STEP_BLOCKS = 4
B_STEP_BLOCKS = 2

V7X_VMEM_LIMIT = 60000 * 1024

ROW_TILE = 512
BF16_ROWS = 16
CAST_CHUNK_BYTES = 3 << 18
CAST_SLOTS = 4


def _rms(xf, g):
    return xf * lax.rsqrt(jnp.mean(xf * xf, axis=-1, keepdims=True) + EPS) * g


def _dot(a, b):
    return jnp.dot(a, b, preferred_element_type=jnp.float32)


def _swiglu(h, w_in_ref, w_out_ref):
    d_ff = w_out_ref.shape[0]
    gate = _dot(h, w_in_ref[:, :d_ff])
    up = _dot(h, w_in_ref[:, d_ff:])
    act = (gate * jax.nn.sigmoid(gate) * up).astype(jnp.bfloat16)
    return _dot(act, w_out_ref[...])


def _cast_chunk_rows(rows, width):
    fits = [r for r in range(BF16_ROWS, rows + 1, BF16_ROWS)
            if rows % r == 0 and r * width * 4 <= CAST_CHUNK_BYTES]
    return max(fits)


def _load_cast(src_hbm, dst_ref):
    rows, width = src_hbm.shape
    chunk = _cast_chunk_rows(rows, width)
    n_chunks = rows // chunk

    ahead = min(CAST_SLOTS, n_chunks) - 1

    def body(stage_ref, sem):
        def copy(c):
            slot = c % CAST_SLOTS
            return pltpu.make_async_copy(src_hbm.at[pl.ds(c * chunk, chunk)], stage_ref.at[slot], sem.at[slot])

        for c in range(ahead):
            copy(c).start()

        def one_chunk(c, carry):
            @pl.when(c + ahead < n_chunks)
            def _():
                copy(c + ahead).start()

            copy(c).wait()
            dst_ref[pl.ds(pl.multiple_of(c * chunk, chunk), chunk)] = (
                stage_ref[c % CAST_SLOTS].astype(dst_ref.dtype))
            return carry

        lax.fori_loop(0, n_chunks, one_chunk, 0)

    pl.run_scoped(body, pltpu.VMEM((CAST_SLOTS, chunk, width), src_hbm.dtype),
                  pltpu.SemaphoreType.DMA((CAST_SLOTS,)))


def _load_weights_once(pairs):
    @pl.when(pl.program_id(0) == 0)
    def _():
        for src_hbm, dst_ref in pairs:
            _load_cast(src_hbm, dst_ref)


def _ffn_mix_kernel(x_ref, g1_ref, w_in_hbm, w_out_hbm, gm_ref, w_mix_hbm, b_gate_ref,
                    x1_ref, qa_ref, ka_ref, va_ref, qb_ref, kb_ref, vb_ref, gates_ref,
                    w_in_ref, w_out_ref, w_mix_ref):
    _load_weights_once([(w_in_hbm, w_in_ref), (w_out_hbm, w_out_ref), (w_mix_hbm, w_mix_ref)])
    x = x_ref[...]
    h = _rms(x, g1_ref[...]).astype(jnp.bfloat16)
    x1 = x + 0.5 * _swiglu(h, w_in_ref, w_out_ref)
    x1_ref[...] = x1

    u = _rms(x1, gm_ref[...]).astype(jnp.bfloat16)
    width = ka_ref.shape[1]
    scale = HEAD_DIM ** -0.5 * LOG2E
    outs = (qa_ref, ka_ref, va_ref, qb_ref, kb_ref, vb_ref)
    for n, o_ref in enumerate(outs):
        p = _dot(u, w_mix_ref[:, n * width:(n + 1) * width])
        if o_ref is ka_ref or o_ref is kb_ref:
            o_ref[...] = p.astype(o_ref.dtype)
        elif o_ref is qa_ref or o_ref is qb_ref:
            _store_transposed(p * scale, o_ref)
        else:
            _store_transposed(p, o_ref)
    logits = _dot(u, w_mix_ref[:, len(outs) * width:]) + b_gate_ref[...]
    gates_ref[...] = jax.nn.sigmoid(logits)


def _store_transposed(p, o_ref):
    n_blk, n_pairs, _, blk = o_ref.shape[1:]
    pt = p.T
    for c in range(n_blk):
        for pr in range(n_pairs):
            o_ref[0, c, pr] = pt[pr * PAIR:(pr + 1) * PAIR, c * blk:(c + 1) * blk].astype(o_ref.dtype)


def _out_ffn_kernel(x1_ref, ya_ref, yb_ref, gates_ref, wa_hbm, wb_hbm, wo_hbm,
                    g2_ref, w_in_hbm, w_out_hbm, gf_ref, o_ref,
                    wa_ref, wb_ref, wo_ref, w_in_ref, w_out_ref, *, final_norm):
    _load_weights_once([(wa_hbm, wa_ref), (wb_hbm, wb_ref), (wo_hbm, wo_ref),
                        (w_in_hbm, w_in_ref), (w_out_hbm, w_out_ref)])
    d = x1_ref.shape[1]
    merged = (gates_ref[:, :d] * _dot(ya_ref[...], wa_ref[...])
              + gates_ref[:, d:] * _dot(yb_ref[...], wb_ref[...]))
    x2 = x1_ref[...] + _dot(merged.astype(jnp.bfloat16), wo_ref[...])
    h = _rms(x2, g2_ref[...]).astype(jnp.bfloat16)
    x3 = x2 + 0.5 * _swiglu(h, w_in_ref, w_out_ref)
    o_ref[...] = _rms(x3, gf_ref[...]) if final_norm else x3


def _resident(shape):
    return pl.BlockSpec(shape, lambda *_: (0,) * len(shape), pipeline_mode=pl.Buffered(1))


_IN_HBM = pl.BlockSpec(memory_space=pl.ANY)


def _bf16_scratch(*weights):
    return [pltpu.VMEM(w.shape, jnp.bfloat16) for w in weights]


def _rows(tm, width):
    return pl.BlockSpec((tm, width), lambda i: (i, 0))


def _ffn_mix(x, g1, w_in, w_out, gm, w_mix, b_gate, width, seq):
    m, d = x.shape
    tm = ROW_TILE
    bf = jnp.bfloat16
    n_pairs = width // PAIR
    steps_per_seq = seq // tm

    def transposed(blk):
        shape = (m // seq, seq // blk, n_pairs, PAIR, blk)
        spec = pl.BlockSpec((1, tm // blk) + shape[2:], lambda i: (i // steps_per_seq, i % steps_per_seq, 0, 0, 0))
        return jax.ShapeDtypeStruct(shape, bf), spec

    keys = jax.ShapeDtypeStruct((m, width), bf), _rows(tm, width)
    outs = [(jax.ShapeDtypeStruct((m, d), jnp.float32), _rows(tm, d)),
            transposed(A_BLK), keys, transposed(A_BLK), transposed(B_BLK), keys, transposed(B_BLK),
            (jax.ShapeDtypeStruct((m, b_gate.shape[1]), jnp.float32), _rows(tm, b_gate.shape[1]))]
    return pl.pallas_call(
        _ffn_mix_kernel,
        grid=(m // tm,),
        in_specs=[_rows(tm, d), _resident(g1.shape), _IN_HBM, _IN_HBM,
                  _resident(gm.shape), _IN_HBM, _resident(b_gate.shape)],
        out_specs=[spec for _, spec in outs],
        out_shape=[shape for shape, _ in outs],
        scratch_shapes=_bf16_scratch(w_in, w_out, w_mix),
        compiler_params=pltpu.CompilerParams(dimension_semantics=("arbitrary",),
                                             vmem_limit_bytes=V7X_VMEM_LIMIT),
        name="ffn_mix",
    )(x, g1, w_in, w_out, gm, w_mix, b_gate)


def _out_ffn(x1, ya, yb, gates, wa, wb, wo, g2, w_in, w_out, gf, final_norm):
    m, d = x1.shape
    tm = ROW_TILE
    return pl.pallas_call(
        functools.partial(_out_ffn_kernel, final_norm=final_norm),
        grid=(m // tm,),
        in_specs=[_rows(tm, d), _rows(tm, ya.shape[1]), _rows(tm, yb.shape[1]), _rows(tm, gates.shape[1]),
                  _IN_HBM, _IN_HBM, _IN_HBM, _resident(g2.shape), _IN_HBM, _IN_HBM, _resident(gf.shape)],
        out_specs=_rows(tm, d),
        out_shape=jax.ShapeDtypeStruct((m, d), jnp.float32),
        scratch_shapes=_bf16_scratch(wa, wb, wo, w_in, w_out),
        compiler_params=pltpu.CompilerParams(dimension_semantics=("arbitrary",),
                                             vmem_limit_bytes=V7X_VMEM_LIMIT),
        name="out_ffn",
    )(x1, ya, yb, gates, wa, wb, wo, g2, w_in, w_out, gf)


def _t5_bucket(rel):
    nb = T5_BUCKETS // 2
    ret = jnp.where(rel > 0, nb, 0)
    n = jnp.abs(rel)
    max_exact = nb // 2
    nf = jnp.maximum(n, 1).astype(jnp.float32)
    large = max_exact + (jnp.log(nf / max_exact) / math.log(T5_MAX_DIST / max_exact)
                         * (nb - max_exact)).astype(jnp.int32)
    large = jnp.minimum(large, nb - 1)
    return ret + jnp.where(n < max_exact, n, large)


def _window_rel(n_tiles, blk):
    nb = blk // LANES
    u = jnp.arange(-(nb - 1), n_tiles * nb, dtype=jnp.int32)[:, None]
    return LANES - jnp.arange(2 * LANES, dtype=jnp.int32)[None, :] - LANES * u


def _bias_tiles_kernel(rel_bias_ref, tiles_ref, *, left_chunks):
    n_pairs, n_tiles, blk, _ = tiles_ref.shape
    nb = blk // LANES
    kj = lax.broadcasted_iota(jnp.int32, (LANES, LANES), 0)
    qi = lax.broadcasted_iota(jnp.int32, (LANES, LANES), 1)
    chunk_diff = kj // CHUNK - qi // CHUNK
    for p in range(n_pairs):
        for half in range(2):
            for w in range(rel_bias_ref.shape[2]):
                u = w - (nb - 1)
                row = jnp.broadcast_to(rel_bias_ref[p, half, w], (LANES, 2 * LANES))
                sub = pltpu.roll(row, 0, 1, stride=1, stride_axis=0)[:, LANES:]
                for t in range(n_tiles):
                    for a in range(nb):
                        c = t * nb + a - u
                        if not 0 <= c < nb:
                            continue
                        diff = chunk_diff - u * SUB_CHUNKS
                        allowed = diff <= 0
                        if left_chunks is not None:
                            allowed = allowed & (diff >= -left_chunks)
                        tiles_ref[p, t, pl.ds(c * LANES, LANES), pl.ds(half * blk + a * LANES, LANES)] = (
                            jnp.where(allowed, sub * LOG2E, NEG))


def _bias_tiles(rel_bias, n_tiles, blk, left_chunks):
    n_pairs = rel_bias.shape[0]
    return pl.pallas_call(
        functools.partial(_bias_tiles_kernel, left_chunks=left_chunks),
        out_shape=jax.ShapeDtypeStruct((n_pairs, n_tiles, blk, 2 * blk), jnp.float32),
        compiler_params=pltpu.CompilerParams(vmem_limit_bytes=V7X_VMEM_LIMIT),
        name="bias_tiles",
    )(rel_bias)


def _sum_rows(x):
    acc = x[:SUBLANES]
    for r in range(1, x.shape[0] // SUBLANES):
        acc = acc + x[r * SUBLANES:(r + 1) * SUBLANES]
    return acc


def _fold_rows(x):
    m = x[:SUBLANES]
    for r in range(1, x.shape[0] // SUBLANES):
        m = jnp.maximum(m, x[r * SUBLANES:(r + 1) * SUBLANES])
    return m


def _attn_kernel(*refs, left_blocks, lambda_init):
    if lambda_init is None:
        qt_ref, k_ref, vt_ref, tiles_ref, o_ref, q2t_ref, s_ref, acc_ref = refs
    else:
        (qt_ref, k_ref, vt_ref, tiles_ref, lq1_ref, lk1_ref, lq2_ref, lk2_ref, subg_ref, o_ref,
         q2t_ref, *s_refs, m_ref, acc_ref) = refs
        lam = (jnp.exp(jnp.sum(lq1_ref[...] * lk1_ref[...], axis=-1, keepdims=True))
               - jnp.exp(jnp.sum(lq2_ref[...] * lk2_ref[...], axis=-1, keepdims=True))
               + lambda_init)
    n_q, n_pairs, _, blk = qt_ref.shape[1:]
    top_half = lax.broadcasted_iota(jnp.int32, (PAIR, blk), 0) < HEAD_DIM

    def cols(p):
        return slice(p * PAIR, (p + 1) * PAIR)

    def key_rows(j, n=1):
        return pl.ds(pl.multiple_of(j * blk, blk), n * blk)

    def weights_update(vt, e):
        return jnp.concatenate([_dot(vt, e.astype(jnp.bfloat16)), _sum_rows(e)], axis=0)

    def stack_queries(c, after=None):
        for p in range(n_pairs):
            qt = qt_ref[0, c, p]
            if after is not None:
                qt = qt + (after * 0.0).astype(qt.dtype)
            zero = jnp.zeros_like(qt)
            q2t_ref[p, :, :blk] = jnp.where(top_half, qt, zero)
            q2t_ref[p, :, blk:] = jnp.where(top_half, zero, qt)

    def finish(c):
        for p in range(n_pairs):
            o = acc_ref[p, :PAIR] / jnp.sum(acc_ref[p, PAIR:], axis=0, keepdims=True)
            if lambda_init is None:
                y = jnp.where(top_half, o[:, :blk], o[:, blk:]).T
            else:
                d = o[:, :blk] - lam * o[:, blk:]
                d = d * lax.rsqrt(jnp.mean(d * d, axis=0, keepdims=True) + EPS)
                y = d.T * subg_ref[...] * (1.0 - lambda_init)
            o_ref[0, c * blk:(c + 1) * blk, cols(p)] = y.astype(o_ref.dtype)

    if left_blocks is None:
        n_tiles = tiles_ref.shape[1]
        ring = len(s_refs)

        def causal_prefix(c, after):
            i = pl.program_id(1) * n_q + c
            stack_queries(c, after)
            n_blocks = i + 1
            acc_ref[...] = jnp.zeros(acc_ref.shape, jnp.float32)
            m_ref[...] = jnp.full(m_ref.shape, NEG, jnp.float32)

            def prefetch_scores(j, buf_ref, p):
                tile = jnp.minimum(i - j, n_tiles - 1)
                buf_ref[p] = _dot(k_ref[0, key_rows(j), cols(p)], q2t_ref[p]) + tiles_ref[p, tile]

            def online(j, buf_ref, p):
                s = buf_ref[p]
                m_old = m_ref[p]
                m = jnp.maximum(m_old, jnp.max(_fold_rows(s), axis=0, keepdims=True))
                acc_ref[p] = jnp.exp2(m_old - m) * acc_ref[p] + weights_update(vt_ref[0, j, p], jnp.exp2(s - m))
                m_ref[p] = m

            def step(j, cur_ref, nxt_ref=None):
                if nxt_ref is not None:
                    for p in range(n_pairs):
                        prefetch_scores(jnp.minimum(j + 1, i), nxt_ref, p)
                for p in range(n_pairs):
                    online(j, cur_ref, p)

            def ring_trip(t, carry):
                for r in range(ring):
                    step(ring * t + r, s_refs[r], s_refs[(r + 1) % ring])
                return carry

            for p in range(n_pairs):
                prefetch_scores(0, s_refs[0], p)
            lax.fori_loop(0, n_blocks // ring, ring_trip, 0)
            first_left = n_blocks - n_blocks % ring
            for left in range(1, ring):
                @pl.when(n_blocks % ring == left)
                def _(left=left):
                    for r in range(left):
                        step(first_left + r, s_refs[r], s_refs[r + 1] if r + 1 < left else None)
            finish(c)
            return m_ref[n_pairs - 1, :, :blk]

        after = None
        for c in range(n_q):
            after = causal_prefix(c, after)
    else:
        def band(i, n_left):
            order = range(n_left, -1, -1)
            ms = [jnp.full((SUBLANES, 2 * blk), NEG, jnp.float32) for _ in range(n_pairs)]
            for d in order:
                for p in range(n_pairs):
                    s = _dot(k_ref[0, key_rows(i - d), cols(p)], q2t_ref[p]) + tiles_ref[p, d]
                    s_ref[d, p] = s
                    ms[p] = jnp.maximum(ms[p], _fold_rows(s))
            ms = [jnp.max(m, axis=0, keepdims=True) for m in ms]
            for d in order:
                for p in range(n_pairs):
                    update = weights_update(vt_ref[0, i - d, p], jnp.exp2(s_ref[d, p] - ms[p]))
                    acc_ref[p] = update if d == n_left else acc_ref[p] + update
            return update[PAIR:PAIR + 1, :blk]

        def grid_step(first):
            after = None
            for c in range(n_q):
                stack_queries(c, after)
                after = band(pl.program_id(1) * n_q + c, min(c, left_blocks) if first else left_blocks)
                finish(c)

        assert n_q >= left_blocks
        pl.when(pl.program_id(1) == 0)(functools.partial(grid_step, True))
        pl.when(pl.program_id(1) > 0)(functools.partial(grid_step, False))


def _attention(qt, k, vt, tiles, left_blocks, lambda_init=None, extra=()):
    b, s, w = k.shape
    n_pairs, _, blk = qt.shape[2:]
    n_q = B_STEP_BLOCKS if left_blocks is None else A_STEP_BLOCKS
    blk_q = pl.BlockSpec((1, n_q) + qt.shape[2:], lambda bi, i: (bi, i, 0, 0, 0))
    whole_k = pl.BlockSpec((1, s, w), lambda bi, i: (bi, 0, 0))
    whole_v = pl.BlockSpec((1,) + vt.shape[1:], lambda bi, i: (bi, 0, 0, 0, 0))
    f32 = jnp.float32
    score_block = (n_pairs, blk, 2 * blk)
    if left_blocks is None:
        score_scratch = [pltpu.VMEM(score_block, f32)] * B_SCORE_RING + [pltpu.VMEM((n_pairs, 1, 2 * blk), f32)]
    else:
        score_scratch = [pltpu.VMEM((left_blocks + 1,) + score_block, f32)]
    return pl.pallas_call(
        functools.partial(_attn_kernel, left_blocks=left_blocks, lambda_init=lambda_init),
        grid=(b, s // (n_q * blk)),
        in_specs=[blk_q, whole_k, whole_v, _resident(tiles.shape)] + [_resident(e.shape) for e in extra],
        out_specs=pl.BlockSpec((1, n_q * blk, w), lambda bi, i: (bi, i, 0)),
        out_shape=jax.ShapeDtypeStruct((b, s, w), jnp.bfloat16),
        scratch_shapes=[pltpu.VMEM((n_pairs, PAIR, 2 * blk), jnp.bfloat16), *score_scratch,
                        pltpu.VMEM((n_pairs, ACC_ROWS, 2 * blk), f32)],
        compiler_params=pltpu.CompilerParams(dimension_semantics=("arbitrary", "arbitrary"),
                                             vmem_limit_bytes=V7X_VMEM_LIMIT),
        name="attn_a" if lambda_init is None else "attn_b",
    )(qt, k, vt, tiles, *extra)


def kernel(x, ffn1_norm, ffn1_w_in, ffn1_w_out, mix_norm, w_mix_in, b_gate, rel_bias_a, lambda_q1, lambda_k1, lambda_q2, lambda_k2, subln_g, t5_bias, w_branch_a, w_branch_b, w_o, ffn2_norm, ffn2_w_in, ffn2_w_out, final_norm):
    b, s, d = x.shape
    depth = ffn1_norm.shape[0]
    assert depth >= 1
    width = w_branch_a.shape[1]
    n_pairs = width // PAIR
    f32 = jnp.float32

    t5_rel = t5_bias.astype(f32)[:, _t5_bucket(_window_rel(B_TILES, B_BLK))]
    t5_rel = jnp.broadcast_to(t5_rel[:, None, :, None, :], (n_pairs, 2) + t5_rel.shape[1:2] + (1, 2 * LANES))
    tiles_b = _bias_tiles(t5_rel, B_TILES, B_BLK, None)

    xf = x.astype(f32).reshape(b * s, d)
    for li in range(depth):
        x1, qa, ka, va, qb, kb, vb, gates = _ffn_mix(
            xf, ffn1_norm[li][None].astype(f32), ffn1_w_in[li].astype(f32), ffn1_w_out[li].astype(f32),
            mix_norm[li][None].astype(f32), w_mix_in[li].astype(f32), b_gate[li][None].astype(f32), width, s)

        rel_a = jnp.clip(_window_rel(A_TILES, A_BLK), -REL_CLIP, REL_CLIP) + REL_CLIP
        a_rel = rel_bias_a[li].astype(f32)[:, rel_a]
        tiles_a = _bias_tiles(a_rel.reshape(n_pairs, 2, rel_a.shape[0], 1, 2 * LANES), A_TILES, A_BLK, LEFT_CHUNKS)

        def seq(t):
            return t.reshape(b, s, width)

        ya = _attention(qa, seq(ka), va, tiles_a, A_TILES - 1)
        lambda_init = 0.8 - 0.6 * math.exp(-0.3 * li)
        lams = [t[li][None].astype(f32) for t in (lambda_q1, lambda_k1, lambda_q2, lambda_k2)]
        yb = _attention(qb, seq(kb), vb, tiles_b, None, lambda_init=lambda_init,
                        extra=(*lams, subln_g[li][None].astype(f32)))

        xf = _out_ffn(x1, ya.reshape(b * s, width), yb.reshape(b * s, width), gates,
                      w_branch_a[li].astype(f32), w_branch_b[li].astype(f32), w_o[li].astype(f32),
                      ffn2_norm[li][None].astype(f32), ffn2_w_in[li].astype(f32), ffn2_w_out[li].astype(f32),
                      final_norm[None].astype(f32), final_norm=(li == depth - 1))
    return xf.reshape(b, s, d).astype(x.dtype)
```

```python
import functools
import math

import jax
import jax.numpy as jnp
from jax import lax
from jax.experimental import pallas as pl
from jax.experimental.pallas import tpu as pltpu

EPS = 1e-6
NEG = -1e30
LOG2E = math.log2(math.e)

CHUNK = 64
LEFT_CHUNKS = 8
REL_CLIP = 128
T5_BUCKETS = 32
T5_MAX_DIST = 128

LANES = 128
SUBLANES = 8
HEAD_DIM = 64
PAIR = 2 * HEAD_DIM
SUB_CHUNKS = LANES // CHUNK
ACC_ROWS = PAIR + SUBLANES

A_BLK = 128
A_TILES = LEFT_CHUNKS * CHUNK // A_BLK + 1
B_BLK = 256
B_TILES = 3
B_SCORE_RING = 2
A_STEP_BLOCKS = 4
B_STEP_BLOCKS = 2

V7X_VMEM_LIMIT = 60000 * 1024

ROW_TILE = 512
BF16_ROWS = 16
CAST_CHUNK_BYTES = 3 << 18
CAST_SLOTS = 4


def _rms(xf, g):
    return xf * lax.rsqrt(jnp.mean(xf * xf, axis=-1, keepdims=True) + EPS) * g


def _dot(a, b):
    return jnp.dot(a, b, preferred_element_type=jnp.float32)


def _swiglu(h, w_in_ref, w_out_ref):
    d_ff = w_out_ref.shape[0]
    gate = _dot(h, w_in_ref[:, :d_ff])
    up = _dot(h, w_in_ref[:, d_ff:])
    act = (gate * jax.nn.sigmoid(gate) * up).astype(jnp.bfloat16)
    return _dot(act, w_out_ref[...])


def _cast_chunk_rows(rows, width):
    fits = [r for r in range(BF16_ROWS, rows + 1, BF16_ROWS)
            if rows % r == 0 and r * width * 4 <= CAST_CHUNK_BYTES]
    return max(fits)


def _load_cast(src_hbm, dst_ref):
    rows, width = src_hbm.shape
    chunk = _cast_chunk_rows(rows, width)
    n_chunks = rows // chunk

    ahead = min(CAST_SLOTS, n_chunks) - 1

    def body(stage_ref, sem):
        def copy(c):
            slot = c % CAST_SLOTS
            return pltpu.make_async_copy(src_hbm.at[pl.ds(c * chunk, chunk)], stage_ref.at[slot], sem.at[slot])

        for c in range(ahead):
            copy(c).start()

        def one_chunk(c, carry):
            @pl.when(c + ahead < n_chunks)
            def _():
                copy(c + ahead).start()

            copy(c).wait()
            dst_ref[pl.ds(pl.multiple_of(c * chunk, chunk), chunk)] = (
                stage_ref[c % CAST_SLOTS].astype(dst_ref.dtype))
            return carry

        lax.fori_loop(0, n_chunks, one_chunk, 0)

    pl.run_scoped(body, pltpu.VMEM((CAST_SLOTS, chunk, width), src_hbm.dtype),
                  pltpu.SemaphoreType.DMA((CAST_SLOTS,)))


def _load_weights_once(pairs):
    @pl.when(pl.program_id(0) == 0)
    def _():
        for src_hbm, dst_ref in pairs:
            _load_cast(src_hbm, dst_ref)


def _ffn_mix_kernel(x_ref, g1_ref, w_in_hbm, w_out_hbm, gm_ref, w_mix_hbm, b_gate_ref,
                    x1_ref, qa_ref, ka_ref, va_ref, qb_ref, kb_ref, vb_ref, gates_ref,
                    w_in_ref, w_out_ref, w_mix_ref):
    _load_weights_once([(w_in_hbm, w_in_ref), (w_out_hbm, w_out_ref), (w_mix_hbm, w_mix_ref)])
    x = x_ref[...]
    h = _rms(x, g1_ref[...]).astype(jnp.bfloat16)
    x1 = x + 0.5 * _swiglu(h, w_in_ref, w_out_ref)
    x1_ref[...] = x1

    u = _rms(x1, gm_ref[...]).astype(jnp.bfloat16)
    width = ka_ref.shape[1]
    scale = HEAD_DIM ** -0.5 * LOG2E
    outs = (qa_ref, ka_ref, va_ref, qb_ref, kb_ref, vb_ref)
    for n, o_ref in enumerate(outs):
        p = _dot(u, w_mix_ref[:, n * width:(n + 1) * width])
        if o_ref is ka_ref or o_ref is kb_ref:
            o_ref[...] = p.astype(o_ref.dtype)
        elif o_ref is qa_ref or o_ref is qb_ref:
            _store_transposed(p * scale, o_ref)
        else:
            _store_transposed(p, o_ref)
    logits = _dot(u, w_mix_ref[:, len(outs) * width:]) + b_gate_ref[...]
    gates_ref[...] = jax.nn.sigmoid(logits)


def _store_transposed(p, o_ref):
    n_blk, n_pairs, _, blk = o_ref.shape[1:]
    pt = p.T
    for c in range(n_blk):
        for pr in range(n_pairs):
            o_ref[0, c, pr] = pt[pr * PAIR:(pr + 1) * PAIR, c * blk:(c + 1) * blk].astype(o_ref.dtype)


def _out_ffn_kernel(x1_ref, ya_ref, yb_ref, gates_ref, wa_hbm, wb_hbm, wo_hbm,
                    g2_ref, w_in_hbm, w_out_hbm, gf_ref, o_ref,
                    wa_ref, wb_ref, wo_ref, w_in_ref, w_out_ref, *, final_norm):
    _load_weights_once([(wa_hbm, wa_ref), (wb_hbm, wb_ref), (wo_hbm, wo_ref),
                        (w_in_hbm, w_in_ref), (w_out_hbm, w_out_ref)])
    d = x1_ref.shape[1]
    merged = (gates_ref[:, :d] * _dot(ya_ref[...], wa_ref[...])
              + gates_ref[:, d:] * _dot(yb_ref[...], wb_ref[...]))
    x2 = x1_ref[...] + _dot(merged.astype(jnp.bfloat16), wo_ref[...])
    h = _rms(x2, g2_ref[...]).astype(jnp.bfloat16)
    x3 = x2 + 0.5 * _swiglu(h, w_in_ref, w_out_ref)
    o_ref[...] = _rms(x3, gf_ref[...]) if final_norm else x3


def _resident(shape):
    return pl.BlockSpec(shape, lambda *_: (0,) * len(shape), pipeline_mode=pl.Buffered(1))


_IN_HBM = pl.BlockSpec(memory_space=pl.ANY)


def _bf16_scratch(*weights):
    return [pltpu.VMEM(w.shape, jnp.bfloat16) for w in weights]


def _rows(tm, width):
    return pl.BlockSpec((tm, width), lambda i: (i, 0))


def _ffn_mix(x, g1, w_in, w_out, gm, w_mix, b_gate, width, seq):
    m, d = x.shape
    tm = ROW_TILE
    bf = jnp.bfloat16
    n_pairs = width // PAIR
    steps_per_seq = seq // tm

    def transposed(blk):
        shape = (m // seq, seq // blk, n_pairs, PAIR, blk)
        spec = pl.BlockSpec((1, tm // blk) + shape[2:], lambda i: (i // steps_per_seq, i % steps_per_seq, 0, 0, 0))
        return jax.ShapeDtypeStruct(shape, bf), spec

    keys = jax.ShapeDtypeStruct((m, width), bf), _rows(tm, width)
    outs = [(jax.ShapeDtypeStruct((m, d), jnp.float32), _rows(tm, d)),
            transposed(A_BLK), keys, transposed(A_BLK), transposed(B_BLK), keys, transposed(B_BLK),
            (jax.ShapeDtypeStruct((m, b_gate.shape[1]), jnp.float32), _rows(tm, b_gate.shape[1]))]
    return pl.pallas_call(
        _ffn_mix_kernel,
        grid=(m // tm,),
        in_specs=[_rows(tm, d), _resident(g1.shape), _IN_HBM, _IN_HBM,
                  _resident(gm.shape), _IN_HBM, _resident(b_gate.shape)],
        out_specs=[spec for _, spec in outs],
        out_shape=[shape for shape, _ in outs],
        scratch_shapes=_bf16_scratch(w_in, w_out, w_mix),
        compiler_params=pltpu.CompilerParams(dimension_semantics=("arbitrary",),
                                             vmem_limit_bytes=V7X_VMEM_LIMIT),
        name="ffn_mix",
    )(x, g1, w_in, w_out, gm, w_mix, b_gate)


def _out_ffn(x1, ya, yb, gates, wa, wb, wo, g2, w_in, w_out, gf, final_norm):
    m, d = x1.shape
    tm = ROW_TILE
    return pl.pallas_call(
        functools.partial(_out_ffn_kernel, final_norm=final_norm),
        grid=(m // tm,),
        in_specs=[_rows(tm, d), _rows(tm, ya.shape[1]), _rows(tm, yb.shape[1]), _rows(tm, gates.shape[1]),
                  _IN_HBM, _IN_HBM, _IN_HBM, _resident(g2.shape), _IN_HBM, _IN_HBM, _resident(gf.shape)],
        out_specs=_rows(tm, d),
        out_shape=jax.ShapeDtypeStruct((m, d), jnp.float32),
        scratch_shapes=_bf16_scratch(wa, wb, wo, w_in, w_out),
        compiler_params=pltpu.CompilerParams(dimension_semantics=("arbitrary",),
                                             vmem_limit_bytes=V7X_VMEM_LIMIT),
        name="out_ffn",
    )(x1, ya, yb, gates, wa, wb, wo, g2, w_in, w_out, gf)


def _t5_bucket(rel):
    nb = T5_BUCKETS // 2
    ret = jnp.where(rel > 0, nb, 0)
    n = jnp.abs(rel)
    max_exact = nb // 2
    nf = jnp.maximum(n, 1).astype(jnp.float32)
    large = max_exact + (jnp.log(nf / max_exact) / math.log(T5_MAX_DIST / max_exact)
                         * (nb - max_exact)).astype(jnp.int32)
    large = jnp.minimum(large, nb - 1)
    return ret + jnp.where(n < max_exact, n, large)


def _window_rel(n_tiles, blk):
    nb = blk // LANES
    u = jnp.arange(-(nb - 1), n_tiles * nb, dtype=jnp.int32)[:, None]
    return LANES - jnp.arange(2 * LANES, dtype=jnp.int32)[None, :] - LANES * u


def _bias_tiles_kernel(rel_bias_ref, tiles_ref, *, left_chunks):
    n_pairs, n_tiles, blk, _ = tiles_ref.shape
    nb = blk // LANES
    kj = lax.broadcasted_iota(jnp.int32, (LANES, LANES), 0)
    qi = lax.broadcasted_iota(jnp.int32, (LANES, LANES), 1)
    chunk_diff = kj // CHUNK - qi // CHUNK
    for p in range(n_pairs):
        for half in range(2):
            for w in range(rel_bias_ref.shape[2]):
                u = w - (nb - 1)
                row = jnp.broadcast_to(rel_bias_ref[p, half, w], (LANES, 2 * LANES))
                sub = pltpu.roll(row, 0, 1, stride=1, stride_axis=0)[:, LANES:]
                for t in range(n_tiles):
                    for a in range(nb):
                        c = t * nb + a - u
                        if not 0 <= c < nb:
                            continue
                        diff = chunk_diff - u * SUB_CHUNKS
                        allowed = diff <= 0
                        if left_chunks is not None:
                            allowed = allowed & (diff >= -left_chunks)
                        tiles_ref[p, t, pl.ds(c * LANES, LANES), pl.ds(half * blk + a * LANES, LANES)] = (
                            jnp.where(allowed, sub * LOG2E, NEG))


def _bias_tiles(rel_bias, n_tiles, blk, left_chunks):
    n_pairs = rel_bias.shape[0]
    return pl.pallas_call(
        functools.partial(_bias_tiles_kernel, left_chunks=left_chunks),
        out_shape=jax.ShapeDtypeStruct((n_pairs, n_tiles, blk, 2 * blk), jnp.float32),
        compiler_params=pltpu.CompilerParams(vmem_limit_bytes=V7X_VMEM_LIMIT),
        name="bias_tiles",
    )(rel_bias)


def _sum_rows(x):
    acc = x[:SUBLANES]
    for r in range(1, x.shape[0] // SUBLANES):
        acc = acc + x[r * SUBLANES:(r + 1) * SUBLANES]
    return acc


def _fold_rows(x):
    m = x[:SUBLANES]
    for r in range(1, x.shape[0] // SUBLANES):
        m = jnp.maximum(m, x[r * SUBLANES:(r + 1) * SUBLANES])
    return m


def _attn_kernel(*refs, left_blocks, lambda_init):
    if lambda_init is None:
        qt_ref, k_ref, vt_ref, tiles_ref, o_ref, q2t_ref, s_ref, acc_ref = refs
    else:
        (qt_ref, k_ref, vt_ref, tiles_ref, lq1_ref, lk1_ref, lq2_ref, lk2_ref, subg_ref, o_ref,
         q2t_ref, *s_refs, m_ref, acc_ref) = refs
        lam = (jnp.exp(jnp.sum(lq1_ref[...] * lk1_ref[...], axis=-1, keepdims=True))
               - jnp.exp(jnp.sum(lq2_ref[...] * lk2_ref[...], axis=-1, keepdims=True))
               + lambda_init)
    n_q, n_pairs, _, blk = qt_ref.shape[1:]
    top_half = lax.broadcasted_iota(jnp.int32, (PAIR, blk), 0) < HEAD_DIM

    def cols(p):
        return slice(p * PAIR, (p + 1) * PAIR)

    def key_rows(j, n=1):
        return pl.ds(pl.multiple_of(j * blk, blk), n * blk)

    def weights_update(vt, e):
        return jnp.concatenate([_dot(vt, e.astype(jnp.bfloat16)), _sum_rows(e)], axis=0)

    def stack_queries(c, after=None):
        for p in range(n_pairs):
            qt = qt_ref[0, c, p]
            if after is not None:
                dep = after[p] if isinstance(after, list) else after
                qt = qt + (dep[:1, :1] * 0.0).astype(qt.dtype)
            zero = jnp.zeros_like(qt)
            q2t_ref[p, :, :blk] = jnp.where(top_half, qt, zero)
            q2t_ref[p, :, blk:] = jnp.where(top_half, zero, qt)

    def finish(c):
        late = []
        for p in range(n_pairs):
            o = acc_ref[p, :PAIR] / jnp.sum(acc_ref[p, PAIR:], axis=0, keepdims=True)
            if lambda_init is None:
                y = jnp.where(top_half, o[:, :blk], o[:, blk:]).T
            else:
                d = o[:, :blk] - lam * o[:, blk:]
                d = d * lax.rsqrt(jnp.mean(d * d, axis=0, keepdims=True) + EPS)
                y = d.T * subg_ref[...] * (1.0 - lambda_init)
            o_ref[0, c * blk:(c + 1) * blk, cols(p)] = y.astype(o_ref.dtype)
            late.append(y[:1])
        return late

    if left_blocks is None:
        n_tiles = tiles_ref.shape[1]
        ring = len(s_refs)

        def causal_prefix(c, i, after):
            stack_queries(c, after)

            def scores(j, last_readers):
                tile = min(i - j, n_tiles - 1)
                for p in range(n_pairs):
                    kb = k_ref[0, j * blk:(j + 1) * blk, cols(p)]
                    if last_readers is not None:
                        kb = kb + (last_readers[p][:1, :1] * 0.0).astype(kb.dtype)
                    s_refs[j % ring][p] = _dot(kb, q2t_ref[p]) + tiles_ref[p, tile]

            def online(j):
                late = []
                for p in range(n_pairs):
                    s = s_refs[j % ring][p]
                    m = jnp.max(_fold_rows(s), axis=0, keepdims=True)
                    if j == 0:
                        update = weights_update(vt_ref[0, j, p], jnp.exp2(s - m))
                        acc_ref[p] = update
                    else:
                        m_old = m_ref[p]
                        m = jnp.maximum(m_old, m)
                        update = weights_update(vt_ref[0, j, p], jnp.exp2(s - m))
                        acc_ref[p] = jnp.exp2(m_old - m) * acc_ref[p] + update
                    m_ref[p] = m
                    late.append(update[PAIR:PAIR + 1, :PAIR])
                return late

            readers = [after] * ring
            scores(0, readers[0])
            for j in range(i + 1):
                if j < i:
                    scores(j + 1, readers[(j + 1) % ring])
                readers[j % ring] = online(j)
            return finish(c)

        def grid_step(step):
            after = None
            for c in range(n_q):
                after = causal_prefix(c, step * n_q + c, after)

        for step in range(k_ref.shape[1] // (n_q * blk)):
            pl.when(pl.program_id(1) == step)(functools.partial(grid_step, step))
    else:
        def band(i, n_left):
            order = range(n_left, -1, -1)
            ms = [jnp.full((SUBLANES, 2 * blk), NEG, jnp.float32) for _ in range(n_pairs)]
            for d in order:
                for p in range(n_pairs):
                    s = _dot(k_ref[0, key_rows(i - d), cols(p)], q2t_ref[p]) + tiles_ref[p, d]
                    s_ref[d, p] = s
                    ms[p] = jnp.maximum(ms[p], _fold_rows(s))
            ms = [jnp.max(m, axis=0, keepdims=True) for m in ms]
            for d in order:
                for p in range(n_pairs):
                    update = weights_update(vt_ref[0, i - d, p], jnp.exp2(s_ref[d, p] - ms[p]))
                    acc_ref[p] = update if d == n_left else acc_ref[p] + update
            return update[PAIR:PAIR + 1, :blk]

        def grid_step(first):
            after = None
            for c in range(n_q):
                stack_queries(c, after)
                after = band(pl.program_id(1) * n_q + c, min(c, left_blocks) if first else left_blocks)
                finish(c)

        assert n_q >= left_blocks
        pl.when(pl.program_id(1) == 0)(functools.partial(grid_step, True))
        pl.when(pl.program_id(1) > 0)(functools.partial(grid_step, False))


def _attention(qt, k, vt, tiles, left_blocks, lambda_init=None, extra=()):
    b, s, w = k.shape
    n_pairs, _, blk = qt.shape[2:]
    n_q = B_STEP_BLOCKS if left_blocks is None else A_STEP_BLOCKS
    blk_q = pl.BlockSpec((1, n_q) + qt.shape[2:], lambda bi, i: (bi, i, 0, 0, 0))
    whole_k = pl.BlockSpec((1, s, w), lambda bi, i: (bi, 0, 0))
    whole_v = pl.BlockSpec((1,) + vt.shape[1:], lambda bi, i: (bi, 0, 0, 0, 0))
    f32 = jnp.float32
    score_block = (n_pairs, blk, 2 * blk)
    if left_blocks is None:
        score_scratch = [pltpu.VMEM(score_block, f32)] * B_SCORE_RING + [pltpu.VMEM((n_pairs, 1, 2 * blk), f32)]
    else:
        score_scratch = [pltpu.VMEM((left_blocks + 1,) + score_block, f32)]
    return pl.pallas_call(
        functools.partial(_attn_kernel, left_blocks=left_blocks, lambda_init=lambda_init),
        grid=(b, s // (n_q * blk)),
        in_specs=[blk_q, whole_k, whole_v, _resident(tiles.shape)] + [_resident(e.shape) for e in extra],
        out_specs=pl.BlockSpec((1, n_q * blk, w), lambda bi, i: (bi, i, 0)),
        out_shape=jax.ShapeDtypeStruct((b, s, w), jnp.bfloat16),
        scratch_shapes=[pltpu.VMEM((n_pairs, PAIR, 2 * blk), jnp.bfloat16), *score_scratch,
                        pltpu.VMEM((n_pairs, ACC_ROWS, 2 * blk), f32)],
        compiler_params=pltpu.CompilerParams(dimension_semantics=("arbitrary", "arbitrary"),
                                             vmem_limit_bytes=V7X_VMEM_LIMIT),
        name="attn_a" if lambda_init is None else "attn_b",
    )(qt, k, vt, tiles, *extra)


def kernel(x, ffn1_norm, ffn1_w_in, ffn1_w_out, mix_norm, w_mix_in, b_gate, rel_bias_a, lambda_q1, lambda_k1, lambda_q2, lambda_k2, subln_g, t5_bias, w_branch_a, w_branch_b, w_o, ffn2_norm, ffn2_w_in, ffn2_w_out, final_norm):
    b, s, d = x.shape
    depth = ffn1_norm.shape[0]
    assert depth >= 1
    width = w_branch_a.shape[1]
    n_pairs = width // PAIR
    f32 = jnp.float32

    t5_rel = t5_bias.astype(f32)[:, _t5_bucket(_window_rel(B_TILES, B_BLK))]
    t5_rel = jnp.broadcast_to(t5_rel[:, None, :, None, :], (n_pairs, 2) + t5_rel.shape[1:2] + (1, 2 * LANES))
    tiles_b = _bias_tiles(t5_rel, B_TILES, B_BLK, None)

    xf = x.astype(f32).reshape(b * s, d)
    for li in range(depth):
        x1, qa, ka, va, qb, kb, vb, gates = _ffn_mix(
            xf, ffn1_norm[li][None].astype(f32), ffn1_w_in[li].astype(f32), ffn1_w_out[li].astype(f32),
            mix_norm[li][None].astype(f32), w_mix_in[li].astype(f32), b_gate[li][None].astype(f32), width, s)

        rel_a = jnp.clip(_window_rel(A_TILES, A_BLK), -REL_CLIP, REL_CLIP) + REL_CLIP
        a_rel = rel_bias_a[li].astype(f32)[:, rel_a]
        tiles_a = _bias_tiles(a_rel.reshape(n_pairs, 2, rel_a.shape[0], 1, 2 * LANES), A_TILES, A_BLK, LEFT_CHUNKS)

        def seq(t):
            return t.reshape(b, s, width)

        ya = _attention(qa, seq(ka), va, tiles_a, A_TILES - 1)
        lambda_init = 0.8 - 0.6 * math.exp(-0.3 * li)
        lams = [t[li][None].astype(f32) for t in (lambda_q1, lambda_k1, lambda_q2, lambda_k2)]
        yb = _attention(qb, seq(kb), vb, tiles_b, None, lambda_init=lambda_init,
                        extra=(*lams, subln_g[li][None].astype(f32)))

        xf = _out_ffn(x1, ya.reshape(b * s, width), yb.reshape(b * s, width), gates,
                      w_branch_a[li].astype(f32), w_branch_b[li].astype(f32), w_o[li].astype(f32),
                      ffn2_norm[li][None].astype(f32), ffn2_w_in[li].astype(f32), ffn2_w_out[li].astype(f32),
                      final_norm[None].astype(f32), final_norm=(li == depth - 1))
    return xf.reshape(b, s, d).astype(x.dtype)
```

```python
import functools
import math

import jax
import jax.numpy as jnp
from jax import lax
from jax.experimental import pallas as pl
from jax.experimental.pallas import tpu as pltpu

EPS = 1e-6
NEG = -1e30
LOG2E = math.log2(math.e)

CHUNK = 64
LEFT_CHUNKS = 8
REL_CLIP = 128
T5_BUCKETS = 32
T5_MAX_DIST = 128

LANES = 128
SUBLANES = 8
HEAD_DIM = 64
PAIR = 2 * HEAD_DIM
SUB_CHUNKS = LANES // CHUNK
ACC_ROWS = PAIR + SUBLANES

A_BLK = 128
A_TILES = LEFT_CHUNKS * CHUNK // A_BLK + 1
B_BLK = 256
B_TILES = 3
B_SCORE_RING = 3
A_STEP_BLOCKS = 4
B_STEP_BLOCKS = 2

V7X_VMEM_LIMIT = 60000 * 1024

ROW_TILE = 512
BF16_ROWS = 16
CAST_CHUNK_BYTES = 3 << 18
CAST_SLOTS = 4


def _rms(xf, g):
    return xf * lax.rsqrt(jnp.mean(xf * xf, axis=-1, keepdims=True) + EPS) * g


def _dot(a, b):
    return jnp.dot(a, b, preferred_element_type=jnp.float32)


def _swiglu(h, w_in_ref, w_out_ref):
    d_ff = w_out_ref.shape[0]
    gate = _dot(h, w_in_ref[:, :d_ff])
    up = _dot(h, w_in_ref[:, d_ff:])
    act = (gate * jax.nn.sigmoid(gate) * up).astype(jnp.bfloat16)
    return _dot(act, w_out_ref[...])


def _cast_chunk_rows(rows, width):
    fits = [r for r in range(BF16_ROWS, rows + 1, BF16_ROWS)
            if rows % r == 0 and r * width * 4 <= CAST_CHUNK_BYTES]
    return max(fits)


def _load_cast(src_hbm, dst_ref):
    rows, width = src_hbm.shape
    chunk = _cast_chunk_rows(rows, width)
    n_chunks = rows // chunk

    ahead = min(CAST_SLOTS, n_chunks) - 1

    def body(stage_ref, sem):
        def copy(c):
            slot = c % CAST_SLOTS
            return pltpu.make_async_copy(src_hbm.at[pl.ds(c * chunk, chunk)], stage_ref.at[slot], sem.at[slot])

        for c in range(ahead):
            copy(c).start()

        def one_chunk(c, carry):
            @pl.when(c + ahead < n_chunks)
            def _():
                copy(c + ahead).start()

            copy(c).wait()
            dst_ref[pl.ds(pl.multiple_of(c * chunk, chunk), chunk)] = (
                stage_ref[c % CAST_SLOTS].astype(dst_ref.dtype))
            return carry

        lax.fori_loop(0, n_chunks, one_chunk, 0)

    pl.run_scoped(body, pltpu.VMEM((CAST_SLOTS, chunk, width), src_hbm.dtype),
                  pltpu.SemaphoreType.DMA((CAST_SLOTS,)))


def _load_weights_once(pairs):
    @pl.when(pl.program_id(0) == 0)
    def _():
        for src_hbm, dst_ref in pairs:
            _load_cast(src_hbm, dst_ref)


def _ffn_mix_kernel(x_ref, g1_ref, w_in_hbm, w_out_hbm, gm_ref, w_mix_hbm, b_gate_ref,
                    x1_ref, qa_ref, ka_ref, va_ref, qb_ref, kb_ref, vb_ref, gates_ref,
                    w_in_ref, w_out_ref, w_mix_ref):
    _load_weights_once([(w_in_hbm, w_in_ref), (w_out_hbm, w_out_ref), (w_mix_hbm, w_mix_ref)])
    x = x_ref[...]
    h = _rms(x, g1_ref[...]).astype(jnp.bfloat16)
    x1 = x + 0.5 * _swiglu(h, w_in_ref, w_out_ref)
    x1_ref[...] = x1

    u = _rms(x1, gm_ref[...]).astype(jnp.bfloat16)
    width = ka_ref.shape[1]
    scale = HEAD_DIM ** -0.5 * LOG2E
    outs = (qa_ref, ka_ref, va_ref, qb_ref, kb_ref, vb_ref)
    for n, o_ref in enumerate(outs):
        p = _dot(u, w_mix_ref[:, n * width:(n + 1) * width])
        if o_ref is ka_ref or o_ref is kb_ref:
            o_ref[...] = p.astype(o_ref.dtype)
        elif o_ref is qa_ref or o_ref is qb_ref:
            _store_transposed(p * scale, o_ref)
        else:
            _store_transposed(p, o_ref)
    logits = _dot(u, w_mix_ref[:, len(outs) * width:]) + b_gate_ref[...]
    gates_ref[...] = jax.nn.sigmoid(logits)


def _store_transposed(p, o_ref):
    n_blk, n_pairs, _, blk = o_ref.shape[1:]
    pt = p.T
    for c in range(n_blk):
        for pr in range(n_pairs):
            o_ref[0, c, pr] = pt[pr * PAIR:(pr + 1) * PAIR, c * blk:(c + 1) * blk].astype(o_ref.dtype)


def _out_ffn_kernel(x1_ref, ya_ref, yb_ref, gates_ref, wa_hbm, wb_hbm, wo_hbm,
                    g2_ref, w_in_hbm, w_out_hbm, gf_ref, o_ref,
                    wa_ref, wb_ref, wo_ref, w_in_ref, w_out_ref, *, final_norm):
    _load_weights_once([(wa_hbm, wa_ref), (wb_hbm, wb_ref), (wo_hbm, wo_ref),
                        (w_in_hbm, w_in_ref), (w_out_hbm, w_out_ref)])
    d = x1_ref.shape[1]
    merged = (gates_ref[:, :d] * _dot(ya_ref[...], wa_ref[...])
              + gates_ref[:, d:] * _dot(yb_ref[...], wb_ref[...]))
    x2 = x1_ref[...] + _dot(merged.astype(jnp.bfloat16), wo_ref[...])
    h = _rms(x2, g2_ref[...]).astype(jnp.bfloat16)
    x3 = x2 + 0.5 * _swiglu(h, w_in_ref, w_out_ref)
    o_ref[...] = _rms(x3, gf_ref[...]) if final_norm else x3


def _resident(shape):
    return pl.BlockSpec(shape, lambda *_: (0,) * len(shape), pipeline_mode=pl.Buffered(1))


_IN_HBM = pl.BlockSpec(memory_space=pl.ANY)


def _bf16_scratch(*weights):
    return [pltpu.VMEM(w.shape, jnp.bfloat16) for w in weights]


def _rows(tm, width):
    return pl.BlockSpec((tm, width), lambda i: (i, 0))


def _ffn_mix(x, g1, w_in, w_out, gm, w_mix, b_gate, width, seq):
    m, d = x.shape
    tm = ROW_TILE
    bf = jnp.bfloat16
    n_pairs = width // PAIR
    steps_per_seq = seq // tm

    def transposed(blk):
        shape = (m // seq, seq // blk, n_pairs, PAIR, blk)
        spec = pl.BlockSpec((1, tm // blk) + shape[2:], lambda i: (i // steps_per_seq, i % steps_per_seq, 0, 0, 0))
        return jax.ShapeDtypeStruct(shape, bf), spec

    keys = jax.ShapeDtypeStruct((m, width), bf), _rows(tm, width)
    outs = [(jax.ShapeDtypeStruct((m, d), jnp.float32), _rows(tm, d)),
            transposed(A_BLK), keys, transposed(A_BLK), transposed(B_BLK), keys, transposed(B_BLK),
            (jax.ShapeDtypeStruct((m, b_gate.shape[1]), jnp.float32), _rows(tm, b_gate.shape[1]))]
    return pl.pallas_call(
        _ffn_mix_kernel,
        grid=(m // tm,),
        in_specs=[_rows(tm, d), _resident(g1.shape), _IN_HBM, _IN_HBM,
                  _resident(gm.shape), _IN_HBM, _resident(b_gate.shape)],
        out_specs=[spec for _, spec in outs],
        out_shape=[shape for shape, _ in outs],
        scratch_shapes=_bf16_scratch(w_in, w_out, w_mix),
        compiler_params=pltpu.CompilerParams(dimension_semantics=("arbitrary",),
                                             vmem_limit_bytes=V7X_VMEM_LIMIT),
        name="ffn_mix",
    )(x, g1, w_in, w_out, gm, w_mix, b_gate)


def _out_ffn(x1, ya, yb, gates, wa, wb, wo, g2, w_in, w_out, gf, final_norm):
    m, d = x1.shape
    tm = ROW_TILE
    return pl.pallas_call(
        functools.partial(_out_ffn_kernel, final_norm=final_norm),
        grid=(m // tm,),
        in_specs=[_rows(tm, d), _rows(tm, ya.shape[1]), _rows(tm, yb.shape[1]), _rows(tm, gates.shape[1]),
                  _IN_HBM, _IN_HBM, _IN_HBM, _resident(g2.shape), _IN_HBM, _IN_HBM, _resident(gf.shape)],
        out_specs=_rows(tm, d),
        out_shape=jax.ShapeDtypeStruct((m, d), jnp.float32),
        scratch_shapes=_bf16_scratch(wa, wb, wo, w_in, w_out),
        compiler_params=pltpu.CompilerParams(dimension_semantics=("arbitrary",),
                                             vmem_limit_bytes=V7X_VMEM_LIMIT),
        name="out_ffn",
    )(x1, ya, yb, gates, wa, wb, wo, g2, w_in, w_out, gf)


def _t5_bucket(rel):
    nb = T5_BUCKETS // 2
    ret = jnp.where(rel > 0, nb, 0)
    n = jnp.abs(rel)
    max_exact = nb // 2
    nf = jnp.maximum(n, 1).astype(jnp.float32)
    large = max_exact + (jnp.log(nf / max_exact) / math.log(T5_MAX_DIST / max_exact)
                         * (nb - max_exact)).astype(jnp.int32)
    large = jnp.minimum(large, nb - 1)
    return ret + jnp.where(n < max_exact, n, large)


def _window_rel(n_tiles, blk):
    nb = blk // LANES
    u = jnp.arange(-(nb - 1), n_tiles * nb, dtype=jnp.int32)[:, None]
    return LANES - jnp.arange(2 * LANES, dtype=jnp.int32)[None, :] - LANES * u


def _bias_tiles_kernel(rel_bias_ref, tiles_ref, *, left_chunks):
    n_pairs, n_tiles, blk, _ = tiles_ref.shape
    nb = blk // LANES
    kj = lax.broadcasted_iota(jnp.int32, (LANES, LANES), 0)
    qi = lax.broadcasted_iota(jnp.int32, (LANES, LANES), 1)
    chunk_diff = kj // CHUNK - qi // CHUNK
    for p in range(n_pairs):
        for half in range(2):
            for w in range(rel_bias_ref.shape[2]):
                u = w - (nb - 1)
                row = jnp.broadcast_to(rel_bias_ref[p, half, w], (LANES, 2 * LANES))
                sub = pltpu.roll(row, 0, 1, stride=1, stride_axis=0)[:, LANES:]
                for t in range(n_tiles):
                    for a in range(nb):
                        c = t * nb + a - u
                        if not 0 <= c < nb:
                            continue
                        diff = chunk_diff - u * SUB_CHUNKS
                        allowed = diff <= 0
                        if left_chunks is not None:
                            allowed = allowed & (diff >= -left_chunks)
                        tiles_ref[p, t, pl.ds(c * LANES, LANES), pl.ds(half * blk + a * LANES, LANES)] = (
                            jnp.where(allowed, sub * LOG2E, NEG))


def _bias_tiles(rel_bias, n_tiles, blk, left_chunks):
    n_pairs = rel_bias.shape[0]
    return pl.pallas_call(
        functools.partial(_bias_tiles_kernel, left_chunks=left_chunks),
        out_shape=jax.ShapeDtypeStruct((n_pairs, n_tiles, blk, 2 * blk), jnp.float32),
        compiler_params=pltpu.CompilerParams(vmem_limit_bytes=V7X_VMEM_LIMIT),
        name="bias_tiles",
    )(rel_bias)


def _sum_rows(x):
    acc = x[:SUBLANES]
    for r in range(1, x.shape[0] // SUBLANES):
        acc = acc + x[r * SUBLANES:(r + 1) * SUBLANES]
    return acc


def _fold_rows(x):
    m = x[:SUBLANES]
    for r in range(1, x.shape[0] // SUBLANES):
        m = jnp.maximum(m, x[r * SUBLANES:(r + 1) * SUBLANES])
    return m


def _attn_kernel(*refs, left_blocks, lambda_init):
    if lambda_init is None:
        qt_ref, k_ref, vt_ref, tiles_ref, o_ref, q2t_ref, s_ref, acc_ref = refs
    else:
        (qt_ref, k_ref, vt_ref, tiles_ref, lq1_ref, lk1_ref, lq2_ref, lk2_ref, subg_ref, o_ref,
         q2t_ref, *s_refs, m_ref, acc_ref) = refs
        lam = (jnp.exp(jnp.sum(lq1_ref[...] * lk1_ref[...], axis=-1, keepdims=True))
               - jnp.exp(jnp.sum(lq2_ref[...] * lk2_ref[...], axis=-1, keepdims=True))
               + lambda_init)
    n_q, n_pairs, _, blk = qt_ref.shape[1:]
    top_half = lax.broadcasted_iota(jnp.int32, (PAIR, blk), 0) < HEAD_DIM

    def cols(p):
        return slice(p * PAIR, (p + 1) * PAIR)

    def key_rows(j, n=1):
        return pl.ds(pl.multiple_of(j * blk, blk), n * blk)

    def weights_update(vt, e):
        return jnp.concatenate([_dot(vt, e.astype(jnp.bfloat16)), _sum_rows(e)], axis=0)

    def stack_queries(c, after=None):
        for p in range(n_pairs):
            qt = qt_ref[0, c, p]
            if after is not None:
                dep = after[p] if isinstance(after, list) else after
                qt = qt + (dep[:1, :1] * 0.0).astype(qt.dtype)
            zero = jnp.zeros_like(qt)
            q2t_ref[p, :, :blk] = jnp.where(top_half, qt, zero)
            q2t_ref[p, :, blk:] = jnp.where(top_half, zero, qt)

    def finish(c):
        late = []
        for p in range(n_pairs):
            o = acc_ref[p, :PAIR] / jnp.sum(acc_ref[p, PAIR:], axis=0, keepdims=True)
            if lambda_init is None:
                y = jnp.where(top_half, o[:, :blk], o[:, blk:]).T
            else:
                d = o[:, :blk] - lam * o[:, blk:]
                d = d * lax.rsqrt(jnp.mean(d * d, axis=0, keepdims=True) + EPS)
                y = d.T * subg_ref[...] * (1.0 - lambda_init)
            o_ref[0, c * blk:(c + 1) * blk, cols(p)] = y.astype(o_ref.dtype)
            late.append(y[:1])
        return late

    if left_blocks is None:
        n_tiles = tiles_ref.shape[1]
        ring = len(s_refs)

        def causal_prefix(c, i, readers, after):
            stack_queries(c, after)

            def scores(j, last_readers):
                tile = min(i - j, n_tiles - 1)
                for p in range(n_pairs):
                    kb = k_ref[0, j * blk:(j + 1) * blk, cols(p)]
                    if last_readers is not None:
                        kb = kb + (last_readers[p][:1, :1] * 0.0).astype(kb.dtype)
                    s_refs[j % ring][p] = _dot(kb, q2t_ref[p]) + tiles_ref[p, tile]

            def online(j):
                late = []
                for p in range(n_pairs):
                    s = s_refs[j % ring][p]
                    m = jnp.max(_fold_rows(s), axis=0, keepdims=True)
                    if j == 0:
                        update = weights_update(vt_ref[0, j, p], jnp.exp2(s - m))
                        acc_ref[p] = update
                    else:
                        m_old = m_ref[p]
                        m = jnp.maximum(m_old, m)
                        update = weights_update(vt_ref[0, j, p], jnp.exp2(s - m))
                        acc_ref[p] = jnp.exp2(m_old - m) * acc_ref[p] + update
                    m_ref[p] = m
                    late.append(update[PAIR:PAIR + 1, :PAIR])
                return late

            scores(0, readers[0])
            for j in range(i + 1):
                if j < i:
                    scores(j + 1, readers[(j + 1) % ring])
                readers[j % ring] = online(j)
            finish(c)
            return readers[i % ring]

        def grid_step(step):
            readers, after = [None] * ring, None
            for c in range(n_q):
                after = causal_prefix(c, step * n_q + c, readers, after)

        for step in range(k_ref.shape[1] // (n_q * blk)):
            pl.when(pl.program_id(1) == step)(functools.partial(grid_step, step))
    else:
        def band(i, n_left):
            order = range(n_left, -1, -1)
            ms = [jnp.full((SUBLANES, 2 * blk), NEG, jnp.float32) for _ in range(n_pairs)]
            for d in order:
                for p in range(n_pairs):
                    s = _dot(k_ref[0, key_rows(i - d), cols(p)], q2t_ref[p]) + tiles_ref[p, d]
                    s_ref[d, p] = s
                    ms[p] = jnp.maximum(ms[p], _fold_rows(s))
            ms = [jnp.max(m, axis=0, keepdims=True) for m in ms]
            for d in order:
                for p in range(n_pairs):
                    update = weights_update(vt_ref[0, i - d, p], jnp.exp2(s_ref[d, p] - ms[p]))
                    acc_ref[p] = update if d == n_left else acc_ref[p] + update
            return update[PAIR:PAIR + 1, :blk]

        def grid_step(first):
            after = None
            for c in range(n_q):
                stack_queries(c, after)
                after = band(pl.program_id(1) * n_q + c, min(c, left_blocks) if first else left_blocks)
                finish(c)

        assert n_q >= left_blocks
        pl.when(pl.program_id(1) == 0)(functools.partial(grid_step, True))
        pl.when(pl.program_id(1) > 0)(functools.partial(grid_step, False))


def _attention(qt, k, vt, tiles, left_blocks, lambda_init=None, extra=()):
    b, s, w = k.shape
    n_pairs, _, blk = qt.shape[2:]
    n_q = B_STEP_BLOCKS if left_blocks is None else A_STEP_BLOCKS
    blk_q = pl.BlockSpec((1, n_q) + qt.shape[2:], lambda bi, i: (bi, i, 0, 0, 0))
    whole_k = pl.BlockSpec((1, s, w), lambda bi, i: (bi, 0, 0))
    whole_v = pl.BlockSpec((1,) + vt.shape[1:], lambda bi, i: (bi, 0, 0, 0, 0))
    f32 = jnp.float32
    score_block = (n_pairs, blk, 2 * blk)
    if left_blocks is None:
        score_scratch = [pltpu.VMEM(score_block, f32)] * B_SCORE_RING + [pltpu.VMEM((n_pairs, 1, 2 * blk), f32)]
    else:
        score_scratch = [pltpu.VMEM((left_blocks + 1,) + score_block, f32)]
    return pl.pallas_call(
        functools.partial(_attn_kernel, left_blocks=left_blocks, lambda_init=lambda_init),
        grid=(b, s // (n_q * blk)),
        in_specs=[blk_q, whole_k, whole_v, _resident(tiles.shape)] + [_resident(e.shape) for e in extra],
        out_specs=pl.BlockSpec((1, n_q * blk, w), lambda bi, i: (bi, i, 0)),
        out_shape=jax.ShapeDtypeStruct((b, s, w), jnp.bfloat16),
        scratch_shapes=[pltpu.VMEM((n_pairs, PAIR, 2 * blk), jnp.bfloat16), *score_scratch,
                        pltpu.VMEM((n_pairs, ACC_ROWS, 2 * blk), f32)],
        compiler_params=pltpu.CompilerParams(dimension_semantics=("arbitrary", "arbitrary"),
                                             vmem_limit_bytes=V7X_VMEM_LIMIT),
        name="attn_a" if lambda_init is None else "attn_b",
    )(qt, k, vt, tiles, *extra)


def kernel(x, ffn1_norm, ffn1_w_in, ffn1_w_out, mix_norm, w_mix_in, b_gate, rel_bias_a, lambda_q1, lambda_k1, lambda_q2, lambda_k2, subln_g, t5_bias, w_branch_a, w_branch_b, w_o, ffn2_norm, ffn2_w_in, ffn2_w_out, final_norm):
    b, s, d = x.shape
    depth = ffn1_norm.shape[0]
    assert depth >= 1
    width = w_branch_a.shape[1]
    n_pairs = width // PAIR
    f32 = jnp.float32

    t5_rel = t5_bias.astype(f32)[:, _t5_bucket(_window_rel(B_TILES, B_BLK))]
    t5_rel = jnp.broadcast_to(t5_rel[:, None, :, None, :], (n_pairs, 2) + t5_rel.shape[1:2] + (1, 2 * LANES))
    tiles_b = _bias_tiles(t5_rel, B_TILES, B_BLK, None)

    xf = x.astype(f32).reshape(b * s, d)
    for li in range(depth):
        x1, qa, ka, va, qb, kb, vb, gates = _ffn_mix(
            xf, ffn1_norm[li][None].astype(f32), ffn1_w_in[li].astype(f32), ffn1_w_out[li].astype(f32),
            mix_norm[li][None].astype(f32), w_mix_in[li].astype(f32), b_gate[li][None].astype(f32), width, s)

        rel_a = jnp.clip(_window_rel(A_TILES, A_BLK), -REL_CLIP, REL_CLIP) + REL_CLIP
        a_rel = rel_bias_a[li].astype(f32)[:, rel_a]
        tiles_a = _bias_tiles(a_rel.reshape(n_pairs, 2, rel_a.shape[0], 1, 2 * LANES), A_TILES, A_BLK, LEFT_CHUNKS)

        def seq(t):
            return t.reshape(b, s, width)

        ya = _attention(qa, seq(ka), va, tiles_a, A_TILES - 1)
        lambda_init = 0.8 - 0.6 * math.exp(-0.3 * li)
        lams = [t[li][None].astype(f32) for t in (lambda_q1, lambda_k1, lambda_q2, lambda_k2)]
        yb = _attention(qb, seq(kb), vb, tiles_b, None, lambda_init=lambda_init,
                        extra=(*lams, subln_g[li][None].astype(f32)))

        xf = _out_ffn(x1, ya.reshape(b * s, width), yb.reshape(b * s, width), gates,
                      w_branch_a[li].astype(f32), w_branch_b[li].astype(f32), w_o[li].astype(f32),
                      ffn2_norm[li][None].astype(f32), ffn2_w_in[li].astype(f32), ffn2_w_out[li].astype(f32),
                      final_norm[None].astype(f32), final_norm=(li == depth - 1))
    return xf.reshape(b, s, d).astype(x.dtype)
```

```python
import functools
import math

import jax
import jax.numpy as jnp
from jax import lax
from jax.experimental import pallas as pl
from jax.experimental.pallas import tpu as pltpu

EPS = 1e-6
NEG = -1e30
LOG2E = math.log2(math.e)

CHUNK = 64
LEFT_CHUNKS = 8
REL_CLIP = 128
T5_BUCKETS = 32
T5_MAX_DIST = 128

LANES = 128
SUBLANES = 8
HEAD_DIM = 64
PAIR = 2 * HEAD_DIM
SUB_CHUNKS = LANES // CHUNK
ACC_ROWS = PAIR + SUBLANES

A_BLK = 128
A_TILES = LEFT_CHUNKS * CHUNK // A_BLK + 1
B_BLK = 256
B_TILES = 3
B_SCORE_RING = 3
A_STEP_BLOCKS = 8
B_STEP_BLOCKS = 2

V7X_VMEM_LIMIT = 60000 * 1024

ROW_TILE = 512
BF16_ROWS = 16
CAST_CHUNK_BYTES = 3 << 18
CAST_SLOTS = 4


def _rms(xf, g):
    return xf * lax.rsqrt(jnp.mean(xf * xf, axis=-1, keepdims=True) + EPS) * g


def _dot(a, b):
    return jnp.dot(a, b, preferred_element_type=jnp.float32)


def _swiglu(h, w_in_ref, w_out_ref):
    d_ff = w_out_ref.shape[0]
    gate = _dot(h, w_in_ref[:, :d_ff])
    up = _dot(h, w_in_ref[:, d_ff:])
    act = (gate * jax.nn.sigmoid(gate) * up).astype(jnp.bfloat16)
    return _dot(act, w_out_ref[...])


def _cast_chunk_rows(rows, width):
    fits = [r for r in range(BF16_ROWS, rows + 1, BF16_ROWS)
            if rows % r == 0 and r * width * 4 <= CAST_CHUNK_BYTES]
    return max(fits)


def _load_cast(src_hbm, dst_ref):
    rows, width = src_hbm.shape
    chunk = _cast_chunk_rows(rows, width)
    n_chunks = rows // chunk

    ahead = min(CAST_SLOTS, n_chunks) - 1

    def body(stage_ref, sem):
        def copy(c):
            slot = c % CAST_SLOTS
            return pltpu.make_async_copy(src_hbm.at[pl.ds(c * chunk, chunk)], stage_ref.at[slot], sem.at[slot])

        for c in range(ahead):
            copy(c).start()

        def one_chunk(c, carry):
            @pl.when(c + ahead < n_chunks)
            def _():
                copy(c + ahead).start()

            copy(c).wait()
            dst_ref[pl.ds(pl.multiple_of(c * chunk, chunk), chunk)] = (
                stage_ref[c % CAST_SLOTS].astype(dst_ref.dtype))
            return carry

        lax.fori_loop(0, n_chunks, one_chunk, 0)

    pl.run_scoped(body, pltpu.VMEM((CAST_SLOTS, chunk, width), src_hbm.dtype),
                  pltpu.SemaphoreType.DMA((CAST_SLOTS,)))


def _load_weights_once(pairs):
    @pl.when(pl.program_id(0) == 0)
    def _():
        for src_hbm, dst_ref in pairs:
            _load_cast(src_hbm, dst_ref)


def _ffn_mix_kernel(x_ref, g1_ref, w_in_hbm, w_out_hbm, gm_ref, w_mix_hbm, b_gate_ref,
                    x1_ref, qa_ref, ka_ref, va_ref, qb_ref, kb_ref, vb_ref, gates_ref,
                    w_in_ref, w_out_ref, w_mix_ref):
    _load_weights_once([(w_in_hbm, w_in_ref), (w_out_hbm, w_out_ref), (w_mix_hbm, w_mix_ref)])
    x = x_ref[...]
    h = _rms(x, g1_ref[...]).astype(jnp.bfloat16)
    x1 = x + 0.5 * _swiglu(h, w_in_ref, w_out_ref)
    x1_ref[...] = x1

    u = _rms(x1, gm_ref[...]).astype(jnp.bfloat16)
    width = ka_ref.shape[1]
    scale = HEAD_DIM ** -0.5 * LOG2E
    outs = (qa_ref, ka_ref, va_ref, qb_ref, kb_ref, vb_ref)
    for n, o_ref in enumerate(outs):
        p = _dot(u, w_mix_ref[:, n * width:(n + 1) * width])
        if o_ref is ka_ref or o_ref is kb_ref:
            o_ref[...] = p.astype(o_ref.dtype)
        elif o_ref is qa_ref or o_ref is qb_ref:
            _store_transposed(p * scale, o_ref)
        else:
            _store_transposed(p, o_ref)
    logits = _dot(u, w_mix_ref[:, len(outs) * width:]) + b_gate_ref[...]
    gates_ref[...] = jax.nn.sigmoid(logits)


def _store_transposed(p, o_ref):
    n_blk, n_pairs, _, blk = o_ref.shape[1:]
    pt = p.T
    for c in range(n_blk):
        for pr in range(n_pairs):
            o_ref[0, c, pr] = pt[pr * PAIR:(pr + 1) * PAIR, c * blk:(c + 1) * blk].astype(o_ref.dtype)


def _out_ffn_kernel(x1_ref, ya_ref, yb_ref, gates_ref, wa_hbm, wb_hbm, wo_hbm,
                    g2_ref, w_in_hbm, w_out_hbm, gf_ref, o_ref,
                    wa_ref, wb_ref, wo_ref, w_in_ref, w_out_ref, *, final_norm):
    _load_weights_once([(wa_hbm, wa_ref), (wb_hbm, wb_ref), (wo_hbm, wo_ref),
                        (w_in_hbm, w_in_ref), (w_out_hbm, w_out_ref)])
    d = x1_ref.shape[1]
    merged = (gates_ref[:, :d] * _dot(ya_ref[...], wa_ref[...])
              + gates_ref[:, d:] * _dot(yb_ref[...], wb_ref[...]))
    x2 = x1_ref[...] + _dot(merged.astype(jnp.bfloat16), wo_ref[...])
    h = _rms(x2, g2_ref[...]).astype(jnp.bfloat16)
    x3 = x2 + 0.5 * _swiglu(h, w_in_ref, w_out_ref)
    o_ref[...] = _rms(x3, gf_ref[...]) if final_norm else x3


def _resident(shape):
    return pl.BlockSpec(shape, lambda *_: (0,) * len(shape), pipeline_mode=pl.Buffered(1))


_IN_HBM = pl.BlockSpec(memory_space=pl.ANY)


def _bf16_scratch(*weights):
    return [pltpu.VMEM(w.shape, jnp.bfloat16) for w in weights]


def _rows(tm, width):
    return pl.BlockSpec((tm, width), lambda i: (i, 0))


def _ffn_mix(x, g1, w_in, w_out, gm, w_mix, b_gate, width, seq):
    m, d = x.shape
    tm = ROW_TILE
    bf = jnp.bfloat16
    n_pairs = width // PAIR
    steps_per_seq = seq // tm

    def transposed(blk):
        shape = (m // seq, seq // blk, n_pairs, PAIR, blk)
        spec = pl.BlockSpec((1, tm // blk) + shape[2:], lambda i: (i // steps_per_seq, i % steps_per_seq, 0, 0, 0))
        return jax.ShapeDtypeStruct(shape, bf), spec

    keys = jax.ShapeDtypeStruct((m, width), bf), _rows(tm, width)
    outs = [(jax.ShapeDtypeStruct((m, d), jnp.float32), _rows(tm, d)),
            transposed(A_BLK), keys, transposed(A_BLK), transposed(B_BLK), keys, transposed(B_BLK),
            (jax.ShapeDtypeStruct((m, b_gate.shape[1]), jnp.float32), _rows(tm, b_gate.shape[1]))]
    return pl.pallas_call(
        _ffn_mix_kernel,
        grid=(m // tm,),
        in_specs=[_rows(tm, d), _resident(g1.shape), _IN_HBM, _IN_HBM,
                  _resident(gm.shape), _IN_HBM, _resident(b_gate.shape)],
        out_specs=[spec for _, spec in outs],
        out_shape=[shape for shape, _ in outs],
        scratch_shapes=_bf16_scratch(w_in, w_out, w_mix),
        compiler_params=pltpu.CompilerParams(dimension_semantics=("arbitrary",),
                                             vmem_limit_bytes=V7X_VMEM_LIMIT),
        name="ffn_mix",
    )(x, g1, w_in, w_out, gm, w_mix, b_gate)


def _out_ffn(x1, ya, yb, gates, wa, wb, wo, g2, w_in, w_out, gf, final_norm):
    m, d = x1.shape
    tm = ROW_TILE
    return pl.pallas_call(
        functools.partial(_out_ffn_kernel, final_norm=final_norm),
        grid=(m // tm,),
        in_specs=[_rows(tm, d), _rows(tm, ya.shape[1]), _rows(tm, yb.shape[1]), _rows(tm, gates.shape[1]),
                  _IN_HBM, _IN_HBM, _IN_HBM, _resident(g2.shape), _IN_HBM, _IN_HBM, _resident(gf.shape)],
        out_specs=_rows(tm, d),
        out_shape=jax.ShapeDtypeStruct((m, d), jnp.float32),
        scratch_shapes=_bf16_scratch(wa, wb, wo, w_in, w_out),
        compiler_params=pltpu.CompilerParams(dimension_semantics=("arbitrary",),
                                             vmem_limit_bytes=V7X_VMEM_LIMIT),
        name="out_ffn",
    )(x1, ya, yb, gates, wa, wb, wo, g2, w_in, w_out, gf)


def _t5_bucket(rel):
    nb = T5_BUCKETS // 2
    ret = jnp.where(rel > 0, nb, 0)
    n = jnp.abs(rel)
    max_exact = nb // 2
    nf = jnp.maximum(n, 1).astype(jnp.float32)
    large = max_exact + (jnp.log(nf / max_exact) / math.log(T5_MAX_DIST / max_exact)
                         * (nb - max_exact)).astype(jnp.int32)
    large = jnp.minimum(large, nb - 1)
    return ret + jnp.where(n < max_exact, n, large)


def _window_rel(n_tiles, blk):
    nb = blk // LANES
    u = jnp.arange(-(nb - 1), n_tiles * nb, dtype=jnp.int32)[:, None]
    return LANES - jnp.arange(2 * LANES, dtype=jnp.int32)[None, :] - LANES * u


def _bias_tiles_kernel(rel_bias_ref, tiles_ref, *, left_chunks):
    n_pairs, n_tiles, blk, _ = tiles_ref.shape
    nb = blk // LANES
    kj = lax.broadcasted_iota(jnp.int32, (LANES, LANES), 0)
    qi = lax.broadcasted_iota(jnp.int32, (LANES, LANES), 1)
    chunk_diff = kj // CHUNK - qi // CHUNK
    for p in range(n_pairs):
        for half in range(2):
            for w in range(rel_bias_ref.shape[2]):
                u = w - (nb - 1)
                row = jnp.broadcast_to(rel_bias_ref[p, half, w], (LANES, 2 * LANES))
                sub = pltpu.roll(row, 0, 1, stride=1, stride_axis=0)[:, LANES:]
                for t in range(n_tiles):
                    for a in range(nb):
                        c = t * nb + a - u
                        if not 0 <= c < nb:
                            continue
                        diff = chunk_diff - u * SUB_CHUNKS
                        allowed = diff <= 0
                        if left_chunks is not None:
                            allowed = allowed & (diff >= -left_chunks)
                        tiles_ref[p, t, pl.ds(c * LANES, LANES), pl.ds(half * blk + a * LANES, LANES)] = (
                            jnp.where(allowed, sub * LOG2E, NEG))


def _bias_tiles(rel_bias, n_tiles, blk, left_chunks):
    n_pairs = rel_bias.shape[0]
    return pl.pallas_call(
        functools.partial(_bias_tiles_kernel, left_chunks=left_chunks),
        out_shape=jax.ShapeDtypeStruct((n_pairs, n_tiles, blk, 2 * blk), jnp.float32),
        compiler_params=pltpu.CompilerParams(vmem_limit_bytes=V7X_VMEM_LIMIT),
        name="bias_tiles",
    )(rel_bias)


def _sum_rows(x):
    acc = x[:SUBLANES]
    for r in range(1, x.shape[0] // SUBLANES):
        acc = acc + x[r * SUBLANES:(r + 1) * SUBLANES]
    return acc


def _fold_rows(x):
    m = x[:SUBLANES]
    for r in range(1, x.shape[0] // SUBLANES):
        m = jnp.maximum(m, x[r * SUBLANES:(r + 1) * SUBLANES])
    return m


def _attn_kernel(*refs, left_blocks, lambda_init):
    if lambda_init is None:
        qt_ref, k_ref, vt_ref, tiles_ref, o_ref, q2t_ref, s_ref, acc_ref = refs
    else:
        (qt_ref, k_ref, vt_ref, tiles_ref, lq1_ref, lk1_ref, lq2_ref, lk2_ref, subg_ref, o_ref,
         q2t_ref, *s_refs, m_ref, acc_ref) = refs
        lam = (jnp.exp(jnp.sum(lq1_ref[...] * lk1_ref[...], axis=-1, keepdims=True))
               - jnp.exp(jnp.sum(lq2_ref[...] * lk2_ref[...], axis=-1, keepdims=True))
               + lambda_init)
    n_q, n_pairs, _, blk = qt_ref.shape[1:]
    top_half = lax.broadcasted_iota(jnp.int32, (PAIR, blk), 0) < HEAD_DIM

    def cols(p):
        return slice(p * PAIR, (p + 1) * PAIR)

    def key_rows(j, n=1):
        return pl.ds(pl.multiple_of(j * blk, blk), n * blk)

    def weights_update(vt, e):
        return jnp.concatenate([_dot(vt, e.astype(jnp.bfloat16)), _sum_rows(e)], axis=0)

    def stack_queries(c, after=None):
        for p in range(n_pairs):
            qt = qt_ref[0, c, p]
            if after is not None:
                dep = after[p] if isinstance(after, list) else after
                dep = jnp.concatenate([dep] * (blk // dep.shape[1]), axis=1)
                qt = qt + (dep * 0.0).astype(qt.dtype)
            zero = jnp.zeros_like(qt)
            q2t_ref[p, :, :blk] = jnp.where(top_half, qt, zero)
            q2t_ref[p, :, blk:] = jnp.where(top_half, zero, qt)

    def finish(c):
        late = []
        for p in range(n_pairs):
            o = acc_ref[p, :PAIR] / jnp.sum(acc_ref[p, PAIR:], axis=0, keepdims=True)
            if lambda_init is None:
                y = jnp.where(top_half, o[:, :blk], o[:, blk:]).T
            else:
                d = o[:, :blk] - lam * o[:, blk:]
                d = d * lax.rsqrt(jnp.mean(d * d, axis=0, keepdims=True) + EPS)
                y = d.T * subg_ref[...] * (1.0 - lambda_init)
            o_ref[0, c * blk:(c + 1) * blk, cols(p)] = y.astype(o_ref.dtype)
            late.append(y[:1])
        return late

    if left_blocks is None:
        n_tiles = tiles_ref.shape[1]
        ring = len(s_refs)

        def causal_prefix(c, i, readers, after):
            stack_queries(c, after)

            def scores(j, last_readers):
                tile = min(i - j, n_tiles - 1)
                for p in range(n_pairs):
                    kb = k_ref[0, j * blk:(j + 1) * blk, cols(p)]
                    if last_readers is not None:
                        kb = kb + (last_readers[p] * 0.0).astype(kb.dtype)
                    s_refs[j % ring][p] = _dot(kb, q2t_ref[p]) + tiles_ref[p, tile]

            def online(j):
                late = []
                for p in range(n_pairs):
                    s = s_refs[j % ring][p]
                    m = jnp.max(_fold_rows(s), axis=0, keepdims=True)
                    if j == 0:
                        update = weights_update(vt_ref[0, j, p], jnp.exp2(s - m))
                        acc_ref[p] = update
                    else:
                        m_old = m_ref[p]
                        m = jnp.maximum(m_old, m)
                        update = weights_update(vt_ref[0, j, p], jnp.exp2(s - m))
                        acc_ref[p] = jnp.exp2(m_old - m) * acc_ref[p] + update
                    m_ref[p] = m
                    late.append(update[PAIR:PAIR + 1, :PAIR])
                return late

            scores(0, readers[0])
            for j in range(i + 1):
                if j < i:
                    scores(j + 1, readers[(j + 1) % ring])
                readers[j % ring] = online(j)
            finish(c)
            return readers[i % ring]

        def grid_step(step):
            readers, after = [None] * ring, None
            for c in range(n_q):
                after = causal_prefix(c, step * n_q + c, readers, after)

        for step in range(k_ref.shape[1] // (n_q * blk)):
            pl.when(pl.program_id(1) == step)(functools.partial(grid_step, step))
    else:
        def band(i, n_left):
            order = range(n_left, -1, -1)
            ms = [jnp.full((SUBLANES, 2 * blk), NEG, jnp.float32) for _ in range(n_pairs)]
            for d in order:
                for p in range(n_pairs):
                    s = _dot(k_ref[0, key_rows(i - d), cols(p)], q2t_ref[p]) + tiles_ref[p, d]
                    s_ref[d, p] = s
                    ms[p] = jnp.maximum(ms[p], _fold_rows(s))
            ms = [jnp.max(m, axis=0, keepdims=True) for m in ms]
            for d in order:
                for p in range(n_pairs):
                    update = weights_update(vt_ref[0, i - d, p], jnp.exp2(s_ref[d, p] - ms[p]))
                    acc_ref[p] = update if d == n_left else acc_ref[p] + update
            return update[PAIR:PAIR + 1, :blk]

        def grid_step(first):
            after = None
            for c in range(n_q):
                stack_queries(c, after)
                after = band(pl.program_id(1) * n_q + c, min(c, left_blocks) if first else left_blocks)
                finish(c)

        assert n_q >= left_blocks
        pl.when(pl.program_id(1) == 0)(functools.partial(grid_step, True))
        pl.when(pl.program_id(1) > 0)(functools.partial(grid_step, False))


def _attention(qt, k, vt, tiles, left_blocks, lambda_init=None, extra=()):
    b, s, w = k.shape
    n_pairs, _, blk = qt.shape[2:]
    n_q = B_STEP_BLOCKS if left_blocks is None else A_STEP_BLOCKS
    blk_q = pl.BlockSpec((1, n_q) + qt.shape[2:], lambda bi, i: (bi, i, 0, 0, 0))
    whole_k = pl.BlockSpec((1, s, w), lambda bi, i: (bi, 0, 0))
    whole_v = pl.BlockSpec((1,) + vt.shape[1:], lambda bi, i: (bi, 0, 0, 0, 0))
    f32 = jnp.float32
    score_block = (n_pairs, blk, 2 * blk)
    if left_blocks is None:
        score_scratch = [pltpu.VMEM(score_block, f32)] * B_SCORE_RING + [pltpu.VMEM((n_pairs, 1, 2 * blk), f32)]
    else:
        score_scratch = [pltpu.VMEM((left_blocks + 1,) + score_block, f32)]
    return pl.pallas_call(
        functools.partial(_attn_kernel, left_blocks=left_blocks, lambda_init=lambda_init),
        grid=(b, s // (n_q * blk)),
        in_specs=[blk_q, whole_k, whole_v, _resident(tiles.shape)] + [_resident(e.shape) for e in extra],
        out_specs=pl.BlockSpec((1, n_q * blk, w), lambda bi, i: (bi, i, 0)),
        out_shape=jax.ShapeDtypeStruct((b, s, w), jnp.bfloat16),
        scratch_shapes=[pltpu.VMEM((n_pairs, PAIR, 2 * blk), jnp.bfloat16), *score_scratch,
                        pltpu.VMEM((n_pairs, ACC_ROWS, 2 * blk), f32)],
        compiler_params=pltpu.CompilerParams(dimension_semantics=("arbitrary", "arbitrary"),
                                             vmem_limit_bytes=V7X_VMEM_LIMIT),
        name="attn_a" if lambda_init is None else "attn_b",
    )(qt, k, vt, tiles, *extra)


def kernel(x, ffn1_norm, ffn1_w_in, ffn1_w_out, mix_norm, w_mix_in, b_gate, rel_bias_a, lambda_q1, lambda_k1, lambda_q2, lambda_k2, subln_g, t5_bias, w_branch_a, w_branch_b, w_o, ffn2_norm, ffn2_w_in, ffn2_w_out, final_norm):
    b, s, d = x.shape
    depth = ffn1_norm.shape[0]
    assert depth >= 1
    width = w_branch_a.shape[1]
    n_pairs = width // PAIR
    f32 = jnp.float32

    t5_rel = t5_bias.astype(f32)[:, _t5_bucket(_window_rel(B_TILES, B_BLK))]
    t5_rel = jnp.broadcast_to(t5_rel[:, None, :, None, :], (n_pairs, 2) + t5_rel.shape[1:2] + (1, 2 * LANES))
    tiles_b = _bias_tiles(t5_rel, B_TILES, B_BLK, None)

    xf = x.astype(f32).reshape(b * s, d)
    for li in range(depth):
        x1, qa, ka, va, qb, kb, vb, gates = _ffn_mix(
            xf, ffn1_norm[li][None].astype(f32), ffn1_w_in[li].astype(f32), ffn1_w_out[li].astype(f32),
            mix_norm[li][None].astype(f32), w_mix_in[li].astype(f32), b_gate[li][None].astype(f32), width, s)

        rel_a = jnp.clip(_window_rel(A_TILES, A_BLK), -REL_CLIP, REL_CLIP) + REL_CLIP
        a_rel = rel_bias_a[li].astype(f32)[:, rel_a]
        tiles_a = _bias_tiles(a_rel.reshape(n_pairs, 2, rel_a.shape[0], 1, 2 * LANES), A_TILES, A_BLK, LEFT_CHUNKS)

        def seq(t):
            return t.reshape(b, s, width)

        ya = _attention(qa, seq(ka), va, tiles_a, A_TILES - 1)
        lambda_init = 0.8 - 0.6 * math.exp(-0.3 * li)
        lams = [t[li][None].astype(f32) for t in (lambda_q1, lambda_k1, lambda_q2, lambda_k2)]
        yb = _attention(qb, seq(kb), vb, tiles_b, None, lambda_init=lambda_init,
                        extra=(*lams, subln_g[li][None].astype(f32)))

        xf = _out_ffn(x1, ya.reshape(b * s, width), yb.reshape(b * s, width), gates,
                      w_branch_a[li].astype(f32), w_branch_b[li].astype(f32), w_o[li].astype(f32),
                      ffn2_norm[li][None].astype(f32), ffn2_w_in[li].astype(f32), ffn2_w_out[li].astype(f32),
                      final_norm[None].astype(f32), final_norm=(li == depth - 1))
    return xf.reshape(b, s, d).astype(x.dtype)
```

```python
import functools
import math

import jax
import jax.numpy as jnp
from jax import lax
from jax.experimental import pallas as pl
from jax.experimental.pallas import tpu as pltpu

EPS = 1e-6
NEG = -1e30
LOG2E = math.log2(math.e)

CHUNK = 64
LEFT_CHUNKS = 8
REL_CLIP = 128
T5_BUCKETS = 32
T5_MAX_DIST = 128

LANES = 128
SUBLANES = 8
HEAD_DIM = 64
PAIR = 2 * HEAD_DIM
SUB_CHUNKS = LANES // CHUNK
ACC_ROWS = PAIR + SUBLANES

A_BLK = 128
A_TILES = LEFT_CHUNKS * CHUNK // A_BLK + 1
B_BLK = 256
B_TILES = 3
B_SCORE_RING = 3
A_STEP_BLOCKS = 16
B_STEP_BLOCKS = 2

V7X_VMEM_LIMIT = 60000 * 1024

ROW_TILE = 512
BF16_ROWS = 16
CAST_CHUNK_BYTES = 3 << 18
CAST_SLOTS = 4


def _rms(xf, g):
    return xf * lax.rsqrt(jnp.mean(xf * xf, axis=-1, keepdims=True) + EPS) * g


def _dot(a, b):
    return jnp.dot(a, b, preferred_element_type=jnp.float32)


def _swiglu(h, w_in_ref, w_out_ref):
    d_ff = w_out_ref.shape[0]
    gate = _dot(h, w_in_ref[:, :d_ff])
    up = _dot(h, w_in_ref[:, d_ff:])
    act = (gate * jax.nn.sigmoid(gate) * up).astype(jnp.bfloat16)
    return _dot(act, w_out_ref[...])


def _cast_chunk_rows(rows, width):
    fits = [r for r in range(BF16_ROWS, rows + 1, BF16_ROWS)
            if rows % r == 0 and r * width * 4 <= CAST_CHUNK_BYTES]
    return max(fits)


def _load_cast(src_hbm, dst_ref):
    rows, width = src_hbm.shape
    chunk = _cast_chunk_rows(rows, width)
    n_chunks = rows // chunk

    ahead = min(CAST_SLOTS, n_chunks) - 1

    def body(stage_ref, sem):
        def copy(c):
            slot = c % CAST_SLOTS
            return pltpu.make_async_copy(src_hbm.at[pl.ds(c * chunk, chunk)], stage_ref.at[slot], sem.at[slot])

        for c in range(ahead):
            copy(c).start()

        def one_chunk(c, carry):
            @pl.when(c + ahead < n_chunks)
            def _():
                copy(c + ahead).start()

            copy(c).wait()
            dst_ref[pl.ds(pl.multiple_of(c * chunk, chunk), chunk)] = (
                stage_ref[c % CAST_SLOTS].astype(dst_ref.dtype))
            return carry

        lax.fori_loop(0, n_chunks, one_chunk, 0)

    pl.run_scoped(body, pltpu.VMEM((CAST_SLOTS, chunk, width), src_hbm.dtype),
                  pltpu.SemaphoreType.DMA((CAST_SLOTS,)))


def _load_weights_once(pairs):
    @pl.when(pl.program_id(0) == 0)
    def _():
        for src_hbm, dst_ref in pairs:
            _load_cast(src_hbm, dst_ref)


def _ffn_mix_kernel(x_ref, g1_ref, w_in_hbm, w_out_hbm, gm_ref, w_mix_hbm, b_gate_ref,
                    x1_ref, qa_ref, ka_ref, va_ref, qb_ref, kb_ref, vb_ref, gates_ref,
                    w_in_ref, w_out_ref, w_mix_ref):
    _load_weights_once([(w_in_hbm, w_in_ref), (w_out_hbm, w_out_ref), (w_mix_hbm, w_mix_ref)])
    x = x_ref[...]
    h = _rms(x, g1_ref[...]).astype(jnp.bfloat16)
    x1 = x + 0.5 * _swiglu(h, w_in_ref, w_out_ref)
    x1_ref[...] = x1

    u = _rms(x1, gm_ref[...]).astype(jnp.bfloat16)
    width = ka_ref.shape[1]
    scale = HEAD_DIM ** -0.5 * LOG2E
    outs = (qa_ref, ka_ref, va_ref, qb_ref, kb_ref, vb_ref)
    for n, o_ref in enumerate(outs):
        p = _dot(u, w_mix_ref[:, n * width:(n + 1) * width])
        if o_ref is ka_ref or o_ref is kb_ref:
            o_ref[...] = p.astype(o_ref.dtype)
        elif o_ref is qa_ref or o_ref is qb_ref:
            _store_transposed(p * scale, o_ref)
        else:
            _store_transposed(p, o_ref)
    logits = _dot(u, w_mix_ref[:, len(outs) * width:]) + b_gate_ref[...]
    gates_ref[...] = jax.nn.sigmoid(logits)


def _store_transposed(p, o_ref):
    n_blk, n_pairs, _, blk = o_ref.shape[1:]
    pt = p.T
    for c in range(n_blk):
        for pr in range(n_pairs):
            o_ref[0, c, pr] = pt[pr * PAIR:(pr + 1) * PAIR, c * blk:(c + 1) * blk].astype(o_ref.dtype)


def _out_ffn_kernel(x1_ref, ya_ref, yb_ref, gates_ref, wa_hbm, wb_hbm, wo_hbm,
                    g2_ref, w_in_hbm, w_out_hbm, gf_ref, o_ref,
                    wa_ref, wb_ref, wo_ref, w_in_ref, w_out_ref, *, final_norm):
    _load_weights_once([(wa_hbm, wa_ref), (wb_hbm, wb_ref), (wo_hbm, wo_ref),
                        (w_in_hbm, w_in_ref), (w_out_hbm, w_out_ref)])
    d = x1_ref.shape[1]
    merged = (gates_ref[:, :d] * _dot(ya_ref[...], wa_ref[...])
              + gates_ref[:, d:] * _dot(yb_ref[...], wb_ref[...]))
    x2 = x1_ref[...] + _dot(merged.astype(jnp.bfloat16), wo_ref[...])
    h = _rms(x2, g2_ref[...]).astype(jnp.bfloat16)
    x3 = x2 + 0.5 * _swiglu(h, w_in_ref, w_out_ref)
    o_ref[...] = _rms(x3, gf_ref[...]) if final_norm else x3


def _resident(shape):
    return pl.BlockSpec(shape, lambda *_: (0,) * len(shape), pipeline_mode=pl.Buffered(1))


_IN_HBM = pl.BlockSpec(memory_space=pl.ANY)


def _bf16_scratch(*weights):
    return [pltpu.VMEM(w.shape, jnp.bfloat16) for w in weights]


def _rows(tm, width):
    return pl.BlockSpec((tm, width), lambda i: (i, 0))


def _ffn_mix(x, g1, w_in, w_out, gm, w_mix, b_gate, width, seq):
    m, d = x.shape
    tm = ROW_TILE
    bf = jnp.bfloat16
    n_pairs = width // PAIR
    steps_per_seq = seq // tm

    def transposed(blk):
        shape = (m // seq, seq // blk, n_pairs, PAIR, blk)
        spec = pl.BlockSpec((1, tm // blk) + shape[2:], lambda i: (i // steps_per_seq, i % steps_per_seq, 0, 0, 0))
        return jax.ShapeDtypeStruct(shape, bf), spec

    keys = jax.ShapeDtypeStruct((m, width), bf), _rows(tm, width)
    outs = [(jax.ShapeDtypeStruct((m, d), jnp.float32), _rows(tm, d)),
            transposed(A_BLK), keys, transposed(A_BLK), transposed(B_BLK), keys, transposed(B_BLK),
            (jax.ShapeDtypeStruct((m, b_gate.shape[1]), jnp.float32), _rows(tm, b_gate.shape[1]))]
    return pl.pallas_call(
        _ffn_mix_kernel,
        grid=(m // tm,),
        in_specs=[_rows(tm, d), _resident(g1.shape), _IN_HBM, _IN_HBM,
                  _resident(gm.shape), _IN_HBM, _resident(b_gate.shape)],
        out_specs=[spec for _, spec in outs],
        out_shape=[shape for shape, _ in outs],
        scratch_shapes=_bf16_scratch(w_in, w_out, w_mix),
        compiler_params=pltpu.CompilerParams(dimension_semantics=("arbitrary",),
                                             vmem_limit_bytes=V7X_VMEM_LIMIT),
        name="ffn_mix",
    )(x, g1, w_in, w_out, gm, w_mix, b_gate)


def _out_ffn(x1, ya, yb, gates, wa, wb, wo, g2, w_in, w_out, gf, final_norm):
    m, d = x1.shape
    tm = ROW_TILE
    return pl.pallas_call(
        functools.partial(_out_ffn_kernel, final_norm=final_norm),
        grid=(m // tm,),
        in_specs=[_rows(tm, d), _rows(tm, ya.shape[1]), _rows(tm, yb.shape[1]), _rows(tm, gates.shape[1]),
                  _IN_HBM, _IN_HBM, _IN_HBM, _resident(g2.shape), _IN_HBM, _IN_HBM, _resident(gf.shape)],
        out_specs=_rows(tm, d),
        out_shape=jax.ShapeDtypeStruct((m, d), jnp.float32),
        scratch_shapes=_bf16_scratch(wa, wb, wo, w_in, w_out),
        compiler_params=pltpu.CompilerParams(dimension_semantics=("arbitrary",),
                                             vmem_limit_bytes=V7X_VMEM_LIMIT),
        name="out_ffn",
    )(x1, ya, yb, gates, wa, wb, wo, g2, w_in, w_out, gf)


def _t5_bucket(rel):
    nb = T5_BUCKETS // 2
    ret = jnp.where(rel > 0, nb, 0)
    n = jnp.abs(rel)
    max_exact = nb // 2
    nf = jnp.maximum(n, 1).astype(jnp.float32)
    large = max_exact + (jnp.log(nf / max_exact) / math.log(T5_MAX_DIST / max_exact)
                         * (nb - max_exact)).astype(jnp.int32)
    large = jnp.minimum(large, nb - 1)
    return ret + jnp.where(n < max_exact, n, large)


def _window_rel(n_tiles, blk):
    nb = blk // LANES
    u = jnp.arange(-(nb - 1), n_tiles * nb, dtype=jnp.int32)[:, None]
    return LANES - jnp.arange(2 * LANES, dtype=jnp.int32)[None, :] - LANES * u


def _bias_tiles_kernel(rel_bias_ref, tiles_ref, *, left_chunks):
    n_pairs, n_tiles, blk, _ = tiles_ref.shape
    nb = blk // LANES
    kj = lax.broadcasted_iota(jnp.int32, (LANES, LANES), 0)
    qi = lax.broadcasted_iota(jnp.int32, (LANES, LANES), 1)
    chunk_diff = kj // CHUNK - qi // CHUNK
    for p in range(n_pairs):
        for half in range(2):
            for w in range(rel_bias_ref.shape[2]):
                u = w - (nb - 1)
                row = jnp.broadcast_to(rel_bias_ref[p, half, w], (LANES, 2 * LANES))
                sub = pltpu.roll(row, 0, 1, stride=1, stride_axis=0)[:, LANES:]
                for t in range(n_tiles):
                    for a in range(nb):
                        c = t * nb + a - u
                        if not 0 <= c < nb:
                            continue
                        diff = chunk_diff - u * SUB_CHUNKS
                        allowed = diff <= 0
                        if left_chunks is not None:
                            allowed = allowed & (diff >= -left_chunks)
                        tiles_ref[p, t, pl.ds(c * LANES, LANES), pl.ds(half * blk + a * LANES, LANES)] = (
                            jnp.where(allowed, sub * LOG2E, NEG))


def _bias_tiles(rel_bias, n_tiles, blk, left_chunks):
    n_pairs = rel_bias.shape[0]
    return pl.pallas_call(
        functools.partial(_bias_tiles_kernel, left_chunks=left_chunks),
        out_shape=jax.ShapeDtypeStruct((n_pairs, n_tiles, blk, 2 * blk), jnp.float32),
        compiler_params=pltpu.CompilerParams(vmem_limit_bytes=V7X_VMEM_LIMIT),
        name="bias_tiles",
    )(rel_bias)


def _sum_rows(x):
    acc = x[:SUBLANES]
    for r in range(1, x.shape[0] // SUBLANES):
        acc = acc + x[r * SUBLANES:(r + 1) * SUBLANES]
    return acc


def _fold_rows(x):
    m = x[:SUBLANES]
    for r in range(1, x.shape[0] // SUBLANES):
        m = jnp.maximum(m, x[r * SUBLANES:(r + 1) * SUBLANES])
    return m


def _attn_kernel(*refs, left_blocks, lambda_init):
    if lambda_init is None:
        qt_ref, k_ref, vt_ref, tiles_ref, o_ref, q2t_ref, s_ref, acc_ref = refs
    else:
        (qt_ref, k_ref, vt_ref, tiles_ref, lq1_ref, lk1_ref, lq2_ref, lk2_ref, subg_ref, o_ref,
         q2t_ref, *s_refs, m_ref, acc_ref) = refs
        lam = (jnp.exp(jnp.sum(lq1_ref[...] * lk1_ref[...], axis=-1, keepdims=True))
               - jnp.exp(jnp.sum(lq2_ref[...] * lk2_ref[...], axis=-1, keepdims=True))
               + lambda_init)
    n_q, n_pairs, _, blk = qt_ref.shape[1:]
    top_half = lax.broadcasted_iota(jnp.int32, (PAIR, blk), 0) < HEAD_DIM

    def cols(p):
        return slice(p * PAIR, (p + 1) * PAIR)

    def key_rows(j, n=1):
        return pl.ds(pl.multiple_of(j * blk, blk), n * blk)

    def weights_update(vt, e):
        return jnp.concatenate([_dot(vt, e.astype(jnp.bfloat16)), _sum_rows(e)], axis=0)

    def stack_queries(c, after=None):
        for p in range(n_pairs):
            qt = qt_ref[0, c, p]
            if after is not None:
                dep = after[p] if isinstance(after, list) else after
                dep = jnp.concatenate([dep] * (blk // dep.shape[1]), axis=1)
                qt = qt + (dep * 0.0).astype(qt.dtype)
            zero = jnp.zeros_like(qt)
            q2t_ref[p, :, :blk] = jnp.where(top_half, qt, zero)
            q2t_ref[p, :, blk:] = jnp.where(top_half, zero, qt)

    def finish(c):
        late = []
        for p in range(n_pairs):
            o = acc_ref[p, :PAIR] / jnp.sum(acc_ref[p, PAIR:], axis=0, keepdims=True)
            if lambda_init is None:
                y = jnp.where(top_half, o[:, :blk], o[:, blk:]).T
            else:
                d = o[:, :blk] - lam * o[:, blk:]
                d = d * lax.rsqrt(jnp.mean(d * d, axis=0, keepdims=True) + EPS)
                y = d.T * subg_ref[...] * (1.0 - lambda_init)
            o_ref[0, c * blk:(c + 1) * blk, cols(p)] = y.astype(o_ref.dtype)
            late.append(y[:1])
        return late

    if left_blocks is None:
        n_tiles = tiles_ref.shape[1]
        ring = len(s_refs)

        def causal_prefix(c, i, readers, after):
            stack_queries(c, after)

            def scores(j, last_readers):
                tile = min(i - j, n_tiles - 1)
                for p in range(n_pairs):
                    kb = k_ref[0, j * blk:(j + 1) * blk, cols(p)]
                    if last_readers is not None:
                        kb = kb + (last_readers[p] * 0.0).astype(kb.dtype)
                    s_refs[j % ring][p] = _dot(kb, q2t_ref[p]) + tiles_ref[p, tile]

            def online(j):
                late = []
                for p in range(n_pairs):
                    s = s_refs[j % ring][p]
                    m = jnp.max(_fold_rows(s), axis=0, keepdims=True)
                    if j == 0:
                        update = weights_update(vt_ref[0, j, p], jnp.exp2(s - m))
                        acc_ref[p] = update
                    else:
                        m_old = m_ref[p]
                        m = jnp.maximum(m_old, m)
                        update = weights_update(vt_ref[0, j, p], jnp.exp2(s - m))
                        acc_ref[p] = jnp.exp2(m_old - m) * acc_ref[p] + update
                    m_ref[p] = m
                    late.append(update[PAIR:PAIR + 1, :PAIR])
                return late

            scores(0, readers[0])
            for j in range(i + 1):
                if j < i:
                    scores(j + 1, readers[(j + 1) % ring])
                readers[j % ring] = online(j)
            finish(c)
            return readers[i % ring]

        def grid_step(step):
            readers, after = [None] * ring, None
            for c in range(n_q):
                after = causal_prefix(c, step * n_q + c, readers, after)

        for step in range(k_ref.shape[1] // (n_q * blk)):
            pl.when(pl.program_id(1) == step)(functools.partial(grid_step, step))
    else:
        def band(i, n_left):
            order = range(n_left, -1, -1)
            ms = [jnp.full((SUBLANES, 2 * blk), NEG, jnp.float32) for _ in range(n_pairs)]
            for d in order:
                for p in range(n_pairs):
                    s = _dot(k_ref[0, key_rows(i - d), cols(p)], q2t_ref[p]) + tiles_ref[p, d]
                    s_ref[d, p] = s
                    ms[p] = jnp.maximum(ms[p], _fold_rows(s))
            ms = [jnp.max(m, axis=0, keepdims=True) for m in ms]
            for d in order:
                for p in range(n_pairs):
                    update = weights_update(vt_ref[0, i - d, p], jnp.exp2(s_ref[d, p] - ms[p]))
                    acc_ref[p] = update if d == n_left else acc_ref[p] + update
            return update[PAIR:PAIR + 1, :blk]

        def grid_step(first):
            after = None
            for c in range(n_q):
                stack_queries(c, after)
                after = band(pl.program_id(1) * n_q + c, min(c, left_blocks) if first else left_blocks)
                finish(c)

        assert n_q >= left_blocks
        pl.when(pl.program_id(1) == 0)(functools.partial(grid_step, True))
        pl.when(pl.program_id(1) > 0)(functools.partial(grid_step, False))


def _attention(qt, k, vt, tiles, left_blocks, lambda_init=None, extra=()):
    b, s, w = k.shape
    n_pairs, _, blk = qt.shape[2:]
    n_q = B_STEP_BLOCKS if left_blocks is None else A_STEP_BLOCKS
    blk_q = pl.BlockSpec((1, n_q) + qt.shape[2:], lambda bi, i: (bi, i, 0, 0, 0))
    whole_k = pl.BlockSpec((1, s, w), lambda bi, i: (bi, 0, 0))
    whole_v = pl.BlockSpec((1,) + vt.shape[1:], lambda bi, i: (bi, 0, 0, 0, 0))
    f32 = jnp.float32
    score_block = (n_pairs, blk, 2 * blk)
    if left_blocks is None:
        score_scratch = [pltpu.VMEM(score_block, f32)] * B_SCORE_RING + [pltpu.VMEM((n_pairs, 1, 2 * blk), f32)]
    else:
        score_scratch = [pltpu.VMEM((left_blocks + 1,) + score_block, f32)]
    return pl.pallas_call(
        functools.partial(_attn_kernel, left_blocks=left_blocks, lambda_init=lambda_init),
        grid=(b, s // (n_q * blk)),
        in_specs=[blk_q, whole_k, whole_v, _resident(tiles.shape)] + [_resident(e.shape) for e in extra],
        out_specs=pl.BlockSpec((1, n_q * blk, w), lambda bi, i: (bi, i, 0)),
        out_shape=jax.ShapeDtypeStruct((b, s, w), jnp.bfloat16),
        scratch_shapes=[pltpu.VMEM((n_pairs, PAIR, 2 * blk), jnp.bfloat16), *score_scratch,
                        pltpu.VMEM((n_pairs, ACC_ROWS, 2 * blk), f32)],
        compiler_params=pltpu.CompilerParams(dimension_semantics=("arbitrary", "arbitrary"),
                                             vmem_limit_bytes=V7X_VMEM_LIMIT),
        name="attn_a" if lambda_init is None else "attn_b",
    )(qt, k, vt, tiles, *extra)


def kernel(x, ffn1_norm, ffn1_w_in, ffn1_w_out, mix_norm, w_mix_in, b_gate, rel_bias_a, lambda_q1, lambda_k1, lambda_q2, lambda_k2, subln_g, t5_bias, w_branch_a, w_branch_b, w_o, ffn2_norm, ffn2_w_in, ffn2_w_out, final_norm):
    b, s, d = x.shape
    depth = ffn1_norm.shape[0]
    assert depth >= 1
    width = w_branch_a.shape[1]
    n_pairs = width // PAIR
    f32 = jnp.float32

    t5_rel = t5_bias.astype(f32)[:, _t5_bucket(_window_rel(B_TILES, B_BLK))]
    t5_rel = jnp.broadcast_to(t5_rel[:, None, :, None, :], (n_pairs, 2) + t5_rel.shape[1:2] + (1, 2 * LANES))
    tiles_b = _bias_tiles(t5_rel, B_TILES, B_BLK, None)

    xf = x.astype(f32).reshape(b * s, d)
    for li in range(depth):
        x1, qa, ka, va, qb, kb, vb, gates = _ffn_mix(
            xf, ffn1_norm[li][None].astype(f32), ffn1_w_in[li].astype(f32), ffn1_w_out[li].astype(f32),
            mix_norm[li][None].astype(f32), w_mix_in[li].astype(f32), b_gate[li][None].astype(f32), width, s)

        rel_a = jnp.clip(_window_rel(A_TILES, A_BLK), -REL_CLIP, REL_CLIP) + REL_CLIP
        a_rel = rel_bias_a[li].astype(f32)[:, rel_a]
        tiles_a = _bias_tiles(a_rel.reshape(n_pairs, 2, rel_a.shape[0], 1, 2 * LANES), A_TILES, A_BLK, LEFT_CHUNKS)

        def seq(t):
            return t.reshape(b, s, width)

        ya = _attention(qa, seq(ka), va, tiles_a, A_TILES - 1)
        lambda_init = 0.8 - 0.6 * math.exp(-0.3 * li)
        lams = [t[li][None].astype(f32) for t in (lambda_q1, lambda_k1, lambda_q2, lambda_k2)]
        yb = _attention(qb, seq(kb), vb, tiles_b, None, lambda_init=lambda_init,
                        extra=(*lams, subln_g[li][None].astype(f32)))

        xf = _out_ffn(x1, ya.reshape(b * s, width), yb.reshape(b * s, width), gates,
                      w_branch_a[li].astype(f32), w_branch_b[li].astype(f32), w_o[li].astype(f32),
                      ffn2_norm[li][None].astype(f32), ffn2_w_in[li].astype(f32), ffn2_w_out[li].astype(f32),
                      final_norm[None].astype(f32), final_norm=(li == depth - 1))
    return xf.reshape(b, s, d).astype(x.dtype)
```

```python
import functools
import math

import jax
import jax.numpy as jnp
from jax import lax
from jax.experimental import pallas as pl
from jax.experimental.pallas import tpu as pltpu

EPS = 1e-6
NEG = -1e30
LOG2E = math.log2(math.e)

CHUNK = 64
LEFT_CHUNKS = 8
REL_CLIP = 128
T5_BUCKETS = 32
T5_MAX_DIST = 128

LANES = 128
SUBLANES = 8
HEAD_DIM = 64
PAIR = 2 * HEAD_DIM
SUB_CHUNKS = LANES // CHUNK
ACC_ROWS = PAIR + SUBLANES

A_BLK = 128
A_TILES = LEFT_CHUNKS * CHUNK // A_BLK + 1
B_BLK = 256
B_TILES = 3
B_SCORE_RING = 3
A_STEP_BLOCKS = 8
B_STEP_BLOCKS = 4

V7X_VMEM_LIMIT = 60000 * 1024

ROW_TILE = 512
BF16_ROWS = 16
CAST_CHUNK_BYTES = 3 << 18
CAST_SLOTS = 4


def _rms(xf, g):
    return xf * lax.rsqrt(jnp.mean(xf * xf, axis=-1, keepdims=True) + EPS) * g


def _dot(a, b):
    return jnp.dot(a, b, preferred_element_type=jnp.float32)


def _swiglu(h, w_in_ref, w_out_ref):
    d_ff = w_out_ref.shape[0]
    gate = _dot(h, w_in_ref[:, :d_ff])
    up = _dot(h, w_in_ref[:, d_ff:])
    act = (gate * jax.nn.sigmoid(gate) * up).astype(jnp.bfloat16)
    return _dot(act, w_out_ref[...])


def _cast_chunk_rows(rows, width):
    fits = [r for r in range(BF16_ROWS, rows + 1, BF16_ROWS)
            if rows % r == 0 and r * width * 4 <= CAST_CHUNK_BYTES]
    return max(fits)


def _load_cast(src_hbm, dst_ref):
    rows, width = src_hbm.shape
    chunk = _cast_chunk_rows(rows, width)
    n_chunks = rows // chunk

    ahead = min(CAST_SLOTS, n_chunks) - 1

    def body(stage_ref, sem):
        def copy(c):
            slot = c % CAST_SLOTS
            return pltpu.make_async_copy(src_hbm.at[pl.ds(c * chunk, chunk)], stage_ref.at[slot], sem.at[slot])

        for c in range(ahead):
            copy(c).start()

        def one_chunk(c, carry):
            @pl.when(c + ahead < n_chunks)
            def _():
                copy(c + ahead).start()

            copy(c).wait()
            dst_ref[pl.ds(pl.multiple_of(c * chunk, chunk), chunk)] = (
                stage_ref[c % CAST_SLOTS].astype(dst_ref.dtype))
            return carry

        lax.fori_loop(0, n_chunks, one_chunk, 0)

    pl.run_scoped(body, pltpu.VMEM((CAST_SLOTS, chunk, width), src_hbm.dtype),
                  pltpu.SemaphoreType.DMA((CAST_SLOTS,)))


def _load_weights_once(pairs):
    @pl.when(pl.program_id(0) == 0)
    def _():
        for src_hbm, dst_ref in pairs:
            _load_cast(src_hbm, dst_ref)


def _ffn_mix_kernel(x_ref, g1_ref, w_in_hbm, w_out_hbm, gm_ref, w_mix_hbm, b_gate_ref,
                    x1_ref, qa_ref, ka_ref, va_ref, qb_ref, kb_ref, vb_ref, gates_ref,
                    w_in_ref, w_out_ref, w_mix_ref):
    _load_weights_once([(w_in_hbm, w_in_ref), (w_out_hbm, w_out_ref), (w_mix_hbm, w_mix_ref)])
    x = x_ref[...]
    h = _rms(x, g1_ref[...]).astype(jnp.bfloat16)
    x1 = x + 0.5 * _swiglu(h, w_in_ref, w_out_ref)
    x1_ref[...] = x1

    u = _rms(x1, gm_ref[...]).astype(jnp.bfloat16)
    width = ka_ref.shape[1]
    scale = HEAD_DIM ** -0.5 * LOG2E
    outs = (qa_ref, ka_ref, va_ref, qb_ref, kb_ref, vb_ref)
    for n, o_ref in enumerate(outs):
        p = _dot(u, w_mix_ref[:, n * width:(n + 1) * width])
        if o_ref is ka_ref or o_ref is kb_ref:
            o_ref[...] = p.astype(o_ref.dtype)
        elif o_ref is qa_ref or o_ref is qb_ref:
            _store_transposed(p * scale, o_ref)
        else:
            _store_transposed(p, o_ref)
    logits = _dot(u, w_mix_ref[:, len(outs) * width:]) + b_gate_ref[...]
    gates_ref[...] = jax.nn.sigmoid(logits)


def _store_transposed(p, o_ref):
    n_blk, n_pairs, _, blk = o_ref.shape[1:]
    pt = p.T
    for c in range(n_blk):
        for pr in range(n_pairs):
            o_ref[0, c, pr] = pt[pr * PAIR:(pr + 1) * PAIR, c * blk:(c + 1) * blk].astype(o_ref.dtype)


def _out_ffn_kernel(x1_ref, ya_ref, yb_ref, gates_ref, wa_hbm, wb_hbm, wo_hbm,
                    g2_ref, w_in_hbm, w_out_hbm, gf_ref, o_ref,
                    wa_ref, wb_ref, wo_ref, w_in_ref, w_out_ref, *, final_norm):
    _load_weights_once([(wa_hbm, wa_ref), (wb_hbm, wb_ref), (wo_hbm, wo_ref),
                        (w_in_hbm, w_in_ref), (w_out_hbm, w_out_ref)])
    d = x1_ref.shape[1]
    merged = (gates_ref[:, :d] * _dot(ya_ref[...], wa_ref[...])
              + gates_ref[:, d:] * _dot(yb_ref[...], wb_ref[...]))
    x2 = x1_ref[...] + _dot(merged.astype(jnp.bfloat16), wo_ref[...])
    h = _rms(x2, g2_ref[...]).astype(jnp.bfloat16)
    x3 = x2 + 0.5 * _swiglu(h, w_in_ref, w_out_ref)
    o_ref[...] = _rms(x3, gf_ref[...]) if final_norm else x3


def _resident(shape):
    return pl.BlockSpec(shape, lambda *_: (0,) * len(shape), pipeline_mode=pl.Buffered(1))


_IN_HBM = pl.BlockSpec(memory_space=pl.ANY)


def _bf16_scratch(*weights):
    return [pltpu.VMEM(w.shape, jnp.bfloat16) for w in weights]


def _rows(tm, width):
    return pl.BlockSpec((tm, width), lambda i: (i, 0))


def _ffn_mix(x, g1, w_in, w_out, gm, w_mix, b_gate, width, seq):
    m, d = x.shape
    tm = ROW_TILE
    bf = jnp.bfloat16
    n_pairs = width // PAIR
    steps_per_seq = seq // tm

    def transposed(blk):
        shape = (m // seq, seq // blk, n_pairs, PAIR, blk)
        spec = pl.BlockSpec((1, tm // blk) + shape[2:], lambda i: (i // steps_per_seq, i % steps_per_seq, 0, 0, 0))
        return jax.ShapeDtypeStruct(shape, bf), spec

    keys = jax.ShapeDtypeStruct((m, width), bf), _rows(tm, width)
    outs = [(jax.ShapeDtypeStruct((m, d), jnp.float32), _rows(tm, d)),
            transposed(A_BLK), keys, transposed(A_BLK), transposed(B_BLK), keys, transposed(B_BLK),
            (jax.ShapeDtypeStruct((m, b_gate.shape[1]), jnp.float32), _rows(tm, b_gate.shape[1]))]
    return pl.pallas_call(
        _ffn_mix_kernel,
        grid=(m // tm,),
        in_specs=[_rows(tm, d), _resident(g1.shape), _IN_HBM, _IN_HBM,
                  _resident(gm.shape), _IN_HBM, _resident(b_gate.shape)],
        out_specs=[spec for _, spec in outs],
        out_shape=[shape for shape, _ in outs],
        scratch_shapes=_bf16_scratch(w_in, w_out, w_mix),
        compiler_params=pltpu.CompilerParams(dimension_semantics=("arbitrary",),
                                             vmem_limit_bytes=V7X_VMEM_LIMIT),
        name="ffn_mix",
    )(x, g1, w_in, w_out, gm, w_mix, b_gate)


def _out_ffn(x1, ya, yb, gates, wa, wb, wo, g2, w_in, w_out, gf, final_norm):
    m, d = x1.shape
    tm = ROW_TILE
    return pl.pallas_call(
        functools.partial(_out_ffn_kernel, final_norm=final_norm),
        grid=(m // tm,),
        in_specs=[_rows(tm, d), _rows(tm, ya.shape[1]), _rows(tm, yb.shape[1]), _rows(tm, gates.shape[1]),
                  _IN_HBM, _IN_HBM, _IN_HBM, _resident(g2.shape), _IN_HBM, _IN_HBM, _resident(gf.shape)],
        out_specs=_rows(tm, d),
        out_shape=jax.ShapeDtypeStruct((m, d), jnp.float32),
        scratch_shapes=_bf16_scratch(wa, wb, wo, w_in, w_out),
        compiler_params=pltpu.CompilerParams(dimension_semantics=("arbitrary",),
                                             vmem_limit_bytes=V7X_VMEM_LIMIT),
        name="out_ffn",
    )(x1, ya, yb, gates, wa, wb, wo, g2, w_in, w_out, gf)


def _t5_bucket(rel):
    nb = T5_BUCKETS // 2
    ret = jnp.where(rel > 0, nb, 0)
    n = jnp.abs(rel)
    max_exact = nb // 2
    nf = jnp.maximum(n, 1).astype(jnp.float32)
    large = max_exact + (jnp.log(nf / max_exact) / math.log(T5_MAX_DIST / max_exact)
                         * (nb - max_exact)).astype(jnp.int32)
    large = jnp.minimum(large, nb - 1)
    return ret + jnp.where(n < max_exact, n, large)


def _window_rel(n_tiles, blk):
    nb = blk // LANES
    u = jnp.arange(-(nb - 1), n_tiles * nb, dtype=jnp.int32)[:, None]
    return LANES - jnp.arange(2 * LANES, dtype=jnp.int32)[None, :] - LANES * u


def _bias_tiles_kernel(rel_bias_ref, tiles_ref, *, left_chunks):
    n_pairs, n_tiles, blk, _ = tiles_ref.shape
    nb = blk // LANES
    kj = lax.broadcasted_iota(jnp.int32, (LANES, LANES), 0)
    qi = lax.broadcasted_iota(jnp.int32, (LANES, LANES), 1)
    chunk_diff = kj // CHUNK - qi // CHUNK
    for p in range(n_pairs):
        for half in range(2):
            for w in range(rel_bias_ref.shape[2]):
                u = w - (nb - 1)
                row = jnp.broadcast_to(rel_bias_ref[p, half, w], (LANES, 2 * LANES))
                sub = pltpu.roll(row, 0, 1, stride=1, stride_axis=0)[:, LANES:]
                for t in range(n_tiles):
                    for a in range(nb):
                        c = t * nb + a - u
                        if not 0 <= c < nb:
                            continue
                        diff = chunk_diff - u * SUB_CHUNKS
                        allowed = diff <= 0
                        if left_chunks is not None:
                            allowed = allowed & (diff >= -left_chunks)
                        tiles_ref[p, t, pl.ds(c * LANES, LANES), pl.ds(half * blk + a * LANES, LANES)] = (
                            jnp.where(allowed, sub * LOG2E, NEG))


def _bias_tiles(rel_bias, n_tiles, blk, left_chunks):
    n_pairs = rel_bias.shape[0]
    return pl.pallas_call(
        functools.partial(_bias_tiles_kernel, left_chunks=left_chunks),
        out_shape=jax.ShapeDtypeStruct((n_pairs, n_tiles, blk, 2 * blk), jnp.float32),
        compiler_params=pltpu.CompilerParams(vmem_limit_bytes=V7X_VMEM_LIMIT),
        name="bias_tiles",
    )(rel_bias)


def _sum_rows(x):
    acc = x[:SUBLANES]
    for r in range(1, x.shape[0] // SUBLANES):
        acc = acc + x[r * SUBLANES:(r + 1) * SUBLANES]
    return acc


def _fold_rows(x):
    m = x[:SUBLANES]
    for r in range(1, x.shape[0] // SUBLANES):
        m = jnp.maximum(m, x[r * SUBLANES:(r + 1) * SUBLANES])
    return m


def _attn_kernel(*refs, left_blocks, lambda_init):
    if lambda_init is None:
        qt_ref, k_ref, vt_ref, tiles_ref, o_ref, q2t_ref, s_ref, acc_ref = refs
    else:
        (qt_ref, k_ref, vt_ref, tiles_ref, lq1_ref, lk1_ref, lq2_ref, lk2_ref, subg_ref, o_ref,
         q2t_ref, *s_refs, m_ref, acc_ref) = refs
        lam = (jnp.exp(jnp.sum(lq1_ref[...] * lk1_ref[...], axis=-1, keepdims=True))
               - jnp.exp(jnp.sum(lq2_ref[...] * lk2_ref[...], axis=-1, keepdims=True))
               + lambda_init)
    n_q, n_pairs, _, blk = qt_ref.shape[1:]
    top_half = lax.broadcasted_iota(jnp.int32, (PAIR, blk), 0) < HEAD_DIM

    def cols(p):
        return slice(p * PAIR, (p + 1) * PAIR)

    def key_rows(j, n=1):
        return pl.ds(pl.multiple_of(j * blk, blk), n * blk)

    def weights_update(vt, e):
        return jnp.concatenate([_dot(vt, e.astype(jnp.bfloat16)), _sum_rows(e)], axis=0)

    def stack_queries(c, after=None):
        for p in range(n_pairs):
            qt = qt_ref[0, c, p]
            if after is not None:
                dep = after[p] if isinstance(after, list) else after
                dep = jnp.concatenate([dep] * (blk // dep.shape[1]), axis=1)
                qt = qt + (dep * 0.0).astype(qt.dtype)
            zero = jnp.zeros_like(qt)
            q2t_ref[p, :, :blk] = jnp.where(top_half, qt, zero)
            q2t_ref[p, :, blk:] = jnp.where(top_half, zero, qt)

    def finish(c):
        late = []
        for p in range(n_pairs):
            o = acc_ref[p, :PAIR] / jnp.sum(acc_ref[p, PAIR:], axis=0, keepdims=True)
            if lambda_init is None:
                y = jnp.where(top_half, o[:, :blk], o[:, blk:]).T
            else:
                d = o[:, :blk] - lam * o[:, blk:]
                d = d * lax.rsqrt(jnp.mean(d * d, axis=0, keepdims=True) + EPS)
                y = d.T * subg_ref[...] * (1.0 - lambda_init)
            o_ref[0, c * blk:(c + 1) * blk, cols(p)] = y.astype(o_ref.dtype)
            late.append(y[:1])
        return late

    if left_blocks is None:
        n_tiles = tiles_ref.shape[1]
        ring = len(s_refs)

        def causal_prefix(c, i, readers, after):
            stack_queries(c, after)

            def scores(j, last_readers):
                tile = min(i - j, n_tiles - 1)
                for p in range(n_pairs):
                    kb = k_ref[0, j * blk:(j + 1) * blk, cols(p)]
                    if last_readers is not None:
                        kb = kb + (last_readers[p] * 0.0).astype(kb.dtype)
                    s_refs[j % ring][p] = _dot(kb, q2t_ref[p]) + tiles_ref[p, tile]

            def online(j):
                late = []
                for p in range(n_pairs):
                    s = s_refs[j % ring][p]
                    m = jnp.max(_fold_rows(s), axis=0, keepdims=True)
                    if j == 0:
                        update = weights_update(vt_ref[0, j, p], jnp.exp2(s - m))
                        acc_ref[p] = update
                    else:
                        m_old = m_ref[p]
                        m = jnp.maximum(m_old, m)
                        update = weights_update(vt_ref[0, j, p], jnp.exp2(s - m))
                        acc_ref[p] = jnp.exp2(m_old - m) * acc_ref[p] + update
                    m_ref[p] = m
                    late.append(update[PAIR:PAIR + 1, :PAIR])
                return late

            scores(0, readers[0])
            for j in range(i + 1):
                if j < i:
                    scores(j + 1, readers[(j + 1) % ring])
                readers[j % ring] = online(j)
            finish(c)
            return readers[i % ring]

        def grid_step(step):
            readers, after = [None] * ring, None
            for c in range(n_q):
                after = causal_prefix(c, step * n_q + c, readers, after)

        for step in range(k_ref.shape[1] // (n_q * blk)):
            pl.when(pl.program_id(1) == step)(functools.partial(grid_step, step))
    else:
        def band(i, n_left):
            order = range(n_left, -1, -1)
            ms = [jnp.full((SUBLANES, 2 * blk), NEG, jnp.float32) for _ in range(n_pairs)]
            for d in order:
                for p in range(n_pairs):
                    s = _dot(k_ref[0, key_rows(i - d), cols(p)], q2t_ref[p]) + tiles_ref[p, d]
                    s_ref[d, p] = s
                    ms[p] = jnp.maximum(ms[p], _fold_rows(s))
            ms = [jnp.max(m, axis=0, keepdims=True) for m in ms]
            for d in order:
                for p in range(n_pairs):
                    update = weights_update(vt_ref[0, i - d, p], jnp.exp2(s_ref[d, p] - ms[p]))
                    acc_ref[p] = update if d == n_left else acc_ref[p] + update
            return update[PAIR:PAIR + 1, :blk]

        def grid_step(first):
            after = None
            for c in range(n_q):
                stack_queries(c, after)
                after = band(pl.program_id(1) * n_q + c, min(c, left_blocks) if first else left_blocks)
                finish(c)

        assert n_q >= left_blocks
        pl.when(pl.program_id(1) == 0)(functools.partial(grid_step, True))
        pl.when(pl.program_id(1) > 0)(functools.partial(grid_step, False))


def _attention(qt, k, vt, tiles, left_blocks, lambda_init=None, extra=()):
    b, s, w = k.shape
    n_pairs, _, blk = qt.shape[2:]
    n_q = B_STEP_BLOCKS if left_blocks is None else A_STEP_BLOCKS
    blk_q = pl.BlockSpec((1, n_q) + qt.shape[2:], lambda bi, i: (bi, i, 0, 0, 0))
    whole_k = pl.BlockSpec((1, s, w), lambda bi, i: (bi, 0, 0))
    whole_v = pl.BlockSpec((1,) + vt.shape[1:], lambda bi, i: (bi, 0, 0, 0, 0))
    f32 = jnp.float32
    score_block = (n_pairs, blk, 2 * blk)
    if left_blocks is None:
        score_scratch = [pltpu.VMEM(score_block, f32)] * B_SCORE_RING + [pltpu.VMEM((n_pairs, 1, 2 * blk), f32)]
    else:
        score_scratch = [pltpu.VMEM((left_blocks + 1,) + score_block, f32)]
    return pl.pallas_call(
        functools.partial(_attn_kernel, left_blocks=left_blocks, lambda_init=lambda_init),
        grid=(b, s // (n_q * blk)),
        in_specs=[blk_q, whole_k, whole_v, _resident(tiles.shape)] + [_resident(e.shape) for e in extra],
        out_specs=pl.BlockSpec((1, n_q * blk, w), lambda bi, i: (bi, i, 0)),
        out_shape=jax.ShapeDtypeStruct((b, s, w), jnp.bfloat16),
        scratch_shapes=[pltpu.VMEM((n_pairs, PAIR, 2 * blk), jnp.bfloat16), *score_scratch,
                        pltpu.VMEM((n_pairs, ACC_ROWS, 2 * blk), f32)],
        compiler_params=pltpu.CompilerParams(dimension_semantics=("arbitrary", "arbitrary"),
                                             vmem_limit_bytes=V7X_VMEM_LIMIT),
        name="attn_a" if lambda_init is None else "attn_b",
    )(qt, k, vt, tiles, *extra)


def kernel(x, ffn1_norm, ffn1_w_in, ffn1_w_out, mix_norm, w_mix_in, b_gate, rel_bias_a, lambda_q1, lambda_k1, lambda_q2, lambda_k2, subln_g, t5_bias, w_branch_a, w_branch_b, w_o, ffn2_norm, ffn2_w_in, ffn2_w_out, final_norm):
    b, s, d = x.shape
    depth = ffn1_norm.shape[0]
    assert depth >= 1
    width = w_branch_a.shape[1]
    n_pairs = width // PAIR
    f32 = jnp.float32

    t5_rel = t5_bias.astype(f32)[:, _t5_bucket(_window_rel(B_TILES, B_BLK))]
    t5_rel = jnp.broadcast_to(t5_rel[:, None, :, None, :], (n_pairs, 2) + t5_rel.shape[1:2] + (1, 2 * LANES))
    tiles_b = _bias_tiles(t5_rel, B_TILES, B_BLK, None)

    xf = x.astype(f32).reshape(b * s, d)
    for li in range(depth):
        x1, qa, ka, va, qb, kb, vb, gates = _ffn_mix(
            xf, ffn1_norm[li][None].astype(f32), ffn1_w_in[li].astype(f32), ffn1_w_out[li].astype(f32),
            mix_norm[li][None].astype(f32), w_mix_in[li].astype(f32), b_gate[li][None].astype(f32), width, s)

        rel_a = jnp.clip(_window_rel(A_TILES, A_BLK), -REL_CLIP, REL_CLIP) + REL_CLIP
        a_rel = rel_bias_a[li].astype(f32)[:, rel_a]
        tiles_a = _bias_tiles(a_rel.reshape(n_pairs, 2, rel_a.shape[0], 1, 2 * LANES), A_TILES, A_BLK, LEFT_CHUNKS)

        def seq(t):
            return t.reshape(b, s, width)

        ya = _attention(qa, seq(ka), va, tiles_a, A_TILES - 1)
        lambda_init = 0.8 - 0.6 * math.exp(-0.3 * li)
        lams = [t[li][None].astype(f32) for t in (lambda_q1, lambda_k1, lambda_q2, lambda_k2)]
        yb = _attention(qb, seq(kb), vb, tiles_b, None, lambda_init=lambda_init,
                        extra=(*lams, subln_g[li][None].astype(f32)))

        xf = _out_ffn(x1, ya.reshape(b * s, width), yb.reshape(b * s, width), gates,
                      w_branch_a[li].astype(f32), w_branch_b[li].astype(f32), w_o[li].astype(f32),
                      ffn2_norm[li][None].astype(f32), ffn2_w_in[li].astype(f32), ffn2_w_out[li].astype(f32),
                      final_norm[None].astype(f32), final_norm=(li == depth - 1))
    return xf.reshape(b, s, d).astype(x.dtype)
```

```python
import functools
import math

import jax
import jax.numpy as jnp
from jax import lax
from jax.experimental import pallas as pl
from jax.experimental.pallas import tpu as pltpu

EPS = 1e-6
NEG = -1e30
LOG2E = math.log2(math.e)

CHUNK = 64
LEFT_CHUNKS = 8
REL_CLIP = 128
T5_BUCKETS = 32
T5_MAX_DIST = 128

LANES = 128
SUBLANES = 8
HEAD_DIM = 64
PAIR = 2 * HEAD_DIM
SUB_CHUNKS = LANES // CHUNK
ACC_ROWS = PAIR + SUBLANES

A_BLK = 128
A_TILES = LEFT_CHUNKS * CHUNK // A_BLK + 1
B_BLK = 256
B_TILES = 3
A_FLAT_TILES = tuple(t for t in range(A_TILES)
                     if t * A_BLK - (A_BLK - 1) >= REL_CLIP and (t + 1) * A_BLK <= LEFT_CHUNKS * CHUNK)
B_FLAT_TILES = (B_TILES - 1,)
B_SCORE_RING = 3
A_STEP_BLOCKS = 8
B_STEP_BLOCKS = 2

V7X_VMEM_LIMIT = 60000 * 1024

ROW_TILE = 512
BF16_ROWS = 16
CAST_CHUNK_BYTES = 3 << 18
CAST_SLOTS = 4


def _rms(xf, g):
    return xf * lax.rsqrt(jnp.mean(xf * xf, axis=-1, keepdims=True) + EPS) * g


def _dot(a, b):
    return jnp.dot(a, b, preferred_element_type=jnp.float32)


def _swiglu(h, w_in_ref, w_out_ref):
    d_ff = w_out_ref.shape[0]
    gate = _dot(h, w_in_ref[:, :d_ff])
    up = _dot(h, w_in_ref[:, d_ff:])
    act = (gate * jax.nn.sigmoid(gate) * up).astype(jnp.bfloat16)
    return _dot(act, w_out_ref[...])


def _cast_chunk_rows(rows, width):
    fits = [r for r in range(BF16_ROWS, rows + 1, BF16_ROWS)
            if rows % r == 0 and r * width * 4 <= CAST_CHUNK_BYTES]
    return max(fits)


def _load_cast(src_hbm, dst_ref):
    rows, width = src_hbm.shape
    chunk = _cast_chunk_rows(rows, width)
    n_chunks = rows // chunk

    ahead = min(CAST_SLOTS, n_chunks) - 1

    def body(stage_ref, sem):
        def copy(c):
            slot = c % CAST_SLOTS
            return pltpu.make_async_copy(src_hbm.at[pl.ds(c * chunk, chunk)], stage_ref.at[slot], sem.at[slot])

        for c in range(ahead):
            copy(c).start()

        def one_chunk(c, carry):
            @pl.when(c + ahead < n_chunks)
            def _():
                copy(c + ahead).start()

            copy(c).wait()
            dst_ref[pl.ds(pl.multiple_of(c * chunk, chunk), chunk)] = (
                stage_ref[c % CAST_SLOTS].astype(dst_ref.dtype))
            return carry

        lax.fori_loop(0, n_chunks, one_chunk, 0)

    pl.run_scoped(body, pltpu.VMEM((CAST_SLOTS, chunk, width), src_hbm.dtype),
                  pltpu.SemaphoreType.DMA((CAST_SLOTS,)))


def _load_weights_once(pairs):
    @pl.when(pl.program_id(0) == 0)
    def _():
        for src_hbm, dst_ref in pairs:
            _load_cast(src_hbm, dst_ref)


def _ffn_mix_kernel(x_ref, g1_ref, w_in_hbm, w_out_hbm, gm_ref, w_mix_hbm, b_gate_ref,
                    x1_ref, qa_ref, ka_ref, va_ref, qb_ref, kb_ref, vb_ref, gates_ref,
                    w_in_ref, w_out_ref, w_mix_ref):
    _load_weights_once([(w_in_hbm, w_in_ref), (w_out_hbm, w_out_ref), (w_mix_hbm, w_mix_ref)])
    x = x_ref[...]
    h = _rms(x, g1_ref[...]).astype(jnp.bfloat16)
    x1 = x + 0.5 * _swiglu(h, w_in_ref, w_out_ref)
    x1_ref[...] = x1

    u = _rms(x1, gm_ref[...]).astype(jnp.bfloat16)
    width = ka_ref.shape[1]
    scale = HEAD_DIM ** -0.5 * LOG2E
    outs = (qa_ref, ka_ref, va_ref, qb_ref, kb_ref, vb_ref)
    for n, o_ref in enumerate(outs):
        p = _dot(u, w_mix_ref[:, n * width:(n + 1) * width])
        if o_ref is ka_ref or o_ref is kb_ref:
            o_ref[...] = p.astype(o_ref.dtype)
        elif o_ref is qa_ref or o_ref is qb_ref:
            _store_transposed(p * scale, o_ref)
        else:
            _store_transposed(p, o_ref)
    logits = _dot(u, w_mix_ref[:, len(outs) * width:]) + b_gate_ref[...]
    gates_ref[...] = jax.nn.sigmoid(logits)


def _store_transposed(p, o_ref):
    n_blk, n_pairs, _, blk = o_ref.shape[1:]
    pt = p.T
    for c in range(n_blk):
        for pr in range(n_pairs):
            o_ref[0, c, pr] = pt[pr * PAIR:(pr + 1) * PAIR, c * blk:(c + 1) * blk].astype(o_ref.dtype)


def _out_ffn_kernel(x1_ref, ya_ref, yb_ref, gates_ref, wa_hbm, wb_hbm, wo_hbm,
                    g2_ref, w_in_hbm, w_out_hbm, gf_ref, o_ref,
                    wa_ref, wb_ref, wo_ref, w_in_ref, w_out_ref, *, final_norm):
    _load_weights_once([(wa_hbm, wa_ref), (wb_hbm, wb_ref), (wo_hbm, wo_ref),
                        (w_in_hbm, w_in_ref), (w_out_hbm, w_out_ref)])
    d = x1_ref.shape[1]
    merged = (gates_ref[:, :d] * _dot(ya_ref[...], wa_ref[...])
              + gates_ref[:, d:] * _dot(yb_ref[...], wb_ref[...]))
    x2 = x1_ref[...] + _dot(merged.astype(jnp.bfloat16), wo_ref[...])
    h = _rms(x2, g2_ref[...]).astype(jnp.bfloat16)
    x3 = x2 + 0.5 * _swiglu(h, w_in_ref, w_out_ref)
    o_ref[...] = _rms(x3, gf_ref[...]) if final_norm else x3


def _resident(shape):
    return pl.BlockSpec(shape, lambda *_: (0,) * len(shape), pipeline_mode=pl.Buffered(1))


_IN_HBM = pl.BlockSpec(memory_space=pl.ANY)


def _bf16_scratch(*weights):
    return [pltpu.VMEM(w.shape, jnp.bfloat16) for w in weights]


def _rows(tm, width):
    return pl.BlockSpec((tm, width), lambda i: (i, 0))


def _ffn_mix(x, g1, w_in, w_out, gm, w_mix, b_gate, width, seq):
    m, d = x.shape
    tm = ROW_TILE
    bf = jnp.bfloat16
    n_pairs = width // PAIR
    steps_per_seq = seq // tm

    def transposed(blk):
        shape = (m // seq, seq // blk, n_pairs, PAIR, blk)
        spec = pl.BlockSpec((1, tm // blk) + shape[2:], lambda i: (i // steps_per_seq, i % steps_per_seq, 0, 0, 0))
        return jax.ShapeDtypeStruct(shape, bf), spec

    keys = jax.ShapeDtypeStruct((m, width), bf), _rows(tm, width)
    outs = [(jax.ShapeDtypeStruct((m, d), jnp.float32), _rows(tm, d)),
            transposed(A_BLK), keys, transposed(A_BLK), transposed(B_BLK), keys, transposed(B_BLK),
            (jax.ShapeDtypeStruct((m, b_gate.shape[1]), jnp.float32), _rows(tm, b_gate.shape[1]))]
    return pl.pallas_call(
        _ffn_mix_kernel,
        grid=(m // tm,),
        in_specs=[_rows(tm, d), _resident(g1.shape), _IN_HBM, _IN_HBM,
                  _resident(gm.shape), _IN_HBM, _resident(b_gate.shape)],
        out_specs=[spec for _, spec in outs],
        out_shape=[shape for shape, _ in outs],
        scratch_shapes=_bf16_scratch(w_in, w_out, w_mix),
        compiler_params=pltpu.CompilerParams(dimension_semantics=("arbitrary",),
                                             vmem_limit_bytes=V7X_VMEM_LIMIT),
        name="ffn_mix",
    )(x, g1, w_in, w_out, gm, w_mix, b_gate)


def _out_ffn(x1, ya, yb, gates, wa, wb, wo, g2, w_in, w_out, gf, final_norm):
    m, d = x1.shape
    tm = ROW_TILE
    return pl.pallas_call(
        functools.partial(_out_ffn_kernel, final_norm=final_norm),
        grid=(m // tm,),
        in_specs=[_rows(tm, d), _rows(tm, ya.shape[1]), _rows(tm, yb.shape[1]), _rows(tm, gates.shape[1]),
                  _IN_HBM, _IN_HBM, _IN_HBM, _resident(g2.shape), _IN_HBM, _IN_HBM, _resident(gf.shape)],
        out_specs=_rows(tm, d),
        out_shape=jax.ShapeDtypeStruct((m, d), jnp.float32),
        scratch_shapes=_bf16_scratch(wa, wb, wo, w_in, w_out),
        compiler_params=pltpu.CompilerParams(dimension_semantics=("arbitrary",),
                                             vmem_limit_bytes=V7X_VMEM_LIMIT),
        name="out_ffn",
    )(x1, ya, yb, gates, wa, wb, wo, g2, w_in, w_out, gf)


def _t5_bucket(rel):
    nb = T5_BUCKETS // 2
    ret = jnp.where(rel > 0, nb, 0)
    n = jnp.abs(rel)
    max_exact = nb // 2
    nf = jnp.maximum(n, 1).astype(jnp.float32)
    large = max_exact + (jnp.log(nf / max_exact) / math.log(T5_MAX_DIST / max_exact)
                         * (nb - max_exact)).astype(jnp.int32)
    large = jnp.minimum(large, nb - 1)
    return ret + jnp.where(n < max_exact, n, large)


def _window_rel(n_tiles, blk):
    nb = blk // LANES
    u = jnp.arange(-(nb - 1), n_tiles * nb, dtype=jnp.int32)[:, None]
    return LANES - jnp.arange(2 * LANES, dtype=jnp.int32)[None, :] - LANES * u


def _bias_tiles_kernel(rel_bias_ref, tiles_ref, *, left_chunks, flat):
    n_pairs, n_tiles, blk, _ = tiles_ref.shape
    nb = blk // LANES
    kj = lax.broadcasted_iota(jnp.int32, (LANES, LANES), 0)
    qi = lax.broadcasted_iota(jnp.int32, (LANES, LANES), 1)
    chunk_diff = kj // CHUNK - qi // CHUNK
    for p in range(n_pairs):
        for half in range(2):
            for w in range(rel_bias_ref.shape[2]):
                u = w - (nb - 1)
                row = jnp.broadcast_to(rel_bias_ref[p, half, w], (LANES, 2 * LANES))
                sub = pltpu.roll(row, 0, 1, stride=1, stride_axis=0)[:, LANES:]
                for t in range(n_tiles):
                    for a in range(nb):
                        c = t * nb + a - u
                        if not 0 <= c < nb:
                            continue
                        diff = chunk_diff - u * SUB_CHUNKS
                        allowed = diff <= 0
                        if left_chunks is not None:
                            allowed = allowed & (diff >= -left_chunks)
                        tiles_ref[p, t, pl.ds(c * LANES, LANES), pl.ds(half * blk + a * LANES, LANES)] = (
                            jnp.where(allowed, sub * LOG2E, NEG))
        shift = tiles_ref[p, flat[0]]
        for t in range(n_tiles):
            tiles_ref[p, t] = tiles_ref[p, t] - shift


def _bias_tiles(rel_bias, n_tiles, blk, left_chunks, flat):
    n_pairs = rel_bias.shape[0]
    return pl.pallas_call(
        functools.partial(_bias_tiles_kernel, left_chunks=left_chunks, flat=flat),
        out_shape=jax.ShapeDtypeStruct((n_pairs, n_tiles, blk, 2 * blk), jnp.float32),
        compiler_params=pltpu.CompilerParams(vmem_limit_bytes=V7X_VMEM_LIMIT),
        name="bias_tiles",
    )(rel_bias)


def _sum_rows(x):
    acc = x[:SUBLANES]
    for r in range(1, x.shape[0] // SUBLANES):
        acc = acc + x[r * SUBLANES:(r + 1) * SUBLANES]
    return acc


def _fold_rows(x):
    m = x[:SUBLANES]
    for r in range(1, x.shape[0] // SUBLANES):
        m = jnp.maximum(m, x[r * SUBLANES:(r + 1) * SUBLANES])
    return m


def _attn_kernel(*refs, left_blocks, lambda_init):
    if lambda_init is None:
        qt_ref, k_ref, vt_ref, tiles_ref, o_ref, q2t_ref, s_ref, acc_ref = refs
    else:
        (qt_ref, k_ref, vt_ref, tiles_ref, lq1_ref, lk1_ref, lq2_ref, lk2_ref, subg_ref, o_ref,
         q2t_ref, *s_refs, m_ref, acc_ref) = refs
        lam = (jnp.exp(jnp.sum(lq1_ref[...] * lk1_ref[...], axis=-1, keepdims=True))
               - jnp.exp(jnp.sum(lq2_ref[...] * lk2_ref[...], axis=-1, keepdims=True))
               + lambda_init)
    n_q, n_pairs, _, blk = qt_ref.shape[1:]
    top_half = lax.broadcasted_iota(jnp.int32, (PAIR, blk), 0) < HEAD_DIM

    def cols(p):
        return slice(p * PAIR, (p + 1) * PAIR)

    def key_rows(j, n=1):
        return pl.ds(pl.multiple_of(j * blk, blk), n * blk)

    def weights_update(vt, e):
        return jnp.concatenate([_dot(vt, e.astype(jnp.bfloat16)), _sum_rows(e)], axis=0)

    def stack_queries(c, after=None):
        for p in range(n_pairs):
            qt = qt_ref[0, c, p]
            if after is not None:
                dep = after[p] if isinstance(after, list) else after
                dep = jnp.concatenate([dep] * (blk // dep.shape[1]), axis=1)
                qt = qt + (dep * 0.0).astype(qt.dtype)
            zero = jnp.zeros_like(qt)
            q2t_ref[p, :, :blk] = jnp.where(top_half, qt, zero)
            q2t_ref[p, :, blk:] = jnp.where(top_half, zero, qt)

    def finish(c):
        late = []
        for p in range(n_pairs):
            o = acc_ref[p, :PAIR] / jnp.sum(acc_ref[p, PAIR:], axis=0, keepdims=True)
            if lambda_init is None:
                y = jnp.where(top_half, o[:, :blk], o[:, blk:]).T
            else:
                d = o[:, :blk] - lam * o[:, blk:]
                d = d * lax.rsqrt(jnp.mean(d * d, axis=0, keepdims=True) + EPS)
                y = d.T * subg_ref[...] * (1.0 - lambda_init)
            o_ref[0, c * blk:(c + 1) * blk, cols(p)] = y.astype(o_ref.dtype)
            late.append(y[:1])
        return late

    if left_blocks is None:
        n_tiles = tiles_ref.shape[1]
        ring = len(s_refs)

        def causal_prefix(c, i, readers, after):
            stack_queries(c, after)

            def scores(j, last_readers):
                tile = min(i - j, n_tiles - 1)
                for p in range(n_pairs):
                    kb = k_ref[0, j * blk:(j + 1) * blk, cols(p)]
                    if last_readers is not None:
                        kb = kb + (last_readers[p] * 0.0).astype(kb.dtype)
                    s = _dot(kb, q2t_ref[p])
                    s_refs[j % ring][p] = s if tile in B_FLAT_TILES else s + tiles_ref[p, tile]

            def online(j):
                late = []
                for p in range(n_pairs):
                    s = s_refs[j % ring][p]
                    m = jnp.max(_fold_rows(s), axis=0, keepdims=True)
                    if j == 0:
                        update = weights_update(vt_ref[0, j, p], jnp.exp2(s - m))
                        acc_ref[p] = update
                    else:
                        m_old = m_ref[p]
                        m = jnp.maximum(m_old, m)
                        update = weights_update(vt_ref[0, j, p], jnp.exp2(s - m))
                        acc_ref[p] = jnp.exp2(m_old - m) * acc_ref[p] + update
                    m_ref[p] = m
                    late.append(update[PAIR:PAIR + 1, :PAIR])
                return late

            scores(0, readers[0])
            for j in range(i + 1):
                if j < i:
                    scores(j + 1, readers[(j + 1) % ring])
                readers[j % ring] = online(j)
            finish(c)
            return readers[i % ring]

        def grid_step(step):
            readers, after = [None] * ring, None
            for c in range(n_q):
                after = causal_prefix(c, step * n_q + c, readers, after)

        for step in range(k_ref.shape[1] // (n_q * blk)):
            pl.when(pl.program_id(1) == step)(functools.partial(grid_step, step))
    else:
        def band(i, n_left):
            order = range(n_left, -1, -1)
            ms = [jnp.full((SUBLANES, 2 * blk), NEG, jnp.float32) for _ in range(n_pairs)]
            for d in order:
                for p in range(n_pairs):
                    s = _dot(k_ref[0, key_rows(i - d), cols(p)], q2t_ref[p])
                    if d not in A_FLAT_TILES:
                        s = s + tiles_ref[p, d]
                    s_ref[d, p] = s
                    ms[p] = jnp.maximum(ms[p], _fold_rows(s))
            ms = [jnp.max(m, axis=0, keepdims=True) for m in ms]
            for d in order:
                for p in range(n_pairs):
                    update = weights_update(vt_ref[0, i - d, p], jnp.exp2(s_ref[d, p] - ms[p]))
                    acc_ref[p] = update if d == n_left else acc_ref[p] + update
            return update[PAIR:PAIR + 1, :blk]

        def grid_step(first):
            after = None
            for c in range(n_q):
                stack_queries(c, after)
                after = band(pl.program_id(1) * n_q + c, min(c, left_blocks) if first else left_blocks)
                finish(c)

        assert n_q >= left_blocks
        pl.when(pl.program_id(1) == 0)(functools.partial(grid_step, True))
        pl.when(pl.program_id(1) > 0)(functools.partial(grid_step, False))


def _attention(qt, k, vt, tiles, left_blocks, lambda_init=None, extra=()):
    b, s, w = k.shape
    n_pairs, _, blk = qt.shape[2:]
    n_q = B_STEP_BLOCKS if left_blocks is None else A_STEP_BLOCKS
    blk_q = pl.BlockSpec((1, n_q) + qt.shape[2:], lambda bi, i: (bi, i, 0, 0, 0))
    whole_k = pl.BlockSpec((1, s, w), lambda bi, i: (bi, 0, 0))
    whole_v = pl.BlockSpec((1,) + vt.shape[1:], lambda bi, i: (bi, 0, 0, 0, 0))
    f32 = jnp.float32
    score_block = (n_pairs, blk, 2 * blk)
    if left_blocks is None:
        score_scratch = [pltpu.VMEM(score_block, f32)] * B_SCORE_RING + [pltpu.VMEM((n_pairs, 1, 2 * blk), f32)]
    else:
        score_scratch = [pltpu.VMEM((left_blocks + 1,) + score_block, f32)]
    return pl.pallas_call(
        functools.partial(_attn_kernel, left_blocks=left_blocks, lambda_init=lambda_init),
        grid=(b, s // (n_q * blk)),
        in_specs=[blk_q, whole_k, whole_v, _resident(tiles.shape)] + [_resident(e.shape) for e in extra],
        out_specs=pl.BlockSpec((1, n_q * blk, w), lambda bi, i: (bi, i, 0)),
        out_shape=jax.ShapeDtypeStruct((b, s, w), jnp.bfloat16),
        scratch_shapes=[pltpu.VMEM((n_pairs, PAIR, 2 * blk), jnp.bfloat16), *score_scratch,
                        pltpu.VMEM((n_pairs, ACC_ROWS, 2 * blk), f32)],
        compiler_params=pltpu.CompilerParams(dimension_semantics=("arbitrary", "arbitrary"),
                                             vmem_limit_bytes=V7X_VMEM_LIMIT),
        name="attn_a" if lambda_init is None else "attn_b",
    )(qt, k, vt, tiles, *extra)


def kernel(x, ffn1_norm, ffn1_w_in, ffn1_w_out, mix_norm, w_mix_in, b_gate, rel_bias_a, lambda_q1, lambda_k1, lambda_q2, lambda_k2, subln_g, t5_bias, w_branch_a, w_branch_b, w_o, ffn2_norm, ffn2_w_in, ffn2_w_out, final_norm):
    b, s, d = x.shape
    depth = ffn1_norm.shape[0]
    assert depth >= 1
    width = w_branch_a.shape[1]
    n_pairs = width // PAIR
    f32 = jnp.float32

    t5_rel = t5_bias.astype(f32)[:, _t5_bucket(_window_rel(B_TILES, B_BLK))]
    t5_rel = jnp.broadcast_to(t5_rel[:, None, :, None, :], (n_pairs, 2) + t5_rel.shape[1:2] + (1, 2 * LANES))
    tiles_b = _bias_tiles(t5_rel, B_TILES, B_BLK, None, B_FLAT_TILES)

    xf = x.astype(f32).reshape(b * s, d)
    for li in range(depth):
        x1, qa, ka, va, qb, kb, vb, gates = _ffn_mix(
            xf, ffn1_norm[li][None].astype(f32), ffn1_w_in[li].astype(f32), ffn1_w_out[li].astype(f32),
            mix_norm[li][None].astype(f32), w_mix_in[li].astype(f32), b_gate[li][None].astype(f32), width, s)

        rel_a = jnp.clip(_window_rel(A_TILES, A_BLK), -REL_CLIP, REL_CLIP) + REL_CLIP
        a_rel = rel_bias_a[li].astype(f32)[:, rel_a]
        tiles_a = _bias_tiles(a_rel.reshape(n_pairs, 2, rel_a.shape[0], 1, 2 * LANES), A_TILES, A_BLK, LEFT_CHUNKS,
                              A_FLAT_TILES)

        def seq(t):
            return t.reshape(b, s, width)

        ya = _attention(qa, seq(ka), va, tiles_a, A_TILES - 1)
        lambda_init = 0.8 - 0.6 * math.exp(-0.3 * li)
        lams = [t[li][None].astype(f32) for t in (lambda_q1, lambda_k1, lambda_q2, lambda_k2)]
        yb = _attention(qb, seq(kb), vb, tiles_b, None, lambda_init=lambda_init,
                        extra=(*lams, subln_g[li][None].astype(f32)))

        xf = _out_ffn(x1, ya.reshape(b * s, width), yb.reshape(b * s, width), gates,
                      w_branch_a[li].astype(f32), w_branch_b[li].astype(f32), w_o[li].astype(f32),
                      ffn2_norm[li][None].astype(f32), ffn2_w_in[li].astype(f32), ffn2_w_out[li].astype(f32),
                      final_norm[None].astype(f32), final_norm=(li == depth - 1))
    return xf.reshape(b, s, d).astype(x.dtype)
```

```python
import functools
import math

import jax
import jax.numpy as jnp
from jax import lax
from jax.experimental import pallas as pl
from jax.experimental.pallas import tpu as pltpu

EPS = 1e-6
NEG = -1e30
LOG2E = math.log2(math.e)

CHUNK = 64
LEFT_CHUNKS = 8
REL_CLIP = 128
T5_BUCKETS = 32
T5_MAX_DIST = 128

LANES = 128
SUBLANES = 8
HEAD_DIM = 64
PAIR = 2 * HEAD_DIM
SUB_CHUNKS = LANES // CHUNK
ACC_ROWS = PAIR + SUBLANES

A_BLK = 128
A_TILES = LEFT_CHUNKS * CHUNK // A_BLK + 1
B_BLK = 256
B_TILES = 3
A_FLAT_TILES = tuple(t for t in range(A_TILES)
                     if t * A_BLK - (A_BLK - 1) >= REL_CLIP and (t + 1) * A_BLK <= LEFT_CHUNKS * CHUNK)
B_FLAT_TILES = (B_TILES - 1,)
B_SCORE_RING = 3
A_STEP_BLOCKS = 8
B_STEP_BLOCKS = 2

V7X_VMEM_LIMIT = 60000 * 1024

ROW_TILE = 512
BF16_ROWS = 16
CAST_CHUNK_BYTES = 3 << 18
CAST_SLOTS = 4


def _rms(xf, g):
    return xf * lax.rsqrt(jnp.mean(xf * xf, axis=-1, keepdims=True) + EPS) * g


def _dot(a, b):
    return jnp.dot(a, b, preferred_element_type=jnp.float32)


def _swiglu(h, w_in_ref, w_out_ref):
    d_ff = w_out_ref.shape[0]
    gate = _dot(h, w_in_ref[:, :d_ff])
    up = _dot(h, w_in_ref[:, d_ff:])
    act = (gate * jax.nn.sigmoid(gate) * up).astype(jnp.bfloat16)
    return _dot(act, w_out_ref[...])


def _cast_chunk_rows(rows, width):
    fits = [r for r in range(BF16_ROWS, rows + 1, BF16_ROWS)
            if rows % r == 0 and r * width * 4 <= CAST_CHUNK_BYTES]
    return max(fits)


def _load_cast(src_hbm, dst_ref):
    rows, width = src_hbm.shape
    chunk = _cast_chunk_rows(rows, width)
    n_chunks = rows // chunk

    ahead = min(CAST_SLOTS, n_chunks) - 1

    def body(stage_ref, sem):
        def copy(c):
            slot = c % CAST_SLOTS
            return pltpu.make_async_copy(src_hbm.at[pl.ds(c * chunk, chunk)], stage_ref.at[slot], sem.at[slot])

        for c in range(ahead):
            copy(c).start()

        def one_chunk(c, carry):
            @pl.when(c + ahead < n_chunks)
            def _():
                copy(c + ahead).start()

            copy(c).wait()
            dst_ref[pl.ds(pl.multiple_of(c * chunk, chunk), chunk)] = (
                stage_ref[c % CAST_SLOTS].astype(dst_ref.dtype))
            return carry

        lax.fori_loop(0, n_chunks, one_chunk, 0)

    pl.run_scoped(body, pltpu.VMEM((CAST_SLOTS, chunk, width), src_hbm.dtype),
                  pltpu.SemaphoreType.DMA((CAST_SLOTS,)))


def _load_weights_once(pairs):
    @pl.when(pl.program_id(0) == 0)
    def _():
        for src_hbm, dst_ref in pairs:
            _load_cast(src_hbm, dst_ref)


def _ffn_mix_kernel(x_ref, g1_ref, w_in_hbm, w_out_hbm, gm_ref, w_mix_hbm, b_gate_ref,
                    x1_ref, qa_ref, ka_ref, va_ref, qb_ref, kb_ref, vb_ref, gates_ref,
                    w_in_ref, w_out_ref, w_mix_ref):
    _load_weights_once([(w_in_hbm, w_in_ref), (w_out_hbm, w_out_ref), (w_mix_hbm, w_mix_ref)])
    x = x_ref[...]
    h = _rms(x, g1_ref[...]).astype(jnp.bfloat16)
    x1 = x + 0.5 * _swiglu(h, w_in_ref, w_out_ref)
    x1_ref[...] = x1

    u = _rms(x1, gm_ref[...]).astype(jnp.bfloat16)
    width = ka_ref.shape[1]
    scale = HEAD_DIM ** -0.5 * LOG2E
    outs = (qa_ref, ka_ref, va_ref, qb_ref, kb_ref, vb_ref)
    for n, o_ref in enumerate(outs):
        p = _dot(u, w_mix_ref[:, n * width:(n + 1) * width])
        if o_ref is ka_ref or o_ref is kb_ref:
            o_ref[...] = p.astype(o_ref.dtype)
        elif o_ref is qa_ref or o_ref is qb_ref:
            _store_transposed(p * scale, o_ref)
        else:
            _store_transposed(p, o_ref)
    logits = _dot(u, w_mix_ref[:, len(outs) * width:]) + b_gate_ref[...]
    gates_ref[...] = jax.nn.sigmoid(logits)


def _store_transposed(p, o_ref):
    n_blk, n_pairs, _, blk = o_ref.shape[1:]
    pt = p.T
    for c in range(n_blk):
        for pr in range(n_pairs):
            o_ref[0, c, pr] = pt[pr * PAIR:(pr + 1) * PAIR, c * blk:(c + 1) * blk].astype(o_ref.dtype)


def _out_ffn_kernel(x1_ref, ya_ref, yb_ref, gates_ref, wa_hbm, wb_hbm, wo_hbm,
                    g2_ref, w_in_hbm, w_out_hbm, gf_ref, o_ref,
                    wa_ref, wb_ref, wo_ref, w_in_ref, w_out_ref, *, final_norm):
    _load_weights_once([(wa_hbm, wa_ref), (wb_hbm, wb_ref), (wo_hbm, wo_ref),
                        (w_in_hbm, w_in_ref), (w_out_hbm, w_out_ref)])
    d = x1_ref.shape[1]
    merged = (gates_ref[:, :d] * _dot(ya_ref[...], wa_ref[...])
              + gates_ref[:, d:] * _dot(yb_ref[...], wb_ref[...]))
    x2 = x1_ref[...] + _dot(merged.astype(jnp.bfloat16), wo_ref[...])
    h = _rms(x2, g2_ref[...]).astype(jnp.bfloat16)
    x3 = x2 + 0.5 * _swiglu(h, w_in_ref, w_out_ref)
    o_ref[...] = _rms(x3, gf_ref[...]) if final_norm else x3


def _resident(shape):
    return pl.BlockSpec(shape, lambda *_: (0,) * len(shape), pipeline_mode=pl.Buffered(1))


_IN_HBM = pl.BlockSpec(memory_space=pl.ANY)


def _bf16_scratch(*weights):
    return [pltpu.VMEM(w.shape, jnp.bfloat16) for w in weights]


def _rows(tm, width):
    return pl.BlockSpec((tm, width), lambda i: (i, 0))


def _ffn_mix(x, g1, w_in, w_out, gm, w_mix, b_gate, width, seq):
    m, d = x.shape
    tm = ROW_TILE
    bf = jnp.bfloat16
    n_pairs = width // PAIR
    steps_per_seq = seq // tm

    def transposed(blk):
        shape = (m // seq, seq // blk, n_pairs, PAIR, blk)
        spec = pl.BlockSpec((1, tm // blk) + shape[2:], lambda i: (i // steps_per_seq, i % steps_per_seq, 0, 0, 0))
        return jax.ShapeDtypeStruct(shape, bf), spec

    keys = jax.ShapeDtypeStruct((m, width), bf), _rows(tm, width)
    outs = [(jax.ShapeDtypeStruct((m, d), jnp.float32), _rows(tm, d)),
            transposed(A_BLK), keys, transposed(A_BLK), transposed(B_BLK), keys, transposed(B_BLK),
            (jax.ShapeDtypeStruct((m, b_gate.shape[1]), jnp.float32), _rows(tm, b_gate.shape[1]))]
    return pl.pallas_call(
        _ffn_mix_kernel,
        grid=(m // tm,),
        in_specs=[_rows(tm, d), _resident(g1.shape), _IN_HBM, _IN_HBM,
                  _resident(gm.shape), _IN_HBM, _resident(b_gate.shape)],
        out_specs=[spec for _, spec in outs],
        out_shape=[shape for shape, _ in outs],
        scratch_shapes=_bf16_scratch(w_in, w_out, w_mix),
        compiler_params=pltpu.CompilerParams(dimension_semantics=("arbitrary",),
                                             vmem_limit_bytes=V7X_VMEM_LIMIT),
        name="ffn_mix",
    )(x, g1, w_in, w_out, gm, w_mix, b_gate)


def _out_ffn(x1, ya, yb, gates, wa, wb, wo, g2, w_in, w_out, gf, final_norm):
    m, d = x1.shape
    tm = ROW_TILE
    return pl.pallas_call(
        functools.partial(_out_ffn_kernel, final_norm=final_norm),
        grid=(m // tm,),
        in_specs=[_rows(tm, d), _rows(tm, ya.shape[1]), _rows(tm, yb.shape[1]), _rows(tm, gates.shape[1]),
                  _IN_HBM, _IN_HBM, _IN_HBM, _resident(g2.shape), _IN_HBM, _IN_HBM, _resident(gf.shape)],
        out_specs=_rows(tm, d),
        out_shape=jax.ShapeDtypeStruct((m, d), jnp.float32),
        scratch_shapes=_bf16_scratch(wa, wb, wo, w_in, w_out),
        compiler_params=pltpu.CompilerParams(dimension_semantics=("arbitrary",),
                                             vmem_limit_bytes=V7X_VMEM_LIMIT),
        name="out_ffn",
    )(x1, ya, yb, gates, wa, wb, wo, g2, w_in, w_out, gf)


def _t5_bucket(rel):
    nb = T5_BUCKETS // 2
    ret = jnp.where(rel > 0, nb, 0)
    n = jnp.abs(rel)
    max_exact = nb // 2
    nf = jnp.maximum(n, 1).astype(jnp.float32)
    large = max_exact + (jnp.log(nf / max_exact) / math.log(T5_MAX_DIST / max_exact)
                         * (nb - max_exact)).astype(jnp.int32)
    large = jnp.minimum(large, nb - 1)
    return ret + jnp.where(n < max_exact, n, large)


def _window_rel(n_tiles, blk):
    nb = blk // LANES
    u = jnp.arange(-(nb - 1), n_tiles * nb, dtype=jnp.int32)[:, None]
    return LANES - jnp.arange(2 * LANES, dtype=jnp.int32)[None, :] - LANES * u


def _bias_tiles_kernel(rel_bias_ref, tiles_ref, *, left_chunks, flat):
    n_pairs, n_tiles, blk, _ = tiles_ref.shape
    nb = blk // LANES
    kj = lax.broadcasted_iota(jnp.int32, (LANES, LANES), 0)
    qi = lax.broadcasted_iota(jnp.int32, (LANES, LANES), 1)
    chunk_diff = kj // CHUNK - qi // CHUNK
    for p in range(n_pairs):
        for half in range(2):
            for w in range(rel_bias_ref.shape[2]):
                u = w - (nb - 1)
                row = jnp.broadcast_to(rel_bias_ref[p, half, w], (LANES, 2 * LANES))
                sub = pltpu.roll(row, 0, 1, stride=1, stride_axis=0)[:, LANES:]
                for t in range(n_tiles):
                    for a in range(nb):
                        c = t * nb + a - u
                        if not 0 <= c < nb:
                            continue
                        diff = chunk_diff - u * SUB_CHUNKS
                        allowed = diff <= 0
                        if left_chunks is not None:
                            allowed = allowed & (diff >= -left_chunks)
                        tiles_ref[p, t, pl.ds(c * LANES, LANES), pl.ds(half * blk + a * LANES, LANES)] = (
                            jnp.where(allowed, sub * LOG2E, NEG))
        shift = tiles_ref[p, flat[0]]
        for t in range(n_tiles):
            tiles_ref[p, t] = tiles_ref[p, t] - shift


def _bias_tiles(rel_bias, n_tiles, blk, left_chunks, flat):
    n_pairs = rel_bias.shape[0]
    return pl.pallas_call(
        functools.partial(_bias_tiles_kernel, left_chunks=left_chunks, flat=flat),
        out_shape=jax.ShapeDtypeStruct((n_pairs, n_tiles, blk, 2 * blk), jnp.float32),
        compiler_params=pltpu.CompilerParams(vmem_limit_bytes=V7X_VMEM_LIMIT),
        name="bias_tiles",
    )(rel_bias)


def _sum_rows(x):
    acc = x[:SUBLANES]
    for r in range(1, x.shape[0] // SUBLANES):
        acc = acc + x[r * SUBLANES:(r + 1) * SUBLANES]
    return acc


def _fold_rows(x):
    m = x[:SUBLANES]
    for r in range(1, x.shape[0] // SUBLANES):
        m = jnp.maximum(m, x[r * SUBLANES:(r + 1) * SUBLANES])
    return m


def _attn_kernel(*refs, left_blocks, lambda_init):
    if lambda_init is None:
        qt_ref, k_ref, vt_ref, tiles_ref, o_ref, q2t_ref, s_ref, acc_ref = refs
    else:
        (qt_ref, k_ref, vt_ref, tiles_ref, lq1_ref, lk1_ref, lq2_ref, lk2_ref, subg_ref, o_ref,
         q2t_ref, *s_refs, m_ref, acc_ref) = refs
        lam = (jnp.exp(jnp.sum(lq1_ref[...] * lk1_ref[...], axis=-1, keepdims=True))
               - jnp.exp(jnp.sum(lq2_ref[...] * lk2_ref[...], axis=-1, keepdims=True))
               + lambda_init)
    n_q, n_pairs, _, blk = qt_ref.shape[1:]
    top_half = lax.broadcasted_iota(jnp.int32, (PAIR, blk), 0) < HEAD_DIM

    def cols(p):
        return slice(p * PAIR, (p + 1) * PAIR)

    def key_rows(j, n=1):
        return pl.ds(pl.multiple_of(j * blk, blk), n * blk)

    def weights_update(vt, e):
        return jnp.concatenate([_dot(vt, e.astype(jnp.bfloat16)), _sum_rows(e)], axis=0)

    def stack_queries(c, after=None):
        for p in range(n_pairs):
            qt = qt_ref[0, c, p]
            if after is not None:
                dep = after[p] if isinstance(after, list) else after
                dep = jnp.concatenate([dep] * (blk // dep.shape[1]), axis=1)
                qt = qt + (dep * 0.0).astype(qt.dtype)
            zero = jnp.zeros_like(qt)
            q2t_ref[p, :, :blk] = jnp.where(top_half, qt, zero)
            q2t_ref[p, :, blk:] = jnp.where(top_half, zero, qt)

    def finish(c):
        late = []
        for p in range(n_pairs):
            o = acc_ref[p, :PAIR] / jnp.sum(acc_ref[p, PAIR:], axis=0, keepdims=True)
            if lambda_init is None:
                y = jnp.where(top_half, o[:, :blk], o[:, blk:]).T
            else:
                d = o[:, :blk] - lam * o[:, blk:]
                d = d * lax.rsqrt(jnp.mean(d * d, axis=0, keepdims=True) + EPS)
                y = d.T * subg_ref[...] * (1.0 - lambda_init)
            o_ref[0, c * blk:(c + 1) * blk, cols(p)] = y.astype(o_ref.dtype)
            late.append(y[:1])
        return late

    if left_blocks is None:
        n_tiles = tiles_ref.shape[1]
        ring = len(s_refs)

        def causal_prefix(c, i, readers, after):
            stack_queries(c, after)

            def scores(j, last_readers):
                tile = min(i - j, n_tiles - 1)
                for p in range(n_pairs):
                    kb = k_ref[0, j * blk:(j + 1) * blk, cols(p)]
                    if last_readers is not None:
                        kb = kb + (last_readers[p] * 0.0).astype(kb.dtype)
                    s = _dot(kb, q2t_ref[p])
                    s_refs[j % ring][p] = s if tile in B_FLAT_TILES else s + tiles_ref[p, tile]

            def online(j):
                late = []
                for p in range(n_pairs):
                    s = s_refs[j % ring][p]
                    m = jnp.max(_fold_rows(s), axis=0, keepdims=True)
                    if j == 0:
                        update = weights_update(vt_ref[0, j, p], jnp.exp2(s - m))
                        acc_ref[p] = update
                    else:
                        m_old = m_ref[p]
                        m = jnp.maximum(m_old, m)
                        update = weights_update(vt_ref[0, j, p], jnp.exp2(s - m))
                        acc_ref[p] = jnp.exp2(m_old - m) * acc_ref[p] + update
                    m_ref[p] = m
                    late.append(update[PAIR:PAIR + 1, :PAIR])
                return late

            scores(0, readers[0])
            for j in range(i + 1):
                if j < i:
                    scores(j + 1, readers[(j + 1) % ring])
                readers[j % ring] = online(j)
            finish(c)
            return readers[i % ring]

        def grid_step(step):
            readers, after = [None] * ring, None
            for c in range(n_q):
                after = causal_prefix(c, step * n_q + c, readers, after)

        for step in range(k_ref.shape[1] // (n_q * blk)):
            pl.when(pl.program_id(1) == step)(functools.partial(grid_step, step))
    else:
        def band(i, n_left):
            order = range(n_left, -1, -1)
            ms = [jnp.full((SUBLANES, 2 * blk), NEG, jnp.float32) for _ in range(n_pairs)]
            for d in order:
                for p in range(n_pairs):
                    s = _dot(k_ref[0, key_rows(i - d), cols(p)], q2t_ref[p])
                    if d not in A_FLAT_TILES:
                        s = s + tiles_ref[p, d]
                    s_ref[d, p] = s
                    ms[p] = jnp.maximum(ms[p], _fold_rows(s))
            ms = [jnp.max(m, axis=0, keepdims=True) for m in ms]
            groups = [list(order)[g:g + 2] for g in range(0, n_left + 1, 2)]
            for group in groups:
                for p in range(n_pairs):
                    e = jnp.concatenate([jnp.exp2(s_ref[d, p] - ms[p]) for d in group], axis=0)
                    vt = jnp.concatenate([vt_ref[0, i - d, p] for d in group], axis=1)
                    update = weights_update(vt, e)
                    acc_ref[p] = update if group is groups[0] else acc_ref[p] + update
            return update[PAIR:PAIR + 1, :blk]

        def grid_step(first):
            after = None
            for c in range(n_q):
                stack_queries(c, after)
                after = band(pl.program_id(1) * n_q + c, min(c, left_blocks) if first else left_blocks)
                finish(c)

        assert n_q >= left_blocks
        pl.when(pl.program_id(1) == 0)(functools.partial(grid_step, True))
        pl.when(pl.program_id(1) > 0)(functools.partial(grid_step, False))


def _attention(qt, k, vt, tiles, left_blocks, lambda_init=None, extra=()):
    b, s, w = k.shape
    n_pairs, _, blk = qt.shape[2:]
    n_q = B_STEP_BLOCKS if left_blocks is None else A_STEP_BLOCKS
    blk_q = pl.BlockSpec((1, n_q) + qt.shape[2:], lambda bi, i: (bi, i, 0, 0, 0))
    whole_k = pl.BlockSpec((1, s, w), lambda bi, i: (bi, 0, 0))
    whole_v = pl.BlockSpec((1,) + vt.shape[1:], lambda bi, i: (bi, 0, 0, 0, 0))
    f32 = jnp.float32
    score_block = (n_pairs, blk, 2 * blk)
    if left_blocks is None:
        score_scratch = [pltpu.VMEM(score_block, f32)] * B_SCORE_RING + [pltpu.VMEM((n_pairs, 1, 2 * blk), f32)]
    else:
        score_scratch = [pltpu.VMEM((left_blocks + 1,) + score_block, f32)]
    return pl.pallas_call(
        functools.partial(_attn_kernel, left_blocks=left_blocks, lambda_init=lambda_init),
        grid=(b, s // (n_q * blk)),
        in_specs=[blk_q, whole_k, whole_v, _resident(tiles.shape)] + [_resident(e.shape) for e in extra],
        out_specs=pl.BlockSpec((1, n_q * blk, w), lambda bi, i: (bi, i, 0)),
        out_shape=jax.ShapeDtypeStruct((b, s, w), jnp.bfloat16),
        scratch_shapes=[pltpu.VMEM((n_pairs, PAIR, 2 * blk), jnp.bfloat16), *score_scratch,
                        pltpu.VMEM((n_pairs, ACC_ROWS, 2 * blk), f32)],
        compiler_params=pltpu.CompilerParams(dimension_semantics=("arbitrary", "arbitrary"),
                                             vmem_limit_bytes=V7X_VMEM_LIMIT),
        name="attn_a" if lambda_init is None else "attn_b",
    )(qt, k, vt, tiles, *extra)


def kernel(x, ffn1_norm, ffn1_w_in, ffn1_w_out, mix_norm, w_mix_in, b_gate, rel_bias_a, lambda_q1, lambda_k1, lambda_q2, lambda_k2, subln_g, t5_bias, w_branch_a, w_branch_b, w_o, ffn2_norm, ffn2_w_in, ffn2_w_out, final_norm):
    b, s, d = x.shape
    depth = ffn1_norm.shape[0]
    assert depth >= 1
    width = w_branch_a.shape[1]
    n_pairs = width // PAIR
    f32 = jnp.float32

    t5_rel = t5_bias.astype(f32)[:, _t5_bucket(_window_rel(B_TILES, B_BLK))]
    t5_rel = jnp.broadcast_to(t5_rel[:, None, :, None, :], (n_pairs, 2) + t5_rel.shape[1:2] + (1, 2 * LANES))
    tiles_b = _bias_tiles(t5_rel, B_TILES, B_BLK, None, B_FLAT_TILES)

    xf = x.astype(f32).reshape(b * s, d)
    for li in range(depth):
        x1, qa, ka, va, qb, kb, vb, gates = _ffn_mix(
            xf, ffn1_norm[li][None].astype(f32), ffn1_w_in[li].astype(f32), ffn1_w_out[li].astype(f32),
            mix_norm[li][None].astype(f32), w_mix_in[li].astype(f32), b_gate[li][None].astype(f32), width, s)

        rel_a = jnp.clip(_window_rel(A_TILES, A_BLK), -REL_CLIP, REL_CLIP) + REL_CLIP
        a_rel = rel_bias_a[li].astype(f32)[:, rel_a]
        tiles_a = _bias_tiles(a_rel.reshape(n_pairs, 2, rel_a.shape[0], 1, 2 * LANES), A_TILES, A_BLK, LEFT_CHUNKS,
                              A_FLAT_TILES)

        def seq(t):
            return t.reshape(b, s, width)

        ya = _attention(qa, seq(ka), va, tiles_a, A_TILES - 1)
        lambda_init = 0.8 - 0.6 * math.exp(-0.3 * li)
        lams = [t[li][None].astype(f32) for t in (lambda_q1, lambda_k1, lambda_q2, lambda_k2)]
        yb = _attention(qb, seq(kb), vb, tiles_b, None, lambda_init=lambda_init,
                        extra=(*lams, subln_g[li][None].astype(f32)))

        xf = _out_ffn(x1, ya.reshape(b * s, width), yb.reshape(b * s, width), gates,
                      w_branch_a[li].astype(f32), w_branch_b[li].astype(f32), w_o[li].astype(f32),
                      ffn2_norm[li][None].astype(f32), ffn2_w_in[li].astype(f32), ffn2_w_out[li].astype(f32),
                      final_norm[None].astype(f32), final_norm=(li == depth - 1))
    return xf.reshape(b, s, d).astype(x.dtype)
```

```python
import functools
import math

import jax
import jax.numpy as jnp
from jax import lax
from jax.experimental import pallas as pl
from jax.experimental.pallas import tpu as pltpu

EPS = 1e-6
NEG = -1e30
LOG2E = math.log2(math.e)

CHUNK = 64
LEFT_CHUNKS = 8
REL_CLIP = 128
T5_BUCKETS = 32
T5_MAX_DIST = 128

LANES = 128
SUBLANES = 8
HEAD_DIM = 64
PAIR = 2 * HEAD_DIM
SUB_CHUNKS = LANES // CHUNK
ACC_ROWS = PAIR + SUBLANES

A_BLK = 128
A_TILES = LEFT_CHUNKS * CHUNK // A_BLK + 1
B_BLK = 256
B_TILES = 3
A_FLAT_TILES = tuple(t for t in range(A_TILES)
                     if t * A_BLK - (A_BLK - 1) >= REL_CLIP and (t + 1) * A_BLK <= LEFT_CHUNKS * CHUNK)
B_FLAT_TILES = (B_TILES - 1,)
B_SCORE_RING = 4
A_STEP_BLOCKS = 8
B_STEP_BLOCKS = 2

V7X_VMEM_LIMIT = 60000 * 1024

ROW_TILE = 512
BF16_ROWS = 16
CAST_CHUNK_BYTES = 3 << 18
CAST_SLOTS = 4


def _rms(xf, g):
    return xf * lax.rsqrt(jnp.mean(xf * xf, axis=-1, keepdims=True) + EPS) * g


def _dot(a, b):
    return jnp.dot(a, b, preferred_element_type=jnp.float32)


def _swiglu(h, w_in_ref, w_out_ref):
    d_ff = w_out_ref.shape[0]
    gate = _dot(h, w_in_ref[:, :d_ff])
    up = _dot(h, w_in_ref[:, d_ff:])
    act = (gate * jax.nn.sigmoid(gate) * up).astype(jnp.bfloat16)
    return _dot(act, w_out_ref[...])


def _cast_chunk_rows(rows, width):
    fits = [r for r in range(BF16_ROWS, rows + 1, BF16_ROWS)
            if rows % r == 0 and r * width * 4 <= CAST_CHUNK_BYTES]
    return max(fits)


def _load_cast(src_hbm, dst_ref):
    rows, width = src_hbm.shape
    chunk = _cast_chunk_rows(rows, width)
    n_chunks = rows // chunk

    ahead = min(CAST_SLOTS, n_chunks) - 1

    def body(stage_ref, sem):
        def copy(c):
            slot = c % CAST_SLOTS
            return pltpu.make_async_copy(src_hbm.at[pl.ds(c * chunk, chunk)], stage_ref.at[slot], sem.at[slot])

        for c in range(ahead):
            copy(c).start()

        def one_chunk(c, carry):
            @pl.when(c + ahead < n_chunks)
            def _():
                copy(c + ahead).start()

            copy(c).wait()
            dst_ref[pl.ds(pl.multiple_of(c * chunk, chunk), chunk)] = (
                stage_ref[c % CAST_SLOTS].astype(dst_ref.dtype))
            return carry

        lax.fori_loop(0, n_chunks, one_chunk, 0)

    pl.run_scoped(body, pltpu.VMEM((CAST_SLOTS, chunk, width), src_hbm.dtype),
                  pltpu.SemaphoreType.DMA((CAST_SLOTS,)))


def _load_weights_once(pairs):
    @pl.when(pl.program_id(0) == 0)
    def _():
        for src_hbm, dst_ref in pairs:
            _load_cast(src_hbm, dst_ref)


def _ffn_mix_kernel(x_ref, g1_ref, w_in_hbm, w_out_hbm, gm_ref, w_mix_hbm, b_gate_ref,
                    x1_ref, qa_ref, ka_ref, va_ref, qb_ref, kb_ref, vb_ref, gates_ref,
                    w_in_ref, w_out_ref, w_mix_ref):
    _load_weights_once([(w_in_hbm, w_in_ref), (w_out_hbm, w_out_ref), (w_mix_hbm, w_mix_ref)])
    x = x_ref[...]
    h = _rms(x, g1_ref[...]).astype(jnp.bfloat16)
    x1 = x + 0.5 * _swiglu(h, w_in_ref, w_out_ref)
    x1_ref[...] = x1

    u = _rms(x1, gm_ref[...]).astype(jnp.bfloat16)
    width = ka_ref.shape[1]
    scale = HEAD_DIM ** -0.5 * LOG2E
    outs = (qa_ref, ka_ref, va_ref, qb_ref, kb_ref, vb_ref)
    for n, o_ref in enumerate(outs):
        p = _dot(u, w_mix_ref[:, n * width:(n + 1) * width])
        if o_ref is ka_ref or o_ref is kb_ref:
            o_ref[...] = p.astype(o_ref.dtype)
        elif o_ref is qa_ref or o_ref is qb_ref:
            _store_transposed(p * scale, o_ref)
        else:
            _store_transposed(p, o_ref)
    logits = _dot(u, w_mix_ref[:, len(outs) * width:]) + b_gate_ref[...]
    gates_ref[...] = jax.nn.sigmoid(logits)


def _store_transposed(p, o_ref):
    n_blk, n_pairs, _, blk = o_ref.shape[1:]
    pt = p.T
    for c in range(n_blk):
        for pr in range(n_pairs):
            o_ref[0, c, pr] = pt[pr * PAIR:(pr + 1) * PAIR, c * blk:(c + 1) * blk].astype(o_ref.dtype)


def _out_ffn_kernel(x1_ref, ya_ref, yb_ref, gates_ref, wa_hbm, wb_hbm, wo_hbm,
                    g2_ref, w_in_hbm, w_out_hbm, gf_ref, o_ref,
                    wa_ref, wb_ref, wo_ref, w_in_ref, w_out_ref, *, final_norm):
    _load_weights_once([(wa_hbm, wa_ref), (wb_hbm, wb_ref), (wo_hbm, wo_ref),
                        (w_in_hbm, w_in_ref), (w_out_hbm, w_out_ref)])
    d = x1_ref.shape[1]
    merged = (gates_ref[:, :d] * _dot(ya_ref[...], wa_ref[...])
              + gates_ref[:, d:] * _dot(yb_ref[...], wb_ref[...]))
    x2 = x1_ref[...] + _dot(merged.astype(jnp.bfloat16), wo_ref[...])
    h = _rms(x2, g2_ref[...]).astype(jnp.bfloat16)
    x3 = x2 + 0.5 * _swiglu(h, w_in_ref, w_out_ref)
    o_ref[...] = _rms(x3, gf_ref[...]) if final_norm else x3


def _resident(shape):
    return pl.BlockSpec(shape, lambda *_: (0,) * len(shape), pipeline_mode=pl.Buffered(1))


_IN_HBM = pl.BlockSpec(memory_space=pl.ANY)


def _bf16_scratch(*weights):
    return [pltpu.VMEM(w.shape, jnp.bfloat16) for w in weights]


def _rows(tm, width):
    return pl.BlockSpec((tm, width), lambda i: (i, 0))


def _ffn_mix(x, g1, w_in, w_out, gm, w_mix, b_gate, width, seq):
    m, d = x.shape
    tm = ROW_TILE
    bf = jnp.bfloat16
    n_pairs = width // PAIR
    steps_per_seq = seq // tm

    def transposed(blk):
        shape = (m // seq, seq // blk, n_pairs, PAIR, blk)
        spec = pl.BlockSpec((1, tm // blk) + shape[2:], lambda i: (i // steps_per_seq, i % steps_per_seq, 0, 0, 0))
        return jax.ShapeDtypeStruct(shape, bf), spec

    keys = jax.ShapeDtypeStruct((m, width), bf), _rows(tm, width)
    outs = [(jax.ShapeDtypeStruct((m, d), jnp.float32), _rows(tm, d)),
            transposed(A_BLK), keys, transposed(A_BLK), transposed(B_BLK), keys, transposed(B_BLK),
            (jax.ShapeDtypeStruct((m, b_gate.shape[1]), jnp.float32), _rows(tm, b_gate.shape[1]))]
    return pl.pallas_call(
        _ffn_mix_kernel,
        grid=(m // tm,),
        in_specs=[_rows(tm, d), _resident(g1.shape), _IN_HBM, _IN_HBM,
                  _resident(gm.shape), _IN_HBM, _resident(b_gate.shape)],
        out_specs=[spec for _, spec in outs],
        out_shape=[shape for shape, _ in outs],
        scratch_shapes=_bf16_scratch(w_in, w_out, w_mix),
        compiler_params=pltpu.CompilerParams(dimension_semantics=("arbitrary",),
                                             vmem_limit_bytes=V7X_VMEM_LIMIT),
        name="ffn_mix",
    )(x, g1, w_in, w_out, gm, w_mix, b_gate)


def _out_ffn(x1, ya, yb, gates, wa, wb, wo, g2, w_in, w_out, gf, final_norm):
    m, d = x1.shape
    tm = ROW_TILE
    return pl.pallas_call(
        functools.partial(_out_ffn_kernel, final_norm=final_norm),
        grid=(m // tm,),
        in_specs=[_rows(tm, d), _rows(tm, ya.shape[1]), _rows(tm, yb.shape[1]), _rows(tm, gates.shape[1]),
                  _IN_HBM, _IN_HBM, _IN_HBM, _resident(g2.shape), _IN_HBM, _IN_HBM, _resident(gf.shape)],
        out_specs=_rows(tm, d),
        out_shape=jax.ShapeDtypeStruct((m, d), jnp.float32),
        scratch_shapes=_bf16_scratch(wa, wb, wo, w_in, w_out),
        compiler_params=pltpu.CompilerParams(dimension_semantics=("arbitrary",),
                                             vmem_limit_bytes=V7X_VMEM_LIMIT),
        name="out_ffn",
    )(x1, ya, yb, gates, wa, wb, wo, g2, w_in, w_out, gf)


def _t5_bucket(rel):
    nb = T5_BUCKETS // 2
    ret = jnp.where(rel > 0, nb, 0)
    n = jnp.abs(rel)
    max_exact = nb // 2
    nf = jnp.maximum(n, 1).astype(jnp.float32)
    large = max_exact + (jnp.log(nf / max_exact) / math.log(T5_MAX_DIST / max_exact)
                         * (nb - max_exact)).astype(jnp.int32)
    large = jnp.minimum(large, nb - 1)
    return ret + jnp.where(n < max_exact, n, large)


def _window_rel(n_tiles, blk):
    nb = blk // LANES
    u = jnp.arange(-(nb - 1), n_tiles * nb, dtype=jnp.int32)[:, None]
    return LANES - jnp.arange(2 * LANES, dtype=jnp.int32)[None, :] - LANES * u


def _bias_tiles_kernel(rel_bias_ref, tiles_ref, *, left_chunks, flat):
    n_pairs, n_tiles, blk, _ = tiles_ref.shape
    nb = blk // LANES
    kj = lax.broadcasted_iota(jnp.int32, (LANES, LANES), 0)
    qi = lax.broadcasted_iota(jnp.int32, (LANES, LANES), 1)
    chunk_diff = kj // CHUNK - qi // CHUNK
    for p in range(n_pairs):
        for half in range(2):
            for w in range(rel_bias_ref.shape[2]):
                u = w - (nb - 1)
                row = jnp.broadcast_to(rel_bias_ref[p, half, w], (LANES, 2 * LANES))
                sub = pltpu.roll(row, 0, 1, stride=1, stride_axis=0)[:, LANES:]
                for t in range(n_tiles):
                    for a in range(nb):
                        c = t * nb + a - u
                        if not 0 <= c < nb:
                            continue
                        diff = chunk_diff - u * SUB_CHUNKS
                        allowed = diff <= 0
                        if left_chunks is not None:
                            allowed = allowed & (diff >= -left_chunks)
                        tiles_ref[p, t, pl.ds(c * LANES, LANES), pl.ds(half * blk + a * LANES, LANES)] = (
                            jnp.where(allowed, sub * LOG2E, NEG))
        shift = tiles_ref[p, flat[0]]
        for t in range(n_tiles):
            tiles_ref[p, t] = tiles_ref[p, t] - shift


def _bias_tiles(rel_bias, n_tiles, blk, left_chunks, flat):
    n_pairs = rel_bias.shape[0]
    return pl.pallas_call(
        functools.partial(_bias_tiles_kernel, left_chunks=left_chunks, flat=flat),
        out_shape=jax.ShapeDtypeStruct((n_pairs, n_tiles, blk, 2 * blk), jnp.float32),
        compiler_params=pltpu.CompilerParams(vmem_limit_bytes=V7X_VMEM_LIMIT),
        name="bias_tiles",
    )(rel_bias)


def _sum_rows(x):
    acc = x[:SUBLANES]
    for r in range(1, x.shape[0] // SUBLANES):
        acc = acc + x[r * SUBLANES:(r + 1) * SUBLANES]
    return acc


def _fold_rows(x):
    m = x[:SUBLANES]
    for r in range(1, x.shape[0] // SUBLANES):
        m = jnp.maximum(m, x[r * SUBLANES:(r + 1) * SUBLANES])
    return m


def _attn_kernel(*refs, left_blocks, lambda_init):
    if lambda_init is None:
        qt_ref, k_ref, vt_ref, tiles_ref, o_ref, q2t_ref, s_ref, acc_ref = refs
    else:
        (qt_ref, k_ref, vt_ref, tiles_ref, lq1_ref, lk1_ref, lq2_ref, lk2_ref, subg_ref, o_ref,
         q2t_ref, *s_refs, m_ref, acc_ref) = refs
        lam = (jnp.exp(jnp.sum(lq1_ref[...] * lk1_ref[...], axis=-1, keepdims=True))
               - jnp.exp(jnp.sum(lq2_ref[...] * lk2_ref[...], axis=-1, keepdims=True))
               + lambda_init)
    n_q, n_pairs, _, blk = qt_ref.shape[1:]
    top_half = lax.broadcasted_iota(jnp.int32, (PAIR, blk), 0) < HEAD_DIM

    def cols(p):
        return slice(p * PAIR, (p + 1) * PAIR)

    def key_rows(j, n=1):
        return pl.ds(pl.multiple_of(j * blk, blk), n * blk)

    def weights_update(vt, e):
        return jnp.concatenate([_dot(vt, e.astype(jnp.bfloat16)), _sum_rows(e)], axis=0)

    def stack_queries(c, after=None):
        for p in range(n_pairs):
            qt = qt_ref[0, c, p]
            if after is not None:
                dep = after[p] if isinstance(after, list) else after
                dep = jnp.concatenate([dep] * (blk // dep.shape[1]), axis=1)
                qt = qt + (dep * 0.0).astype(qt.dtype)
            zero = jnp.zeros_like(qt)
            q2t_ref[p, :, :blk] = jnp.where(top_half, qt, zero)
            q2t_ref[p, :, blk:] = jnp.where(top_half, zero, qt)

    def finish(c):
        late = []
        for p in range(n_pairs):
            o = acc_ref[p, :PAIR] / jnp.sum(acc_ref[p, PAIR:], axis=0, keepdims=True)
            if lambda_init is None:
                y = jnp.where(top_half, o[:, :blk], o[:, blk:]).T
            else:
                d = o[:, :blk] - lam * o[:, blk:]
                d = d * lax.rsqrt(jnp.mean(d * d, axis=0, keepdims=True) + EPS)
                y = d.T * subg_ref[...] * (1.0 - lambda_init)
            o_ref[0, c * blk:(c + 1) * blk, cols(p)] = y.astype(o_ref.dtype)
            late.append(y[:1])
        return late

    if left_blocks is None:
        n_tiles = tiles_ref.shape[1]
        ring = len(s_refs)

        def causal_prefix(c, i, readers, after):
            stack_queries(c, after)

            def scores(j, last_readers):
                tile = min(i - j, n_tiles - 1)
                for p in range(n_pairs):
                    kb = k_ref[0, j * blk:(j + 1) * blk, cols(p)]
                    if last_readers is not None:
                        kb = kb + (last_readers[p] * 0.0).astype(kb.dtype)
                    s = _dot(kb, q2t_ref[p])
                    s_refs[j % ring][p] = s if tile in B_FLAT_TILES else s + tiles_ref[p, tile]

            def online(j):
                late = []
                for p in range(n_pairs):
                    s = s_refs[j % ring][p]
                    m = jnp.max(_fold_rows(s), axis=0, keepdims=True)
                    if j == 0:
                        update = weights_update(vt_ref[0, j, p], jnp.exp2(s - m))
                        acc_ref[p] = update
                    else:
                        m_old = m_ref[p]
                        m = jnp.maximum(m_old, m)
                        update = weights_update(vt_ref[0, j, p], jnp.exp2(s - m))
                        acc_ref[p] = jnp.exp2(m_old - m) * acc_ref[p] + update
                    m_ref[p] = m
                    late.append(update[PAIR:PAIR + 1, :PAIR])
                return late

            scores(0, readers[0])
            for j in range(i + 1):
                if j < i:
                    scores(j + 1, readers[(j + 1) % ring])
                readers[j % ring] = online(j)
            finish(c)
            return readers[i % ring]

        def grid_step(step):
            readers, after = [None] * ring, None
            for c in range(n_q):
                after = causal_prefix(c, step * n_q + c, readers, after)

        for step in range(k_ref.shape[1] // (n_q * blk)):
            pl.when(pl.program_id(1) == step)(functools.partial(grid_step, step))
    else:
        def band(i, n_left):
            order = range(n_left, -1, -1)
            ms = [jnp.full((SUBLANES, 2 * blk), NEG, jnp.float32) for _ in range(n_pairs)]
            for d in order:
                for p in range(n_pairs):
                    s = _dot(k_ref[0, key_rows(i - d), cols(p)], q2t_ref[p])
                    if d not in A_FLAT_TILES:
                        s = s + tiles_ref[p, d]
                    s_ref[d, p] = s
                    ms[p] = jnp.maximum(ms[p], _fold_rows(s))
            ms = [jnp.max(m, axis=0, keepdims=True) for m in ms]
            for d in order:
                for p in range(n_pairs):
                    update = weights_update(vt_ref[0, i - d, p], jnp.exp2(s_ref[d, p] - ms[p]))
                    acc_ref[p] = update if d == n_left else acc_ref[p] + update
            return update[PAIR:PAIR + 1, :blk]

        def grid_step(first):
            after = None
            for c in range(n_q):
                stack_queries(c, after)
                after = band(pl.program_id(1) * n_q + c, min(c, left_blocks) if first else left_blocks)
                finish(c)

        assert n_q >= left_blocks
        pl.when(pl.program_id(1) == 0)(functools.partial(grid_step, True))
        pl.when(pl.program_id(1) > 0)(functools.partial(grid_step, False))


def _attention(qt, k, vt, tiles, left_blocks, lambda_init=None, extra=()):
    b, s, w = k.shape
    n_pairs, _, blk = qt.shape[2:]
    n_q = B_STEP_BLOCKS if left_blocks is None else A_STEP_BLOCKS
    blk_q = pl.BlockSpec((1, n_q) + qt.shape[2:], lambda bi, i: (bi, i, 0, 0, 0))
    whole_k = pl.BlockSpec((1, s, w), lambda bi, i: (bi, 0, 0))
    whole_v = pl.BlockSpec((1,) + vt.shape[1:], lambda bi, i: (bi, 0, 0, 0, 0))
    f32 = jnp.float32
    score_block = (n_pairs, blk, 2 * blk)
    if left_blocks is None:
        score_scratch = [pltpu.VMEM(score_block, f32)] * B_SCORE_RING + [pltpu.VMEM((n_pairs, 1, 2 * blk), f32)]
    else:
        score_scratch = [pltpu.VMEM((left_blocks + 1,) + score_block, f32)]
    return pl.pallas_call(
        functools.partial(_attn_kernel, left_blocks=left_blocks, lambda_init=lambda_init),
        grid=(b, s // (n_q * blk)),
        in_specs=[blk_q, whole_k, whole_v, _resident(tiles.shape)] + [_resident(e.shape) for e in extra],
        out_specs=pl.BlockSpec((1, n_q * blk, w), lambda bi, i: (bi, i, 0)),
        out_shape=jax.ShapeDtypeStruct((b, s, w), jnp.bfloat16),
        scratch_shapes=[pltpu.VMEM((n_pairs, PAIR, 2 * blk), jnp.bfloat16), *score_scratch,
                        pltpu.VMEM((n_pairs, ACC_ROWS, 2 * blk), f32)],
        compiler_params=pltpu.CompilerParams(dimension_semantics=("arbitrary", "arbitrary"),
                                             vmem_limit_bytes=V7X_VMEM_LIMIT),
        name="attn_a" if lambda_init is None else "attn_b",
    )(qt, k, vt, tiles, *extra)


def kernel(x, ffn1_norm, ffn1_w_in, ffn1_w_out, mix_norm, w_mix_in, b_gate, rel_bias_a, lambda_q1, lambda_k1, lambda_q2, lambda_k2, subln_g, t5_bias, w_branch_a, w_branch_b, w_o, ffn2_norm, ffn2_w_in, ffn2_w_out, final_norm):
    b, s, d = x.shape
    depth = ffn1_norm.shape[0]
    assert depth >= 1
    width = w_branch_a.shape[1]
    n_pairs = width // PAIR
    f32 = jnp.float32

    t5_rel = t5_bias.astype(f32)[:, _t5_bucket(_window_rel(B_TILES, B_BLK))]
    t5_rel = jnp.broadcast_to(t5_rel[:, None, :, None, :], (n_pairs, 2) + t5_rel.shape[1:2] + (1, 2 * LANES))
    tiles_b = _bias_tiles(t5_rel, B_TILES, B_BLK, None, B_FLAT_TILES)

    xf = x.astype(f32).reshape(b * s, d)
    for li in range(depth):
        x1, qa, ka, va, qb, kb, vb, gates = _ffn_mix(
            xf, ffn1_norm[li][None].astype(f32), ffn1_w_in[li].astype(f32), ffn1_w_out[li].astype(f32),
            mix_norm[li][None].astype(f32), w_mix_in[li].astype(f32), b_gate[li][None].astype(f32), width, s)

        rel_a = jnp.clip(_window_rel(A_TILES, A_BLK), -REL_CLIP, REL_CLIP) + REL_CLIP
        a_rel = rel_bias_a[li].astype(f32)[:, rel_a]
        tiles_a = _bias_tiles(a_rel.reshape(n_pairs, 2, rel_a.shape[0], 1, 2 * LANES), A_TILES, A_BLK, LEFT_CHUNKS,
                              A_FLAT_TILES)

        def seq(t):
            return t.reshape(b, s, width)

        ya = _attention(qa, seq(ka), va, tiles_a, A_TILES - 1)
        lambda_init = 0.8 - 0.6 * math.exp(-0.3 * li)
        lams = [t[li][None].astype(f32) for t in (lambda_q1, lambda_k1, lambda_q2, lambda_k2)]
        yb = _attention(qb, seq(kb), vb, tiles_b, None, lambda_init=lambda_init,
                        extra=(*lams, subln_g[li][None].astype(f32)))

        xf = _out_ffn(x1, ya.reshape(b * s, width), yb.reshape(b * s, width), gates,
                      w_branch_a[li].astype(f32), w_branch_b[li].astype(f32), w_o[li].astype(f32),
                      ffn2_norm[li][None].astype(f32), ffn2_w_in[li].astype(f32), ffn2_w_out[li].astype(f32),
                      final_norm[None].astype(f32), final_norm=(li == depth - 1))
    return xf.reshape(b, s, d).astype(x.dtype)
```

```python
import functools
import math

import jax
import jax.numpy as jnp
from jax import lax
from jax.experimental import pallas as pl
from jax.experimental.pallas import tpu as pltpu

EPS = 1e-6
NEG = -1e30
LOG2E = math.log2(math.e)

CHUNK = 64
LEFT_CHUNKS = 8
REL_CLIP = 128
T5_BUCKETS = 32
T5_MAX_DIST = 128

LANES = 128
SUBLANES = 8
HEAD_DIM = 64
PAIR = 2 * HEAD_DIM
SUB_CHUNKS = LANES // CHUNK
ACC_ROWS = PAIR + SUBLANES

A_BLK = 128
A_TILES = LEFT_CHUNKS * CHUNK // A_BLK + 1
B_BLK = 256
B_TILES = 3
A_FLAT_TILES = tuple(t for t in range(A_TILES)
                     if t * A_BLK - (A_BLK - 1) >= REL_CLIP and (t + 1) * A_BLK <= LEFT_CHUNKS * CHUNK)
B_FLAT_TILES = (B_TILES - 1,)
B_SCORE_RING = 3
A_STEP_BLOCKS = 8
B_STEP_BLOCKS = 2

V7X_VMEM_LIMIT = 60000 * 1024

ROW_TILE = 512
BF16_ROWS = 16
CAST_CHUNK_BYTES = 3 << 18
CAST_SLOTS = 4


def _rms(xf, g):
    return xf * lax.rsqrt(jnp.mean(xf * xf, axis=-1, keepdims=True) + EPS) * g


def _dot(a, b):
    return jnp.dot(a, b, preferred_element_type=jnp.float32)


def _swiglu(h, w_in_ref, w_out_ref):
    d_ff = w_out_ref.shape[0]
    gate = _dot(h, w_in_ref[:, :d_ff])
    up = _dot(h, w_in_ref[:, d_ff:])
    act = (gate * jax.nn.sigmoid(gate) * up).astype(jnp.bfloat16)
    return _dot(act, w_out_ref[...])


def _cast_chunk_rows(rows, width):
    fits = [r for r in range(BF16_ROWS, rows + 1, BF16_ROWS)
            if rows % r == 0 and r * width * 4 <= CAST_CHUNK_BYTES]
    return max(fits)


def _load_cast(src_hbm, dst_ref):
    rows, width = src_hbm.shape
    chunk = _cast_chunk_rows(rows, width)
    n_chunks = rows // chunk

    ahead = min(CAST_SLOTS, n_chunks) - 1

    def body(stage_ref, sem):
        def copy(c):
            slot = c % CAST_SLOTS
            return pltpu.make_async_copy(src_hbm.at[pl.ds(c * chunk, chunk)], stage_ref.at[slot], sem.at[slot])

        for c in range(ahead):
            copy(c).start()

        def one_chunk(c, carry):
            @pl.when(c + ahead < n_chunks)
            def _():
                copy(c + ahead).start()

            copy(c).wait()
            dst_ref[pl.ds(pl.multiple_of(c * chunk, chunk), chunk)] = (
                stage_ref[c % CAST_SLOTS].astype(dst_ref.dtype))
            return carry

        lax.fori_loop(0, n_chunks, one_chunk, 0)

    pl.run_scoped(body, pltpu.VMEM((CAST_SLOTS, chunk, width), src_hbm.dtype),
                  pltpu.SemaphoreType.DMA((CAST_SLOTS,)))


def _load_weights_once(pairs):
    @pl.when(pl.program_id(0) == 0)
    def _():
        for src_hbm, dst_ref in pairs:
            _load_cast(src_hbm, dst_ref)


def _ffn_mix_kernel(x_ref, g1_ref, w_in_hbm, w_out_hbm, gm_ref, w_mix_hbm, b_gate_ref,
                    x1_ref, qa_ref, ka_ref, va_ref, qb_ref, kb_ref, vb_ref, gates_ref,
                    w_in_ref, w_out_ref, w_mix_ref):
    _load_weights_once([(w_in_hbm, w_in_ref), (w_out_hbm, w_out_ref), (w_mix_hbm, w_mix_ref)])
    x = x_ref[...]
    h = _rms(x, g1_ref[...]).astype(jnp.bfloat16)
    x1 = x + 0.5 * _swiglu(h, w_in_ref, w_out_ref)
    x1_ref[...] = x1

    u = _rms(x1, gm_ref[...]).astype(jnp.bfloat16)
    width = ka_ref.shape[1]
    scale = HEAD_DIM ** -0.5 * LOG2E
    outs = (qa_ref, ka_ref, va_ref, qb_ref, kb_ref, vb_ref)
    for n, o_ref in enumerate(outs):
        p = _dot(u, w_mix_ref[:, n * width:(n + 1) * width])
        if o_ref is ka_ref or o_ref is kb_ref:
            o_ref[...] = p.astype(o_ref.dtype)
        elif o_ref is qa_ref or o_ref is qb_ref:
            _store_transposed(p * scale, o_ref)
        else:
            _store_transposed(p, o_ref)
    logits = _dot(u, w_mix_ref[:, len(outs) * width:]) + b_gate_ref[...]
    gates_ref[...] = jax.nn.sigmoid(logits)


def _store_transposed(p, o_ref):
    n_blk, n_pairs, _, blk = o_ref.shape[1:]
    pt = p.T
    for c in range(n_blk):
        for pr in range(n_pairs):
            o_ref[0, c, pr] = pt[pr * PAIR:(pr + 1) * PAIR, c * blk:(c + 1) * blk].astype(o_ref.dtype)


def _out_ffn_kernel(x1_ref, ya_ref, yb_ref, gates_ref, wa_hbm, wb_hbm, wo_hbm,
                    g2_ref, w_in_hbm, w_out_hbm, gf_ref, o_ref,
                    wa_ref, wb_ref, wo_ref, w_in_ref, w_out_ref, *, final_norm):
    _load_weights_once([(wa_hbm, wa_ref), (wb_hbm, wb_ref), (wo_hbm, wo_ref),
                        (w_in_hbm, w_in_ref), (w_out_hbm, w_out_ref)])
    d = x1_ref.shape[1]
    merged = (gates_ref[:, :d] * _dot(ya_ref[...], wa_ref[...])
              + gates_ref[:, d:] * _dot(yb_ref[...], wb_ref[...]))
    x2 = x1_ref[...] + _dot(merged.astype(jnp.bfloat16), wo_ref[...])
    h = _rms(x2, g2_ref[...]).astype(jnp.bfloat16)
    x3 = x2 + 0.5 * _swiglu(h, w_in_ref, w_out_ref)
    o_ref[...] = _rms(x3, gf_ref[...]) if final_norm else x3


def _resident(shape):
    return pl.BlockSpec(shape, lambda *_: (0,) * len(shape), pipeline_mode=pl.Buffered(1))


_IN_HBM = pl.BlockSpec(memory_space=pl.ANY)


def _bf16_scratch(*weights):
    return [pltpu.VMEM(w.shape, jnp.bfloat16) for w in weights]


def _rows(tm, width):
    return pl.BlockSpec((tm, width), lambda i: (i, 0))


def _ffn_mix(x, g1, w_in, w_out, gm, w_mix, b_gate, width, seq):
    m, d = x.shape
    tm = ROW_TILE
    bf = jnp.bfloat16
    n_pairs = width // PAIR
    steps_per_seq = seq // tm

    def transposed(blk):
        shape = (m // seq, seq // blk, n_pairs, PAIR, blk)
        spec = pl.BlockSpec((1, tm // blk) + shape[2:], lambda i: (i // steps_per_seq, i % steps_per_seq, 0, 0, 0))
        return jax.ShapeDtypeStruct(shape, bf), spec

    keys = jax.ShapeDtypeStruct((m, width), bf), _rows(tm, width)
    outs = [(jax.ShapeDtypeStruct((m, d), jnp.float32), _rows(tm, d)),
            transposed(A_BLK), keys, transposed(A_BLK), transposed(B_BLK), keys, transposed(B_BLK),
            (jax.ShapeDtypeStruct((m, b_gate.shape[1]), jnp.float32), _rows(tm, b_gate.shape[1]))]
    return pl.pallas_call(
        _ffn_mix_kernel,
        grid=(m // tm,),
        in_specs=[_rows(tm, d), _resident(g1.shape), _IN_HBM, _IN_HBM,
                  _resident(gm.shape), _IN_HBM, _resident(b_gate.shape)],
        out_specs=[spec for _, spec in outs],
        out_shape=[shape for shape, _ in outs],
        scratch_shapes=_bf16_scratch(w_in, w_out, w_mix),
        compiler_params=pltpu.CompilerParams(dimension_semantics=("arbitrary",),
                                             vmem_limit_bytes=V7X_VMEM_LIMIT),
        name="ffn_mix",
    )(x, g1, w_in, w_out, gm, w_mix, b_gate)


def _out_ffn(x1, ya, yb, gates, wa, wb, wo, g2, w_in, w_out, gf, final_norm):
    m, d = x1.shape
    tm = ROW_TILE
    return pl.pallas_call(
        functools.partial(_out_ffn_kernel, final_norm=final_norm),
        grid=(m // tm,),
        in_specs=[_rows(tm, d), _rows(tm, ya.shape[1]), _rows(tm, yb.shape[1]), _rows(tm, gates.shape[1]),
                  _IN_HBM, _IN_HBM, _IN_HBM, _resident(g2.shape), _IN_HBM, _IN_HBM, _resident(gf.shape)],
        out_specs=_rows(tm, d),
        out_shape=jax.ShapeDtypeStruct((m, d), jnp.float32),
        scratch_shapes=_bf16_scratch(wa, wb, wo, w_in, w_out),
        compiler_params=pltpu.CompilerParams(dimension_semantics=("arbitrary",),
                                             vmem_limit_bytes=V7X_VMEM_LIMIT),
        name="out_ffn",
    )(x1, ya, yb, gates, wa, wb, wo, g2, w_in, w_out, gf)


def _t5_bucket(rel):
    nb = T5_BUCKETS // 2
    ret = jnp.where(rel > 0, nb, 0)
    n = jnp.abs(rel)
    max_exact = nb // 2
    nf = jnp.maximum(n, 1).astype(jnp.float32)
    large = max_exact + (jnp.log(nf / max_exact) / math.log(T5_MAX_DIST / max_exact)
                         * (nb - max_exact)).astype(jnp.int32)
    large = jnp.minimum(large, nb - 1)
    return ret + jnp.where(n < max_exact, n, large)


def _window_rel(n_tiles, blk):
    nb = blk // LANES
    u = jnp.arange(-(nb - 1), n_tiles * nb, dtype=jnp.int32)[:, None]
    return LANES - jnp.arange(2 * LANES, dtype=jnp.int32)[None, :] - LANES * u


def _bias_tiles_kernel(rel_bias_ref, tiles_ref, *, left_chunks, flat):
    n_pairs, n_tiles, blk, _ = tiles_ref.shape
    nb = blk // LANES
    kj = lax.broadcasted_iota(jnp.int32, (LANES, LANES), 0)
    qi = lax.broadcasted_iota(jnp.int32, (LANES, LANES), 1)
    chunk_diff = kj // CHUNK - qi // CHUNK
    for p in range(n_pairs):
        for half in range(2):
            for w in range(rel_bias_ref.shape[2]):
                u = w - (nb - 1)
                row = jnp.broadcast_to(rel_bias_ref[p, half, w], (LANES, 2 * LANES))
                sub = pltpu.roll(row, 0, 1, stride=1, stride_axis=0)[:, LANES:]
                for t in range(n_tiles):
                    for a in range(nb):
                        c = t * nb + a - u
                        if not 0 <= c < nb:
                            continue
                        diff = chunk_diff - u * SUB_CHUNKS
                        allowed = diff <= 0
                        if left_chunks is not None:
                            allowed = allowed & (diff >= -left_chunks)
                        tiles_ref[p, t, pl.ds(c * LANES, LANES), pl.ds(half * blk + a * LANES, LANES)] = (
                            jnp.where(allowed, sub * LOG2E, NEG))
        shift = tiles_ref[p, flat[0]]
        for t in range(n_tiles):
            tiles_ref[p, t] = tiles_ref[p, t] - shift


def _bias_tiles(rel_bias, n_tiles, blk, left_chunks, flat):
    n_pairs = rel_bias.shape[0]
    return pl.pallas_call(
        functools.partial(_bias_tiles_kernel, left_chunks=left_chunks, flat=flat),
        out_shape=jax.ShapeDtypeStruct((n_pairs, n_tiles, blk, 2 * blk), jnp.float32),
        compiler_params=pltpu.CompilerParams(vmem_limit_bytes=V7X_VMEM_LIMIT),
        name="bias_tiles",
    )(rel_bias)


def _sum_rows(x):
    acc = x[:SUBLANES]
    for r in range(1, x.shape[0] // SUBLANES):
        acc = acc + x[r * SUBLANES:(r + 1) * SUBLANES]
    return acc


def _fold_rows(x):
    m = x[:SUBLANES]
    for r in range(1, x.shape[0] // SUBLANES):
        m = jnp.maximum(m, x[r * SUBLANES:(r + 1) * SUBLANES])
    return m


def _attn_kernel(*refs, left_blocks, lambda_init):
    if lambda_init is None:
        qt_ref, k_ref, vt_ref, tiles_ref, o_ref, q2t_ref, s_ref, acc_ref = refs
    else:
        (qt_ref, k_ref, vt_ref, tiles_ref, lq1_ref, lk1_ref, lq2_ref, lk2_ref, subg_ref, o_ref,
         q2t_ref, *s_refs, m_ref, acc_ref) = refs
        lam = (jnp.exp(jnp.sum(lq1_ref[...] * lk1_ref[...], axis=-1, keepdims=True))
               - jnp.exp(jnp.sum(lq2_ref[...] * lk2_ref[...], axis=-1, keepdims=True))
               + lambda_init)
    n_q, n_pairs, _, blk = qt_ref.shape[1:]
    top_half = lax.broadcasted_iota(jnp.int32, (PAIR, blk), 0) < HEAD_DIM

    def cols(p):
        return slice(p * PAIR, (p + 1) * PAIR)

    def key_rows(j):
        return pl.ds(pl.multiple_of(j * blk, blk), blk)

    def weights_update(vt, e):
        return jnp.concatenate([_dot(vt, e.astype(jnp.bfloat16)), _sum_rows(e)], axis=0)

    def stack_queries(c, after=None):
        for p in range(n_pairs):
            qt = qt_ref[0, c, p]
            if after is not None:
                dep = after[p] if isinstance(after, list) else after
                dep = jnp.concatenate([dep] * (blk // dep.shape[1]), axis=1)
                qt = qt + (dep * 0.0).astype(qt.dtype)
            zero = jnp.zeros_like(qt)
            q2t_ref[p, :, :blk] = jnp.where(top_half, qt, zero)
            q2t_ref[p, :, blk:] = jnp.where(top_half, zero, qt)

    def finish(c):
        late = []
        for p in range(n_pairs):
            o = acc_ref[p, :PAIR] / jnp.sum(acc_ref[p, PAIR:], axis=0, keepdims=True)
            if lambda_init is None:
                y = jnp.where(top_half, o[:, :blk], o[:, blk:]).T
            else:
                d = o[:, :blk] - lam * o[:, blk:]
                d = d * lax.rsqrt(jnp.mean(d * d, axis=0, keepdims=True) + EPS)
                y = d.T * subg_ref[...] * (1.0 - lambda_init)
            o_ref[0, c * blk:(c + 1) * blk, cols(p)] = y.astype(o_ref.dtype)
            late.append(y[:1])
        return late

    if left_blocks is None:
        n_tiles = tiles_ref.shape[1]
        ring = len(s_refs)

        def causal_prefix(c, i, readers, after):
            stack_queries(c, after)

            def scores(j, last_readers):
                tile = min(i - j, n_tiles - 1)
                for p in range(n_pairs):
                    kb = k_ref[0, j * blk:(j + 1) * blk, cols(p)]
                    if last_readers is not None:
                        kb = kb + (last_readers[p] * 0.0).astype(kb.dtype)
                    s = _dot(kb, q2t_ref[p])
                    s_refs[j % ring][p] = s if tile in B_FLAT_TILES else s + tiles_ref[p, tile]

            def online(j):
                late = []
                for p in range(n_pairs):
                    s = s_refs[j % ring][p]
                    m = jnp.max(_fold_rows(s), axis=0, keepdims=True)
                    if j == 0:
                        update = weights_update(vt_ref[0, j, p], jnp.exp2(s - m))
                        acc_ref[p] = update
                    else:
                        m_old = m_ref[p]
                        m = jnp.maximum(m_old, m)
                        update = weights_update(vt_ref[0, j, p], jnp.exp2(s - m))
                        acc_ref[p] = jnp.exp2(m_old - m) * acc_ref[p] + update
                    m_ref[p] = m
                    late.append(update[PAIR:PAIR + 1, :PAIR])
                return late

            scores(0, readers[0])
            for j in range(i + 1):
                if j < i:
                    scores(j + 1, readers[(j + 1) % ring])
                readers[j % ring] = online(j)
            finish(c)
            return readers[i % ring]

        def grid_step(step):
            readers, after = [None] * ring, None
            for c in range(n_q):
                after = causal_prefix(c, step * n_q + c, readers, after)

        for step in range(k_ref.shape[1] // (n_q * blk)):
            pl.when(pl.program_id(1) == step)(functools.partial(grid_step, step))
    else:
        def band(i, n_left):
            order = range(n_left, -1, -1)
            ms = [jnp.full((SUBLANES, 2 * blk), NEG, jnp.float32) for _ in range(n_pairs)]
            for d in order:
                for p in range(n_pairs):
                    s = _dot(k_ref[0, key_rows(i - d), cols(p)], q2t_ref[p])
                    if d not in A_FLAT_TILES:
                        s = s + tiles_ref[p, d]
                    s_ref[d, p] = s
                    ms[p] = jnp.maximum(ms[p], _fold_rows(s))
            ms = [jnp.max(m, axis=0, keepdims=True) for m in ms]
            for d in order:
                for p in range(n_pairs):
                    update = weights_update(vt_ref[0, i - d, p], jnp.exp2(s_ref[d, p] - ms[p]))
                    acc_ref[p] = update if d == n_left else acc_ref[p] + update
            return update[PAIR:PAIR + 1, :blk]

        def grid_step(first):
            after = None
            for c in range(n_q):
                stack_queries(c, after)
                after = band(pl.program_id(1) * n_q + c, min(c, left_blocks) if first else left_blocks)
                finish(c)

        assert n_q >= left_blocks
        pl.when(pl.program_id(1) == 0)(functools.partial(grid_step, True))
        pl.when(pl.program_id(1) > 0)(functools.partial(grid_step, False))


def _attention(qt, k, vt, tiles, left_blocks, lambda_init=None, extra=()):
    b, s, w = k.shape
    n_pairs, _, blk = qt.shape[2:]
    n_q = B_STEP_BLOCKS if left_blocks is None else A_STEP_BLOCKS
    blk_q = pl.BlockSpec((1, n_q) + qt.shape[2:], lambda bi, i: (bi, i, 0, 0, 0))
    whole_k = pl.BlockSpec((1, s, w), lambda bi, i: (bi, 0, 0))
    whole_v = pl.BlockSpec((1,) + vt.shape[1:], lambda bi, i: (bi, 0, 0, 0, 0))
    f32 = jnp.float32
    score_block = (n_pairs, blk, 2 * blk)
    if left_blocks is None:
        score_scratch = [pltpu.VMEM(score_block, f32)] * B_SCORE_RING + [pltpu.VMEM((n_pairs, 1, 2 * blk), f32)]
    else:
        score_scratch = [pltpu.VMEM((left_blocks + 1,) + score_block, f32)]
    return pl.pallas_call(
        functools.partial(_attn_kernel, left_blocks=left_blocks, lambda_init=lambda_init),
        grid=(b, s // (n_q * blk)),
        in_specs=[blk_q, whole_k, whole_v, _resident(tiles.shape)] + [_resident(e.shape) for e in extra],
        out_specs=pl.BlockSpec((1, n_q * blk, w), lambda bi, i: (bi, i, 0)),
        out_shape=jax.ShapeDtypeStruct((b, s, w), jnp.bfloat16),
        scratch_shapes=[pltpu.VMEM((n_pairs, PAIR, 2 * blk), jnp.bfloat16), *score_scratch,
                        pltpu.VMEM((n_pairs, ACC_ROWS, 2 * blk), f32)],
        compiler_params=pltpu.CompilerParams(dimension_semantics=("arbitrary", "arbitrary"),
                                             vmem_limit_bytes=V7X_VMEM_LIMIT),
        name="attn_a" if lambda_init is None else "attn_b",
    )(qt, k, vt, tiles, *extra)


def kernel(x, ffn1_norm, ffn1_w_in, ffn1_w_out, mix_norm, w_mix_in, b_gate, rel_bias_a, lambda_q1, lambda_k1, lambda_q2, lambda_k2, subln_g, t5_bias, w_branch_a, w_branch_b, w_o, ffn2_norm, ffn2_w_in, ffn2_w_out, final_norm):
    b, s, d = x.shape
    depth = ffn1_norm.shape[0]
    assert depth >= 1
    width = w_branch_a.shape[1]
    n_pairs = width // PAIR
    f32 = jnp.float32

    t5_rel = t5_bias.astype(f32)[:, _t5_bucket(_window_rel(B_TILES, B_BLK))]
    t5_rel = jnp.broadcast_to(t5_rel[:, None, :, None, :], (n_pairs, 2) + t5_rel.shape[1:2] + (1, 2 * LANES))
    tiles_b = _bias_tiles(t5_rel, B_TILES, B_BLK, None, B_FLAT_TILES)

    xf = x.astype(f32).reshape(b * s, d)
    for li in range(depth):
        x1, qa, ka, va, qb, kb, vb, gates = _ffn_mix(
            xf, ffn1_norm[li][None].astype(f32), ffn1_w_in[li].astype(f32), ffn1_w_out[li].astype(f32),
            mix_norm[li][None].astype(f32), w_mix_in[li].astype(f32), b_gate[li][None].astype(f32), width, s)

        rel_a = jnp.clip(_window_rel(A_TILES, A_BLK), -REL_CLIP, REL_CLIP) + REL_CLIP
        a_rel = rel_bias_a[li].astype(f32)[:, rel_a]
        tiles_a = _bias_tiles(a_rel.reshape(n_pairs, 2, rel_a.shape[0], 1, 2 * LANES), A_TILES, A_BLK, LEFT_CHUNKS,
                              A_FLAT_TILES)

        def seq(t):
            return t.reshape(b, s, width)

        ya = _attention(qa, seq(ka), va, tiles_a, A_TILES - 1)
        lambda_init = 0.8 - 0.6 * math.exp(-0.3 * li)
        lams = [t[li][None].astype(f32) for t in (lambda_q1, lambda_k1, lambda_q2, lambda_k2)]
        yb = _attention(qb, seq(kb), vb, tiles_b, None, lambda_init=lambda_init,
                        extra=(*lams, subln_g[li][None].astype(f32)))

        xf = _out_ffn(x1, ya.reshape(b * s, width), yb.reshape(b * s, width), gates,
                      w_branch_a[li].astype(f32), w_branch_b[li].astype(f32), w_o[li].astype(f32),
                      ffn2_norm[li][None].astype(f32), ffn2_w_in[li].astype(f32), ffn2_w_out[li].astype(f32),
                      final_norm[None].astype(f32), final_norm=(li == depth - 1))
    return xf.reshape(b, s, d).astype(x.dtype)
```

```python
import functools
import math

import jax
import jax.numpy as jnp
from jax import lax
from jax.experimental import pallas as pl
from jax.experimental.pallas import tpu as pltpu

EPS = 1e-6
NEG = -1e30
LOG2E = math.log2(math.e)

CHUNK = 64
LEFT_CHUNKS = 8
REL_CLIP = 128
T5_BUCKETS = 32
T5_MAX_DIST = 128

LANES = 128
SUBLANES = 8
HEAD_DIM = 64
PAIR = 2 * HEAD_DIM
SUB_CHUNKS = LANES // CHUNK
ACC_ROWS = PAIR + SUBLANES

A_BLK = 128
A_TILES = LEFT_CHUNKS * CHUNK // A_BLK + 1
B_BLK = 256
B_TILES = 3
A_FLAT_TILES = tuple(t for t in range(A_TILES)
                     if t * A_BLK - (A_BLK - 1) >= REL_CLIP and (t + 1) * A_BLK <= LEFT_CHUNKS * CHUNK)
B_FLAT_TILES = (B_TILES - 1,)
B_SCORE_RING = 3
A_STEP_BLOCKS = 8
B_STEP_BLOCKS = 2

V7X_VMEM_LIMIT = 60000 * 1024

ROW_TILE = 512
BF16_ROWS = 16
CAST_CHUNK_BYTES = 3 << 18
CAST_SLOTS = 4


def _rms(xf, g):
    return xf * lax.rsqrt(jnp.mean(xf * xf, axis=-1, keepdims=True) + EPS) * g


def _dot(a, b):
    return jnp.dot(a, b, preferred_element_type=jnp.float32)


def _swiglu(h, w_in_ref, w_out_ref):
    d_ff = w_out_ref.shape[0]
    gate = _dot(h, w_in_ref[:, :d_ff])
    up = _dot(h, w_in_ref[:, d_ff:])
    act = (gate * jax.nn.sigmoid(gate) * up).astype(jnp.bfloat16)
    return _dot(act, w_out_ref[...])


def _cast_chunk_rows(rows, width):
    fits = [r for r in range(BF16_ROWS, rows + 1, BF16_ROWS)
            if rows % r == 0 and r * width * 4 <= CAST_CHUNK_BYTES]
    return max(fits)


def _load_cast(src_hbm, dst_ref):
    rows, width = src_hbm.shape
    chunk = _cast_chunk_rows(rows, width)
    n_chunks = rows // chunk

    ahead = min(CAST_SLOTS, n_chunks) - 1

    def body(stage_ref, sem):
        def copy(c):
            slot = c % CAST_SLOTS
            return pltpu.make_async_copy(src_hbm.at[pl.ds(c * chunk, chunk)], stage_ref.at[slot], sem.at[slot])

        for c in range(ahead):
            copy(c).start()

        def one_chunk(c, carry):
            @pl.when(c + ahead < n_chunks)
            def _():
                copy(c + ahead).start()

            copy(c).wait()
            dst_ref[pl.ds(pl.multiple_of(c * chunk, chunk), chunk)] = (
                stage_ref[c % CAST_SLOTS].astype(dst_ref.dtype))
            return carry

        lax.fori_loop(0, n_chunks, one_chunk, 0)

    pl.run_scoped(body, pltpu.VMEM((CAST_SLOTS, chunk, width), src_hbm.dtype),
                  pltpu.SemaphoreType.DMA((CAST_SLOTS,)))


def _load_weights_once(pairs):
    @pl.when(pl.program_id(0) == 0)
    def _():
        for src_hbm, dst_ref in pairs:
            _load_cast(src_hbm, dst_ref)


def _ffn_mix_kernel(x_ref, g1_ref, w_in_hbm, w_out_hbm, gm_ref, w_mix_hbm, b_gate_ref,
                    x1_ref, qa_ref, ka_ref, va_ref, qb_ref, kb_ref, vb_ref, gates_ref,
                    w_in_ref, w_out_ref, w_mix_ref):
    _load_weights_once([(w_in_hbm, w_in_ref), (w_out_hbm, w_out_ref), (w_mix_hbm, w_mix_ref)])
    x = x_ref[...]
    h = _rms(x, g1_ref[...]).astype(jnp.bfloat16)
    x1 = x + 0.5 * _swiglu(h, w_in_ref, w_out_ref)
    x1_ref[...] = x1

    u = _rms(x1, gm_ref[...]).astype(jnp.bfloat16)
    width = ka_ref.shape[1]
    scale = HEAD_DIM ** -0.5 * LOG2E
    outs = (qa_ref, ka_ref, va_ref, qb_ref, kb_ref, vb_ref)
    for n, o_ref in enumerate(outs):
        p = _dot(u, w_mix_ref[:, n * width:(n + 1) * width])
        if o_ref is ka_ref or o_ref is kb_ref:
            o_ref[...] = p.astype(o_ref.dtype)
        elif o_ref is qa_ref or o_ref is qb_ref:
            _store_transposed(p * scale, o_ref)
        else:
            _store_transposed(p, o_ref)
    logits = _dot(u, w_mix_ref[:, len(outs) * width:]) + b_gate_ref[...]
    gates_ref[...] = jax.nn.sigmoid(logits)


def _store_transposed(p, o_ref):
    n_blk, n_pairs, _, blk = o_ref.shape[1:]
    pt = p.T
    for c in range(n_blk):
        for pr in range(n_pairs):
            o_ref[0, c, pr] = pt[pr * PAIR:(pr + 1) * PAIR, c * blk:(c + 1) * blk].astype(o_ref.dtype)


def _out_ffn_kernel(x1_ref, ya_ref, yb_ref, gates_ref, wa_hbm, wb_hbm, wo_hbm,
                    g2_ref, w_in_hbm, w_out_hbm, gf_ref, o_ref,
                    wa_ref, wb_ref, wo_ref, w_in_ref, w_out_ref, *, final_norm):
    _load_weights_once([(wa_hbm, wa_ref), (wb_hbm, wb_ref), (wo_hbm, wo_ref),
                        (w_in_hbm, w_in_ref), (w_out_hbm, w_out_ref)])
    d = x1_ref.shape[1]
    merged = (gates_ref[:, :d] * _dot(ya_ref[...], wa_ref[...])
              + gates_ref[:, d:] * _dot(yb_ref[...], wb_ref[...]))
    x2 = x1_ref[...] + _dot(merged.astype(jnp.bfloat16), wo_ref[...])
    h = _rms(x2, g2_ref[...]).astype(jnp.bfloat16)
    x3 = x2 + 0.5 * _swiglu(h, w_in_ref, w_out_ref)
    o_ref[...] = _rms(x3, gf_ref[...]) if final_norm else x3


def _resident(shape):
    return pl.BlockSpec(shape, lambda *_: (0,) * len(shape), pipeline_mode=pl.Buffered(1))


_IN_HBM = pl.BlockSpec(memory_space=pl.ANY)


def _bf16_scratch(*weights):
    return [pltpu.VMEM(w.shape, jnp.bfloat16) for w in weights]


def _rows(tm, width):
    return pl.BlockSpec((tm, width), lambda i: (i, 0))


def _ffn_mix(x, g1, w_in, w_out, gm, w_mix, b_gate, width, seq):
    m, d = x.shape
    tm = ROW_TILE
    bf = jnp.bfloat16
    n_pairs = width // PAIR
    steps_per_seq = seq // tm

    def transposed(blk):
        shape = (m // seq, seq // blk, n_pairs, PAIR, blk)
        spec = pl.BlockSpec((1, tm // blk) + shape[2:], lambda i: (i // steps_per_seq, i % steps_per_seq, 0, 0, 0))
        return jax.ShapeDtypeStruct(shape, bf), spec

    keys = jax.ShapeDtypeStruct((m, width), bf), _rows(tm, width)
    outs = [(jax.ShapeDtypeStruct((m, d), jnp.float32), _rows(tm, d)),
            transposed(A_BLK), keys, transposed(A_BLK), transposed(B_BLK), keys, transposed(B_BLK),
            (jax.ShapeDtypeStruct((m, b_gate.shape[1]), jnp.float32), _rows(tm, b_gate.shape[1]))]
    return pl.pallas_call(
        _ffn_mix_kernel,
        grid=(m // tm,),
        in_specs=[_rows(tm, d), _resident(g1.shape), _IN_HBM, _IN_HBM,
                  _resident(gm.shape), _IN_HBM, _resident(b_gate.shape)],
        out_specs=[spec for _, spec in outs],
        out_shape=[shape for shape, _ in outs],
        scratch_shapes=_bf16_scratch(w_in, w_out, w_mix),
        compiler_params=pltpu.CompilerParams(dimension_semantics=("arbitrary",),
                                             vmem_limit_bytes=V7X_VMEM_LIMIT),
        name="ffn_mix",
    )(x, g1, w_in, w_out, gm, w_mix, b_gate)


def _out_ffn(x1, ya, yb, gates, wa, wb, wo, g2, w_in, w_out, gf, final_norm):
    m, d = x1.shape
    tm = ROW_TILE
    return pl.pallas_call(
        functools.partial(_out_ffn_kernel, final_norm=final_norm),
        grid=(m // tm,),
        in_specs=[_rows(tm, d), _rows(tm, ya.shape[1]), _rows(tm, yb.shape[1]), _rows(tm, gates.shape[1]),
                  _IN_HBM, _IN_HBM, _IN_HBM, _resident(g2.shape), _IN_HBM, _IN_HBM, _resident(gf.shape)],
        out_specs=_rows(tm, d),
        out_shape=jax.ShapeDtypeStruct((m, d), jnp.float32),
        scratch_shapes=_bf16_scratch(wa, wb, wo, w_in, w_out),
        compiler_params=pltpu.CompilerParams(dimension_semantics=("arbitrary",),
                                             vmem_limit_bytes=V7X_VMEM_LIMIT),
        name="out_ffn",
    )(x1, ya, yb, gates, wa, wb, wo, g2, w_in, w_out, gf)


def _t5_bucket(rel):
    nb = T5_BUCKETS // 2
    ret = jnp.where(rel > 0, nb, 0)
    n = jnp.abs(rel)
    max_exact = nb // 2
    nf = jnp.maximum(n, 1).astype(jnp.float32)
    large = max_exact + (jnp.log(nf / max_exact) / math.log(T5_MAX_DIST / max_exact)
                         * (nb - max_exact)).astype(jnp.int32)
    large = jnp.minimum(large, nb - 1)
    return ret + jnp.where(n < max_exact, n, large)


def _window_rel(n_tiles, blk):
    nb = blk // LANES
    u = jnp.arange(-(nb - 1), n_tiles * nb, dtype=jnp.int32)[:, None]
    return LANES - jnp.arange(2 * LANES, dtype=jnp.int32)[None, :] - LANES * u


def _bias_tiles_kernel(rel_bias_ref, tiles_ref, *, left_chunks, flat):
    n_pairs, n_tiles, blk, _ = tiles_ref.shape
    nb = blk // LANES
    kj = lax.broadcasted_iota(jnp.int32, (LANES, LANES), 0)
    qi = lax.broadcasted_iota(jnp.int32, (LANES, LANES), 1)
    chunk_diff = kj // CHUNK - qi // CHUNK
    for p in range(n_pairs):
        for half in range(2):
            for w in range(rel_bias_ref.shape[2]):
                u = w - (nb - 1)
                row = jnp.broadcast_to(rel_bias_ref[p, half, w], (LANES, 2 * LANES))
                sub = pltpu.roll(row, 0, 1, stride=1, stride_axis=0)[:, LANES:]
                for t in range(n_tiles):
                    for a in range(nb):
                        c = t * nb + a - u
                        if not 0 <= c < nb:
                            continue
                        diff = chunk_diff - u * SUB_CHUNKS
                        allowed = diff <= 0
                        if left_chunks is not None:
                            allowed = allowed & (diff >= -left_chunks)
                        tiles_ref[p, t, pl.ds(c * LANES, LANES), pl.ds(half * blk + a * LANES, LANES)] = (
                            jnp.where(allowed, sub * LOG2E, NEG))
        shift = tiles_ref[p, flat[0]]
        for t in range(n_tiles):
            tiles_ref[p, t] = tiles_ref[p, t] - shift


def _bias_tiles(rel_bias, n_tiles, blk, left_chunks, flat):
    n_pairs = rel_bias.shape[0]
    return pl.pallas_call(
        functools.partial(_bias_tiles_kernel, left_chunks=left_chunks, flat=flat),
        out_shape=jax.ShapeDtypeStruct((n_pairs, n_tiles, blk, 2 * blk), jnp.float32),
        compiler_params=pltpu.CompilerParams(vmem_limit_bytes=V7X_VMEM_LIMIT),
        name="bias_tiles",
    )(rel_bias)


def _sum_rows(x):
    acc = x[:SUBLANES]
    for r in range(1, x.shape[0] // SUBLANES):
        acc = acc + x[r * SUBLANES:(r + 1) * SUBLANES]
    return acc


def _fold_rows(x):
    m = x[:SUBLANES]
    for r in range(1, x.shape[0] // SUBLANES):
        m = jnp.maximum(m, x[r * SUBLANES:(r + 1) * SUBLANES])
    return m


def _attn_kernel(*refs, left_blocks, lambda_init):
    if lambda_init is None:
        qt_ref, k_ref, vt_ref, tiles_ref, o_ref, q2t_ref, s_ref, acc_ref = refs
    else:
        (qt_ref, k_ref, vt_ref, tiles_ref, lq1_ref, lk1_ref, lq2_ref, lk2_ref, subg_ref, o_ref,
         q2t_ref, *s_refs, acc_ref) = refs
        lam = (jnp.exp(jnp.sum(lq1_ref[...] * lk1_ref[...], axis=-1, keepdims=True))
               - jnp.exp(jnp.sum(lq2_ref[...] * lk2_ref[...], axis=-1, keepdims=True))
               + lambda_init)
    n_q, n_pairs, _, blk = qt_ref.shape[1:]
    top_half = lax.broadcasted_iota(jnp.int32, (PAIR, blk), 0) < HEAD_DIM

    def cols(p):
        return slice(p * PAIR, (p + 1) * PAIR)

    def key_rows(j):
        return pl.ds(pl.multiple_of(j * blk, blk), blk)

    def weights_update(vt, e):
        return jnp.concatenate([_dot(vt, e.astype(jnp.bfloat16)), _sum_rows(e)], axis=0)

    def stack_queries(c, after=None):
        for p in range(n_pairs):
            qt = qt_ref[0, c, p]
            if after is not None:
                dep = after[p] if isinstance(after, list) else after
                dep = jnp.concatenate([dep] * (blk // dep.shape[1]), axis=1)
                qt = qt + (dep * 0.0).astype(qt.dtype)
            zero = jnp.zeros_like(qt)
            q2t_ref[p, :, :blk] = jnp.where(top_half, qt, zero)
            q2t_ref[p, :, blk:] = jnp.where(top_half, zero, qt)

    def finish(c):
        late = []
        for p in range(n_pairs):
            o = acc_ref[p, :PAIR] / jnp.sum(acc_ref[p, PAIR:], axis=0, keepdims=True)
            if lambda_init is None:
                y = jnp.where(top_half, o[:, :blk], o[:, blk:]).T
            else:
                d = o[:, :blk] - lam * o[:, blk:]
                d = d * lax.rsqrt(jnp.mean(d * d, axis=0, keepdims=True) + EPS)
                y = d.T * subg_ref[...] * (1.0 - lambda_init)
            o_ref[0, c * blk:(c + 1) * blk, cols(p)] = y.astype(o_ref.dtype)
            late.append(y[:1])
        return late

    if left_blocks is None:
        n_tiles = tiles_ref.shape[1]
        ring = len(s_refs)

        def causal_prefix(c, i, readers, after):
            stack_queries(c, after)

            def scores(j, last_readers):
                tile = min(i - j, n_tiles - 1)
                for p in range(n_pairs):
                    kb = k_ref[0, j * blk:(j + 1) * blk, cols(p)]
                    if last_readers is not None:
                        kb = kb + (last_readers[p] * 0.0).astype(kb.dtype)
                    s = _dot(kb, q2t_ref[p])
                    s_refs[j % ring][p] = s if tile in B_FLAT_TILES else s + tiles_ref[p, tile]

            def online(j):
                late = []
                for p in range(n_pairs):
                    s = s_refs[j % ring][p]
                    m = jnp.max(_fold_rows(s), axis=0, keepdims=True)
                    if j == 0:
                        update = weights_update(vt_ref[0, j, p], jnp.exp2(s - m))
                        acc_ref[p] = update
                    else:
                        m = jnp.maximum(ms[p], m)
                        update = weights_update(vt_ref[0, j, p], jnp.exp2(s - m))
                        acc_ref[p] = jnp.exp2(ms[p] - m) * acc_ref[p] + update
                    ms[p] = m
                    late.append(update[PAIR:PAIR + 1, :PAIR])
                return late

            ms = [None] * n_pairs
            scores(0, readers[0])
            for j in range(i + 1):
                if j < i:
                    scores(j + 1, readers[(j + 1) % ring])
                readers[j % ring] = online(j)
            finish(c)
            return readers[i % ring]

        def grid_step(step):
            readers, after = [None] * ring, None
            for c in range(n_q):
                after = causal_prefix(c, step * n_q + c, readers, after)

        for step in range(k_ref.shape[1] // (n_q * blk)):
            pl.when(pl.program_id(1) == step)(functools.partial(grid_step, step))
    else:
        def band(i, n_left):
            order = range(n_left, -1, -1)
            ms = [jnp.full((SUBLANES, 2 * blk), NEG, jnp.float32) for _ in range(n_pairs)]
            for d in order:
                for p in range(n_pairs):
                    s = _dot(k_ref[0, key_rows(i - d), cols(p)], q2t_ref[p])
                    if d not in A_FLAT_TILES:
                        s = s + tiles_ref[p, d]
                    s_ref[d, p] = s
                    ms[p] = jnp.maximum(ms[p], _fold_rows(s))
            ms = [jnp.max(m, axis=0, keepdims=True) for m in ms]
            for d in order:
                for p in range(n_pairs):
                    update = weights_update(vt_ref[0, i - d, p], jnp.exp2(s_ref[d, p] - ms[p]))
                    acc_ref[p] = update if d == n_left else acc_ref[p] + update
            return update[PAIR:PAIR + 1, :blk]

        def grid_step(first):
            after = None
            for c in range(n_q):
                stack_queries(c, after)
                after = band(pl.program_id(1) * n_q + c, min(c, left_blocks) if first else left_blocks)
                finish(c)

        assert n_q >= left_blocks
        pl.when(pl.program_id(1) == 0)(functools.partial(grid_step, True))
        pl.when(pl.program_id(1) > 0)(functools.partial(grid_step, False))


def _attention(qt, k, vt, tiles, left_blocks, lambda_init=None, extra=()):
    b, s, w = k.shape
    n_pairs, _, blk = qt.shape[2:]
    n_q = B_STEP_BLOCKS if left_blocks is None else A_STEP_BLOCKS
    blk_q = pl.BlockSpec((1, n_q) + qt.shape[2:], lambda bi, i: (bi, i, 0, 0, 0))
    whole_k = pl.BlockSpec((1, s, w), lambda bi, i: (bi, 0, 0))
    whole_v = pl.BlockSpec((1,) + vt.shape[1:], lambda bi, i: (bi, 0, 0, 0, 0))
    f32 = jnp.float32
    score_block = (n_pairs, blk, 2 * blk)
    if left_blocks is None:
        score_scratch = [pltpu.VMEM(score_block, f32)] * B_SCORE_RING
    else:
        score_scratch = [pltpu.VMEM((left_blocks + 1,) + score_block, f32)]
    return pl.pallas_call(
        functools.partial(_attn_kernel, left_blocks=left_blocks, lambda_init=lambda_init),
        grid=(b, s // (n_q * blk)),
        in_specs=[blk_q, whole_k, whole_v, _resident(tiles.shape)] + [_resident(e.shape) for e in extra],
        out_specs=pl.BlockSpec((1, n_q * blk, w), lambda bi, i: (bi, i, 0)),
        out_shape=jax.ShapeDtypeStruct((b, s, w), jnp.bfloat16),
        scratch_shapes=[pltpu.VMEM((n_pairs, PAIR, 2 * blk), jnp.bfloat16), *score_scratch,
                        pltpu.VMEM((n_pairs, ACC_ROWS, 2 * blk), f32)],
        compiler_params=pltpu.CompilerParams(dimension_semantics=("arbitrary", "arbitrary"),
                                             vmem_limit_bytes=V7X_VMEM_LIMIT),
        name="attn_a" if lambda_init is None else "attn_b",
    )(qt, k, vt, tiles, *extra)


def kernel(x, ffn1_norm, ffn1_w_in, ffn1_w_out, mix_norm, w_mix_in, b_gate, rel_bias_a, lambda_q1, lambda_k1, lambda_q2, lambda_k2, subln_g, t5_bias, w_branch_a, w_branch_b, w_o, ffn2_norm, ffn2_w_in, ffn2_w_out, final_norm):
    b, s, d = x.shape
    depth = ffn1_norm.shape[0]
    assert depth >= 1
    width = w_branch_a.shape[1]
    n_pairs = width // PAIR
    f32 = jnp.float32

    t5_rel = t5_bias.astype(f32)[:, _t5_bucket(_window_rel(B_TILES, B_BLK))]
    t5_rel = jnp.broadcast_to(t5_rel[:, None, :, None, :], (n_pairs, 2) + t5_rel.shape[1:2] + (1, 2 * LANES))
    tiles_b = _bias_tiles(t5_rel, B_TILES, B_BLK, None, B_FLAT_TILES)

    xf = x.astype(f32).reshape(b * s, d)
    for li in range(depth):
        x1, qa, ka, va, qb, kb, vb, gates = _ffn_mix(
            xf, ffn1_norm[li][None].astype(f32), ffn1_w_in[li].astype(f32), ffn1_w_out[li].astype(f32),
            mix_norm[li][None].astype(f32), w_mix_in[li].astype(f32), b_gate[li][None].astype(f32), width, s)

        rel_a = jnp.clip(_window_rel(A_TILES, A_BLK), -REL_CLIP, REL_CLIP) + REL_CLIP
        a_rel = rel_bias_a[li].astype(f32)[:, rel_a]
        tiles_a = _bias_tiles(a_rel.reshape(n_pairs, 2, rel_a.shape[0], 1, 2 * LANES), A_TILES, A_BLK, LEFT_CHUNKS,
                              A_FLAT_TILES)

        def seq(t):
            return t.reshape(b, s, width)

        ya = _attention(qa, seq(ka), va, tiles_a, A_TILES - 1)
        lambda_init = 0.8 - 0.6 * math.exp(-0.3 * li)
        lams = [t[li][None].astype(f32) for t in (lambda_q1, lambda_k1, lambda_q2, lambda_k2)]
        yb = _attention(qb, seq(kb), vb, tiles_b, None, lambda_init=lambda_init,
                        extra=(*lams, subln_g[li][None].astype(f32)))

        xf = _out_ffn(x1, ya.reshape(b * s, width), yb.reshape(b * s, width), gates,
                      w_branch_a[li].astype(f32), w_branch_b[li].astype(f32), w_o[li].astype(f32),
                      ffn2_norm[li][None].astype(f32), ffn2_w_in[li].astype(f32), ffn2_w_out[li].astype(f32),
                      final_norm[None].astype(f32), final_norm=(li == depth - 1))
    return xf.reshape(b, s, d).astype(x.dtype)
```
